```python
import functools
import jax, jax.numpy as jnp
from jax import lax
import numpy as np

D_MODEL = 1024
BATCH = 4
SEQ = 4096
DEPTH = 1
DEC_BATCH = 32
DEC_SEQ = 4
PAST_LEN = 8192
PAGE_SIZE = 128

HEAD_DIM = 64
MIX_W = D_MODEL
RWKV_W = MIX_W // 2
ATT_W = MIX_W - RWKV_W
N_RWKV_HEADS = RWKV_W // HEAD_DIM
N_ATT_HEADS = ATT_W // HEAD_DIM
DECAY_LORA = max(32, int(round(1.8 * D_MODEL ** 0.5 / 32)) * 32)
AAA_LORA = max(32, int(round(1.8 * D_MODEL ** 0.5 / 32)) * 32)
GATE_LORA = max(32, int(round(0.6 * D_MODEL ** 0.8 / 32)) * 32)
N_IDX_HEADS = 8
IDX_DIM = 64
TOPK_MAX = 256
Q_BLOCK = 128
ROPE_THETA = 10000.0
N_EXPERTS = 32
TOP_K = 4
D_FF = D_MODEL
SWIGLU_LIMIT = 7.0
SWIGLU_ALPHA = 1.702
NORM_EPS = 1e-5
LNX_EPS = 64e-5
RWKV_SPLITS = (RWKV_W, RWKV_W, RWKV_W, DECAY_LORA, AAA_LORA, GATE_LORA)
ATT_SPLITS = (ATT_W, ATT_W, ATT_W, N_IDX_HEADS * IDX_DIM, IDX_DIM, N_IDX_HEADS)
RWKV_IN = 3 * RWKV_W + DECAY_LORA + AAA_LORA + GATE_LORA
ATT_IN = 3 * ATT_W + N_IDX_HEADS * IDX_DIM + IDX_DIM + N_IDX_HEADS
IN_W = RWKV_IN + ATT_IN

kernel_name = 'hymba_rwkv7_dsa_moe_step'


def split_cols(a, sizes):
    offs = [int(o) for o in np.cumsum(sizes)[:-1]]
    return jnp.split(a, offs, axis=-1)


def rms_norm(x, g):
    xf = x.astype(jnp.float32)
    y = xf * lax.rsqrt(jnp.mean(xf * xf, axis=-1, keepdims=True) + NORM_EPS)
    return (y * g.astype(jnp.float32)).astype(x.dtype)


def ada_rms_norm(x, g, shift, scale):
    return rms_norm(x, g) * (1 + scale) + shift


def rope(x, pos):
    half = x.shape[-1] // 2
    inv = ROPE_THETA ** (-jnp.arange(half, dtype=jnp.float32) / half)
    ang = pos.astype(jnp.float32)[:, None] * inv[None, :]
    cos, sin = jnp.cos(ang)[:, None, :], jnp.sin(ang)[:, None, :]
    xf = x.astype(jnp.float32)
    x1, x2 = xf[..., :half], xf[..., half:]
    return jnp.concatenate([x1 * cos - x2 * sin, x2 * cos + x1 * sin], axis=-1).astype(x.dtype)


def wkv_scan(s0, r, w, k, v, a, b):
    def step(s, inp):
        r_t, w_t, k_t, v_t, a_t, b_t = inp
        sa = jnp.einsum('bhij,bhj->bhi', s, a_t)
        s = s * w_t[:, :, None, :] + sa[..., None] * b_t[:, :, None, :] + v_t[..., None] * k_t[:, :, None, :]
        return s, jnp.einsum('bhij,bhj->bhi', s, r_t)
    xs = tuple(jnp.moveaxis(t, 1, 0) for t in (r, w, k, v, a, b))
    s_fin, ys = lax.scan(step, s0, xs)
    return s_fin, jnp.moveaxis(ys, 0, 1)


def rwkv7_mixer(z, wkv0, p):
    B, T, _ = z.shape
    r, k, v, zw, za, zg = split_cols(z, RWKV_SPLITS)
    w_log = -jax.nn.softplus(-(p['w0'] + jnp.tanh(zw) @ p['w2'])) - 0.5
    decay = jnp.exp(-jnp.exp(w_log.astype(jnp.float32)))
    a = jax.nn.sigmoid(p['a0'] + za @ p['a2'])
    g = jax.nn.sigmoid(zg) @ p['g2']
    heads = lambda t: t.reshape(B, T, N_RWKV_HEADS, HEAD_DIM).astype(jnp.float32)
    kk = heads(k * p['k_k'])
    kk = kk / jnp.maximum(jnp.sqrt(jnp.sum(kk * kk, axis=-1, keepdims=True)), 1e-12)
    k = k * (1 + (a - 1) * p['k_a'])
    rh, kh, vh, ah = heads(r), heads(k), heads(v), heads(a)
    wkv_new, y = wkv_scan(wkv0.astype(jnp.float32), rh, heads(decay), kh, vh, -kk, kk * ah)
    mean = jnp.mean(y, axis=-1, keepdims=True)
    var = jnp.mean(jnp.square(y - mean), axis=-1, keepdims=True)
    yn = ((y - mean) * lax.rsqrt(var + LNX_EPS)).reshape(B, T, RWKV_W) * p['lnx_g'] + p['lnx_b']
    bonus = (jnp.sum(rh * kh * p['r_k'].astype(jnp.float32), axis=-1, keepdims=True) * vh).reshape(B, T, RWKV_W)
    return ((yn + bonus) * g).astype(z.dtype), wkv_new


def index_scores(qi, wi, ki):
    dots = jnp.einsum('bqhd,bsd->bqsh', qi, ki)
    return jnp.einsum('bqsh,bqh->bqs', jax.nn.relu(dots), wi).astype(jnp.float32)


def select_keys(scores, q_pos, k_top):
    s_pos = jnp.arange(scores.shape[-1])
    allowed = s_pos[None, :] <= q_pos[:, None]
    _, idx = lax.top_k(jnp.where(allowed[None], scores, -jnp.inf), k_top)
    return idx, idx <= q_pos[None, :, None]


def gathered_attention(q, kg, vg, valid):
    s = jnp.einsum('bqhd,bqkhd->bqhk', q, kg).astype(jnp.float32) * (HEAD_DIM ** -0.5)
    s = jnp.where(valid[:, :, None, :], s, -jnp.inf)
    pr = jax.nn.softmax(s, axis=-1).astype(vg.dtype)
    return jnp.einsum('bqhk,bqkhd->bqhd', pr, vg)


def dsa_prompt(q, k, v, qi, wi, ki):
    B, T, H, Dh = q.shape
    k_top = min(TOPK_MAX, T // 4)
    bidx = jnp.arange(B)[:, None, None]

    def block(i):
        start = i * Q_BLOCK
        qb = lax.dynamic_slice_in_dim(q, start, Q_BLOCK, axis=1)
        qib = lax.dynamic_slice_in_dim(qi, start, Q_BLOCK, axis=1)
        wib = lax.dynamic_slice_in_dim(wi, start, Q_BLOCK, axis=1)
        q_pos = start + jnp.arange(Q_BLOCK)
        idx, valid = select_keys(index_scores(qib, wib, ki), q_pos, k_top)
        return gathered_attention(qb, k[bidx, idx], v[bidx, idx], valid)

    out = lax.map(block, jnp.arange(T // Q_BLOCK))
    return jnp.moveaxis(out, 0, 1).reshape(B, T, H, Dh)


def dsa_sample(q, k, v, qi, wi, ki, cache_k, cache_v, cache_idx_k, page_table, layer):
    B, T = q.shape[:2]
    n_pages = page_table.shape[1]
    past = n_pages * PAGE_SIZE
    k_top = min(TOPK_MAX, (past + T) // 4)
    ki_past = cache_idx_k[layer, page_table].reshape(B, past, IDX_DIM).astype(ki.dtype)
    ki_all = jnp.concatenate([ki_past, ki], axis=1)
    q_pos = past + jnp.arange(T)
    idx, valid = select_keys(index_scores(qi, wi, ki_all), q_pos, k_top)
    bidx = jnp.arange(B)[:, None, None]
    phys = page_table[bidx, jnp.minimum(idx // PAGE_SIZE, n_pages - 1)]
    off = idx % PAGE_SIZE
    new_row = jnp.clip(idx - past, 0, T - 1)
    is_new = (idx >= past)[..., None, None]
    kg = jnp.where(is_new, k[bidx, new_row], cache_k[layer, phys, off].astype(k.dtype))
    vg = jnp.where(is_new, v[bidx, new_row], cache_v[layer, phys, off].astype(v.dtype))
    return gathered_attention(q, kg, vg, valid)


def moe_ffn(h, p):
    B, T, D = h.shape
    t = h.reshape(-1, D)
    logits = (t @ p['router_w'] + p['router_b']).astype(jnp.float32)
    top_v, top_i = lax.top_k(logits, TOP_K)
    top_w = jax.nn.softmax(top_v, axis=-1)
    gates = jnp.sum(jax.nn.one_hot(top_i, N_EXPERTS, dtype=jnp.float32) * top_w[..., None], axis=1).astype(h.dtype)
    out = jnp.zeros_like(t)
    for e in range(N_EXPERTS):
        gu = t @ p['w_gate_up'][e] + p['b_gate_up'][e]
        gate = jnp.minimum(gu[:, :D_FF], SWIGLU_LIMIT)
        up = jnp.clip(gu[:, D_FF:], -SWIGLU_LIMIT, SWIGLU_LIMIT)
        act = (up + 1) * gate * jax.nn.sigmoid(SWIGLU_ALPHA * gate)
        out = out + gates[:, e:e + 1] * (act @ p['w_down'][e] + p['b_down'][e])
    return out.reshape(B, T, D)


def trunk_layer(x, c, pos, p, shift_prev, wkv_prev, attend):
    B, T, _ = x.shape
    mod = (jax.nn.silu(c) @ p['w_mod'] + p['b_mod'])[:, None, :]
    sh1, sc1, ga1, sh2, sc2, ga2 = jnp.split(mod, 6, axis=-1)
    h = ada_rms_norm(x, p['norm1_g'], sh1, sc1)
    proj = h @ p['w_in']
    p_rwkv, p_att = proj[..., :RWKV_IN], proj[..., RWKV_IN:]
    prev = jnp.concatenate([shift_prev[:, None, :].astype(proj.dtype), p_rwkv[:, :-1]], axis=1)
    z = p_rwkv + (prev - p_rwkv) * p['mu_shift']
    y_rwkv, wkv_new = rwkv7_mixer(z, wkv_prev, p)
    q, k, v, qi, ki, wi = split_cols(p_att, ATT_SPLITS)
    q = rope(q.reshape(B, T, N_ATT_HEADS, HEAD_DIM), pos)
    k = rope(k.reshape(B, T, N_ATT_HEADS, HEAD_DIM), pos)
    v = v.reshape(B, T, N_ATT_HEADS, HEAD_DIM)
    qi = rope(qi.reshape(B, T, N_IDX_HEADS, IDX_DIM), pos)
    ki = rope(ki[:, :, None, :], pos)[:, :, 0]
    wi = wi * ((N_IDX_HEADS * IDX_DIM) ** -0.5)
    y_att = attend(q, k, v, qi, wi, ki).reshape(B, T, ATT_W)
    mixed = jnp.concatenate([y_rwkv, y_att.astype(y_rwkv.dtype)], axis=-1) @ p['w_out']
    x = x + ga1 * mixed
    h2 = ada_rms_norm(x, p['norm2_g'], sh2, sc2)
    x = x + ga2 * moe_ffn(h2, p)
    return x, (k, v, ki, wkv_new, p_rwkv[:, -1])


def setup_inputs(seed: int = 0) -> dict:
    key = jax.random.key(seed)
    ks = jax.random.split(key, 40)
    nrm = lambda k, shape, s: jax.random.normal(k, shape, jnp.float32) * s
    n_pages = PAST_LEN // PAGE_SIZE
    n_used = DEC_BATCH * n_pages
    n_pool = n_used + n_used // 4
    page_table = jax.random.permutation(ks[0], n_pool)[:n_used].reshape(DEC_BATCH, n_pages).astype(jnp.int32)
    D = D_MODEL
    return {
        'x_prompt': nrm(ks[1], (BATCH, SEQ, D), 1.0),
        'x_sample': nrm(ks[2], (DEC_BATCH, DEC_SEQ, D), 1.0),
        'c_prompt': nrm(ks[3], (BATCH, D), 1.0),
        'c_sample': nrm(ks[4], (DEC_BATCH, D), 1.0),
        'cache_k': nrm(ks[5], (DEPTH, n_pool, PAGE_SIZE, N_ATT_HEADS, HEAD_DIM), 1.0),
        'cache_v': nrm(ks[6], (DEPTH, n_pool, PAGE_SIZE, N_ATT_HEADS, HEAD_DIM), 1.0),
        'cache_idx_k': nrm(ks[7], (DEPTH, n_pool, PAGE_SIZE, IDX_DIM), 1.0),
        'state_wkv': nrm(ks[8], (DEPTH, DEC_BATCH, N_RWKV_HEADS, HEAD_DIM, HEAD_DIM), 0.5),
        'state_shift': nrm(ks[9], (DEPTH, DEC_BATCH, RWKV_IN), 1.0),
        'page_table': page_table,
        'norm1_g': 1.0 + nrm(ks[10], (DEPTH, D), 0.02),
        'norm2_g': 1.0 + nrm(ks[11], (DEPTH, D), 0.02),
        'w_mod': nrm(ks[12], (DEPTH, D, 6 * D), 0.5 * D ** -0.5),
        'b_mod': nrm(ks[13], (DEPTH, 6 * D), 0.02),
        'w_in': nrm(ks[14], (DEPTH, D, IN_W), D ** -0.5),
        'mu_shift': jax.random.uniform(ks[15], (DEPTH, RWKV_IN), jnp.float32),
        'w0': -1.0 + nrm(ks[16], (DEPTH, RWKV_W), 0.5),
        'w2': nrm(ks[17], (DEPTH, DECAY_LORA, RWKV_W), 0.5 * DECAY_LORA ** -0.5),
        'a0': nrm(ks[18], (DEPTH, RWKV_W), 0.5),
        'a2': nrm(ks[19], (DEPTH, AAA_LORA, RWKV_W), 0.5 * AAA_LORA ** -0.5),
        'g2': nrm(ks[20], (DEPTH, GATE_LORA, RWKV_W), GATE_LORA ** -0.5),
        'k_k': 0.85 + nrm(ks[21], (DEPTH, RWKV_W), 0.05),
        'k_a': 1.0 + nrm(ks[22], (DEPTH, RWKV_W), 0.05),
        'r_k': nrm(ks[23], (DEPTH, N_RWKV_HEADS, HEAD_DIM), 0.1),
        'lnx_g': 1.0 + nrm(ks[24], (DEPTH, RWKV_W), 0.02),
        'lnx_b': nrm(ks[25], (DEPTH, RWKV_W), 0.02),
        'w_out': nrm(ks[26], (DEPTH, MIX_W, D), MIX_W ** -0.5),
        'router_w': nrm(ks[27], (DEPTH, D, N_EXPERTS), D ** -0.5),
        'router_b': nrm(ks[28], (DEPTH, N_EXPERTS), 0.01),
        'w_gate_up': nrm(ks[29], (DEPTH, N_EXPERTS, D, 2 * D_FF), D ** -0.5),
        'b_gate_up': nrm(ks[30], (DEPTH, N_EXPERTS, 2 * D_FF), 0.02),
        'w_down': nrm(ks[31], (DEPTH, N_EXPERTS, D_FF, D), D_FF ** -0.5),
        'b_down': nrm(ks[32], (DEPTH, N_EXPERTS, D), 0.02),
        'final_g': 1.0 + nrm(ks[33], (D,), 0.02),
    }


def reference(x_prompt, x_sample, c_prompt, c_sample, cache_k, cache_v, cache_idx_k, state_wkv, state_shift, page_table,
              norm1_g, norm2_g, w_mod, b_mod, w_in, mu_shift, w0, w2, a0, a2, g2, k_k, k_a, r_k, lnx_g, lnx_b, w_out,
              router_w, router_b, w_gate_up, b_gate_up, w_down, b_down, final_g):
    past = page_table.shape[1] * PAGE_SIZE
    pos_prompt = jnp.arange(x_prompt.shape[1])
    pos_sample = past + jnp.arange(x_sample.shape[1])
    xp, xs = x_prompt, x_sample
    st_prompt, st_sample = [], []
    for l in range(DEPTH):
        p = {
            'norm1_g': norm1_g[l], 'norm2_g': norm2_g[l], 'w_mod': w_mod[l], 'b_mod': b_mod[l], 'w_in': w_in[l],
            'mu_shift': mu_shift[l], 'w0': w0[l], 'w2': w2[l], 'a0': a0[l], 'a2': a2[l], 'g2': g2[l],
            'k_k': k_k[l], 'k_a': k_a[l], 'r_k': r_k[l], 'lnx_g': lnx_g[l], 'lnx_b': lnx_b[l], 'w_out': w_out[l],
            'router_w': router_w[l], 'router_b': router_b[l], 'w_gate_up': w_gate_up[l], 'b_gate_up': b_gate_up[l],
            'w_down': w_down[l], 'b_down': b_down[l],
        }
        bp = xp.shape[0]
        xp, stp = trunk_layer(xp, c_prompt, pos_prompt, p,
                              jnp.zeros((bp, RWKV_IN), xp.dtype),
                              jnp.zeros((bp, N_RWKV_HEADS, HEAD_DIM, HEAD_DIM), jnp.float32),
                              dsa_prompt)
        attend_s = functools.partial(dsa_sample, cache_k=cache_k, cache_v=cache_v, cache_idx_k=cache_idx_k,
                                     page_table=page_table, layer=l)
        xs, sts = trunk_layer(xs, c_sample, pos_sample, p, state_shift[l], state_wkv[l], attend_s)
        st_prompt.append(stp)
        st_sample.append(sts)
    y_prompt = rms_norm(xp, final_g)
    y_sample = rms_norm(xs, final_g)
    stk = lambda sts, i: jnp.stack([s[i] for s in sts], axis=0)
    return (y_prompt, y_sample,
            stk(st_prompt, 0), stk(st_prompt, 1), stk(st_prompt, 2), stk(st_prompt, 3), stk(st_prompt, 4),
            stk(st_sample, 0), stk(st_sample, 1), stk(st_sample, 2), stk(st_sample, 3), stk(st_sample, 4))
```

```python
import functools

import numpy as np
import jax
import jax.numpy as jnp
from jax import lax
from jax.experimental import pallas as pl
from jax.experimental.pallas import tpu as pltpu

F32 = jnp.float32
BF16 = jnp.bfloat16
I32 = jnp.int32
HI = lax.Precision.HIGHEST

HEAD_DIM = 64
PAGE_SIZE = 128
TOPK_MAX = 256
ROPE_THETA = 10000.0
TOP_K = 4
SWIGLU_LIMIT = 7.0
SWIGLU_ALPHA = 1.702
NORM_EPS = 1e-5
LNX_EPS = 64e-5

LANES = 128
SUBLANES = 8
VMEM_LIMIT = 56 * 1024 * 1024

NEG_BIG = -1e30
INT_MIN = -(2 ** 31)
KEY_NEG_INF = INT_MIN + 0x7FFFFF


def _dot(a, b, prec=None):
    return lax.dot_general(a, b, (((1,), (0,)), ((), ())), precision=prec, preferred_element_type=F32)


def _dot_nt(a, b, prec=None):
    return lax.dot_general(a, b, (((1,), (1,)), ((), ())), precision=prec, preferred_element_type=F32)


def _dot_tn(a, b, prec=None):
    return lax.dot_general(a, b, (((0,), (0,)), ((), ())), precision=prec, preferred_element_type=F32)


def _params(*sem):
    return pltpu.CompilerParams(dimension_semantics=sem, vmem_limit_bytes=VMEM_LIMIT)


def _sigmoid(x):
    return jax.nn.sigmoid(x)


def _rms(x, g):
    return x * lax.rsqrt(jnp.mean(x * x, axis=-1, keepdims=True) + NORM_EPS) * g


def _mod_kernel(c_ref, w_ref, b_ref, o_ref):
    c = c_ref[...]
    o_ref[...] = _dot(c * _sigmoid(c), w_ref[...], HI) + b_ref[...]


def _modulation(c_pad, w_mod, b_mod):
    rows, d = c_pad.shape
    n = w_mod.shape[1]
    return pl.pallas_call(
        _mod_kernel,
        grid=(n // d,),
        in_specs=[pl.BlockSpec((rows, d), lambda j: (0, 0)),
                  pl.BlockSpec((d, d), lambda j: (0, j)),
                  pl.BlockSpec((1, d), lambda j: (0, j))],
        out_specs=pl.BlockSpec((rows, d), lambda j: (0, j)),
        out_shape=jax.ShapeDtypeStruct((rows, n), F32),
        compiler_params=_params("arbitrary"),
        name="modulation",
    )(c_pad, w_mod, b_mod.reshape(1, n))


def _rope_block(x, cos, sin_signed):
    lane = lax.broadcasted_iota(I32, x.shape, 1)
    partner = jnp.where((lane & 32) == 0, pltpu.roll(x, LANES - 32, axis=1), pltpu.roll(x, 32, axis=1))
    return x * cos + partner * sin_signed


def _inproj_kernel(x_ref, mod_ref, g_ref, w_ref, cs_ref, p_ref, q_ref, k_ref, kb_ref, v_ref, vb_ref, qi_ref,
                   kw_ref, *, n_rw, att_w):
    x = x_ref[...]
    h = _rms(x, g_ref[...]) * (1 + mod_ref[1, 0]) + mod_ref[0, 0]
    hb = h.astype(BF16)
    cb = 512
    for j in range(n_rw // cb):
        p_ref[:, j * cb:(j + 1) * cb] = _dot(hb, w_ref[:, j * cb:(j + 1) * cb])
    cos, sin = cs_ref[0], cs_ref[1]
    base = n_rw
    ng = att_w // LANES

    def roped(off):
        t = _dot(hb, w_ref[:, off:off + att_w])
        return [_rope_block(t[:, g * LANES:(g + 1) * LANES], cos, sin) for g in range(ng)]

    for g, blk in enumerate(roped(base)):
        q_ref[:, g * LANES:(g + 1) * LANES] = (blk * (HEAD_DIM ** -0.5)).astype(BF16)
    for g, blk in enumerate(roped(base + att_w)):
        k_ref[:, g * LANES:(g + 1) * LANES] = blk
        kb_ref[:, g * LANES:(g + 1) * LANES] = blk.astype(BF16)
    v = _dot(hb, w_ref[:, base + 2 * att_w:base + 3 * att_w])
    v_ref[...] = v
    vb_ref[...] = v.astype(BF16)
    for g, blk in enumerate(roped(base + 3 * att_w)):
        qi_ref[:, g * LANES:(g + 1) * LANES] = blk.astype(BF16)
    kw = _dot(hb, w_ref[:, base + 4 * att_w:base + 4 * att_w + LANES])
    kw_ref[...] = _rope_block(kw, cs_ref[2], cs_ref[3])


def _inproj(x2d, mod6, g1, w_cat, cs, *, tm, tiles_per_seq, n_rw, att_w):
    n, d = x2d.shape
    rmod = mod6.shape[2]
    n_cs_tiles = cs.shape[1] // tm
    grid = (n // tm,)
    row = lambda i: (i, 0)
    outs = [
        jax.ShapeDtypeStruct((n, n_rw), F32),
        jax.ShapeDtypeStruct((n, att_w), BF16),
        jax.ShapeDtypeStruct((n, att_w), F32),
        jax.ShapeDtypeStruct((n, att_w), BF16),
        jax.ShapeDtypeStruct((n, att_w), F32),
        jax.ShapeDtypeStruct((n, att_w), BF16),
        jax.ShapeDtypeStruct((n, att_w), BF16),
        jax.ShapeDtypeStruct((n, LANES), F32),
    ]
    return pl.pallas_call(
        functools.partial(_inproj_kernel, n_rw=n_rw, att_w=att_w),
        grid=grid,
        in_specs=[pl.BlockSpec((tm, d), row),
                  pl.BlockSpec((6, 1, rmod, d), lambda i: (0, i // tiles_per_seq, 0, 0)),
                  pl.BlockSpec((1, d), lambda i: (0, 0)),
                  pl.BlockSpec(w_cat.shape, lambda i: (0, 0)),
                  pl.BlockSpec((4, tm, LANES), lambda i: (0, i % n_cs_tiles, 0))],
        out_specs=[pl.BlockSpec((tm, o.shape[1]), row) for o in outs],
        out_shape=outs,
        compiler_params=_params("arbitrary"),
        name="inproj",
    )(x2d, mod6, g1, w_cat, cs)


def _softplus(x):
    return jnp.maximum(x, 0.0) + jnp.log(1.0 + jnp.exp(-jnp.abs(x)))


def _rwkv_kernel(p_ref, sh0_ref, z0_ref, mu_ref, vec_ref, w2_ref, a2_ref, g2_ref, seg_ref, y_ref, zout_ref,
                 carry_ref, z_ref, ops_ref, yh_ref, *, L, t_valid, rw, n_heads):
    c = pl.program_id(1)

    @pl.when(c == 0)
    def _():
        carry_ref[...] = sh0_ref[0]
        z_ref[...] = z0_ref[0]

    p = p_ref[0]
    row = lax.broadcasted_iota(I32, (L, 1), 0)
    prev = jnp.where(row == 0, carry_ref[...], pltpu.roll(p, 1, axis=0))
    carry_ref[...] = p[L - 1:L, :]
    z = p + (prev - p) * mu_ref[...]
    r, k, v = z[:, 0:rw], z[:, rw:2 * rw], z[:, 2 * rw:3 * rw]
    o = 3 * rw
    zw, za, zg = z[:, o:o + 128], z[:, o + 128:o + 256], z[:, o + 256:o + 512]
    w0, a0, k_k, k_a = vec_ref[0:1], vec_ref[1:2], vec_ref[2:3], vec_ref[3:4]
    lnx_g, lnx_b, r_k = vec_ref[4:5], vec_ref[5:6], vec_ref[6:7]
    seg = seg_ref[...]

    w_log = -_softplus(-(w0 + _dot(jnp.tanh(zw), w2_ref[...], HI))) - 0.5
    lw = -jnp.exp(w_log)
    a = _sigmoid(a0 + _dot(za, a2_ref[...], HI))
    g = _dot(_sigmoid(zg), g2_ref[...], HI)
    kk = k * k_k
    kkn = kk / jnp.maximum(jnp.sqrt(_dot(kk * kk, seg, HI)), 1e-12)
    k2 = k * (1 + (a - 1) * k_a)
    an, bn = -kkn, kkn * a
    if t_valid < L:
        valid = row < t_valid
        lw = jnp.where(valid, lw, 0.0)
        an, bn = jnp.where(valid, an, 0.0), jnp.where(valid, bn, 0.0)
        k2s, vs = jnp.where(valid, k2, 0.0), jnp.where(valid, v, 0.0)
    else:
        k2s, vs = k2, v

    ri = lax.broadcasted_iota(I32, (L, L), 0)
    ci = lax.broadcasted_iota(I32, (L, L), 1)
    incl, strict = ri >= ci, ri > ci
    cum = _dot(incl.astype(F32), lw, HI)
    c_last = cum[L - 1:L, :]
    e_c, e_cp, e_n, e_d = jnp.exp(cum), jnp.exp(cum - lw), jnp.exp(-cum), jnp.exp(c_last - cum)
    ops_ref[0] = an * e_cp
    ops_ref[1] = r * e_c
    ops_ref[2] = bn * e_n
    ops_ref[3] = k2s * e_n
    ops_ref[4] = bn * e_d
    ops_ref[5] = k2s * e_d
    ops_ref[6] = vs
    wl_col = jnp.exp(_dot_tn(lw, jnp.ones((L, LANES), F32), HI))
    eye = (ri == ci).astype(F32)
    n_sq = max(int(np.ceil(np.log2(L))) - 1, 0)

    for h in range(n_heads):
        sl = slice(h * HEAD_DIM, (h + 1) * HEAD_DIM)
        at, rt, bt, kt = ops_ref[0, :, sl], ops_ref[1, :, sl], ops_ref[2, :, sl], ops_ref[3, :, sl]
        bh, kh, vh = ops_ref[4, :, sl], ops_ref[5, :, sl], ops_ref[6, :, sl]
        m_ab = jnp.where(strict, _dot_nt(at, bt, HI), 0.0)
        m_ak = jnp.where(strict, _dot_nt(at, kt, HI), 0.0)
        m_rb = jnp.where(incl, _dot_nt(rt, bt, HI), 0.0)
        m_rk = jnp.where(incl, _dot_nt(rt, kt, HI), 0.0)
        t_inv = eye + m_ab
        pw = m_ab
        for _ in range(n_sq):
            pw = _dot(pw, pw, HI)
            t_inv = t_inv + _dot(t_inv, pw, HI)
        zh = z_ref[h]
        u = _dot(t_inv, _dot(at, zh, HI) + _dot(m_ak, vh, HI), HI)
        yh_ref[:, sl] = _dot(rt, zh, HI) + _dot(m_rb, u, HI) + _dot(m_rk, vh, HI)
        z_ref[h] = wl_col[sl, 0:HEAD_DIM] * zh + _dot_tn(bh, u, HI) + _dot_tn(kh, vh, HI)

    y = yh_ref[...]
    inv_n = 1.0 / HEAD_DIM
    mean = _dot(y, seg, HI) * inv_n
    yc = y - mean
    var = _dot(yc * yc, seg, HI) * inv_n
    yn = yc * lax.rsqrt(var + LNX_EPS) * lnx_g + lnx_b
    bonus = _dot(r * k2 * r_k, seg, HI) * v
    y_ref[0] = ((yn + bonus) * g).astype(y_ref.dtype)

    @pl.when(c == pl.num_programs(1) - 1)
    def _():
        zout_ref[0] = z_ref[...]


def _rwkv(p3, shift0, z0, mu, vec, w2, a2, g2, seg, *, L, t_valid, rw):
    b, t, n_rw = p3.shape
    n_heads = rw // HEAD_DIM
    const2 = lambda i, c: (0, 0)
    return pl.pallas_call(
        functools.partial(_rwkv_kernel, L=L, t_valid=t_valid, rw=rw, n_heads=n_heads),
        grid=(b, t // L),
        in_specs=[pl.BlockSpec((1, L, n_rw), lambda i, c: (i, c, 0)),
                  pl.BlockSpec((1, 1, n_rw), lambda i, c: (i, 0, 0)),
                  pl.BlockSpec((1, n_heads, HEAD_DIM, HEAD_DIM), lambda i, c: (i, 0, 0, 0)),
                  pl.BlockSpec((1, n_rw), const2),
                  pl.BlockSpec(vec.shape, const2),
                  pl.BlockSpec(w2.shape, const2),
                  pl.BlockSpec(a2.shape, const2),
                  pl.BlockSpec(g2.shape, const2),
                  pl.BlockSpec(seg.shape, const2)],
        out_specs=[pl.BlockSpec((1, L, rw), lambda i, c: (i, c, 0)),
                   pl.BlockSpec((1, n_heads, HEAD_DIM, HEAD_DIM), lambda i, c: (i, 0, 0, 0))],
        out_shape=[jax.ShapeDtypeStruct((b, t, rw), BF16),
                   jax.ShapeDtypeStruct((b, n_heads, HEAD_DIM, HEAD_DIM), F32)],
        scratch_shapes=[pltpu.VMEM((1, n_rw), F32),
                        pltpu.VMEM((n_heads, HEAD_DIM, HEAD_DIM), F32),
                        pltpu.VMEM((7, L, rw), F32),
                        pltpu.VMEM((L, rw), F32)],
        compiler_params=_params("arbitrary", "arbitrary"),
        name="rwkv_scan",
    )(p3, shift0, z0, mu, vec, w2, a2, g2, seg)


def _key_to_float(key):
    bits = key ^ ((key >> 31) & 0x7FFFFFFF)
    return lax.bitcast_convert_type(bits, F32)


def _fold_lanes(m):
    acc = m[:, 0:LANES]
    for j in range(1, m.shape[1] // LANES):
        acc = acc + m[:, j * LANES:(j + 1) * LANES]
    return acc


def _select_bias(sc_ref, nc, rows, tk, k_top, n_keys_total):
    def count(pred):
        def body(c, acc):
            return acc + _fold_lanes(pred(c, sc_ref[c]).astype(F32))
        acc = lax.fori_loop(0, nc, body, jnp.zeros((rows, LANES), F32))
        return jnp.sum(acc, axis=-1, keepdims=True)

    kf = float(k_top)
    cnt0 = count(lambda c, s: s >= 0.0)
    tau0 = jnp.where(cnt0 >= kf, 0, INT_MIN).astype(I32)

    def bit_body(i, tau):
        cand = tau + jnp.left_shift(jnp.int32(1), 30 - i)
        cand_f = _key_to_float(cand)
        cnt = count(lambda c, s: s >= cand_f)
        return jnp.where((cnt >= kf) | (cand <= KEY_NEG_INF), cand, tau)

    tau = lax.fori_loop(0, 31, bit_body, tau0)
    tau_f = _key_to_float(tau)
    need = kf - count(lambda c, s: s > tau_f)

    def kidx(c):
        return c * tk + lax.broadcasted_iota(I32, (1, tk), 1)

    n_bits = max(int(np.ceil(np.log2(n_keys_total))), 1)

    def tie_body(i, pb):
        cand = pb + jnp.left_shift(jnp.int32(1), n_bits - 1 - i)
        cnt = count(lambda c, s: (s == tau_f) & (kidx(c) < cand))
        return jnp.where(cnt < need, cand, pb)

    pb = lax.fori_loop(0, n_bits, tie_body, jnp.zeros((rows, 1), I32))

    def write(c, _):
        s = sc_ref[c]
        sel = ((s > tau_f) | ((s == tau_f) & (kidx(c) <= pb))) & (s > -jnp.inf)
        sc_ref[c] = jnp.where(sel, 0.0, NEG_BIG)
        return 0

    lax.fori_loop(0, nc, write, 0)


def _dsa_prompt_kernel(q_ref, qi_ref, kw_ref, kb_ref, vb_ref, kib_ref, o_ref, sc_ref, *, tq, tk, k_top, n_heads,
                       n_idx, t_total):
    qb = pl.program_id(1)
    q0 = qb * tq
    nc = (q0 + tq + tk - 1) // tk
    qpos = q0 + lax.broadcasted_iota(I32, (tq, 1), 0)

    def scores(c, _):
        kic = kib_ref[0, pl.ds(pl.multiple_of(c * tk, tk), tk), :]
        acc = jnp.zeros((tq, tk), F32)
        for h in range(n_idx):
            d = _dot_nt(qi_ref[0, :, h * HEAD_DIM:(h + 1) * HEAD_DIM], kic)
            acc = acc + jnp.maximum(d, 0.0) * kw_ref[0, :, HEAD_DIM + h:HEAD_DIM + h + 1]
        spos = c * tk + lax.broadcasted_iota(I32, (1, tk), 1)
        sc_ref[c] = jnp.where(spos <= qpos, acc, -jnp.inf)
        return 0

    lax.fori_loop(0, nc, scores, 0)
    _select_bias(sc_ref, nc, tq, tk, k_top, t_total)

    for h in range(n_heads):
        sl = slice(h * HEAD_DIM, (h + 1) * HEAD_DIM)
        qh = q_ref[0, :, sl]

        def attend(c, carry):
            m, l, acc = carry
            rows = pl.ds(pl.multiple_of(c * tk, tk), tk)
            s = _dot_nt(qh, kb_ref[0, rows, sl]) + sc_ref[c]
            m_new = jnp.maximum(m, jnp.max(s, axis=-1, keepdims=True))
            alpha = jnp.exp(m - m_new)
            p = jnp.exp(s - m_new)
            l = alpha * l + jnp.sum(p, axis=-1, keepdims=True)
            acc = alpha * acc + _dot(p.astype(BF16), vb_ref[0, rows, sl])
            return m_new, l, acc

        m, l, acc = lax.fori_loop(
            0, nc, attend,
            (jnp.full((tq, 1), NEG_BIG, F32), jnp.zeros((tq, 1), F32), jnp.zeros((tq, HEAD_DIM), F32)))
        o_ref[0, :, sl] = (acc / l).astype(o_ref.dtype)


def _dsa_prompt(q, qi, kw, kb, vb, kib, *, tq, tk, k_top):
    b, t, att_w = q.shape
    n_heads = att_w // HEAD_DIM
    n_idx = qi.shape[2] // HEAD_DIM
    qblk = lambda i, j: (i, j, 0)
    full = lambda i, j: (i, 0, 0)
    return pl.pallas_call(
        functools.partial(_dsa_prompt_kernel, tq=tq, tk=tk, k_top=k_top, n_heads=n_heads, n_idx=n_idx,
                          t_total=t),
        grid=(b, t // tq),
        in_specs=[pl.BlockSpec((1, tq, att_w), qblk),
                  pl.BlockSpec((1, tq, qi.shape[2]), qblk),
                  pl.BlockSpec((1, tq, LANES), qblk),
                  pl.BlockSpec((1, t, att_w), full),
                  pl.BlockSpec((1, t, att_w), full),
                  pl.BlockSpec((1, t, HEAD_DIM), full)],
        out_specs=pl.BlockSpec((1, tq, att_w), qblk),
        out_shape=jax.ShapeDtypeStruct((b, t, att_w), BF16),
        scratch_shapes=[pltpu.VMEM((t // tk, tq, tk), F32)],
        compiler_params=_params("arbitrary", "arbitrary"),
        name="dsa_prompt",
    )(q, qi, kw, kb, vb, kib)


def _idx_sample_kernel(pt_ref, qim_ref, wcol_ref, page_ref, new_ref, o_ref, *, n_pages, t_new, n_idx):
    p = pl.program_id(1)
    is_new = p == n_pages
    ki = jnp.where(is_new, new_ref[0], page_ref[0, 0]).astype(BF16)
    d = _dot_nt(qim_ref[0], ki)
    d = jnp.maximum(d, 0.0) * wcol_ref[0]
    sc = d[0:SUBLANES]
    for h in range(1, n_idx):
        sc = sc + d[h * SUBLANES:(h + 1) * SUBLANES]
    qrow = lax.broadcasted_iota(I32, (SUBLANES, PAGE_SIZE), 0)
    lane = lax.broadcasted_iota(I32, (SUBLANES, PAGE_SIZE), 1)
    allowed = (qrow < t_new) & (jnp.logical_not(is_new) | (lane <= qrow))
    o_ref[0, 0] = jnp.where(allowed, sc, -jnp.inf)


def _idx_sample(page_table, qim, wcol, cache_idx, ki_new, *, t_new):
    b, n_pages = page_table.shape
    n_idx = qim.shape[1] // SUBLANES
    grid_spec = pltpu.PrefetchScalarGridSpec(
        num_scalar_prefetch=1,
        grid=(b, n_pages + 1),
        in_specs=[pl.BlockSpec((1,) + qim.shape[1:], lambda i, p, pt: (i, 0, 0)),
                  pl.BlockSpec((1,) + wcol.shape[1:], lambda i, p, pt: (i, 0, 0)),
                  pl.BlockSpec((1, 1, PAGE_SIZE, HEAD_DIM),
                               lambda i, p, pt: (0, pt[i, jnp.minimum(p, n_pages - 1)], 0, 0)),
                  pl.BlockSpec((1, PAGE_SIZE, HEAD_DIM), lambda i, p, pt: (i, 0, 0))],
        out_specs=pl.BlockSpec((1, 1, SUBLANES, PAGE_SIZE), lambda i, p, pt: (i, p, 0, 0)),
    )
    return pl.pallas_call(
        functools.partial(_idx_sample_kernel, n_pages=n_pages, t_new=t_new, n_idx=n_idx),
        grid_spec=grid_spec,
        out_shape=jax.ShapeDtypeStruct((b, n_pages + 1, SUBLANES, PAGE_SIZE), F32),
        compiler_params=_params("arbitrary", "arbitrary"),
        name="idx_sample",
    )(page_table, qim, wcol, cache_idx, ki_new)


def _select_sample_kernel(s_ref, o_ref, sc_ref, *, nc, k_top, n_keys_total):
    sc_ref[...] = s_ref[0]
    _select_bias(sc_ref, nc, SUBLANES, PAGE_SIZE, k_top, n_keys_total)
    o_ref[0] = sc_ref[...]


def _select_sample(scores, *, k_top):
    b, nc = scores.shape[:2]
    blk = pl.BlockSpec((1,) + scores.shape[1:], lambda i: (i, 0, 0, 0))
    return pl.pallas_call(
        functools.partial(_select_sample_kernel, nc=nc, k_top=k_top, n_keys_total=nc * PAGE_SIZE),
        grid=(b,),
        in_specs=[blk],
        out_specs=blk,
        out_shape=jax.ShapeDtypeStruct(scores.shape, F32),
        scratch_shapes=[pltpu.VMEM(scores.shape[1:], F32)],
        compiler_params=_params("arbitrary"),
        name="select_sample",
    )(scores)


def _attn_sample_kernel(pt_ref, qbd_ref, bias_ref, kpage_ref, vpage_ref, knew_ref, vnew_ref, o_ref, m_ref, l_ref,
                        acc_ref, *, n_pages, n_heads):
    p = pl.program_id(1)

    @pl.when(p == 0)
    def _():
        m_ref[...] = jnp.full(m_ref.shape, NEG_BIG, F32)
        l_ref[...] = jnp.zeros(l_ref.shape, F32)
        acc_ref[...] = jnp.zeros(acc_ref.shape, F32)

    is_new = p == n_pages
    kp = jnp.where(is_new, knew_ref[0], kpage_ref[0, 0]).astype(BF16)
    vp = jnp.where(is_new, vnew_ref[0], vpage_ref[0, 0]).astype(BF16)
    bias = bias_ref[0, 0]
    s = _dot_nt(qbd_ref[0], kp) + jnp.concatenate([bias] * n_heads, axis=0)
    m = m_ref[...]
    m_new = jnp.maximum(m, jnp.max(s, axis=-1, keepdims=True))
    alpha = jnp.exp(m - m_new)
    pr = jnp.exp(s - m_new)
    l_ref[...] = alpha * l_ref[...] + jnp.sum(pr, axis=-1, keepdims=True)
    acc_ref[...] = alpha * acc_ref[...] + _dot(pr.astype(BF16), vp)
    m_ref[...] = m_new

    @pl.when(is_new)
    def _():
        full = acc_ref[...] / l_ref[...]
        col_head = lax.broadcasted_iota(I32, (SUBLANES, full.shape[1]), 1) // HEAD_DIM
        out = jnp.zeros((SUBLANES, full.shape[1]), F32)
        for h in range(n_heads):
            out = out + jnp.where(col_head == h, full[h * SUBLANES:(h + 1) * SUBLANES], 0.0)
        o_ref[0] = out.astype(o_ref.dtype)


def _attn_sample(page_table, qbd, bias, cache_k, cache_v, k_new, v_new):
    b, n_pages = page_table.shape
    att_w = qbd.shape[2]
    n_heads = att_w // HEAD_DIM
    page = lambda i, p, pt: (0, pt[i, jnp.minimum(p, n_pages - 1)], 0, 0)
    per_seq = lambda i, p, pt: (i, 0, 0)
    grid_spec = pltpu.PrefetchScalarGridSpec(
        num_scalar_prefetch=1,
        grid=(b, n_pages + 1),
        in_specs=[pl.BlockSpec((1,) + qbd.shape[1:], per_seq),
                  pl.BlockSpec((1, 1, SUBLANES, PAGE_SIZE), lambda i, p, pt: (i, p, 0, 0)),
                  pl.BlockSpec((1, 1, PAGE_SIZE, att_w), page),
                  pl.BlockSpec((1, 1, PAGE_SIZE, att_w), page),
                  pl.BlockSpec((1, PAGE_SIZE, att_w), per_seq),
                  pl.BlockSpec((1, PAGE_SIZE, att_w), per_seq)],
        out_specs=pl.BlockSpec((1, SUBLANES, att_w), per_seq),
        scratch_shapes=[pltpu.VMEM((n_heads * SUBLANES, 1), F32),
                        pltpu.VMEM((n_heads * SUBLANES, 1), F32),
                        pltpu.VMEM((n_heads * SUBLANES, att_w), F32)],
    )
    return pl.pallas_call(
        functools.partial(_attn_sample_kernel, n_pages=n_pages, n_heads=n_heads),
        grid_spec=grid_spec,
        out_shape=jax.ShapeDtypeStruct((b, SUBLANES, att_w), BF16),
        compiler_params=_params("arbitrary", "arbitrary"),
        name="attn_sample",
    )(page_table, qbd, bias, cache_k, cache_v, k_new, v_new)


def _outproj_kernel(x_ref, yr_ref, ya_ref, mod_ref, wo_ref, g_ref, rw_ref, rb_ref, x1_ref, h2_ref, gates_ref, *, rw):
    mixed = _dot(yr_ref[...], wo_ref[0:rw, :]) + _dot(ya_ref[...], wo_ref[rw:, :])
    x1 = x_ref[...] + mod_ref[2, 0] * mixed
    x1_ref[...] = x1
    h2 = _rms(x1, g_ref[...]) * (1 + mod_ref[4, 0]) + mod_ref[3, 0]
    h2_ref[...] = h2.astype(BF16)
    logits = _dot(h2, rw_ref[...], HI) + rb_ref[...]
    lane = lax.broadcasted_iota(I32, logits.shape, 1).astype(F32)
    work = logits
    picked = jnp.zeros(logits.shape, jnp.bool_)
    top = None
    for i in range(TOP_K):
        mx = jnp.max(work, axis=-1, keepdims=True)
        first = jnp.min(jnp.where(work == mx, lane, float(LANES)), axis=-1, keepdims=True)
        hit = lane == first
        picked = picked | hit
        work = jnp.where(hit, -jnp.inf, work)
        if i == 0:
            top = mx
    e = jnp.where(picked, jnp.exp(logits - top), 0.0)
    gates_ref[...] = e / jnp.sum(e, axis=-1, keepdims=True)


def _outproj(x2d, yr, ya, mod6, w_out_b, g2n, router_w_pad, router_b_pad, *, tm, tiles_per_seq, rw):
    n, d = x2d.shape
    rmod = mod6.shape[2]
    row = lambda i: (i, 0)
    const = lambda i: (0, 0)
    outs = [jax.ShapeDtypeStruct((n, d), F32), jax.ShapeDtypeStruct((n, d), BF16),
            jax.ShapeDtypeStruct((n, LANES), F32)]
    return pl.pallas_call(
        functools.partial(_outproj_kernel, rw=rw),
        grid=(n // tm,),
        in_specs=[pl.BlockSpec((tm, d), row),
                  pl.BlockSpec((tm, yr.shape[1]), row),
                  pl.BlockSpec((tm, ya.shape[1]), row),
                  pl.BlockSpec((6, 1, rmod, d), lambda i: (0, i // tiles_per_seq, 0, 0)),
                  pl.BlockSpec(w_out_b.shape, const),
                  pl.BlockSpec((1, d), const),
                  pl.BlockSpec(router_w_pad.shape, const),
                  pl.BlockSpec((1, LANES), const)],
        out_specs=[pl.BlockSpec((tm, o.shape[1]), row) for o in outs],
        out_shape=outs,
        compiler_params=_params("arbitrary"),
        name="outproj_router",
    )(x2d, yr, ya, mod6, w_out_b, g2n, router_w_pad, router_b_pad)


def _moe_kernel(h_ref, gates_ref, x1_ref, mod_ref, wgu_ref, bgu_ref, wd_ref, bd_ref, fg_ref, o_ref, acc_ref, *,
                d_ff):
    e = pl.program_id(1)

    @pl.when(e == 0)
    def _():
        acc_ref[...] = jnp.zeros(acc_ref.shape, F32)

    gu = _dot(h_ref[...], wgu_ref[0]) + bgu_ref[0]
    gate = jnp.minimum(gu[:, :d_ff], SWIGLU_LIMIT)
    up = jnp.clip(gu[:, d_ff:], -SWIGLU_LIMIT, SWIGLU_LIMIT)
    act = (up + 1) * gate * _sigmoid(SWIGLU_ALPHA * gate)
    y = _dot(act.astype(BF16), wd_ref[0]) + bd_ref[0]
    gates = gates_ref[...]
    lane = lax.broadcasted_iota(I32, gates.shape, 1)
    ge = jnp.sum(jnp.where(lane == e, gates, 0.0), axis=-1, keepdims=True)
    acc_ref[...] += ge * y

    @pl.when(e == pl.num_programs(1) - 1)
    def _():
        x2 = x1_ref[...] + mod_ref[5, 0] * acc_ref[...]
        o_ref[...] = _rms(x2, fg_ref[...])


def _moe(h2, gates, x1, mod6, wgu_b, bgu, wd_b, bd, fg, *, tm, tiles_per_seq):
    n, d = x1.shape
    n_exp, _, two_ff = wgu_b.shape
    rmod = mod6.shape[2]
    row = lambda i, e: (i, 0)
    return pl.pallas_call(
        functools.partial(_moe_kernel, d_ff=two_ff // 2),
        grid=(n // tm, n_exp),
        in_specs=[pl.BlockSpec((tm, d), row),
                  pl.BlockSpec((tm, LANES), row),
                  pl.BlockSpec((tm, d), row),
                  pl.BlockSpec((6, 1, rmod, d), lambda i, e: (0, i // tiles_per_seq, 0, 0)),
                  pl.BlockSpec((1, d, two_ff), lambda i, e: (e, 0, 0)),
                  pl.BlockSpec((1, 1, two_ff), lambda i, e: (e, 0, 0)),
                  pl.BlockSpec((1, two_ff // 2, d), lambda i, e: (e, 0, 0)),
                  pl.BlockSpec((1, 1, d), lambda i, e: (e, 0, 0)),
                  pl.BlockSpec((1, d), lambda i, e: (0, 0))],
        out_specs=pl.BlockSpec((tm, d), row),
        out_shape=jax.ShapeDtypeStruct((n, d), F32),
        scratch_shapes=[pltpu.VMEM((tm, d), F32)],
        compiler_params=_params("arbitrary", "arbitrary"),
        name="moe_ffn",
    )(h2, gates, x1, mod6, wgu_b, bgu, wd_b, bd, fg)


def _pad_cols(a, width):
    return jnp.pad(a, ((0, 0), (0, width - a.shape[1])))


def _rope_tables(pos):
    half = HEAD_DIM // 2
    inv = ROPE_THETA ** (-jnp.arange(half, dtype=F32) / half)
    ang = pos.astype(F32)[:, None] * inv[None, :]
    cos, sin = jnp.cos(ang), jnp.sin(ang)
    cos_h = jnp.concatenate([cos, cos], axis=1)
    sin_h = jnp.concatenate([-sin, sin], axis=1)
    ones = jnp.ones_like(cos_h)
    zeros = jnp.zeros_like(cos_h)
    tabs = jnp.stack([jnp.concatenate([cos_h, cos_h], 1), jnp.concatenate([sin_h, sin_h], 1),
                      jnp.concatenate([cos_h, ones], 1), jnp.concatenate([sin_h, zeros], 1)])
    return tabs


def _layer(x, mod_rows, pos, shift_prev, wkv_prev, wts, attend, *, tm, chunk):
    b, t, d = x.shape
    n = b * t
    rw, att_w, n_rw, rwkv_in = wts["rw"], wts["att_w"], wts["n_rw"], wts["rwkv_in"]
    per_token_mod = mod_rows.shape[0] != b or tm > t
    if per_token_mod:
        mod6 = jnp.repeat(mod_rows, t, axis=0).reshape(n // tm, tm, 6, d).transpose(2, 0, 1, 3)
        tiles_per_seq = 1
    else:
        mod6 = mod_rows.reshape(b, 1, 6, d).transpose(2, 0, 1, 3)
        tiles_per_seq = t // tm
    cs = _rope_tables(pos)
    n_idx_scale = wts["idx_scale"]
    cs = cs.at[2, :, HEAD_DIM:].set(n_idx_scale)
    if cs.shape[1] < tm:
        cs = jnp.tile(cs, (1, tm // cs.shape[1], 1))
    x2d = x.reshape(n, d)
    p, q, k32, kb, v32, vb, qi, kw = _inproj(x2d, mod6, wts["g1"], wts["w_cat"], cs, tm=tm,
                                             tiles_per_seq=tiles_per_seq, n_rw=n_rw, att_w=att_w)
    t_pad = -(-t // chunk) * chunk
    p3 = p.reshape(b, t, n_rw)
    if t_pad != t:
        p3 = jnp.pad(p3, ((0, 0), (0, t_pad - t), (0, 0)))
    z0 = jnp.swapaxes(wkv_prev, -1, -2)
    yr, z_fin = _rwkv(p3, shift_prev.reshape(b, 1, n_rw), z0, wts["mu"], wts["vec"], wts["w2"], wts["a2"], wts["g2"],
                      wts["seg"], L=chunk, t_valid=t if t_pad != t else chunk, rw=rw)
    yr = yr[:, :t].reshape(n, rw)
    wkv_new = jnp.swapaxes(z_fin, -1, -2)
    ya = attend(q, qi, kw, k32, kb, v32, vb).reshape(n, att_w)
    x1, h2, gates = _outproj(x2d, yr, ya, mod6, wts["w_out"], wts["g2n"], wts["router_w"], wts["router_b"], tm=tm,
                             tiles_per_seq=tiles_per_seq, rw=rw)
    tm_moe = min(wts["tm_moe"], n)
    if per_token_mod:
        mod6m, tps_m = mod6, 1
        assert tm_moe == tm
    else:
        mod6m, tps_m = mod6, t // tm_moe
    y = _moe(h2, gates, x1, mod6m, wts["wgu"], wts["bgu"], wts["wd"], wts["bd"], wts["fg"], tm=tm_moe,
             tiles_per_seq=tps_m)
    p_last = p3[:, t - 1, :]
    new_shift = jnp.concatenate([p_last[:, :3 * rw], p_last[:, 3 * rw:3 * rw + wts["lora"][0]],
                                 p_last[:, 3 * rw + 128:3 * rw + 128 + wts["lora"][1]],
                                 p_last[:, 3 * rw + 256:3 * rw + 256 + wts["lora"][2]]], axis=1)
    n_heads = att_w // HEAD_DIM
    state = (k32.reshape(b, t, n_heads, HEAD_DIM), v32.reshape(b, t, n_heads, HEAD_DIM),
             kw[:, :HEAD_DIM].reshape(b, t, HEAD_DIM), wkv_new, new_shift)
    return y.reshape(b, t, d), state


def kernel(x_prompt, x_sample, c_prompt, c_sample, cache_k, cache_v, cache_idx_k, state_wkv, state_shift, page_table, norm1_g, norm2_g, w_mod, b_mod, w_in, mu_shift, w0, w2, a0, a2, g2, k_k, k_a, r_k, lnx_g, lnx_b, w_out, router_w, router_b, w_gate_up, b_gate_up, w_down, b_down, final_g):
    depth = w_in.shape[0]
    assert depth == 1, "single-layer trunk"
    l = 0
    bp, tp, d = x_prompt.shape
    bs, ts, _ = x_sample.shape
    rw = w0.shape[1]
    att_w = w_out.shape[1] - rw
    lora = (w2.shape[1], a2.shape[1], g2.shape[1])
    rwkv_in = 3 * rw + sum(lora)
    n_idx = (w_in.shape[2] - rwkv_in - 3 * att_w - HEAD_DIM) // (HEAD_DIM + 1)
    n_exp = router_w.shape[2]
    n_pages = page_table.shape[1]
    past = n_pages * PAGE_SIZE
    n_rw = 3 * rw + 512
    assert lora[0] <= 128 and lora[1] <= 128 and lora[2] <= 256 and n_idx * HEAD_DIM == att_w

    wi = w_in[l]
    o = 3 * rw
    rw_cols = [wi[:, :o], _pad_cols(wi[:, o:o + lora[0]], 128), _pad_cols(wi[:, o + lora[0]:o + lora[0] + lora[1]], 128),
               _pad_cols(wi[:, o + lora[0] + lora[1]:rwkv_in], 256)]
    a0_ = rwkv_in
    att_cols = [wi[:, a0_:a0_ + 4 * att_w], _pad_cols(wi[:, a0_ + 4 * att_w:], LANES)]
    w_cat = jnp.concatenate(rw_cols + att_cols, axis=1).astype(BF16)
    mu = mu_shift[l]
    mu_pad = jnp.concatenate([mu[:o], jnp.pad(mu[o:o + lora[0]], (0, 128 - lora[0])),
                              jnp.pad(mu[o + lora[0]:o + lora[0] + lora[1]], (0, 128 - lora[1])),
                              jnp.pad(mu[o + lora[0] + lora[1]:], (0, 256 - lora[2]))]).reshape(1, n_rw)
    vec = jnp.stack([w0[l], a0[l], k_k[l], k_a[l], lnx_g[l], lnx_b[l], r_k[l].reshape(-1),
                     jnp.zeros((rw,), F32)])
    head_of = np.arange(rw) // HEAD_DIM
    seg = jnp.asarray((head_of[:, None] == head_of[None, :]).astype(np.float32))
    wts = dict(
        rw=rw, att_w=att_w, n_rw=n_rw, rwkv_in=rwkv_in, lora=lora,
        idx_scale=float((n_idx * HEAD_DIM) ** -0.5),
        g1=norm1_g[l].reshape(1, d), w_cat=w_cat, mu=mu_pad, vec=vec,
        w2=jnp.pad(w2[l], ((0, 128 - lora[0]), (0, 0))), a2=jnp.pad(a2[l], ((0, 128 - lora[1]), (0, 0))),
        g2=jnp.pad(g2[l], ((0, 256 - lora[2]), (0, 0))), seg=seg,
        w_out=w_out[l].astype(BF16), g2n=norm2_g[l].reshape(1, d),
        router_w=_pad_cols(router_w[l], LANES),
        router_b=jnp.concatenate([router_b[l], jnp.full((LANES - n_exp,), NEG_BIG, F32)]).reshape(1, LANES),
        wgu=w_gate_up[l].astype(BF16), bgu=b_gate_up[l].reshape(n_exp, 1, -1),
        wd=w_down[l].astype(BF16), bd=b_down[l].reshape(n_exp, 1, -1), fg=final_g.reshape(1, d),
        tm_moe=512,
    )

    c_all = jnp.concatenate([c_prompt, c_sample], axis=0)
    rows_pad = -(-c_all.shape[0] // SUBLANES) * SUBLANES
    mod_all = _modulation(jnp.pad(c_all, ((0, rows_pad - c_all.shape[0]), (0, 0))), w_mod[l], b_mod[l])
    mod_p, mod_s = mod_all[:bp], mod_all[bp:bp + bs]

    k_top_p = min(TOPK_MAX, tp // 4)

    def attend_prompt(q, qi, kw, k32, kb, v32, vb):
        r3 = lambda a: a.reshape(bp, tp, a.shape[1])
        kib = kw[:, :HEAD_DIM].astype(BF16)
        return _dsa_prompt(r3(q), r3(qi), r3(kw), r3(kb), r3(vb), r3(kib), tq=128, tk=512, k_top=k_top_p)

    y_p, st_p = _layer(x_prompt, mod_p, jnp.arange(tp), jnp.zeros((bp, n_rw), F32),
                       jnp.zeros((bp, rw // HEAD_DIM, HEAD_DIM, HEAD_DIM), F32), wts, attend_prompt,
                       tm=min(512, tp), chunk=64)

    k_top_s = min(TOPK_MAX, (past + ts) // 4)
    n_heads = att_w // HEAD_DIM
    assert ts <= SUBLANES
    ck = cache_k[l].reshape(1, cache_k.shape[1], PAGE_SIZE, att_w)
    cv = cache_v[l].reshape(1, cache_v.shape[1], PAGE_SIZE, att_w)
    cik = cache_idx_k[l].reshape(1, cache_idx_k.shape[1], PAGE_SIZE, HEAD_DIM)

    def head_query_rows(a, nh):
        a = a.reshape(bs, ts, nh, HEAD_DIM).transpose(0, 2, 1, 3)
        a = jnp.pad(a, ((0, 0), (0, 0), (0, SUBLANES - ts), (0, 0)))
        return a.reshape(bs, nh * SUBLANES, HEAD_DIM)

    def new_rows(a):
        a = a.reshape(bs, ts, a.shape[1])
        return jnp.pad(a, ((0, 0), (0, PAGE_SIZE - ts), (0, 0)))

    def attend_sample(q, qi, kw, k32, kb, v32, vb):
        qim = head_query_rows(qi, n_idx)
        wcol = kw[:, HEAD_DIM:HEAD_DIM + n_idx].reshape(bs, ts, n_idx).transpose(0, 2, 1)
        wcol = jnp.pad(wcol, ((0, 0), (0, 0), (0, SUBLANES - ts))).reshape(bs, n_idx * SUBLANES, 1)
        scores = _idx_sample(page_table, qim, wcol, cik, new_rows(kw[:, :HEAD_DIM]), t_new=ts)
        bias = _select_sample(scores, k_top=k_top_s)
        qh = head_query_rows(q, n_heads)
        eye = jnp.eye(n_heads, dtype=BF16)
        qbd = (qh.reshape(bs, n_heads, SUBLANES, 1, HEAD_DIM) * eye[None, :, None, :, None]).reshape(
            bs, n_heads * SUBLANES, att_w)
        out = _attn_sample(page_table, qbd, bias, ck, cv, new_rows(k32), new_rows(v32))
        return out[:, :ts]

    shift_s = state_shift[l]
    o1, o2 = o + lora[0], o + lora[0] + lora[1]
    shift_pad = jnp.concatenate([shift_s[:, :o], _pad_cols(shift_s[:, o:o1], 128), _pad_cols(shift_s[:, o1:o2], 128),
                                 _pad_cols(shift_s[:, o2:], 256)], axis=1)
    y_s, st_s = _layer(x_sample, mod_s, past + jnp.arange(ts), shift_pad, state_wkv[l], wts, attend_sample,
                       tm=bs * ts, chunk=SUBLANES)

    ex = lambda a: a[None]
    return (y_p, y_s, ex(st_p[0]), ex(st_p[1]), ex(st_p[2]), ex(st_p[3]), ex(st_p[4]),
            ex(st_s[0]), ex(st_s[1]), ex(st_s[2]), ex(st_s[3]), ex(st_s[4]))
```

```python
import functools

import numpy as np
import jax
import jax.numpy as jnp
from jax import lax
from jax.experimental import pallas as pl
from jax.experimental.pallas import tpu as pltpu

F32 = jnp.float32
BF16 = jnp.bfloat16
I32 = jnp.int32
HI = lax.Precision.HIGHEST

HEAD_DIM = 64
PAGE_SIZE = 128
TOPK_MAX = 256
ROPE_THETA = 10000.0
TOP_K = 4
SWIGLU_LIMIT = 7.0
SWIGLU_ALPHA = 1.702
NORM_EPS = 1e-5
LNX_EPS = 64e-5

LANES = 128
SUBLANES = 8
VMEM_LIMIT = 56 * 1024 * 1024

NEG_BIG = -1e30
INT_MIN = -(2 ** 31)
KEY_NEG_INF = INT_MIN + 0x7FFFFF


def _dot(a, b, prec=None):
    return lax.dot_general(a, b, (((1,), (0,)), ((), ())), precision=prec, preferred_element_type=F32)


def _dot_nt(a, b, prec=None):
    return lax.dot_general(a, b, (((1,), (1,)), ((), ())), precision=prec, preferred_element_type=F32)


def _dot_tn(a, b, prec=None):
    return lax.dot_general(a, b, (((0,), (0,)), ((), ())), precision=prec, preferred_element_type=F32)


_NN = ((1,), (0,))
_NT = ((1,), (1,))
_TN = ((0,), (0,))


def _split(a, terms=2):
    out = []
    for i in range(terms):
        t = a.astype(BF16)
        out.append(t)
        if i + 1 < terms:
            a = a - t.astype(F32)
    return out


def _mm(a_terms, b_terms, dims=_NN, order=1):
    acc = None
    for i, a in enumerate(a_terms):
        for j, b in enumerate(b_terms):
            if i + j <= order:
                d = lax.dot_general(a, b, (dims, ((), ())), preferred_element_type=F32)
                acc = d if acc is None else acc + d
    return acc


def _params(*sem):
    return pltpu.CompilerParams(dimension_semantics=sem, vmem_limit_bytes=VMEM_LIMIT)


def _sigmoid(x):
    return jax.nn.sigmoid(x)


def _rms(x, g):
    return x * lax.rsqrt(jnp.mean(x * x, axis=-1, keepdims=True) + NORM_EPS) * g


def _mod_kernel(c_ref, w_ref, b_ref, o_ref):
    c = c_ref[...]
    o_ref[...] = _dot(c * _sigmoid(c), w_ref[...], HI) + b_ref[...]


def _modulation(c_pad, w_mod, b_mod):
    rows, d = c_pad.shape
    n = w_mod.shape[1]
    return pl.pallas_call(
        _mod_kernel,
        grid=(n // d,),
        in_specs=[pl.BlockSpec((rows, d), lambda j: (0, 0)),
                  pl.BlockSpec((d, d), lambda j: (0, j)),
                  pl.BlockSpec((1, d), lambda j: (0, j))],
        out_specs=pl.BlockSpec((rows, d), lambda j: (0, j)),
        out_shape=jax.ShapeDtypeStruct((rows, n), F32),
        compiler_params=_params("arbitrary"),
        name="modulation",
    )(c_pad, w_mod, b_mod.reshape(1, n))


def _rope_block(x, cos, sin_signed):
    lane = lax.broadcasted_iota(I32, x.shape, 1)
    partner = jnp.where((lane & 32) == 0, pltpu.roll(x, LANES - 32, axis=1), pltpu.roll(x, 32, axis=1))
    return x * cos + partner * sin_signed


def _inproj_kernel(x_ref, mod_ref, g_ref, w_ref, cs_ref, p_ref, q_ref, k_ref, kb_ref, v_ref, vb_ref, qi_ref,
                   kw_ref, *, n_rw, att_w):
    x = x_ref[...]
    h = _rms(x, g_ref[...]) * (1 + mod_ref[1, 0]) + mod_ref[0, 0]
    hb = h.astype(BF16)
    cb = 512
    for j in range(n_rw // cb):
        p_ref[:, j * cb:(j + 1) * cb] = _dot(hb, w_ref[:, j * cb:(j + 1) * cb])
    cos, sin = cs_ref[0], cs_ref[1]
    base = n_rw
    ng = att_w // LANES

    def roped(off):
        t = _dot(hb, w_ref[:, off:off + att_w])
        return [_rope_block(t[:, g * LANES:(g + 1) * LANES], cos, sin) for g in range(ng)]

    for g, blk in enumerate(roped(base)):
        q_ref[:, g * LANES:(g + 1) * LANES] = (blk * (HEAD_DIM ** -0.5)).astype(BF16)
    for g, blk in enumerate(roped(base + att_w)):
        k_ref[:, g * LANES:(g + 1) * LANES] = blk
        kb_ref[:, g * LANES:(g + 1) * LANES] = blk.astype(BF16)
    v = _dot(hb, w_ref[:, base + 2 * att_w:base + 3 * att_w])
    v_ref[...] = v
    vb_ref[...] = v.astype(BF16)
    for g, blk in enumerate(roped(base + 3 * att_w)):
        qi_ref[:, g * LANES:(g + 1) * LANES] = blk.astype(BF16)
    kw = _dot(hb, w_ref[:, base + 4 * att_w:base + 4 * att_w + LANES])
    kw_ref[...] = _rope_block(kw, cs_ref[2], cs_ref[3])


def _inproj(x2d, mod6, g1, w_cat, cs, *, tm, tiles_per_seq, n_rw, att_w):
    n, d = x2d.shape
    rmod = mod6.shape[2]
    n_cs_tiles = cs.shape[1] // tm
    grid = (n // tm,)
    row = lambda i: (i, 0)
    outs = [
        jax.ShapeDtypeStruct((n, n_rw), F32),
        jax.ShapeDtypeStruct((n, att_w), BF16),
        jax.ShapeDtypeStruct((n, att_w), F32),
        jax.ShapeDtypeStruct((n, att_w), BF16),
        jax.ShapeDtypeStruct((n, att_w), F32),
        jax.ShapeDtypeStruct((n, att_w), BF16),
        jax.ShapeDtypeStruct((n, att_w), BF16),
        jax.ShapeDtypeStruct((n, LANES), F32),
    ]
    return pl.pallas_call(
        functools.partial(_inproj_kernel, n_rw=n_rw, att_w=att_w),
        grid=grid,
        in_specs=[pl.BlockSpec((tm, d), row),
                  pl.BlockSpec((6, 1, rmod, d), lambda i: (0, i // tiles_per_seq, 0, 0)),
                  pl.BlockSpec((1, d), lambda i: (0, 0)),
                  pl.BlockSpec(w_cat.shape, lambda i: (0, 0)),
                  pl.BlockSpec((4, tm, LANES), lambda i: (0, i % n_cs_tiles, 0))],
        out_specs=[pl.BlockSpec((tm, o.shape[1]), row) for o in outs],
        out_shape=outs,
        compiler_params=_params("arbitrary"),
        name="inproj",
    )(x2d, mod6, g1, w_cat, cs)


def _softplus(x):
    return jnp.maximum(x, 0.0) + jnp.log(1.0 + jnp.exp(-jnp.abs(x)))


def _rwkv_kernel(p_ref, sh0_ref, z0_ref, mu_ref, vec_ref, w2_ref, a2_ref, g2_ref, seg_ref, y_ref, zout_ref,
                 carry_ref, z_ref, ops_ref, yh_ref, *, L, t_valid, rw, n_heads):
    c = pl.program_id(1)

    @pl.when(c == 0)
    def _():
        carry_ref[...] = sh0_ref[0]
        z_ref[...] = z0_ref[0]

    p = p_ref[0]
    row = lax.broadcasted_iota(I32, (L, 1), 0)
    prev = jnp.where(row == 0, carry_ref[...], pltpu.roll(p, 1, axis=0))
    carry_ref[...] = p[L - 1:L, :]
    z = p + (prev - p) * mu_ref[...]
    r, k, v = z[:, 0:rw], z[:, rw:2 * rw], z[:, 2 * rw:3 * rw]
    o = 3 * rw
    zw, za, zg = z[:, o:o + 128], z[:, o + 128:o + 256], z[:, o + 256:o + 512]
    w0, a0, k_k, k_a = vec_ref[0:1], vec_ref[1:2], vec_ref[2:3], vec_ref[3:4]
    lnx_g, lnx_b, r_k = vec_ref[4:5], vec_ref[5:6], vec_ref[6:7]
    seg = [seg_ref[...]]

    def seg_sum(t):
        return _mm(_split(t, 3), seg, order=2)

    def lora(t, w_ref):
        return _mm(_split(t), _split(w_ref[...]))

    w_log = -_softplus(-(w0 + lora(jnp.tanh(zw), w2_ref))) - 0.5
    lw = -jnp.exp(w_log)
    a = _sigmoid(a0 + lora(za, a2_ref))
    g = lora(_sigmoid(zg), g2_ref)
    kk = k * k_k
    kkn = kk / jnp.maximum(jnp.sqrt(seg_sum(kk * kk)), 1e-12)
    k2 = k * (1 + (a - 1) * k_a)
    an, bn = -kkn, kkn * a
    if t_valid < L:
        valid = row < t_valid
        lw = jnp.where(valid, lw, 0.0)
        an, bn = jnp.where(valid, an, 0.0), jnp.where(valid, bn, 0.0)
        k2s, vs = jnp.where(valid, k2, 0.0), jnp.where(valid, v, 0.0)
    else:
        k2s, vs = k2, v

    ri = lax.broadcasted_iota(I32, (L, L), 0)
    ci = lax.broadcasted_iota(I32, (L, L), 1)
    incl, strict = ri >= ci, ri > ci
    lw_terms = _split(lw, 3)
    cum = _mm([incl.astype(BF16)], lw_terms, order=2)
    c_last = cum[L - 1:L, :]
    e_c, e_cp, e_n, e_d = jnp.exp(cum), jnp.exp(cum - lw), jnp.exp(-cum), jnp.exp(c_last - cum)
    ops_ref[0] = an * e_cp
    ops_ref[1] = r * e_c
    ops_ref[2] = bn * e_n
    ops_ref[3] = k2s * e_n
    ops_ref[4] = bn * e_d
    ops_ref[5] = k2s * e_d
    ops_ref[6] = vs
    wl_col = jnp.exp(_mm(lw_terms, [jnp.ones((L, LANES), BF16)], _TN, order=2))
    eye = (ri == ci).astype(F32)
    n_sq = max(int(np.ceil(np.log2(L))) - 1, 0)

    for h in range(n_heads):
        sl = slice(h * HEAD_DIM, (h + 1) * HEAD_DIM)
        at, rt, bt, kt, bh, kh, vh = [_split(ops_ref[i, :, sl]) for i in range(7)]
        m_ab = jnp.where(strict, _mm(at, bt, _NT), 0.0)
        m_ak = jnp.where(strict, _mm(at, kt, _NT), 0.0)
        m_rb = jnp.where(incl, _mm(rt[:1], bt[:1], _NT), 0.0)
        m_rk = jnp.where(incl, _mm(rt[:1], kt[:1], _NT), 0.0)
        t_inv = eye + m_ab
        pw = m_ab
        for _ in range(n_sq):
            pw_t = _split(pw)
            pw = _mm(pw_t, pw_t)
            t_inv = t_inv + _mm(_split(t_inv), _split(pw))
        zh = z_ref[h]
        zh_t = _split(zh)
        u = _mm(_split(t_inv), _split(_mm(at, zh_t) + _mm(_split(m_ak), vh)))
        u_t = _split(u)
        yh_ref[:, sl] = (_mm(rt[:1], zh_t[:1]) + _mm([m_rb.astype(BF16)], u_t[:1])
                         + _mm([m_rk.astype(BF16)], vh[:1]))
        z_ref[h] = wl_col[sl, 0:HEAD_DIM] * zh + _mm(bh, u_t, _TN) + _mm(kh, vh, _TN)

    y = yh_ref[...]
    inv_n = 1.0 / HEAD_DIM
    mean = seg_sum(y) * inv_n
    yc = y - mean
    var = seg_sum(yc * yc) * inv_n
    yn = yc * lax.rsqrt(var + LNX_EPS) * lnx_g + lnx_b
    bonus = seg_sum(r * k2 * r_k) * v
    y_ref[0] = ((yn + bonus) * g).astype(y_ref.dtype)

    @pl.when(c == pl.num_programs(1) - 1)
    def _():
        zout_ref[0] = z_ref[...]


def _rwkv(p3, shift0, z0, mu, vec, w2, a2, g2, seg, *, L, t_valid, rw):
    b, t, n_rw = p3.shape
    n_heads = rw // HEAD_DIM
    const2 = lambda i, c: (0, 0)
    return pl.pallas_call(
        functools.partial(_rwkv_kernel, L=L, t_valid=t_valid, rw=rw, n_heads=n_heads),
        grid=(b, t // L),
        in_specs=[pl.BlockSpec((1, L, n_rw), lambda i, c: (i, c, 0)),
                  pl.BlockSpec((1, 1, n_rw), lambda i, c: (i, 0, 0)),
                  pl.BlockSpec((1, n_heads, HEAD_DIM, HEAD_DIM), lambda i, c: (i, 0, 0, 0)),
                  pl.BlockSpec((1, n_rw), const2),
                  pl.BlockSpec(vec.shape, const2),
                  pl.BlockSpec(w2.shape, const2),
                  pl.BlockSpec(a2.shape, const2),
                  pl.BlockSpec(g2.shape, const2),
                  pl.BlockSpec(seg.shape, const2)],
        out_specs=[pl.BlockSpec((1, L, rw), lambda i, c: (i, c, 0)),
                   pl.BlockSpec((1, n_heads, HEAD_DIM, HEAD_DIM), lambda i, c: (i, 0, 0, 0))],
        out_shape=[jax.ShapeDtypeStruct((b, t, rw), BF16),
                   jax.ShapeDtypeStruct((b, n_heads, HEAD_DIM, HEAD_DIM), F32)],
        scratch_shapes=[pltpu.VMEM((1, n_rw), F32),
                        pltpu.VMEM((n_heads, HEAD_DIM, HEAD_DIM), F32),
                        pltpu.VMEM((7, L, rw), F32),
                        pltpu.VMEM((L, rw), F32)],
        compiler_params=_params("arbitrary", "arbitrary"),
        name="rwkv_scan",
    )(p3, shift0, z0, mu, vec, w2, a2, g2, seg)


def _key_to_float(key):
    bits = key ^ ((key >> 31) & 0x7FFFFFFF)
    return lax.bitcast_convert_type(bits, F32)


def _fold_lanes(m):
    acc = m[:, 0:LANES]
    for j in range(1, m.shape[1] // LANES):
        acc = acc + m[:, j * LANES:(j + 1) * LANES]
    return acc


def _select_bias(sc_ref, pb_ref, nc, rows, tk, k_top, n_keys_total):
    def count(pred):
        def body(c, acc):
            return acc + _fold_lanes(pred(c, sc_ref[c]).astype(F32))
        acc = lax.fori_loop(0, nc, body, jnp.zeros((rows, LANES), F32))
        return jnp.sum(acc, axis=-1, keepdims=True)

    kf = float(k_top)
    cnt0 = count(lambda c, s: s >= 0.0)
    tau0 = jnp.where(cnt0 >= kf, 0, INT_MIN).astype(I32)

    def bit_body(i, tau):
        cand = tau + jnp.left_shift(jnp.int32(1), 30 - i)
        cand_f = _key_to_float(cand)
        cnt = count(lambda c, s: s >= cand_f)
        return jnp.where((cnt >= kf) | (cand <= KEY_NEG_INF), cand, tau)

    tau = lax.fori_loop(0, 31, bit_body, tau0)
    tau_f = _key_to_float(tau)
    need = kf - count(lambda c, s: s > tau_f)
    excess = count(lambda c, s: s == tau_f) - need

    def kidx(c):
        return c * tk + lax.broadcasted_iota(I32, (1, tk), 1)

    n_bits = max(int(np.ceil(np.log2(n_keys_total))), 1)
    pb_ref[...] = jnp.full((rows, 1), n_keys_total, I32)

    @pl.when(jnp.max(excess) > 0.0)
    def _():
        def tie_body(i, pb):
            cand = pb + jnp.left_shift(jnp.int32(1), n_bits - 1 - i)
            cnt = count(lambda c, s: (s == tau_f) & (kidx(c) < cand))
            return jnp.where(cnt < need, cand, pb)

        pb_ref[...] = lax.fori_loop(0, n_bits, tie_body, jnp.zeros((rows, 1), I32))

    pb = pb_ref[...]

    def write(c, _):
        s = sc_ref[c]
        sel = ((s > tau_f) | ((s == tau_f) & (kidx(c) <= pb))) & (s > -jnp.inf)
        sc_ref[c] = jnp.where(sel, 0.0, NEG_BIG)
        return 0

    lax.fori_loop(0, nc, write, 0)


def _dsa_prompt_kernel(q_ref, qi_ref, kw_ref, kb_ref, vb_ref, kib_ref, o_ref, sc_ref, pb_ref, m_ref, l_ref, acc_ref,
                       *, tq, tk, k_top, n_heads, n_idx, t_total):
    qb = pl.program_id(1)
    q0 = qb * tq
    nc = (q0 + tq + tk - 1) // tk
    qpos = q0 + lax.broadcasted_iota(I32, (tq, 1), 0)

    def scores(c, _):
        kic = kib_ref[0, pl.ds(pl.multiple_of(c * tk, tk), tk), :]
        acc = jnp.zeros((tq, tk), F32)
        for h in range(n_idx):
            d = _dot_nt(qi_ref[0, :, h * HEAD_DIM:(h + 1) * HEAD_DIM], kic)
            acc = acc + jnp.maximum(d, 0.0) * kw_ref[0, :, HEAD_DIM + h:HEAD_DIM + h + 1]
        spos = c * tk + lax.broadcasted_iota(I32, (1, tk), 1)
        sc_ref[c] = jnp.where(spos <= qpos, acc, -jnp.inf)
        return 0

    lax.fori_loop(0, nc, scores, 0)
    _select_bias(sc_ref, pb_ref, nc, tq, tk, k_top, t_total)

    m_ref[...] = jnp.full(m_ref.shape, NEG_BIG, F32)
    l_ref[...] = jnp.zeros(l_ref.shape, F32)
    acc_ref[...] = jnp.zeros(acc_ref.shape, F32)

    def attend(c, _):
        rows = pl.ds(pl.multiple_of(c * tk, tk), tk)
        bias = sc_ref[c]
        for h in range(n_heads):
            sl = slice(h * HEAD_DIM, (h + 1) * HEAD_DIM)
            s = _dot_nt(q_ref[0, :, sl], kb_ref[0, rows, sl]) + bias
            m = m_ref[h]
            m_new = jnp.maximum(m, jnp.max(s, axis=-1, keepdims=True))
            alpha = jnp.exp(m - m_new)
            p = jnp.exp(s - m_new)
            l_ref[h] = alpha * l_ref[h] + jnp.sum(p, axis=-1, keepdims=True)
            acc_ref[h] = alpha * acc_ref[h] + _dot(p.astype(BF16), vb_ref[0, rows, sl])
            m_ref[h] = m_new
        return 0

    lax.fori_loop(0, nc, attend, 0)
    for h in range(n_heads):
        o_ref[0, :, h * HEAD_DIM:(h + 1) * HEAD_DIM] = (acc_ref[h] / l_ref[h]).astype(o_ref.dtype)


def _dsa_prompt(q, qi, kw, kb, vb, kib, *, tq, tk, k_top):
    b, t, att_w = q.shape
    n_heads = att_w // HEAD_DIM
    n_idx = qi.shape[2] // HEAD_DIM
    qblk = lambda i, j: (i, j, 0)
    full = lambda i, j: (i, 0, 0)
    return pl.pallas_call(
        functools.partial(_dsa_prompt_kernel, tq=tq, tk=tk, k_top=k_top, n_heads=n_heads, n_idx=n_idx,
                          t_total=t),
        grid=(b, t // tq),
        in_specs=[pl.BlockSpec((1, tq, att_w), qblk),
                  pl.BlockSpec((1, tq, qi.shape[2]), qblk),
                  pl.BlockSpec((1, tq, LANES), qblk),
                  pl.BlockSpec((1, t, att_w), full),
                  pl.BlockSpec((1, t, att_w), full),
                  pl.BlockSpec((1, t, HEAD_DIM), full)],
        out_specs=pl.BlockSpec((1, tq, att_w), qblk),
        out_shape=jax.ShapeDtypeStruct((b, t, att_w), BF16),
        scratch_shapes=[pltpu.VMEM((t // tk, tq, tk), F32),
                        pltpu.VMEM((tq, 1), I32),
                        pltpu.VMEM((n_heads, tq, 1), F32),
                        pltpu.VMEM((n_heads, tq, 1), F32),
                        pltpu.VMEM((n_heads, tq, HEAD_DIM), F32)],
        compiler_params=_params("arbitrary", "arbitrary"),
        name="dsa_prompt",
    )(q, qi, kw, kb, vb, kib)


def _idx_page_scores(qim, wcol, ki, n_idx):
    d = jnp.maximum(_dot_nt(qim, ki.astype(BF16)), 0.0) * wcol
    sc = d[0:SUBLANES]
    for h in range(1, n_idx):
        sc = sc + d[h * SUBLANES:(h + 1) * SUBLANES]
    return sc


def _idx_sample_kernel(pt_ref, qim_ref, wcol_ref, *refs, t_new, n_idx, g_pages):
    page_refs, o_ref = refs[:g_pages], refs[g_pages]
    qrow = lax.broadcasted_iota(I32, (SUBLANES, PAGE_SIZE), 0)
    for j in range(g_pages):
        sc = _idx_page_scores(qim_ref[0], wcol_ref[0], page_refs[j][0], n_idx)
        o_ref[j] = jnp.where(qrow < t_new, sc, -jnp.inf)


def _idx_sample(page_table, qim, wcol, cache_idx, *, t_new, g_pages):
    b, n_pages = page_table.shape
    n_idx = qim.shape[1] // SUBLANES
    per_seq = lambda i, g, pt: (i, 0, 0)
    page_specs = [pl.BlockSpec((1, PAGE_SIZE, HEAD_DIM), lambda i, g, pt, j=j: (pt[i, g * g_pages + j], 0, 0))
                  for j in range(g_pages)]
    grid_spec = pltpu.PrefetchScalarGridSpec(
        num_scalar_prefetch=1,
        grid=(b, n_pages // g_pages),
        in_specs=[pl.BlockSpec((1,) + qim.shape[1:], per_seq),
                  pl.BlockSpec((1,) + wcol.shape[1:], per_seq)] + page_specs,
        out_specs=pl.BlockSpec((g_pages, SUBLANES, PAGE_SIZE), lambda i, g, pt: (g, i, 0)),
    )
    return pl.pallas_call(
        functools.partial(_idx_sample_kernel, t_new=t_new, n_idx=n_idx, g_pages=g_pages),
        grid_spec=grid_spec,
        out_shape=jax.ShapeDtypeStruct((n_pages, b * SUBLANES, PAGE_SIZE), F32),
        compiler_params=_params("arbitrary", "arbitrary"),
        name="idx_sample",
    )(page_table, qim, wcol, *([cache_idx] * g_pages))


def _idx_new_kernel(qim_ref, wcol_ref, new_ref, o_ref, *, t_new, n_idx):
    sc = _idx_page_scores(qim_ref[0], wcol_ref[0], new_ref[0], n_idx)
    qrow = lax.broadcasted_iota(I32, (SUBLANES, PAGE_SIZE), 0)
    lane = lax.broadcasted_iota(I32, (SUBLANES, PAGE_SIZE), 1)
    o_ref[...] = jnp.where((qrow < t_new) & (lane <= qrow), sc, -jnp.inf)


def _idx_new(qim, wcol, ki_new, *, t_new):
    b = qim.shape[0]
    n_idx = qim.shape[1] // SUBLANES
    per_seq = lambda i: (i, 0, 0)
    return pl.pallas_call(
        functools.partial(_idx_new_kernel, t_new=t_new, n_idx=n_idx),
        grid=(b,),
        in_specs=[pl.BlockSpec((1,) + qim.shape[1:], per_seq),
                  pl.BlockSpec((1,) + wcol.shape[1:], per_seq),
                  pl.BlockSpec((1, PAGE_SIZE, HEAD_DIM), per_seq)],
        out_specs=pl.BlockSpec((SUBLANES, PAGE_SIZE), lambda i: (i, 0)),
        out_shape=jax.ShapeDtypeStruct((b * SUBLANES, PAGE_SIZE), F32),
        compiler_params=_params("arbitrary"),
        name="idx_new",
    )(qim, wcol, ki_new)


def _select_sample_kernel(past_ref, new_ref, o_ref, pb_ref, *, k_top):
    n_pages, rows = past_ref.shape[0], past_ref.shape[1]
    o_ref[0:n_pages] = past_ref[...]
    o_ref[n_pages] = new_ref[...]
    _select_bias(o_ref, pb_ref, n_pages + 1, rows, PAGE_SIZE, k_top, (n_pages + 1) * PAGE_SIZE)


def _select_sample(past_scores, new_scores, *, k_top):
    n_pages, rows, _ = past_scores.shape
    out_shape = (n_pages + 1, rows, PAGE_SIZE)
    return pl.pallas_call(
        functools.partial(_select_sample_kernel, k_top=k_top),
        grid=(1,),
        in_specs=[pl.BlockSpec(past_scores.shape, lambda i: (0, 0, 0)),
                  pl.BlockSpec(new_scores.shape, lambda i: (0, 0))],
        out_specs=pl.BlockSpec(out_shape, lambda i: (0, 0, 0)),
        out_shape=jax.ShapeDtypeStruct(out_shape, F32),
        scratch_shapes=[pltpu.VMEM((rows, 1), I32)],
        compiler_params=_params("arbitrary"),
        name="select_sample",
    )(past_scores, new_scores)


def _attn_sample_kernel(pt_ref, qbd_ref, bias_ref, biasn_ref, knew_ref, vnew_ref, *refs, n_heads, g_pages):
    k_refs, v_refs = refs[:g_pages], refs[g_pages:2 * g_pages]
    o_ref, m_ref, l_ref, acc_ref, kc_ref, vc_ref = refs[2 * g_pages:]
    g = pl.program_id(1)

    @pl.when(g == 0)
    def _():
        m_ref[...] = jnp.full(m_ref.shape, NEG_BIG, F32)
        l_ref[...] = jnp.zeros(l_ref.shape, F32)
        acc_ref[...] = jnp.zeros(acc_ref.shape, F32)

    def step(kp, vp, bias):
        s = _dot_nt(qbd_ref[0], kp.astype(BF16)) + jnp.concatenate([bias] * n_heads, axis=0)
        m = m_ref[...]
        m_new = jnp.maximum(m, jnp.max(s, axis=-1, keepdims=True))
        alpha = jnp.exp(m - m_new)
        pr = jnp.exp(s - m_new)
        l_ref[...] = alpha * l_ref[...] + jnp.sum(pr, axis=-1, keepdims=True)
        acc_ref[...] = alpha * acc_ref[...] + _dot(pr.astype(BF16), vp.astype(BF16))
        m_ref[...] = m_new

    for j in range(g_pages):
        for h in range(n_heads):
            sl = slice(h * HEAD_DIM, (h + 1) * HEAD_DIM)
            kc_ref[:, sl] = k_refs[j][0, pl.ds(h, PAGE_SIZE, stride=n_heads), :]
            vc_ref[:, sl] = v_refs[j][0, pl.ds(h, PAGE_SIZE, stride=n_heads), :]
        step(kc_ref[...], vc_ref[...], bias_ref[j])

    @pl.when(g == pl.num_programs(1) - 1)
    def _():
        step(knew_ref[0], vnew_ref[0], biasn_ref[0])
        full = acc_ref[...] / l_ref[...]
        col_head = lax.broadcasted_iota(I32, (SUBLANES, full.shape[1]), 1) // HEAD_DIM
        out = jnp.zeros((SUBLANES, full.shape[1]), F32)
        for h in range(n_heads):
            out = out + jnp.where(col_head == h, full[h * SUBLANES:(h + 1) * SUBLANES], 0.0)
        o_ref[0] = out.astype(o_ref.dtype)


def _attn_sample(page_table, qbd, bias, cache_k, cache_v, k_new, v_new, *, g_pages):
    b, n_pages = page_table.shape
    att_w = qbd.shape[2]
    n_heads = att_w // HEAD_DIM
    per_seq = lambda i, g, pt: (i, 0, 0)
    page_specs = [pl.BlockSpec((1, PAGE_SIZE * n_heads, HEAD_DIM),
                               lambda i, g, pt, j=j: (pt[i, g * g_pages + j], 0, 0)) for j in range(g_pages)]
    grid_spec = pltpu.PrefetchScalarGridSpec(
        num_scalar_prefetch=1,
        grid=(b, n_pages // g_pages),
        in_specs=[pl.BlockSpec((1,) + qbd.shape[1:], per_seq),
                  pl.BlockSpec((g_pages, SUBLANES, PAGE_SIZE), lambda i, g, pt: (g, i, 0)),
                  pl.BlockSpec((1, SUBLANES, PAGE_SIZE), lambda i, g, pt: (n_pages, i, 0)),
                  pl.BlockSpec((1, PAGE_SIZE, att_w), per_seq),
                  pl.BlockSpec((1, PAGE_SIZE, att_w), per_seq)] + page_specs + page_specs,
        out_specs=pl.BlockSpec((1, SUBLANES, att_w), per_seq),
        scratch_shapes=[pltpu.VMEM((n_heads * SUBLANES, 1), F32),
                        pltpu.VMEM((n_heads * SUBLANES, 1), F32),
                        pltpu.VMEM((n_heads * SUBLANES, att_w), F32),
                        pltpu.VMEM((PAGE_SIZE, att_w), F32),
                        pltpu.VMEM((PAGE_SIZE, att_w), F32)],
    )
    return pl.pallas_call(
        functools.partial(_attn_sample_kernel, n_heads=n_heads, g_pages=g_pages),
        grid_spec=grid_spec,
        out_shape=jax.ShapeDtypeStruct((b, SUBLANES, att_w), BF16),
        compiler_params=_params("arbitrary", "arbitrary"),
        name="attn_sample",
    )(page_table, qbd, bias, bias, k_new, v_new, *([cache_k] * g_pages), *([cache_v] * g_pages))


def _outproj_kernel(x_ref, yr_ref, ya_ref, mod_ref, wo_ref, g_ref, rw_ref, rb_ref, x1_ref, h2_ref, gates_ref, *, rw):
    mixed = _dot(yr_ref[...], wo_ref[0:rw, :]) + _dot(ya_ref[...], wo_ref[rw:, :])
    x1 = x_ref[...] + mod_ref[2, 0] * mixed
    x1_ref[...] = x1
    h2 = _rms(x1, g_ref[...]) * (1 + mod_ref[4, 0]) + mod_ref[3, 0]
    h2_ref[...] = h2.astype(BF16)
    logits = _dot(h2, rw_ref[...], HI) + rb_ref[...]
    lane = lax.broadcasted_iota(I32, logits.shape, 1).astype(F32)
    work = logits
    picked = jnp.zeros(logits.shape, jnp.bool_)
    top = None
    for i in range(TOP_K):
        mx = jnp.max(work, axis=-1, keepdims=True)
        first = jnp.min(jnp.where(work == mx, lane, float(LANES)), axis=-1, keepdims=True)
        hit = lane == first
        picked = picked | hit
        work = jnp.where(hit, -jnp.inf, work)
        if i == 0:
            top = mx
    e = jnp.where(picked, jnp.exp(logits - top), 0.0)
    gates_ref[...] = e / jnp.sum(e, axis=-1, keepdims=True)


def _outproj(x2d, yr, ya, mod6, w_out_b, g2n, router_w_pad, router_b_pad, *, tm, tiles_per_seq, rw):
    n, d = x2d.shape
    rmod = mod6.shape[2]
    row = lambda i: (i, 0)
    const = lambda i: (0, 0)
    outs = [jax.ShapeDtypeStruct((n, d), F32), jax.ShapeDtypeStruct((n, d), BF16),
            jax.ShapeDtypeStruct((n, LANES), F32)]
    return pl.pallas_call(
        functools.partial(_outproj_kernel, rw=rw),
        grid=(n // tm,),
        in_specs=[pl.BlockSpec((tm, d), row),
                  pl.BlockSpec((tm, yr.shape[1]), row),
                  pl.BlockSpec((tm, ya.shape[1]), row),
                  pl.BlockSpec((6, 1, rmod, d), lambda i: (0, i // tiles_per_seq, 0, 0)),
                  pl.BlockSpec(w_out_b.shape, const),
                  pl.BlockSpec((1, d), const),
                  pl.BlockSpec(router_w_pad.shape, const),
                  pl.BlockSpec((1, LANES), const)],
        out_specs=[pl.BlockSpec((tm, o.shape[1]), row) for o in outs],
        out_shape=outs,
        compiler_params=_params("arbitrary"),
        name="outproj_router",
    )(x2d, yr, ya, mod6, w_out_b, g2n, router_w_pad, router_b_pad)


def _moe_kernel(h_ref, gates_ref, x1_ref, mod_ref, wgu_ref, bgu_ref, wd_ref, bd_ref, fg_ref, o_ref, acc_ref, *,
                d_ff):
    e = pl.program_id(1)

    @pl.when(e == 0)
    def _():
        acc_ref[...] = jnp.zeros(acc_ref.shape, F32)

    gu = _dot(h_ref[...], wgu_ref[0]) + bgu_ref[0]
    gate = jnp.minimum(gu[:, :d_ff], SWIGLU_LIMIT)
    up = jnp.clip(gu[:, d_ff:], -SWIGLU_LIMIT, SWIGLU_LIMIT)
    act = (up + 1) * gate * _sigmoid(SWIGLU_ALPHA * gate)
    y = _dot(act.astype(BF16), wd_ref[0]) + bd_ref[0]
    gates = gates_ref[...]
    lane = lax.broadcasted_iota(I32, gates.shape, 1)
    ge = jnp.sum(jnp.where(lane == e, gates, 0.0), axis=-1, keepdims=True)
    acc_ref[...] += ge * y

    @pl.when(e == pl.num_programs(1) - 1)
    def _():
        x2 = x1_ref[...] + mod_ref[5, 0] * acc_ref[...]
        o_ref[...] = _rms(x2, fg_ref[...])


def _moe(h2, gates, x1, mod6, wgu_b, bgu, wd_b, bd, fg, *, tm, tiles_per_seq):
    n, d = x1.shape
    n_exp, _, two_ff = wgu_b.shape
    rmod = mod6.shape[2]
    row = lambda i, e: (i, 0)
    return pl.pallas_call(
        functools.partial(_moe_kernel, d_ff=two_ff // 2),
        grid=(n // tm, n_exp),
        in_specs=[pl.BlockSpec((tm, d), row),
                  pl.BlockSpec((tm, LANES), row),
                  pl.BlockSpec((tm, d), row),
                  pl.BlockSpec((6, 1, rmod, d), lambda i, e: (0, i // tiles_per_seq, 0, 0)),
                  pl.BlockSpec((1, d, two_ff), lambda i, e: (e, 0, 0)),
                  pl.BlockSpec((1, 1, two_ff), lambda i, e: (e, 0, 0)),
                  pl.BlockSpec((1, two_ff // 2, d), lambda i, e: (e, 0, 0)),
                  pl.BlockSpec((1, 1, d), lambda i, e: (e, 0, 0)),
                  pl.BlockSpec((1, d), lambda i, e: (0, 0))],
        out_specs=pl.BlockSpec((tm, d), row),
        out_shape=jax.ShapeDtypeStruct((n, d), F32),
        scratch_shapes=[pltpu.VMEM((tm, d), F32)],
        compiler_params=_params("arbitrary", "arbitrary"),
        name="moe_ffn",
    )(h2, gates, x1, mod6, wgu_b, bgu, wd_b, bd, fg)


def _pad_cols(a, width):
    return jnp.pad(a, ((0, 0), (0, width - a.shape[1])))


def _rope_tables(pos):
    half = HEAD_DIM // 2
    inv = ROPE_THETA ** (-jnp.arange(half, dtype=F32) / half)
    ang = pos.astype(F32)[:, None] * inv[None, :]
    cos, sin = jnp.cos(ang), jnp.sin(ang)
    cos_h = jnp.concatenate([cos, cos], axis=1)
    sin_h = jnp.concatenate([-sin, sin], axis=1)
    ones = jnp.ones_like(cos_h)
    zeros = jnp.zeros_like(cos_h)
    tabs = jnp.stack([jnp.concatenate([cos_h, cos_h], 1), jnp.concatenate([sin_h, sin_h], 1),
                      jnp.concatenate([cos_h, ones], 1), jnp.concatenate([sin_h, zeros], 1)])
    return tabs


def _layer(x, mod_rows, pos, shift_prev, wkv_prev, wts, attend, *, tm, chunk):
    b, t, d = x.shape
    n = b * t
    rw, att_w, n_rw, rwkv_in = wts["rw"], wts["att_w"], wts["n_rw"], wts["rwkv_in"]
    per_token_mod = mod_rows.shape[0] != b or tm > t
    if per_token_mod:
        mod6 = jnp.repeat(mod_rows, t, axis=0).reshape(n // tm, tm, 6, d).transpose(2, 0, 1, 3)
        tiles_per_seq = 1
    else:
        mod6 = mod_rows.reshape(b, 1, 6, d).transpose(2, 0, 1, 3)
        tiles_per_seq = t // tm
    cs = _rope_tables(pos)
    n_idx_scale = wts["idx_scale"]
    cs = cs.at[2, :, HEAD_DIM:].set(n_idx_scale)
    if cs.shape[1] < tm:
        cs = jnp.tile(cs, (1, tm // cs.shape[1], 1))
    x2d = x.reshape(n, d)
    p, q, k32, kb, v32, vb, qi, kw = _inproj(x2d, mod6, wts["g1"], wts["w_cat"], cs, tm=tm,
                                             tiles_per_seq=tiles_per_seq, n_rw=n_rw, att_w=att_w)
    t_pad = -(-t // chunk) * chunk
    p3 = p.reshape(b, t, n_rw)
    if t_pad != t:
        p3 = jnp.pad(p3, ((0, 0), (0, t_pad - t), (0, 0)))
    z0 = jnp.swapaxes(wkv_prev, -1, -2)
    yr, z_fin = _rwkv(p3, shift_prev.reshape(b, 1, n_rw), z0, wts["mu"], wts["vec"], wts["w2"], wts["a2"], wts["g2"],
                      wts["seg"], L=chunk, t_valid=t if t_pad != t else chunk, rw=rw)
    yr = yr[:, :t].reshape(n, rw)
    wkv_new = jnp.swapaxes(z_fin, -1, -2)
    ya = attend(q, qi, kw, k32, kb, v32, vb).reshape(n, att_w)
    x1, h2, gates = _outproj(x2d, yr, ya, mod6, wts["w_out"], wts["g2n"], wts["router_w"], wts["router_b"], tm=tm,
                             tiles_per_seq=tiles_per_seq, rw=rw)
    tm_moe = min(wts["tm_moe"], n)
    if per_token_mod:
        mod6m, tps_m = mod6, 1
        assert tm_moe == tm
    else:
        mod6m, tps_m = mod6, t // tm_moe
    y = _moe(h2, gates, x1, mod6m, wts["wgu"], wts["bgu"], wts["wd"], wts["bd"], wts["fg"], tm=tm_moe,
             tiles_per_seq=tps_m)
    p_last = p3[:, t - 1, :]
    new_shift = jnp.concatenate([p_last[:, :3 * rw], p_last[:, 3 * rw:3 * rw + wts["lora"][0]],
                                 p_last[:, 3 * rw + 128:3 * rw + 128 + wts["lora"][1]],
                                 p_last[:, 3 * rw + 256:3 * rw + 256 + wts["lora"][2]]], axis=1)
    n_heads = att_w // HEAD_DIM
    state = (k32.reshape(b, t, n_heads, HEAD_DIM), v32.reshape(b, t, n_heads, HEAD_DIM),
             kw[:, :HEAD_DIM].reshape(b, t, HEAD_DIM), wkv_new, new_shift)
    return y.reshape(b, t, d), state


def kernel(x_prompt, x_sample, c_prompt, c_sample, cache_k, cache_v, cache_idx_k, state_wkv, state_shift, page_table, norm1_g, norm2_g, w_mod, b_mod, w_in, mu_shift, w0, w2, a0, a2, g2, k_k, k_a, r_k, lnx_g, lnx_b, w_out, router_w, router_b, w_gate_up, b_gate_up, w_down, b_down, final_g):
    depth = w_in.shape[0]
    assert depth == 1, "single-layer trunk"
    l = 0
    bp, tp, d = x_prompt.shape
    bs, ts, _ = x_sample.shape
    rw = w0.shape[1]
    att_w = w_out.shape[1] - rw
    lora = (w2.shape[1], a2.shape[1], g2.shape[1])
    rwkv_in = 3 * rw + sum(lora)
    n_idx = (w_in.shape[2] - rwkv_in - 3 * att_w - HEAD_DIM) // (HEAD_DIM + 1)
    n_exp = router_w.shape[2]
    n_pages = page_table.shape[1]
    past = n_pages * PAGE_SIZE
    n_rw = 3 * rw + 512
    assert lora[0] <= 128 and lora[1] <= 128 and lora[2] <= 256 and n_idx * HEAD_DIM == att_w

    wi = w_in[l]
    o = 3 * rw
    rw_cols = [wi[:, :o], _pad_cols(wi[:, o:o + lora[0]], 128), _pad_cols(wi[:, o + lora[0]:o + lora[0] + lora[1]], 128),
               _pad_cols(wi[:, o + lora[0] + lora[1]:rwkv_in], 256)]
    a0_ = rwkv_in
    att_cols = [wi[:, a0_:a0_ + 4 * att_w], _pad_cols(wi[:, a0_ + 4 * att_w:], LANES)]
    w_cat = jnp.concatenate(rw_cols + att_cols, axis=1).astype(BF16)
    mu = mu_shift[l]
    mu_pad = jnp.concatenate([mu[:o], jnp.pad(mu[o:o + lora[0]], (0, 128 - lora[0])),
                              jnp.pad(mu[o + lora[0]:o + lora[0] + lora[1]], (0, 128 - lora[1])),
                              jnp.pad(mu[o + lora[0] + lora[1]:], (0, 256 - lora[2]))]).reshape(1, n_rw)
    vec = jnp.stack([w0[l], a0[l], k_k[l], k_a[l], lnx_g[l], lnx_b[l], r_k[l].reshape(-1),
                     jnp.zeros((rw,), F32)])
    head_of = np.arange(rw) // HEAD_DIM
    seg = jnp.asarray((head_of[:, None] == head_of[None, :]).astype(np.float32)).astype(BF16)
    wts = dict(
        rw=rw, att_w=att_w, n_rw=n_rw, rwkv_in=rwkv_in, lora=lora,
        idx_scale=float((n_idx * HEAD_DIM) ** -0.5),
        g1=norm1_g[l].reshape(1, d), w_cat=w_cat, mu=mu_pad, vec=vec,
        w2=jnp.pad(w2[l], ((0, 128 - lora[0]), (0, 0))), a2=jnp.pad(a2[l], ((0, 128 - lora[1]), (0, 0))),
        g2=jnp.pad(g2[l], ((0, 256 - lora[2]), (0, 0))), seg=seg,
        w_out=w_out[l].astype(BF16), g2n=norm2_g[l].reshape(1, d),
        router_w=_pad_cols(router_w[l], LANES),
        router_b=jnp.concatenate([router_b[l], jnp.full((LANES - n_exp,), NEG_BIG, F32)]).reshape(1, LANES),
        wgu=w_gate_up[l].astype(BF16), bgu=b_gate_up[l].reshape(n_exp, 1, -1),
        wd=w_down[l].astype(BF16), bd=b_down[l].reshape(n_exp, 1, -1), fg=final_g.reshape(1, d),
        tm_moe=512,
    )

    c_all = jnp.concatenate([c_prompt, c_sample], axis=0)
    rows_pad = -(-c_all.shape[0] // SUBLANES) * SUBLANES
    mod_all = _modulation(jnp.pad(c_all, ((0, rows_pad - c_all.shape[0]), (0, 0))), w_mod[l], b_mod[l])
    mod_p, mod_s = mod_all[:bp], mod_all[bp:bp + bs]

    k_top_p = min(TOPK_MAX, tp // 4)

    def attend_prompt(q, qi, kw, k32, kb, v32, vb):
        r3 = lambda a: a.reshape(bp, tp, a.shape[1])
        kib = kw[:, :HEAD_DIM].astype(BF16)
        return _dsa_prompt(r3(q), r3(qi), r3(kw), r3(kb), r3(vb), r3(kib), tq=min(256, tp), tk=min(512, tp), k_top=k_top_p)

    y_p, st_p = _layer(x_prompt, mod_p, jnp.arange(tp), jnp.zeros((bp, n_rw), F32),
                       jnp.zeros((bp, rw // HEAD_DIM, HEAD_DIM, HEAD_DIM), F32), wts, attend_prompt,
                       tm=min(512, tp), chunk=64)

    k_top_s = min(TOPK_MAX, (past + ts) // 4)
    n_heads = att_w // HEAD_DIM
    assert ts <= SUBLANES
    ck = cache_k[l].reshape(cache_k.shape[1], PAGE_SIZE * n_heads, HEAD_DIM)
    cv = cache_v[l].reshape(cache_v.shape[1], PAGE_SIZE * n_heads, HEAD_DIM)
    cik = cache_idx_k[l]
    g_idx = 16 if n_pages % 16 == 0 else 1
    g_att = 8 if n_pages % 8 == 0 else 1

    def head_query_rows(a, nh):
        a = a.reshape(bs, ts, nh, HEAD_DIM).transpose(0, 2, 1, 3)
        a = jnp.pad(a, ((0, 0), (0, 0), (0, SUBLANES - ts), (0, 0)))
        return a.reshape(bs, nh * SUBLANES, HEAD_DIM)

    def new_rows(a):
        a = a.reshape(bs, ts, a.shape[1])
        return jnp.pad(a, ((0, 0), (0, PAGE_SIZE - ts), (0, 0)))

    def attend_sample(q, qi, kw, k32, kb, v32, vb):
        qim = head_query_rows(qi, n_idx)
        wcol = kw[:, HEAD_DIM:HEAD_DIM + n_idx].reshape(bs, ts, n_idx).transpose(0, 2, 1)
        wcol = jnp.pad(wcol, ((0, 0), (0, 0), (0, SUBLANES - ts))).reshape(bs, n_idx * SUBLANES, 1)
        past_scores = _idx_sample(page_table, qim, wcol, cik, t_new=ts, g_pages=g_idx)
        new_scores = _idx_new(qim, wcol, new_rows(kw[:, :HEAD_DIM]), t_new=ts)
        bias = _select_sample(past_scores, new_scores, k_top=k_top_s)
        qh = head_query_rows(q, n_heads)
        eye = jnp.eye(n_heads, dtype=BF16)
        qbd = (qh.reshape(bs, n_heads, SUBLANES, 1, HEAD_DIM) * eye[None, :, None, :, None]).reshape(
            bs, n_heads * SUBLANES, att_w)
        out = _attn_sample(page_table, qbd, bias, ck, cv, new_rows(k32), new_rows(v32), g_pages=g_att)
        return out[:, :ts]

    shift_s = state_shift[l]
    o1, o2 = o + lora[0], o + lora[0] + lora[1]
    shift_pad = jnp.concatenate([shift_s[:, :o], _pad_cols(shift_s[:, o:o1], 128), _pad_cols(shift_s[:, o1:o2], 128),
                                 _pad_cols(shift_s[:, o2:], 256)], axis=1)
    y_s, st_s = _layer(x_sample, mod_s, past + jnp.arange(ts), shift_pad, state_wkv[l], wts, attend_sample,
                       tm=bs * ts, chunk=SUBLANES)

    ex = lambda a: a[None]
    return (y_p, y_s, ex(st_p[0]), ex(st_p[1]), ex(st_p[2]), ex(st_p[3]), ex(st_p[4]),
            ex(st_s[0]), ex(st_s[1]), ex(st_s[2]), ex(st_s[3]), ex(st_s[4]))
```

```python
import functools

import numpy as np
import jax
import jax.numpy as jnp
from jax import lax
from jax.experimental import pallas as pl
from jax.experimental.pallas import tpu as pltpu

F32 = jnp.float32
BF16 = jnp.bfloat16
I32 = jnp.int32
HI = lax.Precision.HIGHEST

HEAD_DIM = 64
PAGE_SIZE = 128
TOPK_MAX = 256
ROPE_THETA = 10000.0
TOP_K = 4
SWIGLU_LIMIT = 7.0
SWIGLU_ALPHA = 1.702
NORM_EPS = 1e-5
LNX_EPS = 64e-5

LANES = 128
SUBLANES = 8
VMEM_LIMIT = 56 * 1024 * 1024

NEG_BIG = -1e30
INT_MIN = -(2 ** 31)
KEY_NEG_INF = INT_MIN + 0x7FFFFF


def _dot(a, b, prec=None):
    return lax.dot_general(a, b, (((1,), (0,)), ((), ())), precision=prec, preferred_element_type=F32)


def _dot_nt(a, b, prec=None):
    return lax.dot_general(a, b, (((1,), (1,)), ((), ())), precision=prec, preferred_element_type=F32)


def _dot_tn(a, b, prec=None):
    return lax.dot_general(a, b, (((0,), (0,)), ((), ())), precision=prec, preferred_element_type=F32)


_NN = ((1,), (0,))
_NT = ((1,), (1,))
_TN = ((0,), (0,))


def _split(a, terms=2):
    out = []
    for i in range(terms):
        t = a.astype(BF16)
        out.append(t)
        if i + 1 < terms:
            a = a - t.astype(F32)
    return out


def _mm(a_terms, b_terms, dims=_NN, order=1):
    acc = None
    for i, a in enumerate(a_terms):
        for j, b in enumerate(b_terms):
            if i + j <= order:
                d = lax.dot_general(a, b, (dims, ((), ())), preferred_element_type=F32)
                acc = d if acc is None else acc + d
    return acc


def _params(*sem):
    return pltpu.CompilerParams(dimension_semantics=sem, vmem_limit_bytes=VMEM_LIMIT)


def _sigmoid(x):
    return jax.nn.sigmoid(x)


def _rms(x, g):
    return x * lax.rsqrt(jnp.mean(x * x, axis=-1, keepdims=True) + NORM_EPS) * g


def _mod_kernel(c_ref, w_ref, b_ref, o_ref):
    c = c_ref[...]
    o_ref[...] = _dot(c * _sigmoid(c), w_ref[...], HI) + b_ref[...]


def _modulation(c_pad, w_mod, b_mod):
    rows, d = c_pad.shape
    n = w_mod.shape[1]
    return pl.pallas_call(
        _mod_kernel,
        grid=(n // d,),
        in_specs=[pl.BlockSpec((rows, d), lambda j: (0, 0)),
                  pl.BlockSpec((d, d), lambda j: (0, j)),
                  pl.BlockSpec((1, d), lambda j: (0, j))],
        out_specs=pl.BlockSpec((rows, d), lambda j: (0, j)),
        out_shape=jax.ShapeDtypeStruct((rows, n), F32),
        compiler_params=_params("arbitrary"),
        name="modulation",
    )(c_pad, w_mod, b_mod.reshape(1, n))


def _rope_block(x, cos, sin_signed):
    lane = lax.broadcasted_iota(I32, x.shape, 1)
    partner = jnp.where((lane & 32) == 0, pltpu.roll(x, LANES - 32, axis=1), pltpu.roll(x, 32, axis=1))
    return x * cos + partner * sin_signed


def _inproj_kernel(x_ref, mod_ref, g_ref, w_ref, cs_ref, p_ref, q_ref, k_ref, kb_ref, v_ref, vb_ref, qi_ref,
                   kw_ref, *, n_rw, att_w):
    x = x_ref[...]
    h = _rms(x, g_ref[...]) * (1 + mod_ref[1, 0]) + mod_ref[0, 0]
    hb = h.astype(BF16)
    cb = 512
    for j in range(n_rw // cb):
        p_ref[:, j * cb:(j + 1) * cb] = _dot(hb, w_ref[:, j * cb:(j + 1) * cb])
    cos, sin = cs_ref[0], cs_ref[1]
    base = n_rw
    ng = att_w // LANES

    def roped(off):
        t = _dot(hb, w_ref[:, off:off + att_w])
        return [_rope_block(t[:, g * LANES:(g + 1) * LANES], cos, sin) for g in range(ng)]

    for g, blk in enumerate(roped(base)):
        q_ref[:, g * LANES:(g + 1) * LANES] = (blk * (HEAD_DIM ** -0.5)).astype(BF16)
    for g, blk in enumerate(roped(base + att_w)):
        k_ref[:, g * LANES:(g + 1) * LANES] = blk
        kb_ref[:, g * LANES:(g + 1) * LANES] = blk.astype(BF16)
    v = _dot(hb, w_ref[:, base + 2 * att_w:base + 3 * att_w])
    v_ref[...] = v
    vb_ref[...] = v.astype(BF16)
    for g, blk in enumerate(roped(base + 3 * att_w)):
        qi_ref[:, g * LANES:(g + 1) * LANES] = blk.astype(BF16)
    kw = _dot(hb, w_ref[:, base + 4 * att_w:base + 4 * att_w + LANES])
    kw_ref[...] = _rope_block(kw, cs_ref[2], cs_ref[3])


def _inproj(x2d, mod6, g1, w_cat, cs, *, tm, tiles_per_seq, n_rw, att_w):
    n, d = x2d.shape
    rmod = mod6.shape[2]
    n_cs_tiles = cs.shape[1] // tm
    grid = (n // tm,)
    row = lambda i: (i, 0)
    outs = [
        jax.ShapeDtypeStruct((n, n_rw), F32),
        jax.ShapeDtypeStruct((n, att_w), BF16),
        jax.ShapeDtypeStruct((n, att_w), F32),
        jax.ShapeDtypeStruct((n, att_w), BF16),
        jax.ShapeDtypeStruct((n, att_w), F32),
        jax.ShapeDtypeStruct((n, att_w), BF16),
        jax.ShapeDtypeStruct((n, att_w), BF16),
        jax.ShapeDtypeStruct((n, LANES), F32),
    ]
    return pl.pallas_call(
        functools.partial(_inproj_kernel, n_rw=n_rw, att_w=att_w),
        grid=grid,
        in_specs=[pl.BlockSpec((tm, d), row),
                  pl.BlockSpec((6, 1, rmod, d), lambda i: (0, i // tiles_per_seq, 0, 0)),
                  pl.BlockSpec((1, d), lambda i: (0, 0)),
                  pl.BlockSpec(w_cat.shape, lambda i: (0, 0)),
                  pl.BlockSpec((4, tm, LANES), lambda i: (0, i % n_cs_tiles, 0))],
        out_specs=[pl.BlockSpec((tm, o.shape[1]), row) for o in outs],
        out_shape=outs,
        compiler_params=_params("arbitrary"),
        name="inproj",
    )(x2d, mod6, g1, w_cat, cs)


def _softplus(x):
    return jnp.maximum(x, 0.0) + jnp.log(1.0 + jnp.exp(-jnp.abs(x)))


def _rwkv_kernel(p_ref, sh0_ref, z0_ref, mu_ref, vec_ref, w2_ref, a2_ref, g2_ref, seg_ref, y_ref, zout_ref,
                 carry_ref, z_ref, ops_ref, yh_ref, *, L, t_valid, rw, n_heads):
    c = pl.program_id(1)

    @pl.when(c == 0)
    def _():
        carry_ref[...] = sh0_ref[0]
        z_ref[...] = z0_ref[0]

    p = p_ref[0]
    row = lax.broadcasted_iota(I32, (L, 1), 0)
    prev = jnp.where(row == 0, carry_ref[...], pltpu.roll(p, 1, axis=0))
    carry_ref[...] = p[L - 1:L, :]
    z = p + (prev - p) * mu_ref[...]
    r, k, v = z[:, 0:rw], z[:, rw:2 * rw], z[:, 2 * rw:3 * rw]
    o = 3 * rw
    zw, za, zg = z[:, o:o + 128], z[:, o + 128:o + 256], z[:, o + 256:o + 512]
    w0, a0, k_k, k_a = vec_ref[0:1], vec_ref[1:2], vec_ref[2:3], vec_ref[3:4]
    lnx_g, lnx_b, r_k = vec_ref[4:5], vec_ref[5:6], vec_ref[6:7]
    seg = [seg_ref[...]]

    def seg_sum(t):
        return _mm(_split(t, 3), seg, order=2)

    def lora(t, w_ref):
        return _mm(_split(t), _split(w_ref[...]))

    w_log = -_softplus(-(w0 + lora(jnp.tanh(zw), w2_ref))) - 0.5
    lw = -jnp.exp(w_log)
    a = _sigmoid(a0 + lora(za, a2_ref))
    g = lora(_sigmoid(zg), g2_ref)
    kk = k * k_k
    kkn = kk / jnp.maximum(jnp.sqrt(seg_sum(kk * kk)), 1e-12)
    k2 = k * (1 + (a - 1) * k_a)
    an, bn = -kkn, kkn * a
    if t_valid < L:
        valid = row < t_valid
        lw = jnp.where(valid, lw, 0.0)
        an, bn = jnp.where(valid, an, 0.0), jnp.where(valid, bn, 0.0)
        k2s, vs = jnp.where(valid, k2, 0.0), jnp.where(valid, v, 0.0)
    else:
        k2s, vs = k2, v

    ri = lax.broadcasted_iota(I32, (L, L), 0)
    ci = lax.broadcasted_iota(I32, (L, L), 1)
    incl, strict = ri >= ci, ri > ci
    lw_terms = _split(lw, 3)
    cum = _mm([incl.astype(BF16)], lw_terms, order=2)
    c_last = cum[L - 1:L, :]
    e_c, e_cp, e_n, e_d = jnp.exp(cum), jnp.exp(cum - lw), jnp.exp(-cum), jnp.exp(c_last - cum)
    ops_ref[0] = an * e_cp
    ops_ref[1] = r * e_c
    ops_ref[2] = bn * e_n
    ops_ref[3] = k2s * e_n
    ops_ref[4] = bn * e_d
    ops_ref[5] = k2s * e_d
    ops_ref[6] = vs
    wl_col = jnp.exp(_mm(lw_terms, [jnp.ones((L, LANES), BF16)], _TN, order=2))
    eye = (ri == ci).astype(F32)
    n_sq = max(int(np.ceil(np.log2(L))) - 1, 0)

    hs = range(n_heads)
    sls = [slice(h * HEAD_DIM, (h + 1) * HEAD_DIM) for h in hs]
    at, rt, bt, kt, bh, kh, vh = [[_split(ops_ref[i, :, sl]) for sl in sls] for i in range(7)]
    m_ab = [jnp.where(strict, _mm(at[h], bt[h], _NT), 0.0) for h in hs]
    t_inv = [eye + m_ab[h] for h in hs]
    pw = m_ab
    for _ in range(n_sq):
        pw_t = [_split(pw[h]) for h in hs]
        pw = [_mm(pw_t[h], pw_t[h]) for h in hs]
        t_inv = [t_inv[h] + _mm(_split(t_inv[h]), _split(pw[h])) for h in hs]
    m_ak = [jnp.where(strict, _mm(at[h], kt[h], _NT), 0.0) for h in hs]
    m_rb = [jnp.where(incl, _mm(rt[h][:1], bt[h][:1], _NT), 0.0).astype(BF16) for h in hs]
    m_rk = [jnp.where(incl, _mm(rt[h][:1], kt[h][:1], _NT), 0.0).astype(BF16) for h in hs]
    zs = [z_ref[h] for h in hs]
    zs_t = [_split(zs[h]) for h in hs]
    rhs = [_mm(at[h], zs_t[h]) + _mm(_split(m_ak[h]), vh[h]) for h in hs]
    u_t = [_split(_mm(_split(t_inv[h]), _split(rhs[h]))) for h in hs]
    for h in hs:
        yh_ref[:, sls[h]] = (_mm(rt[h][:1], zs_t[h][:1]) + _mm([m_rb[h]], u_t[h][:1])
                             + _mm([m_rk[h]], vh[h][:1]))
    for h in hs:
        z_ref[h] = wl_col[sls[h], 0:HEAD_DIM] * zs[h] + _mm(bh[h], u_t[h], _TN) + _mm(kh[h], vh[h], _TN)

    y = yh_ref[...]
    inv_n = 1.0 / HEAD_DIM
    mean = seg_sum(y) * inv_n
    yc = y - mean
    var = seg_sum(yc * yc) * inv_n
    yn = yc * lax.rsqrt(var + LNX_EPS) * lnx_g + lnx_b
    bonus = seg_sum(r * k2 * r_k) * v
    y_ref[0] = ((yn + bonus) * g).astype(y_ref.dtype)

    @pl.when(c == pl.num_programs(1) - 1)
    def _():
        zout_ref[0] = z_ref[...]


def _rwkv(p3, shift0, z0, mu, vec, w2, a2, g2, seg, *, L, t_valid, rw):
    b, t, n_rw = p3.shape
    n_heads = rw // HEAD_DIM
    const2 = lambda i, c: (0, 0)
    return pl.pallas_call(
        functools.partial(_rwkv_kernel, L=L, t_valid=t_valid, rw=rw, n_heads=n_heads),
        grid=(b, t // L),
        in_specs=[pl.BlockSpec((1, L, n_rw), lambda i, c: (i, c, 0)),
                  pl.BlockSpec((1, 1, n_rw), lambda i, c: (i, 0, 0)),
                  pl.BlockSpec((1, n_heads, HEAD_DIM, HEAD_DIM), lambda i, c: (i, 0, 0, 0)),
                  pl.BlockSpec((1, n_rw), const2),
                  pl.BlockSpec(vec.shape, const2),
                  pl.BlockSpec(w2.shape, const2),
                  pl.BlockSpec(a2.shape, const2),
                  pl.BlockSpec(g2.shape, const2),
                  pl.BlockSpec(seg.shape, const2)],
        out_specs=[pl.BlockSpec((1, L, rw), lambda i, c: (i, c, 0)),
                   pl.BlockSpec((1, n_heads, HEAD_DIM, HEAD_DIM), lambda i, c: (i, 0, 0, 0))],
        out_shape=[jax.ShapeDtypeStruct((b, t, rw), BF16),
                   jax.ShapeDtypeStruct((b, n_heads, HEAD_DIM, HEAD_DIM), F32)],
        scratch_shapes=[pltpu.VMEM((1, n_rw), F32),
                        pltpu.VMEM((n_heads, HEAD_DIM, HEAD_DIM), F32),
                        pltpu.VMEM((7, L, rw), F32),
                        pltpu.VMEM((L, rw), F32)],
        compiler_params=_params("arbitrary", "arbitrary"),
        name="rwkv_scan",
    )(p3, shift0, z0, mu, vec, w2, a2, g2, seg)


def _key_to_float(key):
    bits = key ^ ((key >> 31) & 0x7FFFFFFF)
    return lax.bitcast_convert_type(bits, F32)


def _fold_lanes(m):
    acc = m[:, 0:LANES]
    for j in range(1, m.shape[1] // LANES):
        acc = acc + m[:, j * LANES:(j + 1) * LANES]
    return acc


def _select_bias(sc_ref, pb_ref, nc, rows, tk, k_top, n_keys_total):
    def count(pred):
        def body(c, acc):
            return acc + _fold_lanes(pred(c, sc_ref[c]).astype(F32))
        acc = lax.fori_loop(0, nc, body, jnp.zeros((rows, LANES), F32))
        return jnp.sum(acc, axis=-1, keepdims=True)

    kf = float(k_top)
    cnt0 = count(lambda c, s: s >= 0.0)
    tau0 = jnp.where(cnt0 >= kf, 0, INT_MIN).astype(I32)

    def bit_body(i, tau):
        cand = tau + jnp.left_shift(jnp.int32(1), 30 - i)
        cand_f = _key_to_float(cand)
        cnt = count(lambda c, s: s >= cand_f)
        return jnp.where((cnt >= kf) | (cand <= KEY_NEG_INF), cand, tau)

    tau = lax.fori_loop(0, 31, bit_body, tau0)
    tau_f = _key_to_float(tau)
    need = kf - count(lambda c, s: s > tau_f)
    excess = count(lambda c, s: s == tau_f) - need

    def kidx(c):
        return c * tk + lax.broadcasted_iota(I32, (1, tk), 1)

    n_bits = max(int(np.ceil(np.log2(n_keys_total))), 1)
    pb_ref[...] = jnp.full((rows, 1), n_keys_total, I32)

    @pl.when(jnp.max(excess) > 0.0)
    def _():
        def tie_body(i, pb):
            cand = pb + jnp.left_shift(jnp.int32(1), n_bits - 1 - i)
            cnt = count(lambda c, s: (s == tau_f) & (kidx(c) < cand))
            return jnp.where(cnt < need, cand, pb)

        pb_ref[...] = lax.fori_loop(0, n_bits, tie_body, jnp.zeros((rows, 1), I32))

    pb = pb_ref[...]

    def write(c, _):
        s = sc_ref[c]
        sel = ((s > tau_f) | ((s == tau_f) & (kidx(c) <= pb))) & (s > -jnp.inf)
        sc_ref[c] = jnp.where(sel, 0.0, NEG_BIG)
        return 0

    lax.fori_loop(0, nc, write, 0)


def _dsa_prompt_kernel(q_ref, qi_ref, kw_ref, kb_ref, vb_ref, kib_ref, o_ref, sc_ref, pb_ref, m_ref, l_ref, acc_ref,
                       *, tq, tk, k_top, n_heads, n_idx, t_total):
    qb = pl.program_id(1)
    q0 = qb * tq
    nc = (q0 + tq + tk - 1) // tk
    qpos = q0 + lax.broadcasted_iota(I32, (tq, 1), 0)

    def scores(c, _):
        kic = kib_ref[0, pl.ds(pl.multiple_of(c * tk, tk), tk), :]
        acc = jnp.zeros((tq, tk), F32)
        for h in range(n_idx):
            d = _dot_nt(qi_ref[0, :, h * HEAD_DIM:(h + 1) * HEAD_DIM], kic)
            acc = acc + jnp.maximum(d, 0.0) * kw_ref[0, :, HEAD_DIM + h:HEAD_DIM + h + 1]
        spos = c * tk + lax.broadcasted_iota(I32, (1, tk), 1)
        sc_ref[c] = jnp.where(spos <= qpos, acc, -jnp.inf)
        return 0

    lax.fori_loop(0, nc, scores, 0)
    _select_bias(sc_ref, pb_ref, nc, tq, tk, k_top, t_total)

    m_ref[...] = jnp.full(m_ref.shape, NEG_BIG, F32)
    l_ref[...] = jnp.zeros(l_ref.shape, F32)
    acc_ref[...] = jnp.zeros(acc_ref.shape, F32)

    def attend(c, _):
        rows = pl.ds(pl.multiple_of(c * tk, tk), tk)
        bias = sc_ref[c]
        for h in range(n_heads):
            sl = slice(h * HEAD_DIM, (h + 1) * HEAD_DIM)
            s = _dot_nt(q_ref[0, :, sl], kb_ref[0, rows, sl]) + bias
            m = m_ref[h]
            m_new = jnp.maximum(m, jnp.max(s, axis=-1, keepdims=True))
            alpha = jnp.exp(m - m_new)
            p = jnp.exp(s - m_new)
            l_ref[h] = alpha * l_ref[h] + jnp.sum(p, axis=-1, keepdims=True)
            acc_ref[h] = alpha * acc_ref[h] + _dot(p.astype(BF16), vb_ref[0, rows, sl])
            m_ref[h] = m_new
        return 0

    lax.fori_loop(0, nc, attend, 0)
    for h in range(n_heads):
        o_ref[0, :, h * HEAD_DIM:(h + 1) * HEAD_DIM] = (acc_ref[h] / l_ref[h]).astype(o_ref.dtype)


def _dsa_prompt(q, qi, kw, kb, vb, kib, *, tq, tk, k_top):
    b, t, att_w = q.shape
    n_heads = att_w // HEAD_DIM
    n_idx = qi.shape[2] // HEAD_DIM
    qblk = lambda i, j: (i, j, 0)
    full = lambda i, j: (i, 0, 0)
    return pl.pallas_call(
        functools.partial(_dsa_prompt_kernel, tq=tq, tk=tk, k_top=k_top, n_heads=n_heads, n_idx=n_idx,
                          t_total=t),
        grid=(b, t // tq),
        in_specs=[pl.BlockSpec((1, tq, att_w), qblk),
                  pl.BlockSpec((1, tq, qi.shape[2]), qblk),
                  pl.BlockSpec((1, tq, LANES), qblk),
                  pl.BlockSpec((1, t, att_w), full),
                  pl.BlockSpec((1, t, att_w), full),
                  pl.BlockSpec((1, t, HEAD_DIM), full)],
        out_specs=pl.BlockSpec((1, tq, att_w), qblk),
        out_shape=jax.ShapeDtypeStruct((b, t, att_w), BF16),
        scratch_shapes=[pltpu.VMEM((t // tk, tq, tk), F32),
                        pltpu.VMEM((tq, 1), I32),
                        pltpu.VMEM((n_heads, tq, 1), F32),
                        pltpu.VMEM((n_heads, tq, 1), F32),
                        pltpu.VMEM((n_heads, tq, HEAD_DIM), F32)],
        compiler_params=_params("arbitrary", "arbitrary"),
        name="dsa_prompt",
    )(q, qi, kw, kb, vb, kib)


def _idx_page_scores(qim, wcol, ki_t, n_idx):
    d = jnp.maximum(_dot(qim, ki_t.astype(BF16)), 0.0) * wcol
    sc = d[0:SUBLANES]
    for h in range(1, n_idx):
        sc = sc + d[h * SUBLANES:(h + 1) * SUBLANES]
    return sc


def _idx_sample_kernel(pt_ref, qim_ref, wcol_ref, *refs, t_new, n_idx, g_pages):
    page_refs, o_ref = refs[:g_pages], refs[g_pages]
    qrow = lax.broadcasted_iota(I32, (SUBLANES, PAGE_SIZE), 0)
    for j in range(g_pages):
        sc = _idx_page_scores(qim_ref[0], wcol_ref[0], page_refs[j][0, 0], n_idx)
        o_ref[j] = jnp.where(qrow < t_new, sc, -jnp.inf)


def _idx_sample(page_table, qim, wcol, cache_idx_t, *, t_new, g_pages, layer):
    b, n_pages = page_table.shape
    n_idx = qim.shape[1] // SUBLANES
    per_seq = lambda i, g, pt: (i, 0, 0)
    page_specs = [pl.BlockSpec((1, 1, HEAD_DIM, PAGE_SIZE),
                               lambda i, g, pt, j=j: (layer, pt[i, g * g_pages + j], 0, 0))
                  for j in range(g_pages)]
    grid_spec = pltpu.PrefetchScalarGridSpec(
        num_scalar_prefetch=1,
        grid=(b, n_pages // g_pages),
        in_specs=[pl.BlockSpec((1,) + qim.shape[1:], per_seq),
                  pl.BlockSpec((1,) + wcol.shape[1:], per_seq)] + page_specs,
        out_specs=pl.BlockSpec((g_pages, SUBLANES, PAGE_SIZE), lambda i, g, pt: (g, i, 0)),
    )
    return pl.pallas_call(
        functools.partial(_idx_sample_kernel, t_new=t_new, n_idx=n_idx, g_pages=g_pages),
        grid_spec=grid_spec,
        out_shape=jax.ShapeDtypeStruct((n_pages, b * SUBLANES, PAGE_SIZE), F32),
        compiler_params=_params("arbitrary", "arbitrary"),
        name="idx_sample",
    )(page_table, qim, wcol, *([cache_idx_t] * g_pages))


def _idx_new_kernel(qim_ref, wcol_ref, new_ref, o_ref, *, t_new, n_idx):
    sc = _idx_page_scores(qim_ref[0], wcol_ref[0], new_ref[0], n_idx)
    qrow = lax.broadcasted_iota(I32, (SUBLANES, PAGE_SIZE), 0)
    lane = lax.broadcasted_iota(I32, (SUBLANES, PAGE_SIZE), 1)
    o_ref[...] = jnp.where((qrow < t_new) & (lane <= qrow), sc, -jnp.inf)


def _idx_new(qim, wcol, ki_new, *, t_new):
    b = qim.shape[0]
    n_idx = qim.shape[1] // SUBLANES
    per_seq = lambda i: (i, 0, 0)
    return pl.pallas_call(
        functools.partial(_idx_new_kernel, t_new=t_new, n_idx=n_idx),
        grid=(b,),
        in_specs=[pl.BlockSpec((1,) + qim.shape[1:], per_seq),
                  pl.BlockSpec((1,) + wcol.shape[1:], per_seq),
                  pl.BlockSpec((1, HEAD_DIM, PAGE_SIZE), per_seq)],
        out_specs=pl.BlockSpec((SUBLANES, PAGE_SIZE), lambda i: (i, 0)),
        out_shape=jax.ShapeDtypeStruct((b * SUBLANES, PAGE_SIZE), F32),
        compiler_params=_params("arbitrary"),
        name="idx_new",
    )(qim, wcol, ki_new)


def _select_sample_kernel(past_ref, new_ref, o_ref, pb_ref, *, k_top):
    n_pages, rows = past_ref.shape[0], past_ref.shape[1]
    o_ref[0:n_pages] = past_ref[...]
    o_ref[n_pages] = new_ref[...]
    _select_bias(o_ref, pb_ref, n_pages + 1, rows, PAGE_SIZE, k_top, (n_pages + 1) * PAGE_SIZE)


def _select_sample(past_scores, new_scores, *, k_top):
    n_pages, rows, _ = past_scores.shape
    out_shape = (n_pages + 1, rows, PAGE_SIZE)
    return pl.pallas_call(
        functools.partial(_select_sample_kernel, k_top=k_top),
        grid=(1,),
        in_specs=[pl.BlockSpec(past_scores.shape, lambda i: (0, 0, 0)),
                  pl.BlockSpec(new_scores.shape, lambda i: (0, 0))],
        out_specs=pl.BlockSpec(out_shape, lambda i: (0, 0, 0)),
        out_shape=jax.ShapeDtypeStruct(out_shape, F32),
        scratch_shapes=[pltpu.VMEM((rows, 1), I32)],
        compiler_params=_params("arbitrary"),
        name="select_sample",
    )(past_scores, new_scores)


def _attn_sample_kernel(pt_ref, qbd_ref, bias_ref, biasn_ref, knew_ref, vnew_ref, *refs, n_heads, g_pages):
    k_refs, v_refs = refs[:g_pages], refs[g_pages:2 * g_pages]
    o_ref, m_ref, l_ref, acc_ref = refs[2 * g_pages:]
    g = pl.program_id(1)

    @pl.when(g == 0)
    def _():
        m_ref[...] = jnp.full(m_ref.shape, NEG_BIG, F32)
        l_ref[...] = jnp.zeros(l_ref.shape, F32)
        acc_ref[...] = jnp.zeros(acc_ref.shape, F32)

    def step(kts, vts, biases):
        bias = jnp.concatenate([jnp.concatenate([b_] * n_heads, axis=0) for b_ in biases], axis=1)
        s = jnp.concatenate([_dot(qbd_ref[0], kt.astype(BF16)) for kt in kts], axis=1) + bias
        m = m_ref[...]
        m_new = jnp.maximum(m, jnp.max(s, axis=-1, keepdims=True))
        alpha = jnp.exp(m - m_new)
        pr = jnp.exp(s - m_new)
        prb = pr.astype(BF16)
        pv = None
        for j, vt in enumerate(vts):
            d = _dot_nt(prb[:, j * PAGE_SIZE:(j + 1) * PAGE_SIZE], vt.astype(BF16))
            pv = d if pv is None else pv + d
        l_ref[...] = alpha * l_ref[...] + jnp.sum(pr, axis=-1, keepdims=True)
        acc_ref[...] = alpha * acc_ref[...] + pv
        m_ref[...] = m_new

    step([r[0, 0] for r in k_refs], [r[0, 0] for r in v_refs], [bias_ref[j] for j in range(g_pages)])

    @pl.when(g == pl.num_programs(1) - 1)
    def _():
        step([knew_ref[0]], [vnew_ref[0]], [biasn_ref[0]])
        full = acc_ref[...] / l_ref[...]
        col_head = lax.broadcasted_iota(I32, (SUBLANES, full.shape[1]), 1) // HEAD_DIM
        out = jnp.zeros((SUBLANES, full.shape[1]), F32)
        for h in range(n_heads):
            out = out + jnp.where(col_head == h, full[h * SUBLANES:(h + 1) * SUBLANES], 0.0)
        o_ref[0] = out.astype(o_ref.dtype)


def _attn_sample(page_table, qbd, bias, cache_k_t, cache_v_t, k_new_t, v_new_t, *, g_pages, layer):
    b, n_pages = page_table.shape
    att_w = qbd.shape[2]
    n_heads = att_w // HEAD_DIM
    per_seq = lambda i, g, pt: (i, 0, 0)
    page_specs = [pl.BlockSpec((1, 1, att_w, PAGE_SIZE),
                               lambda i, g, pt, j=j: (layer, pt[i, g * g_pages + j], 0, 0))
                  for j in range(g_pages)]
    new_spec = pl.BlockSpec((1, att_w, PAGE_SIZE), per_seq)
    grid_spec = pltpu.PrefetchScalarGridSpec(
        num_scalar_prefetch=1,
        grid=(b, n_pages // g_pages),
        in_specs=[pl.BlockSpec((1,) + qbd.shape[1:], per_seq),
                  pl.BlockSpec((g_pages, SUBLANES, PAGE_SIZE), lambda i, g, pt: (g, i, 0)),
                  pl.BlockSpec((1, SUBLANES, PAGE_SIZE), lambda i, g, pt: (n_pages, i, 0)),
                  new_spec, new_spec] + page_specs + page_specs,
        out_specs=pl.BlockSpec((1, SUBLANES, att_w), per_seq),
        scratch_shapes=[pltpu.VMEM((n_heads * SUBLANES, 1), F32),
                        pltpu.VMEM((n_heads * SUBLANES, 1), F32),
                        pltpu.VMEM((n_heads * SUBLANES, att_w), F32)],
    )
    return pl.pallas_call(
        functools.partial(_attn_sample_kernel, n_heads=n_heads, g_pages=g_pages),
        grid_spec=grid_spec,
        out_shape=jax.ShapeDtypeStruct((b, SUBLANES, att_w), BF16),
        compiler_params=_params("arbitrary", "arbitrary"),
        name="attn_sample",
    )(page_table, qbd, bias, bias, k_new_t, v_new_t, *([cache_k_t] * g_pages), *([cache_v_t] * g_pages))


def _outproj_kernel(x_ref, yr_ref, ya_ref, mod_ref, wo_ref, g_ref, rw_ref, rb_ref, x1_ref, h2_ref, gates_ref, *, rw):
    mixed = _dot(yr_ref[...], wo_ref[0:rw, :]) + _dot(ya_ref[...], wo_ref[rw:, :])
    x1 = x_ref[...] + mod_ref[2, 0] * mixed
    x1_ref[...] = x1
    h2 = _rms(x1, g_ref[...]) * (1 + mod_ref[4, 0]) + mod_ref[3, 0]
    h2_ref[...] = h2.astype(BF16)
    logits = _dot(h2, rw_ref[...], HI) + rb_ref[...]
    lane = lax.broadcasted_iota(I32, logits.shape, 1).astype(F32)
    work = logits
    picked = jnp.zeros(logits.shape, jnp.bool_)
    top = None
    for i in range(TOP_K):
        mx = jnp.max(work, axis=-1, keepdims=True)
        first = jnp.min(jnp.where(work == mx, lane, float(LANES)), axis=-1, keepdims=True)
        hit = lane == first
        picked = picked | hit
        work = jnp.where(hit, -jnp.inf, work)
        if i == 0:
            top = mx
    e = jnp.where(picked, jnp.exp(logits - top), 0.0)
    gates_ref[...] = e / jnp.sum(e, axis=-1, keepdims=True)


def _outproj(x2d, yr, ya, mod6, w_out_b, g2n, router_w_pad, router_b_pad, *, tm, tiles_per_seq, rw):
    n, d = x2d.shape
    rmod = mod6.shape[2]
    row = lambda i: (i, 0)
    const = lambda i: (0, 0)
    outs = [jax.ShapeDtypeStruct((n, d), F32), jax.ShapeDtypeStruct((n, d), BF16),
            jax.ShapeDtypeStruct((n, LANES), F32)]
    return pl.pallas_call(
        functools.partial(_outproj_kernel, rw=rw),
        grid=(n // tm,),
        in_specs=[pl.BlockSpec((tm, d), row),
                  pl.BlockSpec((tm, yr.shape[1]), row),
                  pl.BlockSpec((tm, ya.shape[1]), row),
                  pl.BlockSpec((6, 1, rmod, d), lambda i: (0, i // tiles_per_seq, 0, 0)),
                  pl.BlockSpec(w_out_b.shape, const),
                  pl.BlockSpec((1, d), const),
                  pl.BlockSpec(router_w_pad.shape, const),
                  pl.BlockSpec((1, LANES), const)],
        out_specs=[pl.BlockSpec((tm, o.shape[1]), row) for o in outs],
        out_shape=outs,
        compiler_params=_params("arbitrary"),
        name="outproj_router",
    )(x2d, yr, ya, mod6, w_out_b, g2n, router_w_pad, router_b_pad)


def _moe_kernel(h_ref, gates_ref, x1_ref, mod_ref, wgu_ref, bgu_ref, wd_ref, bd_ref, fg_ref, o_ref, acc_ref, *,
                d_ff):
    e = pl.program_id(1)

    @pl.when(e == 0)
    def _():
        acc_ref[...] = jnp.zeros(acc_ref.shape, F32)

    gu = _dot(h_ref[...], wgu_ref[0]) + bgu_ref[0]
    gate = jnp.minimum(gu[:, :d_ff], SWIGLU_LIMIT)
    up = jnp.clip(gu[:, d_ff:], -SWIGLU_LIMIT, SWIGLU_LIMIT)
    act = (up + 1) * gate * _sigmoid(SWIGLU_ALPHA * gate)
    y = _dot(act.astype(BF16), wd_ref[0]) + bd_ref[0]
    gates = gates_ref[...]
    lane = lax.broadcasted_iota(I32, gates.shape, 1)
    ge = jnp.sum(jnp.where(lane == e, gates, 0.0), axis=-1, keepdims=True)
    acc_ref[...] += ge * y

    @pl.when(e == pl.num_programs(1) - 1)
    def _():
        x2 = x1_ref[...] + mod_ref[5, 0] * acc_ref[...]
        o_ref[...] = _rms(x2, fg_ref[...])


def _moe(h2, gates, x1, mod6, wgu_b, bgu, wd_b, bd, fg, *, tm, tiles_per_seq):
    n, d = x1.shape
    n_exp, _, two_ff = wgu_b.shape
    rmod = mod6.shape[2]
    row = lambda i, e: (i, 0)
    return pl.pallas_call(
        functools.partial(_moe_kernel, d_ff=two_ff // 2),
        grid=(n // tm, n_exp),
        in_specs=[pl.BlockSpec((tm, d), row),
                  pl.BlockSpec((tm, LANES), row),
                  pl.BlockSpec((tm, d), row),
                  pl.BlockSpec((6, 1, rmod, d), lambda i, e: (0, i // tiles_per_seq, 0, 0)),
                  pl.BlockSpec((1, d, two_ff), lambda i, e: (e, 0, 0)),
                  pl.BlockSpec((1, 1, two_ff), lambda i, e: (e, 0, 0)),
                  pl.BlockSpec((1, two_ff // 2, d), lambda i, e: (e, 0, 0)),
                  pl.BlockSpec((1, 1, d), lambda i, e: (e, 0, 0)),
                  pl.BlockSpec((1, d), lambda i, e: (0, 0))],
        out_specs=pl.BlockSpec((tm, d), row),
        out_shape=jax.ShapeDtypeStruct((n, d), F32),
        scratch_shapes=[pltpu.VMEM((tm, d), F32)],
        compiler_params=_params("arbitrary", "arbitrary"),
        name="moe_ffn",
    )(h2, gates, x1, mod6, wgu_b, bgu, wd_b, bd, fg)


def _pad_cols(a, width):
    return jnp.pad(a, ((0, 0), (0, width - a.shape[1])))


def _rope_tables(pos):
    half = HEAD_DIM // 2
    inv = ROPE_THETA ** (-jnp.arange(half, dtype=F32) / half)
    ang = pos.astype(F32)[:, None] * inv[None, :]
    cos, sin = jnp.cos(ang), jnp.sin(ang)
    cos_h = jnp.concatenate([cos, cos], axis=1)
    sin_h = jnp.concatenate([-sin, sin], axis=1)
    ones = jnp.ones_like(cos_h)
    zeros = jnp.zeros_like(cos_h)
    tabs = jnp.stack([jnp.concatenate([cos_h, cos_h], 1), jnp.concatenate([sin_h, sin_h], 1),
                      jnp.concatenate([cos_h, ones], 1), jnp.concatenate([sin_h, zeros], 1)])
    return tabs


def _layer(x, mod_rows, pos, shift_prev, wkv_prev, wts, attend, *, tm, chunk):
    b, t, d = x.shape
    n = b * t
    rw, att_w, n_rw, rwkv_in = wts["rw"], wts["att_w"], wts["n_rw"], wts["rwkv_in"]
    per_token_mod = mod_rows.shape[0] != b or tm > t
    if per_token_mod:
        mod6 = jnp.repeat(mod_rows, t, axis=0).reshape(n // tm, tm, 6, d).transpose(2, 0, 1, 3)
        tiles_per_seq = 1
    else:
        mod6 = mod_rows.reshape(b, 1, 6, d).transpose(2, 0, 1, 3)
        tiles_per_seq = t // tm
    cs = _rope_tables(pos)
    n_idx_scale = wts["idx_scale"]
    cs = cs.at[2, :, HEAD_DIM:].set(n_idx_scale)
    if cs.shape[1] < tm:
        cs = jnp.tile(cs, (1, tm // cs.shape[1], 1))
    x2d = x.reshape(n, d)
    p, q, k32, kb, v32, vb, qi, kw = _inproj(x2d, mod6, wts["g1"], wts["w_cat"], cs, tm=tm,
                                             tiles_per_seq=tiles_per_seq, n_rw=n_rw, att_w=att_w)
    t_pad = -(-t // chunk) * chunk
    p3 = p.reshape(b, t, n_rw)
    if t_pad != t:
        p3 = jnp.pad(p3, ((0, 0), (0, t_pad - t), (0, 0)))
    z0 = jnp.swapaxes(wkv_prev, -1, -2)
    yr, z_fin = _rwkv(p3, shift_prev.reshape(b, 1, n_rw), z0, wts["mu"], wts["vec"], wts["w2"], wts["a2"], wts["g2"],
                      wts["seg"], L=chunk, t_valid=t if t_pad != t else chunk, rw=rw)
    yr = yr[:, :t].reshape(n, rw)
    wkv_new = jnp.swapaxes(z_fin, -1, -2)
    ya = attend(q, qi, kw, k32, kb, v32, vb).reshape(n, att_w)
    x1, h2, gates = _outproj(x2d, yr, ya, mod6, wts["w_out"], wts["g2n"], wts["router_w"], wts["router_b"], tm=tm,
                             tiles_per_seq=tiles_per_seq, rw=rw)
    tm_moe = min(wts["tm_moe"], n)
    if per_token_mod:
        mod6m, tps_m = mod6, 1
        assert tm_moe == tm
    else:
        mod6m, tps_m = mod6, t // tm_moe
    y = _moe(h2, gates, x1, mod6m, wts["wgu"], wts["bgu"], wts["wd"], wts["bd"], wts["fg"], tm=tm_moe,
             tiles_per_seq=tps_m)
    p_last = p3[:, t - 1, :]
    new_shift = jnp.concatenate([p_last[:, :3 * rw], p_last[:, 3 * rw:3 * rw + wts["lora"][0]],
                                 p_last[:, 3 * rw + 128:3 * rw + 128 + wts["lora"][1]],
                                 p_last[:, 3 * rw + 256:3 * rw + 256 + wts["lora"][2]]], axis=1)
    n_heads = att_w // HEAD_DIM
    state = (k32.reshape(b, t, n_heads, HEAD_DIM), v32.reshape(b, t, n_heads, HEAD_DIM),
             kw[:, :HEAD_DIM].reshape(b, t, HEAD_DIM), wkv_new, new_shift)
    return y.reshape(b, t, d), state


def kernel(x_prompt, x_sample, c_prompt, c_sample, cache_k, cache_v, cache_idx_k, state_wkv, state_shift, page_table, norm1_g, norm2_g, w_mod, b_mod, w_in, mu_shift, w0, w2, a0, a2, g2, k_k, k_a, r_k, lnx_g, lnx_b, w_out, router_w, router_b, w_gate_up, b_gate_up, w_down, b_down, final_g):
    depth = w_in.shape[0]
    assert depth == 1, "single-layer trunk"
    l = 0
    bp, tp, d = x_prompt.shape
    bs, ts, _ = x_sample.shape
    rw = w0.shape[1]
    att_w = w_out.shape[1] - rw
    lora = (w2.shape[1], a2.shape[1], g2.shape[1])
    rwkv_in = 3 * rw + sum(lora)
    n_idx = (w_in.shape[2] - rwkv_in - 3 * att_w - HEAD_DIM) // (HEAD_DIM + 1)
    n_exp = router_w.shape[2]
    n_pages = page_table.shape[1]
    past = n_pages * PAGE_SIZE
    n_rw = 3 * rw + 512
    assert lora[0] <= 128 and lora[1] <= 128 and lora[2] <= 256 and n_idx * HEAD_DIM == att_w

    wi = w_in[l]
    o = 3 * rw
    rw_cols = [wi[:, :o], _pad_cols(wi[:, o:o + lora[0]], 128), _pad_cols(wi[:, o + lora[0]:o + lora[0] + lora[1]], 128),
               _pad_cols(wi[:, o + lora[0] + lora[1]:rwkv_in], 256)]
    a0_ = rwkv_in
    att_cols = [wi[:, a0_:a0_ + 4 * att_w], _pad_cols(wi[:, a0_ + 4 * att_w:], LANES)]
    w_cat = jnp.concatenate(rw_cols + att_cols, axis=1).astype(BF16)
    mu = mu_shift[l]
    mu_pad = jnp.concatenate([mu[:o], jnp.pad(mu[o:o + lora[0]], (0, 128 - lora[0])),
                              jnp.pad(mu[o + lora[0]:o + lora[0] + lora[1]], (0, 128 - lora[1])),
                              jnp.pad(mu[o + lora[0] + lora[1]:], (0, 256 - lora[2]))]).reshape(1, n_rw)
    vec = jnp.stack([w0[l], a0[l], k_k[l], k_a[l], lnx_g[l], lnx_b[l], r_k[l].reshape(-1),
                     jnp.zeros((rw,), F32)])
    head_of = np.arange(rw) // HEAD_DIM
    seg = jnp.asarray((head_of[:, None] == head_of[None, :]).astype(np.float32)).astype(BF16)
    wts = dict(
        rw=rw, att_w=att_w, n_rw=n_rw, rwkv_in=rwkv_in, lora=lora,
        idx_scale=float((n_idx * HEAD_DIM) ** -0.5),
        g1=norm1_g[l].reshape(1, d), w_cat=w_cat, mu=mu_pad, vec=vec,
        w2=jnp.pad(w2[l], ((0, 128 - lora[0]), (0, 0))), a2=jnp.pad(a2[l], ((0, 128 - lora[1]), (0, 0))),
        g2=jnp.pad(g2[l], ((0, 256 - lora[2]), (0, 0))), seg=seg,
        w_out=w_out[l].astype(BF16), g2n=norm2_g[l].reshape(1, d),
        router_w=_pad_cols(router_w[l], LANES),
        router_b=jnp.concatenate([router_b[l], jnp.full((LANES - n_exp,), NEG_BIG, F32)]).reshape(1, LANES),
        wgu=w_gate_up[l].astype(BF16), bgu=b_gate_up[l].reshape(n_exp, 1, -1),
        wd=w_down[l].astype(BF16), bd=b_down[l].reshape(n_exp, 1, -1), fg=final_g.reshape(1, d),
        tm_moe=512,
    )

    c_all = jnp.concatenate([c_prompt, c_sample], axis=0)
    rows_pad = -(-c_all.shape[0] // SUBLANES) * SUBLANES
    mod_all = _modulation(jnp.pad(c_all, ((0, rows_pad - c_all.shape[0]), (0, 0))), w_mod[l], b_mod[l])
    mod_p, mod_s = mod_all[:bp], mod_all[bp:bp + bs]

    k_top_p = min(TOPK_MAX, tp // 4)

    def attend_prompt(q, qi, kw, k32, kb, v32, vb):
        r3 = lambda a: a.reshape(bp, tp, a.shape[1])
        kib = kw[:, :HEAD_DIM].astype(BF16)
        return _dsa_prompt(r3(q), r3(qi), r3(kw), r3(kb), r3(vb), r3(kib), tq=min(256, tp), tk=min(512, tp), k_top=k_top_p)

    y_p, st_p = _layer(x_prompt, mod_p, jnp.arange(tp), jnp.zeros((bp, n_rw), F32),
                       jnp.zeros((bp, rw // HEAD_DIM, HEAD_DIM, HEAD_DIM), F32), wts, attend_prompt,
                       tm=min(512, tp), chunk=64)

    k_top_s = min(TOPK_MAX, (past + ts) // 4)
    n_heads = att_w // HEAD_DIM
    assert ts <= SUBLANES
    pool_t = lambda c: jnp.transpose(c, (0, 1, 3, 4, 2)).reshape(c.shape[0], c.shape[1], att_w, PAGE_SIZE)
    ck_t, cv_t = pool_t(cache_k), pool_t(cache_v)
    cik_t = jnp.transpose(cache_idx_k, (0, 1, 3, 2))
    g_idx = 16 if n_pages % 16 == 0 else 1
    g_att = 8 if n_pages % 8 == 0 else 1

    def head_query_rows(a, nh):
        a = a.reshape(bs, ts, nh, HEAD_DIM).transpose(0, 2, 1, 3)
        a = jnp.pad(a, ((0, 0), (0, 0), (0, SUBLANES - ts), (0, 0)))
        return a.reshape(bs, nh * SUBLANES, HEAD_DIM)

    def new_rows_t(a):
        a = a.reshape(bs, ts, a.shape[1]).transpose(0, 2, 1)
        return jnp.pad(a, ((0, 0), (0, 0), (0, PAGE_SIZE - ts)))

    def attend_sample(q, qi, kw, k32, kb, v32, vb):
        qim = head_query_rows(qi, n_idx)
        wcol = kw[:, HEAD_DIM:HEAD_DIM + n_idx].reshape(bs, ts, n_idx).transpose(0, 2, 1)
        wcol = jnp.pad(wcol, ((0, 0), (0, 0), (0, SUBLANES - ts))).reshape(bs, n_idx * SUBLANES, 1)
        past_scores = _idx_sample(page_table, qim, wcol, cik_t, t_new=ts, g_pages=g_idx, layer=l)
        new_scores = _idx_new(qim, wcol, new_rows_t(kw[:, :HEAD_DIM]), t_new=ts)
        bias = _select_sample(past_scores, new_scores, k_top=k_top_s)
        qh = head_query_rows(q, n_heads)
        eye = jnp.eye(n_heads, dtype=BF16)
        qbd = (qh.reshape(bs, n_heads, SUBLANES, 1, HEAD_DIM) * eye[None, :, None, :, None]).reshape(
            bs, n_heads * SUBLANES, att_w)
        out = _attn_sample(page_table, qbd, bias, ck_t, cv_t, new_rows_t(k32), new_rows_t(v32), g_pages=g_att,
                           layer=l)
        return out[:, :ts]

    shift_s = state_shift[l]
    o1, o2 = o + lora[0], o + lora[0] + lora[1]
    shift_pad = jnp.concatenate([shift_s[:, :o], _pad_cols(shift_s[:, o:o1], 128), _pad_cols(shift_s[:, o1:o2], 128),
                                 _pad_cols(shift_s[:, o2:], 256)], axis=1)
    y_s, st_s = _layer(x_sample, mod_s, past + jnp.arange(ts), shift_pad, state_wkv[l], wts, attend_sample,
                       tm=bs * ts, chunk=SUBLANES)

    ex = lambda a: a[None]
    return (y_p, y_s, ex(st_p[0]), ex(st_p[1]), ex(st_p[2]), ex(st_p[3]), ex(st_p[4]),
            ex(st_s[0]), ex(st_s[1]), ex(st_s[2]), ex(st_s[3]), ex(st_s[4]))
```

```python
import functools

import numpy as np
import jax
import jax.numpy as jnp
from jax import lax
from jax.experimental import pallas as pl
from jax.experimental.pallas import tpu as pltpu

F32 = jnp.float32
BF16 = jnp.bfloat16
I32 = jnp.int32
HI = lax.Precision.HIGHEST

HEAD_DIM = 64
PAGE_SIZE = 128
TOPK_MAX = 256
ROPE_THETA = 10000.0
TOP_K = 4
SWIGLU_LIMIT = 7.0
SWIGLU_ALPHA = 1.702
NORM_EPS = 1e-5
LNX_EPS = 64e-5

LANES = 128
SUBLANES = 8
VMEM_LIMIT = 56 * 1024 * 1024

NEG_BIG = -1e30
INT_MIN = -(2 ** 31)
KEY_NEG_INF = INT_MIN + 0x7FFFFF


def _dot(a, b, prec=None):
    return lax.dot_general(a, b, (((1,), (0,)), ((), ())), precision=prec, preferred_element_type=F32)


def _dot_nt(a, b, prec=None):
    return lax.dot_general(a, b, (((1,), (1,)), ((), ())), precision=prec, preferred_element_type=F32)


def _dot_tn(a, b, prec=None):
    return lax.dot_general(a, b, (((0,), (0,)), ((), ())), precision=prec, preferred_element_type=F32)


_NN = ((1,), (0,))
_NT = ((1,), (1,))
_TN = ((0,), (0,))


def _split(a, terms=2):
    out = []
    for i in range(terms):
        t = a.astype(BF16)
        out.append(t)
        if i + 1 < terms:
            a = a - t.astype(F32)
    return out


def _mm(a_terms, b_terms, dims=_NN, order=1):
    acc = None
    for i, a in enumerate(a_terms):
        for j, b in enumerate(b_terms):
            if i + j <= order:
                d = lax.dot_general(a, b, (dims, ((), ())), preferred_element_type=F32)
                acc = d if acc is None else acc + d
    return acc


def _params(*sem):
    return pltpu.CompilerParams(dimension_semantics=sem, vmem_limit_bytes=VMEM_LIMIT)


def _sigmoid(x):
    return jax.nn.sigmoid(x)


def _rms(x, g):
    return x * lax.rsqrt(jnp.mean(x * x, axis=-1, keepdims=True) + NORM_EPS) * g


def _mod_kernel(c_ref, w_ref, b_ref, o_ref):
    c = c_ref[...]
    o_ref[...] = _dot(c * _sigmoid(c), w_ref[...], HI) + b_ref[...]


def _modulation(c_pad, w_mod, b_mod):
    rows, d = c_pad.shape
    n = w_mod.shape[1]
    return pl.pallas_call(
        _mod_kernel,
        grid=(n // d,),
        in_specs=[pl.BlockSpec((rows, d), lambda j: (0, 0)),
                  pl.BlockSpec((d, d), lambda j: (0, j)),
                  pl.BlockSpec((1, d), lambda j: (0, j))],
        out_specs=pl.BlockSpec((rows, d), lambda j: (0, j)),
        out_shape=jax.ShapeDtypeStruct((rows, n), F32),
        compiler_params=_params("arbitrary"),
        name="modulation",
    )(c_pad, w_mod, b_mod.reshape(1, n))


def _rope_block(x, cos, sin_signed):
    lane = lax.broadcasted_iota(I32, x.shape, 1)
    partner = jnp.where((lane & 32) == 0, pltpu.roll(x, LANES - 32, axis=1), pltpu.roll(x, 32, axis=1))
    return x * cos + partner * sin_signed


def _inproj_kernel(x_ref, mod_ref, g_ref, w_ref, cs_ref, p_ref, q_ref, k_ref, kb_ref, v_ref, vb_ref, qi_ref,
                   kw_ref, *, n_rw, att_w):
    x = x_ref[...]
    h = _rms(x, g_ref[...]) * (1 + mod_ref[1, 0]) + mod_ref[0, 0]
    hb = h.astype(BF16)
    cb = 512
    for j in range(n_rw // cb):
        p_ref[:, j * cb:(j + 1) * cb] = _dot(hb, w_ref[:, j * cb:(j + 1) * cb])
    cos, sin = cs_ref[0], cs_ref[1]
    base = n_rw
    ng = att_w // LANES

    def roped(off):
        t = _dot(hb, w_ref[:, off:off + att_w])
        return [_rope_block(t[:, g * LANES:(g + 1) * LANES], cos, sin) for g in range(ng)]

    for g, blk in enumerate(roped(base)):
        q_ref[:, g * LANES:(g + 1) * LANES] = (blk * (HEAD_DIM ** -0.5)).astype(BF16)
    for g, blk in enumerate(roped(base + att_w)):
        k_ref[:, g * LANES:(g + 1) * LANES] = blk
        kb_ref[:, g * LANES:(g + 1) * LANES] = blk.astype(BF16)
    v = _dot(hb, w_ref[:, base + 2 * att_w:base + 3 * att_w])
    v_ref[...] = v
    vb_ref[...] = v.astype(BF16)
    for g, blk in enumerate(roped(base + 3 * att_w)):
        qi_ref[:, g * LANES:(g + 1) * LANES] = blk.astype(BF16)
    kw = _dot(hb, w_ref[:, base + 4 * att_w:base + 4 * att_w + LANES])
    kw_ref[...] = _rope_block(kw, cs_ref[2], cs_ref[3])


def _inproj(x2d, mod6, g1, w_cat, cs, *, tm, tiles_per_seq, n_rw, att_w):
    n, d = x2d.shape
    rmod = mod6.shape[2]
    n_cs_tiles = cs.shape[1] // tm
    grid = (n // tm,)
    row = lambda i: (i, 0)
    outs = [
        jax.ShapeDtypeStruct((n, n_rw), F32),
        jax.ShapeDtypeStruct((n, att_w), BF16),
        jax.ShapeDtypeStruct((n, att_w), F32),
        jax.ShapeDtypeStruct((n, att_w), BF16),
        jax.ShapeDtypeStruct((n, att_w), F32),
        jax.ShapeDtypeStruct((n, att_w), BF16),
        jax.ShapeDtypeStruct((n, att_w), BF16),
        jax.ShapeDtypeStruct((n, LANES), F32),
    ]
    return pl.pallas_call(
        functools.partial(_inproj_kernel, n_rw=n_rw, att_w=att_w),
        grid=grid,
        in_specs=[pl.BlockSpec((tm, d), row),
                  pl.BlockSpec((6, 1, rmod, d), lambda i: (0, i // tiles_per_seq, 0, 0)),
                  pl.BlockSpec((1, d), lambda i: (0, 0)),
                  pl.BlockSpec(w_cat.shape, lambda i: (0, 0)),
                  pl.BlockSpec((4, tm, LANES), lambda i: (0, i % n_cs_tiles, 0))],
        out_specs=[pl.BlockSpec((tm, o.shape[1]), row) for o in outs],
        out_shape=outs,
        compiler_params=_params("arbitrary"),
        name="inproj",
    )(x2d, mod6, g1, w_cat, cs)


def _softplus(x):
    return jnp.maximum(x, 0.0) + jnp.log(1.0 + jnp.exp(-jnp.abs(x)))


def _rwkv_kernel(p_ref, sh0_ref, z0_ref, mu_ref, vec_ref, w2_ref, a2_ref, g2_ref, seg_ref, y_ref, zout_ref,
                 carry_ref, z_ref, ops_ref, yh_ref, *, L, t_valid, rw, n_heads):
    c = pl.program_id(1)

    @pl.when(c == 0)
    def _():
        carry_ref[...] = sh0_ref[0]
        z_ref[...] = z0_ref[0]

    p = p_ref[0]
    row = lax.broadcasted_iota(I32, (L, 1), 0)
    prev = jnp.where(row == 0, carry_ref[...], pltpu.roll(p, 1, axis=0))
    carry_ref[...] = p[L - 1:L, :]
    z = p + (prev - p) * mu_ref[...]
    r, k, v = z[:, 0:rw], z[:, rw:2 * rw], z[:, 2 * rw:3 * rw]
    o = 3 * rw
    zw, za, zg = z[:, o:o + 128], z[:, o + 128:o + 256], z[:, o + 256:o + 512]
    w0, a0, k_k, k_a = vec_ref[0:1], vec_ref[1:2], vec_ref[2:3], vec_ref[3:4]
    lnx_g, lnx_b, r_k = vec_ref[4:5], vec_ref[5:6], vec_ref[6:7]
    seg = [seg_ref[...]]

    def seg_sum(t):
        return _mm(_split(t, 3), seg, order=2)

    def lora(t, w_ref):
        return _mm(_split(t), _split(w_ref[...]))

    w_log = -_softplus(-(w0 + lora(jnp.tanh(zw), w2_ref))) - 0.5
    lw = -jnp.exp(w_log)
    a = _sigmoid(a0 + lora(za, a2_ref))
    g = lora(_sigmoid(zg), g2_ref)
    kk = k * k_k
    kkn = kk / jnp.maximum(jnp.sqrt(seg_sum(kk * kk)), 1e-12)
    k2 = k * (1 + (a - 1) * k_a)
    an, bn = -kkn, kkn * a
    if t_valid < L:
        valid = row < t_valid
        lw = jnp.where(valid, lw, 0.0)
        an, bn = jnp.where(valid, an, 0.0), jnp.where(valid, bn, 0.0)
        k2s, vs = jnp.where(valid, k2, 0.0), jnp.where(valid, v, 0.0)
    else:
        k2s, vs = k2, v

    ri = lax.broadcasted_iota(I32, (L, L), 0)
    ci = lax.broadcasted_iota(I32, (L, L), 1)
    incl, strict = ri >= ci, ri > ci
    lw_terms = _split(lw, 3)
    cum = _mm([incl.astype(BF16)], lw_terms, order=2)
    c_last = cum[L - 1:L, :]
    e_c, e_cp, e_n, e_d = jnp.exp(cum), jnp.exp(cum - lw), jnp.exp(-cum), jnp.exp(c_last - cum)
    ops_ref[0] = an * e_cp
    ops_ref[1] = r * e_c
    ops_ref[2] = bn * e_n
    ops_ref[3] = k2s * e_n
    ops_ref[4] = bn * e_d
    ops_ref[5] = k2s * e_d
    ops_ref[6] = vs
    wl_col = jnp.exp(_mm(lw_terms, [jnp.ones((L, LANES), BF16)], _TN, order=2))
    eye = (ri == ci).astype(F32)
    n_sq = max(int(np.ceil(np.log2(L))) - 1, 0)

    hs = range(n_heads)
    sls = [slice(h * HEAD_DIM, (h + 1) * HEAD_DIM) for h in hs]
    at, rt, bt, kt, bh, kh, vh = [[_split(ops_ref[i, :, sl]) for sl in sls] for i in range(7)]
    m_ab = [jnp.where(strict, _mm(at[h], bt[h], _NT), 0.0) for h in hs]
    t_inv = [eye + m_ab[h] for h in hs]
    pw = m_ab
    for _ in range(n_sq):
        pw_t = [_split(pw[h]) for h in hs]
        pw = [_mm(pw_t[h], pw_t[h]) for h in hs]
        t_inv = [t_inv[h] + _mm(_split(t_inv[h]), _split(pw[h])) for h in hs]
    m_ak = [jnp.where(strict, _mm(at[h], kt[h], _NT), 0.0) for h in hs]
    m_rb = [jnp.where(incl, _mm(rt[h][:1], bt[h][:1], _NT), 0.0).astype(BF16) for h in hs]
    m_rk = [jnp.where(incl, _mm(rt[h][:1], kt[h][:1], _NT), 0.0).astype(BF16) for h in hs]
    zs = [z_ref[h] for h in hs]
    zs_t = [_split(zs[h]) for h in hs]
    rhs = [_mm(at[h], zs_t[h]) + _mm(_split(m_ak[h]), vh[h]) for h in hs]
    u_t = [_split(_mm(_split(t_inv[h]), _split(rhs[h]))) for h in hs]
    for h in hs:
        yh_ref[:, sls[h]] = (_mm(rt[h][:1], zs_t[h][:1]) + _mm([m_rb[h]], u_t[h][:1])
                             + _mm([m_rk[h]], vh[h][:1]))
    for h in hs:
        z_ref[h] = wl_col[sls[h], 0:HEAD_DIM] * zs[h] + _mm(bh[h], u_t[h], _TN) + _mm(kh[h], vh[h], _TN)

    y = yh_ref[...]
    inv_n = 1.0 / HEAD_DIM
    mean = seg_sum(y) * inv_n
    yc = y - mean
    var = seg_sum(yc * yc) * inv_n
    yn = yc * lax.rsqrt(var + LNX_EPS) * lnx_g + lnx_b
    bonus = seg_sum(r * k2 * r_k) * v
    y_ref[0] = ((yn + bonus) * g).astype(y_ref.dtype)

    @pl.when(c == pl.num_programs(1) - 1)
    def _():
        zout_ref[0] = z_ref[...]


def _rwkv(p3, shift0, z0, mu, vec, w2, a2, g2, seg, *, L, t_valid, rw):
    b, t, n_rw = p3.shape
    n_heads = rw // HEAD_DIM
    const2 = lambda i, c: (0, 0)
    return pl.pallas_call(
        functools.partial(_rwkv_kernel, L=L, t_valid=t_valid, rw=rw, n_heads=n_heads),
        grid=(b, t // L),
        in_specs=[pl.BlockSpec((1, L, n_rw), lambda i, c: (i, c, 0)),
                  pl.BlockSpec((1, 1, n_rw), lambda i, c: (i, 0, 0)),
                  pl.BlockSpec((1, n_heads, HEAD_DIM, HEAD_DIM), lambda i, c: (i, 0, 0, 0)),
                  pl.BlockSpec((1, n_rw), const2),
                  pl.BlockSpec(vec.shape, const2),
                  pl.BlockSpec(w2.shape, const2),
                  pl.BlockSpec(a2.shape, const2),
                  pl.BlockSpec(g2.shape, const2),
                  pl.BlockSpec(seg.shape, const2)],
        out_specs=[pl.BlockSpec((1, L, rw), lambda i, c: (i, c, 0)),
                   pl.BlockSpec((1, n_heads, HEAD_DIM, HEAD_DIM), lambda i, c: (i, 0, 0, 0))],
        out_shape=[jax.ShapeDtypeStruct((b, t, rw), BF16),
                   jax.ShapeDtypeStruct((b, n_heads, HEAD_DIM, HEAD_DIM), F32)],
        scratch_shapes=[pltpu.VMEM((1, n_rw), F32),
                        pltpu.VMEM((n_heads, HEAD_DIM, HEAD_DIM), F32),
                        pltpu.VMEM((7, L, rw), F32),
                        pltpu.VMEM((L, rw), F32)],
        compiler_params=_params("arbitrary", "arbitrary"),
        name="rwkv_scan",
    )(p3, shift0, z0, mu, vec, w2, a2, g2, seg)


def _key_to_float(key):
    bits = key ^ ((key >> 31) & 0x7FFFFFFF)
    return lax.bitcast_convert_type(bits, F32)


def _fold_lanes(m):
    acc = m[:, 0:LANES]
    for j in range(1, m.shape[1] // LANES):
        acc = acc + m[:, j * LANES:(j + 1) * LANES]
    return acc


def _select_bias(sc_ref, pb_ref, nc, rows, tk, k_top, n_keys_total):
    def count(pred):
        def body(c, acc):
            return acc + _fold_lanes(pred(c, sc_ref[c]).astype(F32))
        acc = lax.fori_loop(0, nc, body, jnp.zeros((rows, LANES), F32))
        return jnp.sum(acc, axis=-1, keepdims=True)

    kf = float(k_top)
    cnt0 = count(lambda c, s: s >= 0.0)
    tau0 = jnp.where(cnt0 >= kf, 0, INT_MIN).astype(I32)

    def bit_body(i, tau):
        cand = tau + jnp.left_shift(jnp.int32(1), 30 - i)
        cand_f = _key_to_float(cand)
        cnt = count(lambda c, s: s >= cand_f)
        return jnp.where((cnt >= kf) | (cand <= KEY_NEG_INF), cand, tau)

    tau = lax.fori_loop(0, 31, bit_body, tau0)
    tau_f = _key_to_float(tau)
    need = kf - count(lambda c, s: s > tau_f)
    excess = count(lambda c, s: s == tau_f) - need

    def kidx(c):
        return c * tk + lax.broadcasted_iota(I32, (1, tk), 1)

    n_bits = max(int(np.ceil(np.log2(n_keys_total))), 1)
    pb_ref[...] = jnp.full((rows, 1), n_keys_total, I32)

    @pl.when(jnp.max(excess) > 0.0)
    def _():
        def tie_body(i, pb):
            cand = pb + jnp.left_shift(jnp.int32(1), n_bits - 1 - i)
            cnt = count(lambda c, s: (s == tau_f) & (kidx(c) < cand))
            return jnp.where(cnt < need, cand, pb)

        pb_ref[...] = lax.fori_loop(0, n_bits, tie_body, jnp.zeros((rows, 1), I32))

    pb = pb_ref[...]

    def write(c, _):
        s = sc_ref[c]
        sel = ((s > tau_f) | ((s == tau_f) & (kidx(c) <= pb))) & (s > -jnp.inf)
        sc_ref[c] = jnp.where(sel, 0.0, NEG_BIG)
        return 0

    lax.fori_loop(0, nc, write, 0)


def _dsa_prompt_kernel(q_ref, qi_ref, kw_ref, kb_ref, vb_ref, kib_ref, o_ref, sc_ref, pb_ref, m_ref, l_ref, acc_ref,
                       *, tq, tk, k_top, n_heads, n_idx, t_total):
    qb = pl.program_id(1)
    q0 = qb * tq
    nc = (q0 + tq + tk - 1) // tk
    qpos = q0 + lax.broadcasted_iota(I32, (tq, 1), 0)

    def scores(c, _):
        kic = kib_ref[0, pl.ds(pl.multiple_of(c * tk, tk), tk), :]
        acc = jnp.zeros((tq, tk), F32)
        for h in range(n_idx):
            d = _dot_nt(qi_ref[0, :, h * HEAD_DIM:(h + 1) * HEAD_DIM], kic)
            acc = acc + jnp.maximum(d, 0.0) * kw_ref[0, :, HEAD_DIM + h:HEAD_DIM + h + 1]
        spos = c * tk + lax.broadcasted_iota(I32, (1, tk), 1)
        sc_ref[c] = jnp.where(spos <= qpos, acc, -jnp.inf)
        return 0

    lax.fori_loop(0, nc, scores, 0)
    _select_bias(sc_ref, pb_ref, nc, tq, tk, k_top, t_total)

    m_ref[...] = jnp.full(m_ref.shape, NEG_BIG, F32)
    l_ref[...] = jnp.zeros(l_ref.shape, F32)
    acc_ref[...] = jnp.zeros(acc_ref.shape, F32)

    def attend(c, _):
        rows = pl.ds(pl.multiple_of(c * tk, tk), tk)
        bias = sc_ref[c]
        for h in range(n_heads):
            sl = slice(h * HEAD_DIM, (h + 1) * HEAD_DIM)
            s = _dot_nt(q_ref[0, :, sl], kb_ref[0, rows, sl]) + bias
            m = m_ref[h]
            m_new = jnp.maximum(m, jnp.max(s, axis=-1, keepdims=True))
            alpha = jnp.exp(m - m_new)
            p = jnp.exp(s - m_new)
            l_ref[h] = alpha * l_ref[h] + jnp.sum(p, axis=-1, keepdims=True)
            acc_ref[h] = alpha * acc_ref[h] + _dot(p.astype(BF16), vb_ref[0, rows, sl])
            m_ref[h] = m_new
        return 0

    lax.fori_loop(0, nc, attend, 0)
    for h in range(n_heads):
        o_ref[0, :, h * HEAD_DIM:(h + 1) * HEAD_DIM] = (acc_ref[h] / l_ref[h]).astype(o_ref.dtype)


def _dsa_prompt(q, qi, kw, kb, vb, kib, *, tq, tk, k_top):
    b, t, att_w = q.shape
    n_heads = att_w // HEAD_DIM
    n_idx = qi.shape[2] // HEAD_DIM
    qblk = lambda i, j: (i, j, 0)
    full = lambda i, j: (i, 0, 0)
    return pl.pallas_call(
        functools.partial(_dsa_prompt_kernel, tq=tq, tk=tk, k_top=k_top, n_heads=n_heads, n_idx=n_idx,
                          t_total=t),
        grid=(b, t // tq),
        in_specs=[pl.BlockSpec((1, tq, att_w), qblk),
                  pl.BlockSpec((1, tq, qi.shape[2]), qblk),
                  pl.BlockSpec((1, tq, LANES), qblk),
                  pl.BlockSpec((1, t, att_w), full),
                  pl.BlockSpec((1, t, att_w), full),
                  pl.BlockSpec((1, t, HEAD_DIM), full)],
        out_specs=pl.BlockSpec((1, tq, att_w), qblk),
        out_shape=jax.ShapeDtypeStruct((b, t, att_w), BF16),
        scratch_shapes=[pltpu.VMEM((t // tk, tq, tk), F32),
                        pltpu.VMEM((tq, 1), I32),
                        pltpu.VMEM((n_heads, tq, 1), F32),
                        pltpu.VMEM((n_heads, tq, 1), F32),
                        pltpu.VMEM((n_heads, tq, HEAD_DIM), F32)],
        compiler_params=_params("arbitrary", "arbitrary"),
        name="dsa_prompt",
    )(q, qi, kw, kb, vb, kib)


def _idx_page_scores(qim, wcol, ki_t, n_idx):
    d = jnp.maximum(_dot(qim, ki_t.astype(BF16)), 0.0) * wcol
    sc = d[0:SUBLANES]
    for h in range(1, n_idx):
        sc = sc + d[h * SUBLANES:(h + 1) * SUBLANES]
    return sc


def _idx_sample_kernel(pt_ref, qim_ref, wcol_ref, *refs, t_new, n_idx, g_pages):
    page_refs, o_ref = refs[:g_pages], refs[g_pages]
    qrow = lax.broadcasted_iota(I32, (SUBLANES, PAGE_SIZE), 0)
    for j in range(g_pages):
        sc = _idx_page_scores(qim_ref[0], wcol_ref[0], page_refs[j][0, 0], n_idx)
        o_ref[j] = jnp.where(qrow < t_new, sc, -jnp.inf)


def _idx_sample(page_table, qim, wcol, cache_idx_t, *, t_new, g_pages, layer):
    b, n_pages = page_table.shape
    n_idx = qim.shape[1] // SUBLANES
    per_seq = lambda i, g, pt: (i, 0, 0)
    page_specs = [pl.BlockSpec((1, 1, HEAD_DIM, PAGE_SIZE),
                               lambda i, g, pt, j=j: (layer, pt[i, g * g_pages + j], 0, 0))
                  for j in range(g_pages)]
    grid_spec = pltpu.PrefetchScalarGridSpec(
        num_scalar_prefetch=1,
        grid=(b, n_pages // g_pages),
        in_specs=[pl.BlockSpec((1,) + qim.shape[1:], per_seq),
                  pl.BlockSpec((1,) + wcol.shape[1:], per_seq)] + page_specs,
        out_specs=pl.BlockSpec((g_pages, SUBLANES, PAGE_SIZE), lambda i, g, pt: (g, i, 0)),
    )
    return pl.pallas_call(
        functools.partial(_idx_sample_kernel, t_new=t_new, n_idx=n_idx, g_pages=g_pages),
        grid_spec=grid_spec,
        out_shape=jax.ShapeDtypeStruct((n_pages, b * SUBLANES, PAGE_SIZE), F32),
        compiler_params=_params("arbitrary", "arbitrary"),
        name="idx_sample",
    )(page_table, qim, wcol, *([cache_idx_t] * g_pages))


def _idx_new_kernel(qim_ref, wcol_ref, new_ref, o_ref, *, t_new, n_idx):
    sc = _idx_page_scores(qim_ref[0], wcol_ref[0], new_ref[0], n_idx)
    qrow = lax.broadcasted_iota(I32, (SUBLANES, PAGE_SIZE), 0)
    lane = lax.broadcasted_iota(I32, (SUBLANES, PAGE_SIZE), 1)
    o_ref[...] = jnp.where((qrow < t_new) & (lane <= qrow), sc, -jnp.inf)


def _idx_new(qim, wcol, ki_new, *, t_new):
    b = qim.shape[0]
    n_idx = qim.shape[1] // SUBLANES
    per_seq = lambda i: (i, 0, 0)
    return pl.pallas_call(
        functools.partial(_idx_new_kernel, t_new=t_new, n_idx=n_idx),
        grid=(b,),
        in_specs=[pl.BlockSpec((1,) + qim.shape[1:], per_seq),
                  pl.BlockSpec((1,) + wcol.shape[1:], per_seq),
                  pl.BlockSpec((1, HEAD_DIM, PAGE_SIZE), per_seq)],
        out_specs=pl.BlockSpec((SUBLANES, PAGE_SIZE), lambda i: (i, 0)),
        out_shape=jax.ShapeDtypeStruct((b * SUBLANES, PAGE_SIZE), F32),
        compiler_params=_params("arbitrary"),
        name="idx_new",
    )(qim, wcol, ki_new)


def _select_sample_kernel(past_ref, new_ref, o_ref, pb_ref, *, k_top):
    n_pages, rows = past_ref.shape[0], past_ref.shape[1]
    o_ref[0:n_pages] = past_ref[...]
    o_ref[n_pages] = new_ref[...]
    _select_bias(o_ref, pb_ref, n_pages + 1, rows, PAGE_SIZE, k_top, (n_pages + 1) * PAGE_SIZE)


def _select_sample(past_scores, new_scores, *, k_top):
    n_pages, rows, _ = past_scores.shape
    out_shape = (n_pages + 1, rows, PAGE_SIZE)
    return pl.pallas_call(
        functools.partial(_select_sample_kernel, k_top=k_top),
        grid=(1,),
        in_specs=[pl.BlockSpec(past_scores.shape, lambda i: (0, 0, 0)),
                  pl.BlockSpec(new_scores.shape, lambda i: (0, 0))],
        out_specs=pl.BlockSpec(out_shape, lambda i: (0, 0, 0)),
        out_shape=jax.ShapeDtypeStruct(out_shape, F32),
        scratch_shapes=[pltpu.VMEM((rows, 1), I32)],
        compiler_params=_params("arbitrary"),
        name="select_sample",
    )(past_scores, new_scores)


def _attn_sample_kernel(pt_ref, qbd_ref, bias_ref, biasn_ref, knew_ref, vnew_ref, *refs, n_heads, g_pages):
    k_refs, v_refs = refs[:g_pages], refs[g_pages:2 * g_pages]
    o_ref, m_ref, l_ref, acc_ref = refs[2 * g_pages:]
    g = pl.program_id(1)

    @pl.when(g == 0)
    def _():
        m_ref[...] = jnp.full(m_ref.shape, NEG_BIG, F32)
        l_ref[...] = jnp.zeros(l_ref.shape, F32)
        acc_ref[...] = jnp.zeros(acc_ref.shape, F32)

    def step(kts, vts, biases):
        bias = jnp.concatenate([jnp.concatenate([b_] * n_heads, axis=0) for b_ in biases], axis=1)
        s = jnp.concatenate([_dot(qbd_ref[0], kt.astype(BF16)) for kt in kts], axis=1) + bias
        m = m_ref[...]
        m_new = jnp.maximum(m, jnp.max(s, axis=-1, keepdims=True))
        alpha = jnp.exp(m - m_new)
        pr = jnp.exp(s - m_new)
        prb = pr.astype(BF16)
        pv = None
        for j, vt in enumerate(vts):
            d = _dot_nt(prb[:, j * PAGE_SIZE:(j + 1) * PAGE_SIZE], vt.astype(BF16))
            pv = d if pv is None else pv + d
        l_ref[...] = alpha * l_ref[...] + jnp.sum(pr, axis=-1, keepdims=True)
        acc_ref[...] = alpha * acc_ref[...] + pv
        m_ref[...] = m_new

    step([r[0, 0] for r in k_refs], [r[0, 0] for r in v_refs], [bias_ref[j] for j in range(g_pages)])

    @pl.when(g == pl.num_programs(1) - 1)
    def _():
        step([knew_ref[0]], [vnew_ref[0]], [biasn_ref[0]])
        full = acc_ref[...] / l_ref[...]
        col_head = lax.broadcasted_iota(I32, (SUBLANES, full.shape[1]), 1) // HEAD_DIM
        out = jnp.zeros((SUBLANES, full.shape[1]), F32)
        for h in range(n_heads):
            out = out + jnp.where(col_head == h, full[h * SUBLANES:(h + 1) * SUBLANES], 0.0)
        o_ref[0] = out.astype(o_ref.dtype)


def _attn_sample(page_table, qbd, bias, cache_k_t, cache_v_t, k_new_t, v_new_t, *, g_pages, layer):
    b, n_pages = page_table.shape
    att_w = qbd.shape[2]
    n_heads = att_w // HEAD_DIM
    per_seq = lambda i, g, pt: (i, 0, 0)
    page_specs = [pl.BlockSpec((1, 1, att_w, PAGE_SIZE),
                               lambda i, g, pt, j=j: (layer, pt[i, g * g_pages + j], 0, 0))
                  for j in range(g_pages)]
    new_spec = pl.BlockSpec((1, att_w, PAGE_SIZE), per_seq)
    grid_spec = pltpu.PrefetchScalarGridSpec(
        num_scalar_prefetch=1,
        grid=(b, n_pages // g_pages),
        in_specs=[pl.BlockSpec((1,) + qbd.shape[1:], per_seq),
                  pl.BlockSpec((g_pages, SUBLANES, PAGE_SIZE), lambda i, g, pt: (g, i, 0)),
                  pl.BlockSpec((1, SUBLANES, PAGE_SIZE), lambda i, g, pt: (n_pages, i, 0)),
                  new_spec, new_spec] + page_specs + page_specs,
        out_specs=pl.BlockSpec((1, SUBLANES, att_w), per_seq),
        scratch_shapes=[pltpu.VMEM((n_heads * SUBLANES, 1), F32),
                        pltpu.VMEM((n_heads * SUBLANES, 1), F32),
                        pltpu.VMEM((n_heads * SUBLANES, att_w), F32)],
    )
    return pl.pallas_call(
        functools.partial(_attn_sample_kernel, n_heads=n_heads, g_pages=g_pages),
        grid_spec=grid_spec,
        out_shape=jax.ShapeDtypeStruct((b, SUBLANES, att_w), BF16),
        compiler_params=_params("arbitrary", "arbitrary"),
        name="attn_sample",
    )(page_table, qbd, bias, bias, k_new_t, v_new_t, *([cache_k_t] * g_pages), *([cache_v_t] * g_pages))


def _outproj_kernel(x_ref, yr_ref, ya_ref, mod_ref, wo_ref, g_ref, rw_ref, rb_ref, x1_ref, h2_ref, gates_ref, *, rw):
    mixed = _dot(yr_ref[...], wo_ref[0:rw, :]) + _dot(ya_ref[...], wo_ref[rw:, :])
    x1 = x_ref[...] + mod_ref[2, 0] * mixed
    x1_ref[...] = x1
    h2 = _rms(x1, g_ref[...]) * (1 + mod_ref[4, 0]) + mod_ref[3, 0]
    h2_ref[...] = h2.astype(BF16)
    logits = _dot(h2, rw_ref[...], HI) + rb_ref[...]
    lane = lax.broadcasted_iota(I32, logits.shape, 1).astype(F32)
    work = logits
    picked = jnp.zeros(logits.shape, jnp.bool_)
    top = None
    for i in range(TOP_K):
        mx = jnp.max(work, axis=-1, keepdims=True)
        first = jnp.min(jnp.where(work == mx, lane, float(LANES)), axis=-1, keepdims=True)
        hit = lane == first
        picked = picked | hit
        work = jnp.where(hit, -jnp.inf, work)
        if i == 0:
            top = mx
    e = jnp.where(picked, jnp.exp(logits - top), 0.0)
    gates_ref[...] = e / jnp.sum(e, axis=-1, keepdims=True)


def _outproj(x2d, yr, ya, mod6, w_out_b, g2n, router_w_pad, router_b_pad, *, tm, tiles_per_seq, rw):
    n, d = x2d.shape
    rmod = mod6.shape[2]
    row = lambda i: (i, 0)
    const = lambda i: (0, 0)
    outs = [jax.ShapeDtypeStruct((n, d), F32), jax.ShapeDtypeStruct((n, d), BF16),
            jax.ShapeDtypeStruct((n, LANES), F32)]
    return pl.pallas_call(
        functools.partial(_outproj_kernel, rw=rw),
        grid=(n // tm,),
        in_specs=[pl.BlockSpec((tm, d), row),
                  pl.BlockSpec((tm, yr.shape[1]), row),
                  pl.BlockSpec((tm, ya.shape[1]), row),
                  pl.BlockSpec((6, 1, rmod, d), lambda i: (0, i // tiles_per_seq, 0, 0)),
                  pl.BlockSpec(w_out_b.shape, const),
                  pl.BlockSpec((1, d), const),
                  pl.BlockSpec(router_w_pad.shape, const),
                  pl.BlockSpec((1, LANES), const)],
        out_specs=[pl.BlockSpec((tm, o.shape[1]), row) for o in outs],
        out_shape=outs,
        compiler_params=_params("arbitrary"),
        name="outproj_router",
    )(x2d, yr, ya, mod6, w_out_b, g2n, router_w_pad, router_b_pad)


ROUTE_TILE = 512
ROUTE_WIN = 128
ROUTE_ALIGN = 16
FFN_ROWS = 256


def _route_meta_kernel(g_ref, rank_ref, gate_ref, cnt_ref, *, n_exp):
    gt = jnp.transpose(g_ref[...])
    sel = gt > 0.0
    tt = gt.shape[1]
    earlier = (lax.broadcasted_iota(I32, (tt, tt), 0) < lax.broadcasted_iota(I32, (tt, tt), 1)).astype(BF16)
    rank = _dot(sel.astype(BF16), earlier)
    rank_ref[...] = jnp.where(sel, rank, -1.0)[:n_exp]
    gate_ref[...] = gt[:n_exp]
    cnt_ref[0] = jnp.broadcast_to(jnp.sum(sel.astype(F32), axis=-1, keepdims=True), (LANES, LANES))


def _route_meta(gates, *, n_exp):
    n = gates.shape[0]
    n_tiles = n // ROUTE_TILE
    return pl.pallas_call(
        functools.partial(_route_meta_kernel, n_exp=n_exp),
        grid=(n_tiles,),
        in_specs=[pl.BlockSpec((ROUTE_TILE, LANES), lambda i: (i, 0))],
        out_specs=[pl.BlockSpec((n_exp, ROUTE_TILE), lambda i: (0, i)),
                   pl.BlockSpec((n_exp, ROUTE_TILE), lambda i: (0, i)),
                   pl.BlockSpec((1, LANES, LANES), lambda i: (i, 0, 0))],
        out_shape=[jax.ShapeDtypeStruct((n_exp, n), F32), jax.ShapeDtypeStruct((n_exp, n), F32),
                   jax.ShapeDtypeStruct((n_tiles, LANES, LANES), F32)],
        compiler_params=_params("arbitrary"),
        name="route_meta",
    )(gates)


def _window_onehot(rank_row, w):
    rows = lax.broadcasted_iota(I32, (ROUTE_WIN, 1), 0).astype(F32) + float(ROUTE_WIN) * w
    return rank_row == rows


def _dispatch_kernel(base_ref, cnt_ref, h_ref, rank_ref, gate_ref, xs_in, gs_in, xs_out, gs_out, p_ref, sx_ref,
                     sg_ref, sem_x, sem_g, *, n_exp, group):
    del xs_in, gs_in
    i = pl.program_id(0)
    h = h_ref[...]
    win = ROUTE_WIN

    def copies(slot, row0):
        rows = pl.ds(pl.multiple_of(row0, ROUTE_ALIGN), win)
        return (pltpu.make_async_copy(sx_ref.at[slot], xs_out.at[rows], sem_x.at[slot]),
                pltpu.make_async_copy(sg_ref.at[slot], gs_out.at[rows], sem_g.at[slot]))

    def fill(slot, e, w):
        p = _window_onehot(rank_ref[e:e + 1, :], w)
        p_ref[slot * win:(slot + 1) * win, :] = p.astype(BF16)
        g = jnp.sum(jnp.where(p, gate_ref[e:e + 1, :], 0.0), axis=-1, keepdims=True)
        sg_ref[slot] = jnp.broadcast_to(g, (win, LANES))

    for e0 in range(0, n_exp, group):
        for j in range(group):
            fill(e0 + j, e0 + j, 0)
        x = _dot(p_ref[e0 * win:(e0 + group) * win, :], h)
        for j in range(group):
            e = e0 + j
            sx_ref[e] = x[j * win:(j + 1) * win].astype(BF16)
            for c in copies(e, base_ref[i, e]):
                c.start()
    for e in range(n_exp):
        for c in copies(e, base_ref[i, e]):
            c.wait()

    for e in range(n_exp):
        def extra(w, _):
            fill(0, e, w)
            sx_ref[0] = _dot(p_ref[0:win, :], h).astype(BF16)
            cs = copies(0, base_ref[i, e] + w * win)
            for c in cs:
                c.start()
            for c in cs:
                c.wait()
            return 0

        lax.fori_loop(1, (cnt_ref[i, e] + win - 1) // win, extra, 0)


def _dispatch(base, cnt, h_all, rank_t, gate_t, n_rows):
    n, d = h_all.shape
    n_exp = rank_t.shape[0]
    n_tiles = n // ROUTE_TILE
    tile = lambda i, b_, c_: (0, i)
    grid_spec = pltpu.PrefetchScalarGridSpec(
        num_scalar_prefetch=2,
        grid=(n_tiles,),
        in_specs=[pl.BlockSpec((ROUTE_TILE, d), lambda i, b_, c_: (i, 0)),
                  pl.BlockSpec((n_exp, ROUTE_TILE), tile),
                  pl.BlockSpec((n_exp, ROUTE_TILE), tile),
                  pl.BlockSpec(memory_space=pl.ANY),
                  pl.BlockSpec(memory_space=pl.ANY)],
        out_specs=[pl.BlockSpec(memory_space=pl.ANY), pl.BlockSpec(memory_space=pl.ANY)],
        scratch_shapes=[pltpu.VMEM((n_exp * ROUTE_WIN, ROUTE_TILE), BF16),
                        pltpu.VMEM((n_exp, ROUTE_WIN, d), BF16),
                        pltpu.VMEM((n_exp, ROUTE_WIN, LANES), F32),
                        pltpu.SemaphoreType.DMA((n_exp,)),
                        pltpu.SemaphoreType.DMA((n_exp,))],
    )
    xs0 = jnp.zeros((n_rows, d), BF16)
    gs0 = jnp.zeros((n_rows, LANES), F32)
    return pl.pallas_call(
        functools.partial(_dispatch_kernel, n_exp=n_exp, group=8),
        grid_spec=grid_spec,
        out_shape=[jax.ShapeDtypeStruct(xs0.shape, BF16), jax.ShapeDtypeStruct(gs0.shape, F32)],
        input_output_aliases={5: 0, 6: 1},
        compiler_params=_params("arbitrary"),
        name="moe_dispatch",
    )(base, cnt, h_all, rank_t, gate_t, xs0, gs0)


def _ffn_sorted_kernel(te_ref, nu_ref, xs_ref, gs_ref, wgu_ref, bgu_ref, wd_ref, bd_ref, ys_ref, wgu_b, wd_b, *,
                       d_ff):
    r = pl.program_id(0)

    @pl.when(r < nu_ref[0])
    def _():
        @pl.when((r == 0) | (te_ref[r] != te_ref[jnp.maximum(r - 1, 0)]))
        def _():
            wgu_b[...] = wgu_ref[0].astype(BF16)
            wd_b[...] = wd_ref[0].astype(BF16)

        gu = _dot(xs_ref[...], wgu_b[...]) + bgu_ref[0]
        gate = jnp.minimum(gu[:, :d_ff], SWIGLU_LIMIT)
        up = jnp.clip(gu[:, d_ff:], -SWIGLU_LIMIT, SWIGLU_LIMIT)
        act = (up + 1) * gate * _sigmoid(SWIGLU_ALPHA * gate)
        y = _dot(act.astype(BF16), wd_b[...]) + bd_ref[0]
        ys_ref[...] = (gs_ref[:, 0:1] * y).astype(ys_ref.dtype)

    @pl.when(r >= nu_ref[0])
    def _():
        ys_ref[...] = jnp.zeros(ys_ref.shape, ys_ref.dtype)


def _ffn_sorted(tile_expert, n_used, xs, gs, wgu, bgu, wd, bd):
    n_rows, d = xs.shape
    n_exp, _, two_ff = wgu.shape
    row = lambda r, te, nu: (r, 0)
    exp3 = lambda r, te, nu: (te[r], 0, 0)
    grid_spec = pltpu.PrefetchScalarGridSpec(
        num_scalar_prefetch=2,
        grid=(n_rows // FFN_ROWS,),
        in_specs=[pl.BlockSpec((FFN_ROWS, d), row),
                  pl.BlockSpec((FFN_ROWS, LANES), row),
                  pl.BlockSpec((1, d, two_ff), exp3),
                  pl.BlockSpec((1, 1, two_ff), exp3),
                  pl.BlockSpec((1, two_ff // 2, d), exp3),
                  pl.BlockSpec((1, 1, d), exp3)],
        out_specs=pl.BlockSpec((FFN_ROWS, d), row),
        scratch_shapes=[pltpu.VMEM((d, two_ff), BF16), pltpu.VMEM((two_ff // 2, d), BF16)],
    )
    return pl.pallas_call(
        functools.partial(_ffn_sorted_kernel, d_ff=two_ff // 2),
        grid_spec=grid_spec,
        out_shape=jax.ShapeDtypeStruct((n_rows, d), BF16),
        compiler_params=_params("arbitrary"),
        name="moe_ffn_sorted",
    )(tile_expert, n_used, xs, gs, wgu, bgu, wd, bd)


def _combine_kernel(base_ref, cnt_ref, rank_ref, x1_ref, mod_ref, fg_ref, ys_hbm, o_ref, p_ref, yb_ref, sem, *,
                    n_exp, tile0):
    i = pl.program_id(0) + tile0
    win = ROUTE_WIN

    def copy(slot, row0):
        rows = pl.ds(pl.multiple_of(row0, ROUTE_ALIGN), win)
        return pltpu.make_async_copy(ys_hbm.at[rows], yb_ref.at[pl.ds(slot * win, win)], sem.at[slot])

    for e in range(n_exp):
        copy(e, base_ref[i, e]).start()
    for e in range(n_exp):
        p_ref[e * win:(e + 1) * win, :] = _window_onehot(rank_ref[e:e + 1, :], 0).astype(BF16)
    for e in range(n_exp):
        copy(e, base_ref[i, e]).wait()
    acc = _dot_tn(p_ref[...], yb_ref[...])

    for e in range(n_exp):
        def extra(w, acc):
            c = copy(0, base_ref[i, e] + w * win)
            c.start()
            p = _window_onehot(rank_ref[e:e + 1, :], w).astype(BF16)
            c.wait()
            return acc + _dot_tn(p, yb_ref[0:win, :])

        acc = lax.fori_loop(1, (cnt_ref[i, e] + win - 1) // win, extra, acc)

    x2 = x1_ref[...] + mod_ref[5, 0] * acc
    o_ref[...] = _rms(x2, fg_ref[...])


def _combine(base, cnt, rank_t, x1, mod6, fg, ys, *, tile0, tiles_per_seq):
    n, d = x1.shape
    n_exp = rank_t.shape[0]
    rmod = mod6.shape[2]
    grid_spec = pltpu.PrefetchScalarGridSpec(
        num_scalar_prefetch=2,
        grid=(n // ROUTE_TILE,),
        in_specs=[pl.BlockSpec((n_exp, ROUTE_TILE), lambda i, b_, c_: (0, i + tile0)),
                  pl.BlockSpec((ROUTE_TILE, d), lambda i, b_, c_: (i, 0)),
                  pl.BlockSpec((6, 1, rmod, d), lambda i, b_, c_: (0, i // tiles_per_seq, 0, 0)),
                  pl.BlockSpec((1, d), lambda i, b_, c_: (0, 0)),
                  pl.BlockSpec(memory_space=pl.ANY)],
        out_specs=pl.BlockSpec((ROUTE_TILE, d), lambda i, b_, c_: (i, 0)),
        scratch_shapes=[pltpu.VMEM((n_exp * ROUTE_WIN, ROUTE_TILE), BF16),
                        pltpu.VMEM((n_exp * ROUTE_WIN, d), BF16),
                        pltpu.SemaphoreType.DMA((n_exp,))],
    )
    return pl.pallas_call(
        functools.partial(_combine_kernel, n_exp=n_exp, tile0=tile0),
        grid_spec=grid_spec,
        out_shape=jax.ShapeDtypeStruct((n, d), F32),
        compiler_params=_params("arbitrary"),
        name="moe_combine",
    )(base, cnt, rank_t, x1, mod6, fg, ys)


def _round_up(a, m):
    return (a + m - 1) // m * m


def _moe_routed(pre_p, pre_s, wts):
    x1_p, h2_p, gates_p, mod6_p, tps_p = pre_p
    x1_s, h2_s, gates_s, mod6_s, _ = pre_s
    n_p, d = x1_p.shape
    n_s = x1_s.shape[0]
    n_exp = wts["bgu"].shape[0]
    assert n_p % ROUTE_TILE == 0 and n_s <= ROUTE_TILE and mod6_s.shape[2] == n_s
    pad_rows = lambda a: jnp.pad(a, ((0, ROUTE_TILE - n_s), (0, 0)))
    h_all = jnp.concatenate([h2_p, pad_rows(h2_s)], axis=0)
    g_all = jnp.concatenate([gates_p, pad_rows(gates_s)], axis=0)
    n_all = h_all.shape[0]
    n_tiles = n_all // ROUTE_TILE
    rank_t, gate_t, cnt_f = _route_meta(g_all, n_exp=n_exp)

    cnt = cnt_f[:, :n_exp, 0].astype(I32)
    seg = _round_up(cnt, ROUTE_ALIGN)
    seg_off = jnp.cumsum(seg, axis=0) - seg
    region = _round_up(jnp.sum(seg, axis=0) + ROUTE_WIN, FFN_ROWS)
    region_end = jnp.cumsum(region)
    base = (region_end - region)[None, :] + seg_off
    n_rows = _round_up(TOP_K * n_all + (ROUTE_ALIGN - 1) * n_tiles * n_exp + (ROUTE_WIN + FFN_ROWS) * n_exp, FFN_ROWS)
    tile_start = jnp.arange(n_rows // FFN_ROWS, dtype=I32) * FFN_ROWS
    tile_expert = jnp.minimum(jnp.searchsorted(region_end, tile_start, side="right"), n_exp - 1).astype(I32)
    n_used = (region_end[-1:] // FFN_ROWS).astype(I32)

    xs, gs = _dispatch(base, cnt, h_all, rank_t, gate_t, n_rows)
    ys = _ffn_sorted(tile_expert, n_used, xs, gs, wts["wgu32"], wts["bgu"], wts["wd32"], wts["bd"])
    y_p = _combine(base, cnt, rank_t, x1_p, mod6_p, wts["fg"], ys, tile0=0, tiles_per_seq=tps_p)
    mod6_sp = jnp.pad(mod6_s, ((0, 0), (0, 0), (0, ROUTE_TILE - n_s), (0, 0)))
    y_s = _combine(base, cnt, rank_t, pad_rows(x1_s), mod6_sp, wts["fg"], ys, tile0=n_p // ROUTE_TILE,
                   tiles_per_seq=1)
    return y_p, y_s[:n_s]


def _pad_cols(a, width):
    return jnp.pad(a, ((0, 0), (0, width - a.shape[1])))


def _rope_tables(pos):
    half = HEAD_DIM // 2
    inv = ROPE_THETA ** (-jnp.arange(half, dtype=F32) / half)
    ang = pos.astype(F32)[:, None] * inv[None, :]
    cos, sin = jnp.cos(ang), jnp.sin(ang)
    cos_h = jnp.concatenate([cos, cos], axis=1)
    sin_h = jnp.concatenate([-sin, sin], axis=1)
    ones = jnp.ones_like(cos_h)
    zeros = jnp.zeros_like(cos_h)
    tabs = jnp.stack([jnp.concatenate([cos_h, cos_h], 1), jnp.concatenate([sin_h, sin_h], 1),
                      jnp.concatenate([cos_h, ones], 1), jnp.concatenate([sin_h, zeros], 1)])
    return tabs


def _layer(x, mod_rows, pos, shift_prev, wkv_prev, wts, attend, *, tm, chunk):
    b, t, d = x.shape
    n = b * t
    rw, att_w, n_rw, rwkv_in = wts["rw"], wts["att_w"], wts["n_rw"], wts["rwkv_in"]
    per_token_mod = mod_rows.shape[0] != b or tm > t
    if per_token_mod:
        mod6 = jnp.repeat(mod_rows, t, axis=0).reshape(n // tm, tm, 6, d).transpose(2, 0, 1, 3)
        tiles_per_seq = 1
    else:
        mod6 = mod_rows.reshape(b, 1, 6, d).transpose(2, 0, 1, 3)
        tiles_per_seq = t // tm
    cs = _rope_tables(pos)
    n_idx_scale = wts["idx_scale"]
    cs = cs.at[2, :, HEAD_DIM:].set(n_idx_scale)
    if cs.shape[1] < tm:
        cs = jnp.tile(cs, (1, tm // cs.shape[1], 1))
    x2d = x.reshape(n, d)
    p, q, k32, kb, v32, vb, qi, kw = _inproj(x2d, mod6, wts["g1"], wts["w_cat"], cs, tm=tm,
                                             tiles_per_seq=tiles_per_seq, n_rw=n_rw, att_w=att_w)
    t_pad = -(-t // chunk) * chunk
    p3 = p.reshape(b, t, n_rw)
    if t_pad != t:
        p3 = jnp.pad(p3, ((0, 0), (0, t_pad - t), (0, 0)))
    z0 = jnp.swapaxes(wkv_prev, -1, -2)
    yr, z_fin = _rwkv(p3, shift_prev.reshape(b, 1, n_rw), z0, wts["mu"], wts["vec"], wts["w2"], wts["a2"], wts["g2"],
                      wts["seg"], L=chunk, t_valid=t if t_pad != t else chunk, rw=rw)
    yr = yr[:, :t].reshape(n, rw)
    wkv_new = jnp.swapaxes(z_fin, -1, -2)
    ya = attend(q, qi, kw, k32, kb, v32, vb).reshape(n, att_w)
    x1, h2, gates = _outproj(x2d, yr, ya, mod6, wts["w_out"], wts["g2n"], wts["router_w"], wts["router_b"], tm=tm,
                             tiles_per_seq=tiles_per_seq, rw=rw)
    pre_moe = (x1, h2, gates, mod6, 1 if per_token_mod else t // ROUTE_TILE)
    p_last = p3[:, t - 1, :]
    new_shift = jnp.concatenate([p_last[:, :3 * rw], p_last[:, 3 * rw:3 * rw + wts["lora"][0]],
                                 p_last[:, 3 * rw + 128:3 * rw + 128 + wts["lora"][1]],
                                 p_last[:, 3 * rw + 256:3 * rw + 256 + wts["lora"][2]]], axis=1)
    n_heads = att_w // HEAD_DIM
    state = (k32.reshape(b, t, n_heads, HEAD_DIM), v32.reshape(b, t, n_heads, HEAD_DIM),
             kw[:, :HEAD_DIM].reshape(b, t, HEAD_DIM), wkv_new, new_shift)
    return pre_moe, state


def kernel(x_prompt, x_sample, c_prompt, c_sample, cache_k, cache_v, cache_idx_k, state_wkv, state_shift, page_table, norm1_g, norm2_g, w_mod, b_mod, w_in, mu_shift, w0, w2, a0, a2, g2, k_k, k_a, r_k, lnx_g, lnx_b, w_out, router_w, router_b, w_gate_up, b_gate_up, w_down, b_down, final_g):
    depth = w_in.shape[0]
    assert depth == 1, "single-layer trunk"
    l = 0
    bp, tp, d = x_prompt.shape
    bs, ts, _ = x_sample.shape
    rw = w0.shape[1]
    att_w = w_out.shape[1] - rw
    lora = (w2.shape[1], a2.shape[1], g2.shape[1])
    rwkv_in = 3 * rw + sum(lora)
    n_idx = (w_in.shape[2] - rwkv_in - 3 * att_w - HEAD_DIM) // (HEAD_DIM + 1)
    n_exp = router_w.shape[2]
    n_pages = page_table.shape[1]
    past = n_pages * PAGE_SIZE
    n_rw = 3 * rw + 512
    assert lora[0] <= 128 and lora[1] <= 128 and lora[2] <= 256 and n_idx * HEAD_DIM == att_w

    wi = w_in[l]
    o = 3 * rw
    rw_cols = [wi[:, :o], _pad_cols(wi[:, o:o + lora[0]], 128), _pad_cols(wi[:, o + lora[0]:o + lora[0] + lora[1]], 128),
               _pad_cols(wi[:, o + lora[0] + lora[1]:rwkv_in], 256)]
    a0_ = rwkv_in
    att_cols = [wi[:, a0_:a0_ + 4 * att_w], _pad_cols(wi[:, a0_ + 4 * att_w:], LANES)]
    w_cat = jnp.concatenate(rw_cols + att_cols, axis=1).astype(BF16)
    mu = mu_shift[l]
    mu_pad = jnp.concatenate([mu[:o], jnp.pad(mu[o:o + lora[0]], (0, 128 - lora[0])),
                              jnp.pad(mu[o + lora[0]:o + lora[0] + lora[1]], (0, 128 - lora[1])),
                              jnp.pad(mu[o + lora[0] + lora[1]:], (0, 256 - lora[2]))]).reshape(1, n_rw)
    vec = jnp.stack([w0[l], a0[l], k_k[l], k_a[l], lnx_g[l], lnx_b[l], r_k[l].reshape(-1),
                     jnp.zeros((rw,), F32)])
    head_of = np.arange(rw) // HEAD_DIM
    seg = jnp.asarray((head_of[:, None] == head_of[None, :]).astype(np.float32)).astype(BF16)
    wts = dict(
        rw=rw, att_w=att_w, n_rw=n_rw, rwkv_in=rwkv_in, lora=lora,
        idx_scale=float((n_idx * HEAD_DIM) ** -0.5),
        g1=norm1_g[l].reshape(1, d), w_cat=w_cat, mu=mu_pad, vec=vec,
        w2=jnp.pad(w2[l], ((0, 128 - lora[0]), (0, 0))), a2=jnp.pad(a2[l], ((0, 128 - lora[1]), (0, 0))),
        g2=jnp.pad(g2[l], ((0, 256 - lora[2]), (0, 0))), seg=seg,
        w_out=w_out[l].astype(BF16), g2n=norm2_g[l].reshape(1, d),
        router_w=_pad_cols(router_w[l], LANES),
        router_b=jnp.concatenate([router_b[l], jnp.full((LANES - n_exp,), NEG_BIG, F32)]).reshape(1, LANES),
        wgu32=w_gate_up[l], bgu=b_gate_up[l].reshape(n_exp, 1, -1),
        wd32=w_down[l], bd=b_down[l].reshape(n_exp, 1, -1), fg=final_g.reshape(1, d),
    )

    c_all = jnp.concatenate([c_prompt, c_sample], axis=0)
    rows_pad = -(-c_all.shape[0] // SUBLANES) * SUBLANES
    mod_all = _modulation(jnp.pad(c_all, ((0, rows_pad - c_all.shape[0]), (0, 0))), w_mod[l], b_mod[l])
    mod_p, mod_s = mod_all[:bp], mod_all[bp:bp + bs]

    k_top_p = min(TOPK_MAX, tp // 4)

    def attend_prompt(q, qi, kw, k32, kb, v32, vb):
        r3 = lambda a: a.reshape(bp, tp, a.shape[1])
        kib = kw[:, :HEAD_DIM].astype(BF16)
        return _dsa_prompt(r3(q), r3(qi), r3(kw), r3(kb), r3(vb), r3(kib), tq=min(256, tp), tk=min(512, tp), k_top=k_top_p)

    pre_p, st_p = _layer(x_prompt, mod_p, jnp.arange(tp), jnp.zeros((bp, n_rw), F32),
                       jnp.zeros((bp, rw // HEAD_DIM, HEAD_DIM, HEAD_DIM), F32), wts, attend_prompt,
                       tm=min(512, tp), chunk=64)

    k_top_s = min(TOPK_MAX, (past + ts) // 4)
    n_heads = att_w // HEAD_DIM
    assert ts <= SUBLANES
    pool_t = lambda c: jnp.transpose(c, (0, 1, 3, 4, 2)).reshape(c.shape[0], c.shape[1], att_w, PAGE_SIZE)
    ck_t, cv_t = pool_t(cache_k), pool_t(cache_v)
    cik_t = jnp.transpose(cache_idx_k, (0, 1, 3, 2))
    g_idx = 16 if n_pages % 16 == 0 else 1
    g_att = 8 if n_pages % 8 == 0 else 1

    def head_query_rows(a, nh):
        a = a.reshape(bs, ts, nh, HEAD_DIM).transpose(0, 2, 1, 3)
        a = jnp.pad(a, ((0, 0), (0, 0), (0, SUBLANES - ts), (0, 0)))
        return a.reshape(bs, nh * SUBLANES, HEAD_DIM)

    def new_rows_t(a):
        a = a.reshape(bs, ts, a.shape[1]).transpose(0, 2, 1)
        return jnp.pad(a, ((0, 0), (0, 0), (0, PAGE_SIZE - ts)))

    def attend_sample(q, qi, kw, k32, kb, v32, vb):
        qim = head_query_rows(qi, n_idx)
        wcol = kw[:, HEAD_DIM:HEAD_DIM + n_idx].reshape(bs, ts, n_idx).transpose(0, 2, 1)
        wcol = jnp.pad(wcol, ((0, 0), (0, 0), (0, SUBLANES - ts))).reshape(bs, n_idx * SUBLANES, 1)
        past_scores = _idx_sample(page_table, qim, wcol, cik_t, t_new=ts, g_pages=g_idx, layer=l)
        new_scores = _idx_new(qim, wcol, new_rows_t(kw[:, :HEAD_DIM]), t_new=ts)
        bias = _select_sample(past_scores, new_scores, k_top=k_top_s)
        qh = head_query_rows(q, n_heads)
        eye = jnp.eye(n_heads, dtype=BF16)
        qbd = (qh.reshape(bs, n_heads, SUBLANES, 1, HEAD_DIM) * eye[None, :, None, :, None]).reshape(
            bs, n_heads * SUBLANES, att_w)
        out = _attn_sample(page_table, qbd, bias, ck_t, cv_t, new_rows_t(k32), new_rows_t(v32), g_pages=g_att,
                           layer=l)
        return out[:, :ts]

    shift_s = state_shift[l]
    o1, o2 = o + lora[0], o + lora[0] + lora[1]
    shift_pad = jnp.concatenate([shift_s[:, :o], _pad_cols(shift_s[:, o:o1], 128), _pad_cols(shift_s[:, o1:o2], 128),
                                 _pad_cols(shift_s[:, o2:], 256)], axis=1)
    pre_s, st_s = _layer(x_sample, mod_s, past + jnp.arange(ts), shift_pad, state_wkv[l], wts, attend_sample,
                       tm=bs * ts, chunk=SUBLANES)

    y_p, y_s = _moe_routed(pre_p, pre_s, wts)
    y_p, y_s = y_p.reshape(bp, tp, d), y_s.reshape(bs, ts, d)
    ex = lambda a: a[None]
    return (y_p, y_s,ex(st_p[0]), ex(st_p[1]), ex(st_p[2]), ex(st_p[3]), ex(st_p[4]),
            ex(st_s[0]), ex(st_s[1]), ex(st_s[2]), ex(st_s[3]), ex(st_s[4]))
```

```python
import functools

import numpy as np
import jax
import jax.numpy as jnp
from jax import lax
from jax.experimental import pallas as pl
from jax.experimental.pallas import tpu as pltpu

F32 = jnp.float32
BF16 = jnp.bfloat16
I32 = jnp.int32
HI = lax.Precision.HIGHEST

HEAD_DIM = 64
PAGE_SIZE = 128
TOPK_MAX = 256
ROPE_THETA = 10000.0
TOP_K = 4
SWIGLU_LIMIT = 7.0
SWIGLU_ALPHA = 1.702
NORM_EPS = 1e-5
LNX_EPS = 64e-5

LANES = 128
SUBLANES = 8
VMEM_LIMIT = 56 * 1024 * 1024

NEG_BIG = -1e30
INT_MIN = -(2 ** 31)
KEY_NEG_INF = INT_MIN + 0x7FFFFF


def _dot(a, b, prec=None):
    return lax.dot_general(a, b, (((1,), (0,)), ((), ())), precision=prec, preferred_element_type=F32)


def _dot_nt(a, b, prec=None):
    return lax.dot_general(a, b, (((1,), (1,)), ((), ())), precision=prec, preferred_element_type=F32)


def _dot_tn(a, b, prec=None):
    return lax.dot_general(a, b, (((0,), (0,)), ((), ())), precision=prec, preferred_element_type=F32)


_NN = ((1,), (0,))
_NT = ((1,), (1,))
_TN = ((0,), (0,))


def _split(a, terms=2):
    out = []
    for i in range(terms):
        t = a.astype(BF16)
        out.append(t)
        if i + 1 < terms:
            a = a - t.astype(F32)
    return out


def _mm(a_terms, b_terms, dims=_NN, order=1):
    acc = None
    for i, a in enumerate(a_terms):
        for j, b in enumerate(b_terms):
            if i + j <= order:
                d = lax.dot_general(a, b, (dims, ((), ())), preferred_element_type=F32)
                acc = d if acc is None else acc + d
    return acc


def _params(*sem):
    return pltpu.CompilerParams(dimension_semantics=sem, vmem_limit_bytes=VMEM_LIMIT)


def _sigmoid(x):
    return jax.nn.sigmoid(x)


def _rms(x, g):
    return x * lax.rsqrt(jnp.mean(x * x, axis=-1, keepdims=True) + NORM_EPS) * g


def _mod_kernel(c_ref, w_ref, b_ref, o_ref):
    c = c_ref[...]
    o_ref[...] = _dot(c * _sigmoid(c), w_ref[...], HI) + b_ref[...]


def _modulation(c_pad, w_mod, b_mod):
    rows, d = c_pad.shape
    n = w_mod.shape[1]
    return pl.pallas_call(
        _mod_kernel,
        grid=(n // d,),
        in_specs=[pl.BlockSpec((rows, d), lambda j: (0, 0)),
                  pl.BlockSpec((d, d), lambda j: (0, j)),
                  pl.BlockSpec((1, d), lambda j: (0, j))],
        out_specs=pl.BlockSpec((rows, d), lambda j: (0, j)),
        out_shape=jax.ShapeDtypeStruct((rows, n), F32),
        compiler_params=_params("arbitrary"),
        name="modulation",
    )(c_pad, w_mod, b_mod.reshape(1, n))


def _rope_block(x, cos, sin_signed):
    lane = lax.broadcasted_iota(I32, x.shape, 1)
    partner = jnp.where((lane & 32) == 0, pltpu.roll(x, LANES - 32, axis=1), pltpu.roll(x, 32, axis=1))
    return x * cos + partner * sin_signed


def _inproj_kernel(x_ref, mod_ref, g_ref, w_ref, cs_ref, p_ref, q_ref, k_ref, kb_ref, v_ref, vb_ref, qi_ref,
                   kw_ref, *, n_rw, att_w):
    x = x_ref[...]
    h = _rms(x, g_ref[...]) * (1 + mod_ref[1, 0]) + mod_ref[0, 0]
    hb = h.astype(BF16)
    cb = 512
    for j in range(n_rw // cb):
        p_ref[:, j * cb:(j + 1) * cb] = _dot(hb, w_ref[:, j * cb:(j + 1) * cb])
    cos, sin = cs_ref[0], cs_ref[1]
    base = n_rw
    ng = att_w // LANES

    def roped(off):
        t = _dot(hb, w_ref[:, off:off + att_w])
        return [_rope_block(t[:, g * LANES:(g + 1) * LANES], cos, sin) for g in range(ng)]

    for g, blk in enumerate(roped(base)):
        q_ref[:, g * LANES:(g + 1) * LANES] = (blk * (HEAD_DIM ** -0.5)).astype(BF16)
    for g, blk in enumerate(roped(base + att_w)):
        k_ref[:, g * LANES:(g + 1) * LANES] = blk
        kb_ref[:, g * LANES:(g + 1) * LANES] = blk.astype(BF16)
    v = _dot(hb, w_ref[:, base + 2 * att_w:base + 3 * att_w])
    v_ref[...] = v
    vb_ref[...] = v.astype(BF16)
    for g, blk in enumerate(roped(base + 3 * att_w)):
        qi_ref[:, g * LANES:(g + 1) * LANES] = blk.astype(BF16)
    kw = _dot(hb, w_ref[:, base + 4 * att_w:base + 4 * att_w + LANES])
    kw_ref[...] = _rope_block(kw, cs_ref[2], cs_ref[3])


def _inproj(x2d, mod6, g1, w_cat, cs, *, tm, tiles_per_seq, n_rw, att_w):
    n, d = x2d.shape
    rmod = mod6.shape[2]
    n_cs_tiles = cs.shape[1] // tm
    grid = (n // tm,)
    row = lambda i: (i, 0)
    outs = [
        jax.ShapeDtypeStruct((n, n_rw), F32),
        jax.ShapeDtypeStruct((n, att_w), BF16),
        jax.ShapeDtypeStruct((n, att_w), F32),
        jax.ShapeDtypeStruct((n, att_w), BF16),
        jax.ShapeDtypeStruct((n, att_w), F32),
        jax.ShapeDtypeStruct((n, att_w), BF16),
        jax.ShapeDtypeStruct((n, att_w), BF16),
        jax.ShapeDtypeStruct((n, LANES), F32),
    ]
    return pl.pallas_call(
        functools.partial(_inproj_kernel, n_rw=n_rw, att_w=att_w),
        grid=grid,
        in_specs=[pl.BlockSpec((tm, d), row),
                  pl.BlockSpec((6, 1, rmod, d), lambda i: (0, i // tiles_per_seq, 0, 0)),
                  pl.BlockSpec((1, d), lambda i: (0, 0)),
                  pl.BlockSpec(w_cat.shape, lambda i: (0, 0)),
                  pl.BlockSpec((4, tm, LANES), lambda i: (0, i % n_cs_tiles, 0))],
        out_specs=[pl.BlockSpec((tm, o.shape[1]), row) for o in outs],
        out_shape=outs,
        compiler_params=_params("arbitrary"),
        name="inproj",
    )(x2d, mod6, g1, w_cat, cs)


def _softplus(x):
    return jnp.maximum(x, 0.0) + jnp.log(1.0 + jnp.exp(-jnp.abs(x)))


def _rwkv_kernel(p_ref, sh0_ref, z0_ref, mu_ref, vec_ref, w2_ref, a2_ref, g2_ref, seg_ref, y_ref, zout_ref,
                 carry_ref, z_ref, ops_ref, yh_ref, *, L, t_valid, rw, n_heads):
    c = pl.program_id(1)

    @pl.when(c == 0)
    def _():
        carry_ref[...] = sh0_ref[0]
        z_ref[...] = z0_ref[0]

    p = p_ref[0]
    row = lax.broadcasted_iota(I32, (L, 1), 0)
    prev = jnp.where(row == 0, carry_ref[...], pltpu.roll(p, 1, axis=0))
    carry_ref[...] = p[L - 1:L, :]
    z = p + (prev - p) * mu_ref[...]
    r, k, v = z[:, 0:rw], z[:, rw:2 * rw], z[:, 2 * rw:3 * rw]
    o = 3 * rw
    zw, za, zg = z[:, o:o + 128], z[:, o + 128:o + 256], z[:, o + 256:o + 512]
    w0, a0, k_k, k_a = vec_ref[0:1], vec_ref[1:2], vec_ref[2:3], vec_ref[3:4]
    lnx_g, lnx_b, r_k = vec_ref[4:5], vec_ref[5:6], vec_ref[6:7]
    seg = [seg_ref[...]]

    def seg_sum(t):
        return _mm(_split(t, 3), seg, order=2)

    def lora(t, w_ref):
        return _mm(_split(t), _split(w_ref[...]))

    w_log = -_softplus(-(w0 + lora(jnp.tanh(zw), w2_ref))) - 0.5
    lw = -jnp.exp(w_log)
    a = _sigmoid(a0 + lora(za, a2_ref))
    g = lora(_sigmoid(zg), g2_ref)
    kk = k * k_k
    kkn = kk / jnp.maximum(jnp.sqrt(seg_sum(kk * kk)), 1e-12)
    k2 = k * (1 + (a - 1) * k_a)
    an, bn = -kkn, kkn * a
    if t_valid < L:
        valid = row < t_valid
        lw = jnp.where(valid, lw, 0.0)
        an, bn = jnp.where(valid, an, 0.0), jnp.where(valid, bn, 0.0)
        k2s, vs = jnp.where(valid, k2, 0.0), jnp.where(valid, v, 0.0)
    else:
        k2s, vs = k2, v

    ri = lax.broadcasted_iota(I32, (L, L), 0)
    ci = lax.broadcasted_iota(I32, (L, L), 1)
    incl, strict = ri >= ci, ri > ci
    lw_terms = _split(lw, 3)
    cum = _mm([incl.astype(BF16)], lw_terms, order=2)
    c_last = cum[L - 1:L, :]
    e_c, e_cp, e_n, e_d = jnp.exp(cum), jnp.exp(cum - lw), jnp.exp(-cum), jnp.exp(c_last - cum)
    ops_ref[0] = an * e_cp
    ops_ref[1] = r * e_c
    ops_ref[2] = bn * e_n
    ops_ref[3] = k2s * e_n
    ops_ref[4] = bn * e_d
    ops_ref[5] = k2s * e_d
    ops_ref[6] = vs
    wl_col = jnp.exp(_mm(lw_terms, [jnp.ones((L, LANES), BF16)], _TN, order=2))
    eye = (ri == ci).astype(F32)
    n_sq = max(int(np.ceil(np.log2(L))) - 1, 0)

    hs = range(n_heads)
    sls = [slice(h * HEAD_DIM, (h + 1) * HEAD_DIM) for h in hs]
    at, rt, bt, kt, bh, kh, vh = [[_split(ops_ref[i, :, sl]) for sl in sls] for i in range(7)]
    m_ab = [jnp.where(strict, _mm(at[h], bt[h], _NT), 0.0) for h in hs]
    t_inv = [eye + m_ab[h] for h in hs]
    pw = m_ab
    for _ in range(n_sq):
        pw_t = [_split(pw[h]) for h in hs]
        pw = [_mm(pw_t[h], pw_t[h]) for h in hs]
        t_inv = [t_inv[h] + _mm(_split(t_inv[h]), _split(pw[h])) for h in hs]
    m_ak = [jnp.where(strict, _mm(at[h], kt[h], _NT), 0.0) for h in hs]
    m_rb = [jnp.where(incl, _mm(rt[h][:1], bt[h][:1], _NT), 0.0).astype(BF16) for h in hs]
    m_rk = [jnp.where(incl, _mm(rt[h][:1], kt[h][:1], _NT), 0.0).astype(BF16) for h in hs]
    zs = [z_ref[h] for h in hs]
    zs_t = [_split(zs[h]) for h in hs]
    rhs = [_mm(at[h], zs_t[h]) + _mm(_split(m_ak[h]), vh[h]) for h in hs]
    u_t = [_split(_mm(_split(t_inv[h]), _split(rhs[h]))) for h in hs]
    for h in hs:
        yh_ref[:, sls[h]] = (_mm(rt[h][:1], zs_t[h][:1]) + _mm([m_rb[h]], u_t[h][:1])
                             + _mm([m_rk[h]], vh[h][:1]))
    for h in hs:
        z_ref[h] = wl_col[sls[h], 0:HEAD_DIM] * zs[h] + _mm(bh[h], u_t[h], _TN) + _mm(kh[h], vh[h], _TN)

    y = yh_ref[...]
    inv_n = 1.0 / HEAD_DIM
    mean = seg_sum(y) * inv_n
    yc = y - mean
    var = seg_sum(yc * yc) * inv_n
    yn = yc * lax.rsqrt(var + LNX_EPS) * lnx_g + lnx_b
    bonus = seg_sum(r * k2 * r_k) * v
    y_ref[0] = ((yn + bonus) * g).astype(y_ref.dtype)

    @pl.when(c == pl.num_programs(1) - 1)
    def _():
        zout_ref[0] = z_ref[...]


def _rwkv(p3, shift0, z0, mu, vec, w2, a2, g2, seg, *, L, t_valid, rw):
    b, t, n_rw = p3.shape
    n_heads = rw // HEAD_DIM
    const2 = lambda i, c: (0, 0)
    return pl.pallas_call(
        functools.partial(_rwkv_kernel, L=L, t_valid=t_valid, rw=rw, n_heads=n_heads),
        grid=(b, t // L),
        in_specs=[pl.BlockSpec((1, L, n_rw), lambda i, c: (i, c, 0)),
                  pl.BlockSpec((1, 1, n_rw), lambda i, c: (i, 0, 0)),
                  pl.BlockSpec((1, n_heads, HEAD_DIM, HEAD_DIM), lambda i, c: (i, 0, 0, 0)),
                  pl.BlockSpec((1, n_rw), const2),
                  pl.BlockSpec(vec.shape, const2),
                  pl.BlockSpec(w2.shape, const2),
                  pl.BlockSpec(a2.shape, const2),
                  pl.BlockSpec(g2.shape, const2),
                  pl.BlockSpec(seg.shape, const2)],
        out_specs=[pl.BlockSpec((1, L, rw), lambda i, c: (i, c, 0)),
                   pl.BlockSpec((1, n_heads, HEAD_DIM, HEAD_DIM), lambda i, c: (i, 0, 0, 0))],
        out_shape=[jax.ShapeDtypeStruct((b, t, rw), BF16),
                   jax.ShapeDtypeStruct((b, n_heads, HEAD_DIM, HEAD_DIM), F32)],
        scratch_shapes=[pltpu.VMEM((1, n_rw), F32),
                        pltpu.VMEM((n_heads, HEAD_DIM, HEAD_DIM), F32),
                        pltpu.VMEM((7, L, rw), F32),
                        pltpu.VMEM((L, rw), F32)],
        compiler_params=_params("arbitrary", "arbitrary"),
        name="rwkv_scan",
    )(p3, shift0, z0, mu, vec, w2, a2, g2, seg)


def _key_to_float(key):
    bits = key ^ ((key >> 31) & 0x7FFFFFFF)
    return lax.bitcast_convert_type(bits, F32)


def _fold_lanes(m):
    acc = m[:, 0:LANES]
    for j in range(1, m.shape[1] // LANES):
        acc = acc + m[:, j * LANES:(j + 1) * LANES]
    return acc


def _select_bias(sc_ref, pb_ref, nc, n_q, tk, k_top, n_keys_total, key_axis=1):
    if key_axis == 1:
        q_shape, part_shape = (n_q, 1), (n_q, LANES)
        fold = lambda m: _fold_lanes(m.astype(F32))
        total = lambda acc: jnp.sum(acc, axis=1, keepdims=True)
    else:
        q_shape, part_shape = (1, n_q), (SUBLANES, n_q)
        fold = lambda m: jnp.sum(m.astype(F32).reshape(tk // SUBLANES, SUBLANES, n_q), axis=0)
        total = lambda acc: jnp.sum(acc, axis=0, keepdims=True)

    def count(pred):
        def body(c, acc):
            return acc + fold(pred(c, sc_ref[c]))
        return total(lax.fori_loop(0, nc, body, jnp.zeros(part_shape, F32)))

    kf = float(k_top)
    cnt0 = count(lambda c, s: s >= 0.0)
    tau0 = jnp.where(cnt0 >= kf, 0, INT_MIN).astype(I32)

    def bit_body(i, tau):
        cand = tau + jnp.left_shift(jnp.int32(1), 30 - i)
        cand_f = _key_to_float(cand)
        cnt = count(lambda c, s: s >= cand_f)
        return jnp.where((cnt >= kf) | (cand <= KEY_NEG_INF), cand, tau)

    tau = lax.fori_loop(0, 31, bit_body, tau0)
    tau_f = _key_to_float(tau)
    need = kf - count(lambda c, s: s > tau_f)
    excess = count(lambda c, s: s == tau_f) - need

    def kidx(c):
        return c * tk + lax.broadcasted_iota(I32, (1, tk) if key_axis == 1 else (tk, 1), key_axis)

    n_bits = max(int(np.ceil(np.log2(n_keys_total))), 1)
    pb_ref[...] = jnp.full(q_shape, n_keys_total, I32)

    @pl.when(jnp.max(excess) > 0.0)
    def _():
        def tie_body(i, pb):
            cand = pb + jnp.left_shift(jnp.int32(1), n_bits - 1 - i)
            cnt = count(lambda c, s: (s == tau_f) & (kidx(c) < cand))
            return jnp.where(cnt < need, cand, pb)

        pb_ref[...] = lax.fori_loop(0, n_bits, tie_body, jnp.zeros(q_shape, I32))

    pb = pb_ref[...]

    def write(c, _):
        s = sc_ref[c]
        sel = ((s > tau_f) | ((s == tau_f) & (kidx(c) <= pb))) & (s > -jnp.inf)
        sc_ref[c] = jnp.where(sel, 0.0, NEG_BIG)
        return 0

    lax.fori_loop(0, nc, write, 0)


def _dsa_prompt_kernel(qt_ref, qit_ref, wit_ref, k_ref, vt_ref, ki_ref, o_ref, sc_ref, pb_ref, m_ref, l_ref, acc_ref,
                       *, tq, tk, k_top, n_heads, n_idx, t_total):
    qb = pl.program_id(1)
    q0 = qb * tq
    nc = (q0 + tq + tk - 1) // tk
    qpos = q0 + lax.broadcasted_iota(I32, (1, tq), 1)
    hd = lambda h: slice(h * HEAD_DIM, (h + 1) * HEAD_DIM)

    def scores(c, _):
        kic = ki_ref[0, pl.ds(pl.multiple_of(c * tk, tk), tk), :]
        acc = jnp.zeros((tk, tq), F32)
        for h in range(n_idx):
            acc = acc + jnp.maximum(_dot(kic, qit_ref[0, hd(h), :]), 0.0) * wit_ref[0, h:h + 1, :]
        spos = c * tk + lax.broadcasted_iota(I32, (tk, 1), 0)
        sc_ref[c] = jnp.where(spos <= qpos, acc, -jnp.inf)
        return 0

    lax.fori_loop(0, nc, scores, 0)
    _select_bias(sc_ref, pb_ref, nc, tq, tk, k_top, t_total, key_axis=0)

    m_ref[...] = jnp.full(m_ref.shape, NEG_BIG, F32)
    l_ref[...] = jnp.zeros(l_ref.shape, F32)
    acc_ref[...] = jnp.zeros(acc_ref.shape, F32)

    def attend(c, _):
        rows = pl.ds(pl.multiple_of(c * tk, tk), tk)
        bias = sc_ref[c]
        m_all, l_all, acc_all = m_ref[...], l_ref[...], acc_ref[...]
        hs = range(n_heads)
        s = [_dot(k_ref[0, h, rows, :], qt_ref[0, hd(h), :]) + bias for h in hs]
        m_new = [jnp.maximum(m_all[h:h + 1], jnp.max(s[h], axis=0, keepdims=True)) for h in hs]
        p = [jnp.exp(s[h] - m_new[h]) for h in hs]
        pv = [_dot(vt_ref[0, c, hd(h), :], p[h].astype(BF16)) for h in hs]
        alpha = [jnp.exp(m_all[h:h + 1] - m_new[h]) for h in hs]
        l_new = [alpha[h] * l_all[h:h + 1] + jnp.sum(p[h], axis=0, keepdims=True) for h in hs]
        acc_ref[...] = jnp.concatenate([alpha[h] * acc_all[hd(h)] + pv[h] for h in hs], axis=0)
        m_ref[...] = jnp.concatenate(m_new, axis=0)
        l_ref[...] = jnp.concatenate(l_new, axis=0)
        return 0

    lax.fori_loop(0, nc, attend, 0)
    l_all = l_ref[...]
    for h in range(n_heads):
        o_ref[0, hd(h), :] = (acc_ref[hd(h), :] / l_all[h:h + 1]).astype(o_ref.dtype)


def _dsa_prompt(qt, qit, wit, k_hm, vt_ck, ki, *, tq, tk, k_top):
    b, att_w, t = qt.shape
    n_heads = att_w // HEAD_DIM
    n_idx = qit.shape[1] // HEAD_DIM
    assert n_heads == SUBLANES
    qblk = lambda i, j: (i, 0, j)
    return pl.pallas_call(
        functools.partial(_dsa_prompt_kernel, tq=tq, tk=tk, k_top=k_top, n_heads=n_heads, n_idx=n_idx,
                          t_total=t),
        grid=(b, t // tq),
        in_specs=[pl.BlockSpec((1, att_w, tq), qblk),
                  pl.BlockSpec((1, qit.shape[1], tq), qblk),
                  pl.BlockSpec((1, wit.shape[1], tq), qblk),
                  pl.BlockSpec((1,) + k_hm.shape[1:], lambda i, j: (i, 0, 0, 0)),
                  pl.BlockSpec((1,) + vt_ck.shape[1:], lambda i, j: (i, 0, 0, 0)),
                  pl.BlockSpec((1, t, HEAD_DIM), lambda i, j: (i, 0, 0))],
        out_specs=pl.BlockSpec((1, att_w, tq), qblk),
        out_shape=jax.ShapeDtypeStruct((b, att_w, t), BF16),
        scratch_shapes=[pltpu.VMEM((t // tk, tk, tq), F32),
                        pltpu.VMEM((1, tq), I32),
                        pltpu.VMEM((n_heads, tq), F32),
                        pltpu.VMEM((n_heads, tq), F32),
                        pltpu.VMEM((att_w, tq), F32)],
        compiler_params=_params("arbitrary", "arbitrary"),
        name="dsa_prompt",
    )(qt, qit, wit, k_hm, vt_ck, ki)


def _idx_page_scores(qim, wcol, ki_t, n_idx):
    d = jnp.maximum(_dot(qim, ki_t.astype(BF16)), 0.0) * wcol
    sc = d[0:SUBLANES]
    for h in range(1, n_idx):
        sc = sc + d[h * SUBLANES:(h + 1) * SUBLANES]
    return sc


def _idx_sample_kernel(pt_ref, qim_ref, wcol_ref, *refs, t_new, n_idx, g_pages):
    page_refs, o_ref = refs[:g_pages], refs[g_pages]
    qrow = lax.broadcasted_iota(I32, (SUBLANES, PAGE_SIZE), 0)
    for j in range(g_pages):
        sc = _idx_page_scores(qim_ref[0], wcol_ref[0], page_refs[j][0, 0], n_idx)
        o_ref[j] = jnp.where(qrow < t_new, sc, -jnp.inf)


def _idx_sample(page_table, qim, wcol, cache_idx_t, *, t_new, g_pages, layer):
    b, n_pages = page_table.shape
    n_idx = qim.shape[1] // SUBLANES
    per_seq = lambda i, g, pt: (i, 0, 0)
    page_specs = [pl.BlockSpec((1, 1, HEAD_DIM, PAGE_SIZE),
                               lambda i, g, pt, j=j: (layer, pt[i, g * g_pages + j], 0, 0))
                  for j in range(g_pages)]
    grid_spec = pltpu.PrefetchScalarGridSpec(
        num_scalar_prefetch=1,
        grid=(b, n_pages // g_pages),
        in_specs=[pl.BlockSpec((1,) + qim.shape[1:], per_seq),
                  pl.BlockSpec((1,) + wcol.shape[1:], per_seq)] + page_specs,
        out_specs=pl.BlockSpec((g_pages, SUBLANES, PAGE_SIZE), lambda i, g, pt: (g, i, 0)),
    )
    return pl.pallas_call(
        functools.partial(_idx_sample_kernel, t_new=t_new, n_idx=n_idx, g_pages=g_pages),
        grid_spec=grid_spec,
        out_shape=jax.ShapeDtypeStruct((n_pages, b * SUBLANES, PAGE_SIZE), F32),
        compiler_params=_params("arbitrary", "arbitrary"),
        name="idx_sample",
    )(page_table, qim, wcol, *([cache_idx_t] * g_pages))


def _idx_new_kernel(qim_ref, wcol_ref, new_ref, o_ref, *, t_new, n_idx):
    sc = _idx_page_scores(qim_ref[0], wcol_ref[0], new_ref[0], n_idx)
    qrow = lax.broadcasted_iota(I32, (SUBLANES, PAGE_SIZE), 0)
    lane = lax.broadcasted_iota(I32, (SUBLANES, PAGE_SIZE), 1)
    o_ref[...] = jnp.where((qrow < t_new) & (lane <= qrow), sc, -jnp.inf)


def _idx_new(qim, wcol, ki_new, *, t_new):
    b = qim.shape[0]
    n_idx = qim.shape[1] // SUBLANES
    per_seq = lambda i: (i, 0, 0)
    return pl.pallas_call(
        functools.partial(_idx_new_kernel, t_new=t_new, n_idx=n_idx),
        grid=(b,),
        in_specs=[pl.BlockSpec((1,) + qim.shape[1:], per_seq),
                  pl.BlockSpec((1,) + wcol.shape[1:], per_seq),
                  pl.BlockSpec((1, HEAD_DIM, PAGE_SIZE), per_seq)],
        out_specs=pl.BlockSpec((SUBLANES, PAGE_SIZE), lambda i: (i, 0)),
        out_shape=jax.ShapeDtypeStruct((b * SUBLANES, PAGE_SIZE), F32),
        compiler_params=_params("arbitrary"),
        name="idx_new",
    )(qim, wcol, ki_new)


def _select_sample_kernel(past_ref, new_ref, o_ref, pb_ref, *, k_top):
    n_pages, rows = past_ref.shape[0], past_ref.shape[1]
    o_ref[0:n_pages] = past_ref[...]
    o_ref[n_pages] = new_ref[...]
    _select_bias(o_ref, pb_ref, n_pages + 1, rows, PAGE_SIZE, k_top, (n_pages + 1) * PAGE_SIZE)


def _select_sample(past_scores, new_scores, *, k_top):
    n_pages, rows, _ = past_scores.shape
    out_shape = (n_pages + 1, rows, PAGE_SIZE)
    return pl.pallas_call(
        functools.partial(_select_sample_kernel, k_top=k_top),
        grid=(1,),
        in_specs=[pl.BlockSpec(past_scores.shape, lambda i: (0, 0, 0)),
                  pl.BlockSpec(new_scores.shape, lambda i: (0, 0))],
        out_specs=pl.BlockSpec(out_shape, lambda i: (0, 0, 0)),
        out_shape=jax.ShapeDtypeStruct(out_shape, F32),
        scratch_shapes=[pltpu.VMEM((rows, 1), I32)],
        compiler_params=_params("arbitrary"),
        name="select_sample",
    )(past_scores, new_scores)


def _attn_sample_kernel(pt_ref, qbd_ref, bias_ref, biasn_ref, knew_ref, vnew_ref, *refs, n_heads, g_pages):
    k_refs, v_refs = refs[:g_pages], refs[g_pages:2 * g_pages]
    o_ref, m_ref, l_ref, acc_ref = refs[2 * g_pages:]
    g = pl.program_id(1)

    @pl.when(g == 0)
    def _():
        m_ref[...] = jnp.full(m_ref.shape, NEG_BIG, F32)
        l_ref[...] = jnp.zeros(l_ref.shape, F32)
        acc_ref[...] = jnp.zeros(acc_ref.shape, F32)

    def step(kts, vts, biases):
        bias = jnp.concatenate([jnp.concatenate([b_] * n_heads, axis=0) for b_ in biases], axis=1)
        s = jnp.concatenate([_dot(qbd_ref[0], kt.astype(BF16)) for kt in kts], axis=1) + bias
        m = m_ref[...]
        m_new = jnp.maximum(m, jnp.max(s, axis=-1, keepdims=True))
        alpha = jnp.exp(m - m_new)
        pr = jnp.exp(s - m_new)
        prb = pr.astype(BF16)
        pv = None
        for j, vt in enumerate(vts):
            d = _dot_nt(prb[:, j * PAGE_SIZE:(j + 1) * PAGE_SIZE], vt.astype(BF16))
            pv = d if pv is None else pv + d
        l_ref[...] = alpha * l_ref[...] + jnp.sum(pr, axis=-1, keepdims=True)
        acc_ref[...] = alpha * acc_ref[...] + pv
        m_ref[...] = m_new

    step([r[0, 0] for r in k_refs], [r[0, 0] for r in v_refs], [bias_ref[j] for j in range(g_pages)])

    @pl.when(g == pl.num_programs(1) - 1)
    def _():
        step([knew_ref[0]], [vnew_ref[0]], [biasn_ref[0]])
        full = acc_ref[...] / l_ref[...]
        col_head = lax.broadcasted_iota(I32, (SUBLANES, full.shape[1]), 1) // HEAD_DIM
        out = jnp.zeros((SUBLANES, full.shape[1]), F32)
        for h in range(n_heads):
            out = out + jnp.where(col_head == h, full[h * SUBLANES:(h + 1) * SUBLANES], 0.0)
        o_ref[0] = out.astype(o_ref.dtype)


def _attn_sample(page_table, qbd, bias, cache_k_t, cache_v_t, k_new_t, v_new_t, *, g_pages, layer):
    b, n_pages = page_table.shape
    att_w = qbd.shape[2]
    n_heads = att_w // HEAD_DIM
    per_seq = lambda i, g, pt: (i, 0, 0)
    page_specs = [pl.BlockSpec((1, 1, att_w, PAGE_SIZE),
                               lambda i, g, pt, j=j: (layer, pt[i, g * g_pages + j], 0, 0))
                  for j in range(g_pages)]
    new_spec = pl.BlockSpec((1, att_w, PAGE_SIZE), per_seq)
    grid_spec = pltpu.PrefetchScalarGridSpec(
        num_scalar_prefetch=1,
        grid=(b, n_pages // g_pages),
        in_specs=[pl.BlockSpec((1,) + qbd.shape[1:], per_seq),
                  pl.BlockSpec((g_pages, SUBLANES, PAGE_SIZE), lambda i, g, pt: (g, i, 0)),
                  pl.BlockSpec((1, SUBLANES, PAGE_SIZE), lambda i, g, pt: (n_pages, i, 0)),
                  new_spec, new_spec] + page_specs + page_specs,
        out_specs=pl.BlockSpec((1, SUBLANES, att_w), per_seq),
        scratch_shapes=[pltpu.VMEM((n_heads * SUBLANES, 1), F32),
                        pltpu.VMEM((n_heads * SUBLANES, 1), F32),
                        pltpu.VMEM((n_heads * SUBLANES, att_w), F32)],
    )
    return pl.pallas_call(
        functools.partial(_attn_sample_kernel, n_heads=n_heads, g_pages=g_pages),
        grid_spec=grid_spec,
        out_shape=jax.ShapeDtypeStruct((b, SUBLANES, att_w), BF16),
        compiler_params=_params("arbitrary", "arbitrary"),
        name="attn_sample",
    )(page_table, qbd, bias, bias, k_new_t, v_new_t, *([cache_k_t] * g_pages), *([cache_v_t] * g_pages))


def _outproj_kernel(x_ref, yr_ref, ya_ref, mod_ref, wo_ref, g_ref, rw_ref, rb_ref, x1_ref, h2_ref, gates_ref, *, rw):
    mixed = _dot(yr_ref[...], wo_ref[0:rw, :]) + _dot(ya_ref[...], wo_ref[rw:, :])
    x1 = x_ref[...] + mod_ref[2, 0] * mixed
    x1_ref[...] = x1
    h2 = _rms(x1, g_ref[...]) * (1 + mod_ref[4, 0]) + mod_ref[3, 0]
    h2_ref[...] = h2.astype(BF16)
    logits = _dot(h2, rw_ref[...], HI) + rb_ref[...]
    lane = lax.broadcasted_iota(I32, logits.shape, 1).astype(F32)
    work = logits
    picked = jnp.zeros(logits.shape, jnp.bool_)
    top = None
    for i in range(TOP_K):
        mx = jnp.max(work, axis=-1, keepdims=True)
        first = jnp.min(jnp.where(work == mx, lane, float(LANES)), axis=-1, keepdims=True)
        hit = lane == first
        picked = picked | hit
        work = jnp.where(hit, -jnp.inf, work)
        if i == 0:
            top = mx
    e = jnp.where(picked, jnp.exp(logits - top), 0.0)
    gates_ref[...] = e / jnp.sum(e, axis=-1, keepdims=True)


def _outproj(x2d, yr, ya, mod6, w_out_b, g2n, router_w_pad, router_b_pad, *, tm, tiles_per_seq, rw):
    n, d = x2d.shape
    rmod = mod6.shape[2]
    row = lambda i: (i, 0)
    const = lambda i: (0, 0)
    outs = [jax.ShapeDtypeStruct((n, d), F32), jax.ShapeDtypeStruct((n, d), BF16),
            jax.ShapeDtypeStruct((n, LANES), F32)]
    return pl.pallas_call(
        functools.partial(_outproj_kernel, rw=rw),
        grid=(n // tm,),
        in_specs=[pl.BlockSpec((tm, d), row),
                  pl.BlockSpec((tm, yr.shape[1]), row),
                  pl.BlockSpec((tm, ya.shape[1]), row),
                  pl.BlockSpec((6, 1, rmod, d), lambda i: (0, i // tiles_per_seq, 0, 0)),
                  pl.BlockSpec(w_out_b.shape, const),
                  pl.BlockSpec((1, d), const),
                  pl.BlockSpec(router_w_pad.shape, const),
                  pl.BlockSpec((1, LANES), const)],
        out_specs=[pl.BlockSpec((tm, o.shape[1]), row) for o in outs],
        out_shape=outs,
        compiler_params=_params("arbitrary"),
        name="outproj_router",
    )(x2d, yr, ya, mod6, w_out_b, g2n, router_w_pad, router_b_pad)


ROUTE_TILE = 512
ROUTE_WIN = 128
ROUTE_ALIGN = 16
FFN_ROWS = 512


def _route_meta_kernel(g_ref, rank_ref, gate_ref, cnt_ref, *, n_exp):
    gt = jnp.transpose(g_ref[...])
    sel = gt > 0.0
    tt = gt.shape[1]
    earlier = (lax.broadcasted_iota(I32, (tt, tt), 0) < lax.broadcasted_iota(I32, (tt, tt), 1)).astype(BF16)
    rank = _dot(sel.astype(BF16), earlier)
    rank_ref[...] = jnp.where(sel, rank, -1.0)[:n_exp]
    gate_ref[...] = gt[:n_exp]
    cnt_ref[0] = jnp.broadcast_to(jnp.sum(sel.astype(F32), axis=-1, keepdims=True), (LANES, LANES))


def _route_meta(gates, *, n_exp):
    n = gates.shape[0]
    n_tiles = n // ROUTE_TILE
    return pl.pallas_call(
        functools.partial(_route_meta_kernel, n_exp=n_exp),
        grid=(n_tiles,),
        in_specs=[pl.BlockSpec((ROUTE_TILE, LANES), lambda i: (i, 0))],
        out_specs=[pl.BlockSpec((n_exp, ROUTE_TILE), lambda i: (0, i)),
                   pl.BlockSpec((n_exp, ROUTE_TILE), lambda i: (0, i)),
                   pl.BlockSpec((1, LANES, LANES), lambda i: (i, 0, 0))],
        out_shape=[jax.ShapeDtypeStruct((n_exp, n), F32), jax.ShapeDtypeStruct((n_exp, n), F32),
                   jax.ShapeDtypeStruct((n_tiles, LANES, LANES), F32)],
        compiler_params=_params("arbitrary"),
        name="route_meta",
    )(gates)


def _window_onehot(rank_row, w):
    rows = lax.broadcasted_iota(I32, (ROUTE_WIN, 1), 0).astype(F32) + float(ROUTE_WIN) * w
    return rank_row == rows


def _dispatch_kernel(base_ref, cnt_ref, h_ref, rank_ref, gate_ref, xs_in, gs_in, xs_out, gs_out, p_ref, sx_ref,
                     sg_ref, sem_x, sem_g, *, n_exp, group):
    del xs_in, gs_in
    i = pl.program_id(0)
    h = h_ref[...]
    win = ROUTE_WIN

    def copies(slot, row0):
        rows = pl.ds(pl.multiple_of(row0, ROUTE_ALIGN), win)
        return (pltpu.make_async_copy(sx_ref.at[slot], xs_out.at[rows], sem_x.at[slot]),
                pltpu.make_async_copy(sg_ref.at[slot], gs_out.at[rows], sem_g.at[slot]))

    def fill(slot, e, w):
        p = _window_onehot(rank_ref[pl.ds(e, 1), :], w)
        p_ref[slot * win:(slot + 1) * win, :] = p.astype(BF16)
        g = jnp.sum(jnp.where(p, gate_ref[pl.ds(e, 1), :], 0.0), axis=-1, keepdims=True)
        sg_ref[slot] = jnp.broadcast_to(g, (win, LANES))

    for e0 in range(0, n_exp, group):
        for j in range(group):
            fill(e0 + j, e0 + j, 0)
        x = _dot(p_ref[e0 * win:(e0 + group) * win, :], h)
        for j in range(group):
            e = e0 + j
            sx_ref[e] = x[j * win:(j + 1) * win].astype(BF16)
            for c in copies(e, base_ref[i, e]):
                c.start()
    for e in range(n_exp):
        for c in copies(e, base_ref[i, e]):
            c.wait()

    def per_expert(e, _):
        def extra(w, _):
            fill(0, e, w)
            sx_ref[0] = _dot(p_ref[0:win, :], h_ref[...]).astype(BF16)
            cs = copies(0, base_ref[i, e] + w * win)
            for c in cs:
                c.start()
            for c in cs:
                c.wait()
            return 0

        return lax.fori_loop(1, (cnt_ref[i, e] + win - 1) // win, extra, 0)

    lax.fori_loop(0, n_exp, per_expert, 0)


def _dispatch(base, cnt, h_all, rank_t, gate_t, n_rows):
    n, d = h_all.shape
    n_exp = rank_t.shape[0]
    n_tiles = n // ROUTE_TILE
    tile = lambda i, b_, c_: (0, i)
    grid_spec = pltpu.PrefetchScalarGridSpec(
        num_scalar_prefetch=2,
        grid=(n_tiles,),
        in_specs=[pl.BlockSpec((ROUTE_TILE, d), lambda i, b_, c_: (i, 0)),
                  pl.BlockSpec((n_exp, ROUTE_TILE), tile),
                  pl.BlockSpec((n_exp, ROUTE_TILE), tile),
                  pl.BlockSpec(memory_space=pl.ANY),
                  pl.BlockSpec(memory_space=pl.ANY)],
        out_specs=[pl.BlockSpec(memory_space=pl.ANY), pl.BlockSpec(memory_space=pl.ANY)],
        scratch_shapes=[pltpu.VMEM((n_exp * ROUTE_WIN, ROUTE_TILE), BF16),
                        pltpu.VMEM((n_exp, ROUTE_WIN, d), BF16),
                        pltpu.VMEM((n_exp, ROUTE_WIN, LANES), F32),
                        pltpu.SemaphoreType.DMA((n_exp,)),
                        pltpu.SemaphoreType.DMA((n_exp,))],
    )
    xs0 = jnp.zeros((n_rows, d), BF16)
    gs0 = jnp.zeros((n_rows, LANES), F32)
    return pl.pallas_call(
        functools.partial(_dispatch_kernel, n_exp=n_exp, group=8),
        grid_spec=grid_spec,
        out_shape=[jax.ShapeDtypeStruct(xs0.shape, BF16), jax.ShapeDtypeStruct(gs0.shape, F32)],
        input_output_aliases={5: 0, 6: 1},
        compiler_params=_params("arbitrary"),
        name="moe_dispatch",
    )(base, cnt, h_all, rank_t, gate_t, xs0, gs0)


def _ffn_sorted_kernel(te_ref, nu_ref, xs_ref, gs_ref, wgu_ref, bgu_ref, wd_ref, bd_ref, ys_ref, wgu_b, wd_b, *,
                       d_ff):
    r = pl.program_id(0)

    @pl.when(r < nu_ref[0])
    def _():
        @pl.when((r == 0) | (te_ref[r] != te_ref[jnp.maximum(r - 1, 0)]))
        def _():
            wgu_b[...] = wgu_ref[0].astype(BF16)
            wd_b[...] = wd_ref[0].astype(BF16)

        gu = _dot(xs_ref[...], wgu_b[...]) + bgu_ref[0]
        gate = jnp.minimum(gu[:, :d_ff], SWIGLU_LIMIT)
        up = jnp.clip(gu[:, d_ff:], -SWIGLU_LIMIT, SWIGLU_LIMIT)
        act = (up + 1) * gate * _sigmoid(SWIGLU_ALPHA * gate)
        y = _dot(act.astype(BF16), wd_b[...]) + bd_ref[0]
        ys_ref[...] = (gs_ref[:, 0:1] * y).astype(ys_ref.dtype)

    @pl.when(r >= nu_ref[0])
    def _():
        ys_ref[...] = jnp.zeros(ys_ref.shape, ys_ref.dtype)


def _ffn_sorted(tile_expert, n_used, xs, gs, wgu, bgu, wd, bd):
    n_rows, d = xs.shape
    n_exp, _, two_ff = wgu.shape
    row = lambda r, te, nu: (r, 0)
    exp3 = lambda r, te, nu: (te[r], 0, 0)
    grid_spec = pltpu.PrefetchScalarGridSpec(
        num_scalar_prefetch=2,
        grid=(n_rows // FFN_ROWS,),
        in_specs=[pl.BlockSpec((FFN_ROWS, d), row),
                  pl.BlockSpec((FFN_ROWS, LANES), row),
                  pl.BlockSpec((1, d, two_ff), exp3),
                  pl.BlockSpec((1, 1, two_ff), exp3),
                  pl.BlockSpec((1, two_ff // 2, d), exp3),
                  pl.BlockSpec((1, 1, d), exp3)],
        out_specs=pl.BlockSpec((FFN_ROWS, d), row),
        scratch_shapes=[pltpu.VMEM((d, two_ff), BF16), pltpu.VMEM((two_ff // 2, d), BF16)],
    )
    return pl.pallas_call(
        functools.partial(_ffn_sorted_kernel, d_ff=two_ff // 2),
        grid_spec=grid_spec,
        out_shape=jax.ShapeDtypeStruct((n_rows, d), BF16),
        compiler_params=_params("arbitrary"),
        name="moe_ffn_sorted",
    )(tile_expert, n_used, xs, gs, wgu, bgu, wd, bd)


def _combine_kernel(base_ref, cnt_ref, rank_ref, x1_ref, mod_ref, fg_ref, ys_hbm, o_ref, p_ref, yb_ref, acc_ref,
                    sem, *, n_exp, tile0):
    i = pl.program_id(0) + tile0
    win = ROUTE_WIN

    def copy(slot, row0):
        rows = pl.ds(pl.multiple_of(row0, ROUTE_ALIGN), win)
        return pltpu.make_async_copy(ys_hbm.at[rows], yb_ref.at[pl.ds(slot * win, win)], sem.at[slot])

    for e in range(n_exp):
        copy(e, base_ref[i, e]).start()
    for e in range(n_exp):
        p_ref[e * win:(e + 1) * win, :] = _window_onehot(rank_ref[e:e + 1, :], 0).astype(BF16)
    for e in range(n_exp):
        copy(e, base_ref[i, e]).wait()
    acc_ref[...] = _dot_tn(p_ref[...], yb_ref[...])

    def per_expert(e, _):
        def extra(w, _):
            c = copy(0, base_ref[i, e] + w * win)
            c.start()
            p = _window_onehot(rank_ref[pl.ds(e, 1), :], w).astype(BF16)
            c.wait()
            acc_ref[...] += _dot_tn(p, yb_ref[0:win, :])
            return 0

        return lax.fori_loop(1, (cnt_ref[i, e] + win - 1) // win, extra, 0)

    lax.fori_loop(0, n_exp, per_expert, 0)
    x2 = x1_ref[...] + mod_ref[5, 0] * acc_ref[...]
    o_ref[...] = _rms(x2, fg_ref[...])


def _combine(base, cnt, rank_t, x1, mod6, fg, ys, *, tile0, tiles_per_seq):
    n, d = x1.shape
    n_exp = rank_t.shape[0]
    rmod = mod6.shape[2]
    grid_spec = pltpu.PrefetchScalarGridSpec(
        num_scalar_prefetch=2,
        grid=(n // ROUTE_TILE,),
        in_specs=[pl.BlockSpec((n_exp, ROUTE_TILE), lambda i, b_, c_: (0, i + tile0)),
                  pl.BlockSpec((ROUTE_TILE, d), lambda i, b_, c_: (i, 0)),
                  pl.BlockSpec((6, 1, rmod, d), lambda i, b_, c_: (0, i // tiles_per_seq, 0, 0)),
                  pl.BlockSpec((1, d), lambda i, b_, c_: (0, 0)),
                  pl.BlockSpec(memory_space=pl.ANY)],
        out_specs=pl.BlockSpec((ROUTE_TILE, d), lambda i, b_, c_: (i, 0)),
        scratch_shapes=[pltpu.VMEM((n_exp * ROUTE_WIN, ROUTE_TILE), BF16),
                        pltpu.VMEM((n_exp * ROUTE_WIN, d), BF16),
                        pltpu.VMEM((ROUTE_TILE, d), F32),
                        pltpu.SemaphoreType.DMA((n_exp,))],
    )
    return pl.pallas_call(
        functools.partial(_combine_kernel, n_exp=n_exp, tile0=tile0),
        grid_spec=grid_spec,
        out_shape=jax.ShapeDtypeStruct((n, d), F32),
        compiler_params=_params("arbitrary"),
        name="moe_combine",
    )(base, cnt, rank_t, x1, mod6, fg, ys)


def _round_up(a, m):
    return (a + m - 1) // m * m


def _moe_routed(pre_p, pre_s, wts):
    x1_p, h2_p, gates_p, mod6_p, tps_p = pre_p
    x1_s, h2_s, gates_s, mod6_s, _ = pre_s
    n_p, d = x1_p.shape
    n_s = x1_s.shape[0]
    n_exp = wts["bgu"].shape[0]
    assert n_p % ROUTE_TILE == 0 and n_s <= ROUTE_TILE and mod6_s.shape[2] == n_s
    pad_rows = lambda a: jnp.pad(a, ((0, ROUTE_TILE - n_s), (0, 0)))
    h_all = jnp.concatenate([h2_p, pad_rows(h2_s)], axis=0)
    g_all = jnp.concatenate([gates_p, pad_rows(gates_s)], axis=0)
    n_all = h_all.shape[0]
    n_tiles = n_all // ROUTE_TILE
    rank_t, gate_t, cnt_f = _route_meta(g_all, n_exp=n_exp)

    cnt = cnt_f[:, :n_exp, 0].astype(I32)
    seg = _round_up(cnt, ROUTE_ALIGN)
    before = lambda m: jnp.arange(m)[:, None] > jnp.arange(m)[None, :]
    seg_off = jnp.sum(jnp.where(before(n_tiles)[:, :, None], seg[None, :, :], 0), axis=1)
    region = _round_up(jnp.sum(seg, axis=0) + ROUTE_WIN, FFN_ROWS)
    region_start = jnp.sum(jnp.where(before(n_exp), region[None, :], 0), axis=1)
    region_end = region_start + region
    base = (region_start[None, :] + seg_off).astype(I32)
    n_rows = _round_up(TOP_K * n_all + (ROUTE_ALIGN - 1) * n_tiles * n_exp + (ROUTE_WIN + FFN_ROWS) * n_exp, FFN_ROWS)
    tile_start = jnp.arange(n_rows // FFN_ROWS, dtype=I32) * FFN_ROWS
    tile_expert = jnp.minimum(jnp.sum((tile_start[:, None] >= region_end[None, :]).astype(I32), axis=1), n_exp - 1)
    n_used = (region_end[-1:] // FFN_ROWS).astype(I32)

    xs, gs = _dispatch(base, cnt, h_all, rank_t, gate_t, n_rows)
    ys = _ffn_sorted(tile_expert, n_used, xs, gs, wts["wgu32"], wts["bgu"], wts["wd32"], wts["bd"])
    y_p = _combine(base, cnt, rank_t, x1_p, mod6_p, wts["fg"], ys, tile0=0, tiles_per_seq=tps_p)
    mod6_sp = jnp.pad(mod6_s, ((0, 0), (0, 0), (0, ROUTE_TILE - n_s), (0, 0)))
    y_s = _combine(base, cnt, rank_t, pad_rows(x1_s), mod6_sp, wts["fg"], ys, tile0=n_p // ROUTE_TILE,
                   tiles_per_seq=1)
    return y_p, y_s[:n_s]


def _pad_cols(a, width):
    return jnp.pad(a, ((0, 0), (0, width - a.shape[1])))


def _rope_tables(pos):
    half = HEAD_DIM // 2
    inv = ROPE_THETA ** (-jnp.arange(half, dtype=F32) / half)
    ang = pos.astype(F32)[:, None] * inv[None, :]
    cos, sin = jnp.cos(ang), jnp.sin(ang)
    cos_h = jnp.concatenate([cos, cos], axis=1)
    sin_h = jnp.concatenate([-sin, sin], axis=1)
    ones = jnp.ones_like(cos_h)
    zeros = jnp.zeros_like(cos_h)
    tabs = jnp.stack([jnp.concatenate([cos_h, cos_h], 1), jnp.concatenate([sin_h, sin_h], 1),
                      jnp.concatenate([cos_h, ones], 1), jnp.concatenate([sin_h, zeros], 1)])
    return tabs


def _layer(x, mod_rows, pos, shift_prev, wkv_prev, wts, attend, *, tm, chunk):
    b, t, d = x.shape
    n = b * t
    rw, att_w, n_rw, rwkv_in = wts["rw"], wts["att_w"], wts["n_rw"], wts["rwkv_in"]
    per_token_mod = mod_rows.shape[0] != b or tm > t
    if per_token_mod:
        mod6 = jnp.repeat(mod_rows, t, axis=0).reshape(n // tm, tm, 6, d).transpose(2, 0, 1, 3)
        tiles_per_seq = 1
    else:
        mod6 = mod_rows.reshape(b, 1, 6, d).transpose(2, 0, 1, 3)
        tiles_per_seq = t // tm
    cs = _rope_tables(pos)
    n_idx_scale = wts["idx_scale"]
    cs = cs.at[2, :, HEAD_DIM:].set(n_idx_scale)
    if cs.shape[1] < tm:
        cs = jnp.tile(cs, (1, tm // cs.shape[1], 1))
    x2d = x.reshape(n, d)
    p, q, k32, kb, v32, vb, qi, kw = _inproj(x2d, mod6, wts["g1"], wts["w_cat"], cs, tm=tm,
                                             tiles_per_seq=tiles_per_seq, n_rw=n_rw, att_w=att_w)
    t_pad = -(-t // chunk) * chunk
    p3 = p.reshape(b, t, n_rw)
    if t_pad != t:
        p3 = jnp.pad(p3, ((0, 0), (0, t_pad - t), (0, 0)))
    z0 = jnp.swapaxes(wkv_prev, -1, -2)
    yr, z_fin = _rwkv(p3, shift_prev.reshape(b, 1, n_rw), z0, wts["mu"], wts["vec"], wts["w2"], wts["a2"], wts["g2"],
                      wts["seg"], L=chunk, t_valid=t if t_pad != t else chunk, rw=rw)
    yr = yr[:, :t].reshape(n, rw)
    wkv_new = jnp.swapaxes(z_fin, -1, -2)
    ya = attend(q, qi, kw, k32, kb, v32, vb).reshape(n, att_w)
    x1, h2, gates = _outproj(x2d, yr, ya, mod6, wts["w_out"], wts["g2n"], wts["router_w"], wts["router_b"], tm=tm,
                             tiles_per_seq=tiles_per_seq, rw=rw)
    pre_moe = (x1, h2, gates, mod6, 1 if per_token_mod else t // ROUTE_TILE)
    p_last = p3[:, t - 1, :]
    new_shift = jnp.concatenate([p_last[:, :3 * rw], p_last[:, 3 * rw:3 * rw + wts["lora"][0]],
                                 p_last[:, 3 * rw + 128:3 * rw + 128 + wts["lora"][1]],
                                 p_last[:, 3 * rw + 256:3 * rw + 256 + wts["lora"][2]]], axis=1)
    n_heads = att_w // HEAD_DIM
    state = (k32.reshape(b, t, n_heads, HEAD_DIM), v32.reshape(b, t, n_heads, HEAD_DIM),
             kw[:, :HEAD_DIM].reshape(b, t, HEAD_DIM), wkv_new, new_shift)
    return pre_moe, state


def kernel(x_prompt, x_sample, c_prompt, c_sample, cache_k, cache_v, cache_idx_k, state_wkv, state_shift, page_table, norm1_g, norm2_g, w_mod, b_mod, w_in, mu_shift, w0, w2, a0, a2, g2, k_k, k_a, r_k, lnx_g, lnx_b, w_out, router_w, router_b, w_gate_up, b_gate_up, w_down, b_down, final_g):
    depth = w_in.shape[0]
    assert depth == 1, "single-layer trunk"
    l = 0
    bp, tp, d = x_prompt.shape
    bs, ts, _ = x_sample.shape
    rw = w0.shape[1]
    att_w = w_out.shape[1] - rw
    lora = (w2.shape[1], a2.shape[1], g2.shape[1])
    rwkv_in = 3 * rw + sum(lora)
    n_idx = (w_in.shape[2] - rwkv_in - 3 * att_w - HEAD_DIM) // (HEAD_DIM + 1)
    n_exp = router_w.shape[2]
    n_pages = page_table.shape[1]
    past = n_pages * PAGE_SIZE
    n_rw = 3 * rw + 512
    assert lora[0] <= 128 and lora[1] <= 128 and lora[2] <= 256 and n_idx * HEAD_DIM == att_w

    wi = w_in[l]
    o = 3 * rw
    rw_cols = [wi[:, :o], _pad_cols(wi[:, o:o + lora[0]], 128), _pad_cols(wi[:, o + lora[0]:o + lora[0] + lora[1]], 128),
               _pad_cols(wi[:, o + lora[0] + lora[1]:rwkv_in], 256)]
    a0_ = rwkv_in
    att_cols = [wi[:, a0_:a0_ + 4 * att_w], _pad_cols(wi[:, a0_ + 4 * att_w:], LANES)]
    w_cat = jnp.concatenate(rw_cols + att_cols, axis=1).astype(BF16)
    mu = mu_shift[l]
    mu_pad = jnp.concatenate([mu[:o], jnp.pad(mu[o:o + lora[0]], (0, 128 - lora[0])),
                              jnp.pad(mu[o + lora[0]:o + lora[0] + lora[1]], (0, 128 - lora[1])),
                              jnp.pad(mu[o + lora[0] + lora[1]:], (0, 256 - lora[2]))]).reshape(1, n_rw)
    vec = jnp.stack([w0[l], a0[l], k_k[l], k_a[l], lnx_g[l], lnx_b[l], r_k[l].reshape(-1),
                     jnp.zeros((rw,), F32)])
    head_of = np.arange(rw) // HEAD_DIM
    seg = jnp.asarray((head_of[:, None] == head_of[None, :]).astype(np.float32)).astype(BF16)
    wts = dict(
        rw=rw, att_w=att_w, n_rw=n_rw, rwkv_in=rwkv_in, lora=lora,
        idx_scale=float((n_idx * HEAD_DIM) ** -0.5),
        g1=norm1_g[l].reshape(1, d), w_cat=w_cat, mu=mu_pad, vec=vec,
        w2=jnp.pad(w2[l], ((0, 128 - lora[0]), (0, 0))), a2=jnp.pad(a2[l], ((0, 128 - lora[1]), (0, 0))),
        g2=jnp.pad(g2[l], ((0, 256 - lora[2]), (0, 0))), seg=seg,
        w_out=w_out[l].astype(BF16), g2n=norm2_g[l].reshape(1, d),
        router_w=_pad_cols(router_w[l], LANES),
        router_b=jnp.concatenate([router_b[l], jnp.full((LANES - n_exp,), NEG_BIG, F32)]).reshape(1, LANES),
        wgu32=w_gate_up[l], bgu=b_gate_up[l].reshape(n_exp, 1, -1),
        wd32=w_down[l], bd=b_down[l].reshape(n_exp, 1, -1), fg=final_g.reshape(1, d),
    )

    c_all = jnp.concatenate([c_prompt, c_sample], axis=0)
    rows_pad = -(-c_all.shape[0] // SUBLANES) * SUBLANES
    mod_all = _modulation(jnp.pad(c_all, ((0, rows_pad - c_all.shape[0]), (0, 0))), w_mod[l], b_mod[l])
    mod_p, mod_s = mod_all[:bp], mod_all[bp:bp + bs]

    k_top_p = min(TOPK_MAX, tp // 4)

    def attend_prompt(q, qi, kw, k32, kb, v32, vb):
        tk = min(512, tp)
        r3 = lambda a: a.reshape(bp, tp, a.shape[1])
        tr = lambda a: jnp.swapaxes(r3(a), 1, 2)
        nh = att_w // HEAD_DIM
        k_hm = kb.reshape(bp, tp, nh, HEAD_DIM).transpose(0, 2, 1, 3)
        vt_ck = vb.reshape(bp, tp // tk, tk, att_w).transpose(0, 1, 3, 2)
        wit = tr(kw[:, HEAD_DIM:HEAD_DIM + SUBLANES])
        kib = r3(kw[:, :HEAD_DIM].astype(BF16))
        out_t = _dsa_prompt(tr(q), tr(qi), wit, k_hm, vt_ck, kib, tq=min(256, tp), tk=tk, k_top=k_top_p)
        return jnp.swapaxes(out_t, 1, 2)

    pre_p, st_p = _layer(x_prompt, mod_p, jnp.arange(tp), jnp.zeros((bp, n_rw), F32),
                       jnp.zeros((bp, rw // HEAD_DIM, HEAD_DIM, HEAD_DIM), F32), wts, attend_prompt,
                       tm=min(512, tp), chunk=64)

    k_top_s = min(TOPK_MAX, (past + ts) // 4)
    n_heads = att_w // HEAD_DIM
    assert ts <= SUBLANES
    pool_t = lambda c: jnp.transpose(c, (0, 1, 3, 4, 2)).reshape(c.shape[0], c.shape[1], att_w, PAGE_SIZE)
    ck_t, cv_t = pool_t(cache_k), pool_t(cache_v)
    cik_t = jnp.transpose(cache_idx_k, (0, 1, 3, 2))
    g_idx = 16 if n_pages % 16 == 0 else 1
    g_att = 8 if n_pages % 8 == 0 else 1

    def head_query_rows(a, nh):
        a = a.reshape(bs, ts, nh, HEAD_DIM).transpose(0, 2, 1, 3)
        a = jnp.pad(a, ((0, 0), (0, 0), (0, SUBLANES - ts), (0, 0)))
        return a.reshape(bs, nh * SUBLANES, HEAD_DIM)

    def new_rows_t(a):
        a = a.reshape(bs, ts, a.shape[1]).transpose(0, 2, 1)
        return jnp.pad(a, ((0, 0), (0, 0), (0, PAGE_SIZE - ts)))

    def attend_sample(q, qi, kw, k32, kb, v32, vb):
        qim = head_query_rows(qi, n_idx)
        wcol = kw[:, HEAD_DIM:HEAD_DIM + n_idx].reshape(bs, ts, n_idx).transpose(0, 2, 1)
        wcol = jnp.pad(wcol, ((0, 0), (0, 0), (0, SUBLANES - ts))).reshape(bs, n_idx * SUBLANES, 1)
        past_scores = _idx_sample(page_table, qim, wcol, cik_t, t_new=ts, g_pages=g_idx, layer=l)
        new_scores = _idx_new(qim, wcol, new_rows_t(kw[:, :HEAD_DIM]), t_new=ts)
        bias = _select_sample(past_scores, new_scores, k_top=k_top_s)
        qh = head_query_rows(q, n_heads)
        eye = jnp.eye(n_heads, dtype=BF16)
        qbd = (qh.reshape(bs, n_heads, SUBLANES, 1, HEAD_DIM) * eye[None, :, None, :, None]).reshape(
            bs, n_heads * SUBLANES, att_w)
        out = _attn_sample(page_table, qbd, bias, ck_t, cv_t, new_rows_t(k32), new_rows_t(v32), g_pages=g_att,
                           layer=l)
        return out[:, :ts]

    shift_s = state_shift[l]
    o1, o2 = o + lora[0], o + lora[0] + lora[1]
    shift_pad = jnp.concatenate([shift_s[:, :o], _pad_cols(shift_s[:, o:o1], 128), _pad_cols(shift_s[:, o1:o2], 128),
                                 _pad_cols(shift_s[:, o2:], 256)], axis=1)
    pre_s, st_s = _layer(x_sample, mod_s, past + jnp.arange(ts), shift_pad, state_wkv[l], wts, attend_sample,
                       tm=bs * ts, chunk=SUBLANES)

    y_p, y_s = _moe_routed(pre_p, pre_s, wts)
    y_p, y_s = y_p.reshape(bp, tp, d), y_s.reshape(bs, ts, d)
    ex = lambda a: a[None]
    return (y_p, y_s,ex(st_p[0]), ex(st_p[1]), ex(st_p[2]), ex(st_p[3]), ex(st_p[4]),
            ex(st_s[0]), ex(st_s[1]), ex(st_s[2]), ex(st_s[3]), ex(st_s[4]))
```

```python
import functools

import numpy as np
import jax
import jax.numpy as jnp
from jax import lax
from jax.experimental import pallas as pl
from jax.experimental.pallas import tpu as pltpu

F32 = jnp.float32
BF16 = jnp.bfloat16
I32 = jnp.int32
HI = lax.Precision.HIGHEST

HEAD_DIM = 64
PAGE_SIZE = 128
TOPK_MAX = 256
ROPE_THETA = 10000.0
TOP_K = 4
SWIGLU_LIMIT = 7.0
SWIGLU_ALPHA = 1.702
NORM_EPS = 1e-5
LNX_EPS = 64e-5

LANES = 128
SUBLANES = 8
VMEM_LIMIT = 56 * 1024 * 1024

NEG_BIG = -1e30
INT_MIN = -(2 ** 31)
KEY_NEG_INF = INT_MIN + 0x7FFFFF


def _dot(a, b, prec=None):
    return lax.dot_general(a, b, (((1,), (0,)), ((), ())), precision=prec, preferred_element_type=F32)


def _dot_nt(a, b, prec=None):
    return lax.dot_general(a, b, (((1,), (1,)), ((), ())), precision=prec, preferred_element_type=F32)


def _dot_tn(a, b, prec=None):
    return lax.dot_general(a, b, (((0,), (0,)), ((), ())), precision=prec, preferred_element_type=F32)


_NN = ((1,), (0,))
_NT = ((1,), (1,))
_TN = ((0,), (0,))


def _split(a, terms=2):
    out = []
    for i in range(terms):
        t = a.astype(BF16)
        out.append(t)
        if i + 1 < terms:
            a = a - t.astype(F32)
    return out


def _mm(a_terms, b_terms, dims=_NN, order=1):
    acc = None
    for i, a in enumerate(a_terms):
        for j, b in enumerate(b_terms):
            if i + j <= order:
                d = lax.dot_general(a, b, (dims, ((), ())), preferred_element_type=F32)
                acc = d if acc is None else acc + d
    return acc


def _params(*sem):
    return pltpu.CompilerParams(dimension_semantics=sem, vmem_limit_bytes=VMEM_LIMIT)


def _sigmoid(x):
    return jax.nn.sigmoid(x)


def _rms(x, g):
    return x * lax.rsqrt(jnp.mean(x * x, axis=-1, keepdims=True) + NORM_EPS) * g


def _mod_kernel(c_ref, w_ref, b_ref, o_ref):
    c = c_ref[...]
    o_ref[...] = _dot(c * _sigmoid(c), w_ref[...], HI) + b_ref[...]


def _modulation(c_pad, w_mod, b_mod):
    rows, d = c_pad.shape
    n = w_mod.shape[1]
    return pl.pallas_call(
        _mod_kernel,
        grid=(n // d,),
        in_specs=[pl.BlockSpec((rows, d), lambda j: (0, 0)),
                  pl.BlockSpec((d, d), lambda j: (0, j)),
                  pl.BlockSpec((1, d), lambda j: (0, j))],
        out_specs=pl.BlockSpec((rows, d), lambda j: (0, j)),
        out_shape=jax.ShapeDtypeStruct((rows, n), F32),
        compiler_params=_params("arbitrary"),
        name="modulation",
    )(c_pad, w_mod, b_mod.reshape(1, n))


def _rope_block(x, cos, sin_signed):
    lane = lax.broadcasted_iota(I32, x.shape, 1)
    partner = jnp.where((lane & 32) == 0, pltpu.roll(x, LANES - 32, axis=1), pltpu.roll(x, 32, axis=1))
    return x * cos + partner * sin_signed


def _inproj_kernel(x_ref, mod_ref, g_ref, w_ref, cs_ref, p_ref, q_ref, k_ref, kb_ref, v_ref, vb_ref, qi_ref,
                   kw_ref, *, n_rw, att_w):
    x = x_ref[...]
    h = _rms(x, g_ref[...]) * (1 + mod_ref[1, 0]) + mod_ref[0, 0]
    hb = h.astype(BF16)
    cb = 512
    for j in range(n_rw // cb):
        p_ref[:, j * cb:(j + 1) * cb] = _dot(hb, w_ref[:, j * cb:(j + 1) * cb])
    cos, sin = cs_ref[0], cs_ref[1]
    base = n_rw
    ng = att_w // LANES

    def roped(off):
        t = _dot(hb, w_ref[:, off:off + att_w])
        return [_rope_block(t[:, g * LANES:(g + 1) * LANES], cos, sin) for g in range(ng)]

    for g, blk in enumerate(roped(base)):
        q_ref[:, g * LANES:(g + 1) * LANES] = (blk * (HEAD_DIM ** -0.5)).astype(BF16)
    for g, blk in enumerate(roped(base + att_w)):
        k_ref[:, g * LANES:(g + 1) * LANES] = blk
        kb_ref[:, g * LANES:(g + 1) * LANES] = blk.astype(BF16)
    v = _dot(hb, w_ref[:, base + 2 * att_w:base + 3 * att_w])
    v_ref[...] = v
    vb_ref[...] = v.astype(BF16)
    for g, blk in enumerate(roped(base + 3 * att_w)):
        qi_ref[:, g * LANES:(g + 1) * LANES] = blk.astype(BF16)
    kw = _dot(hb, w_ref[:, base + 4 * att_w:base + 4 * att_w + LANES])
    kw_ref[...] = _rope_block(kw, cs_ref[2], cs_ref[3])


def _inproj(x2d, mod6, g1, w_cat, cs, *, tm, tiles_per_seq, n_rw, att_w):
    n, d = x2d.shape
    rmod = mod6.shape[2]
    n_cs_tiles = cs.shape[1] // tm
    grid = (n // tm,)
    row = lambda i: (i, 0)
    outs = [
        jax.ShapeDtypeStruct((n, n_rw), F32),
        jax.ShapeDtypeStruct((n, att_w), BF16),
        jax.ShapeDtypeStruct((n, att_w), F32),
        jax.ShapeDtypeStruct((n, att_w), BF16),
        jax.ShapeDtypeStruct((n, att_w), F32),
        jax.ShapeDtypeStruct((n, att_w), BF16),
        jax.ShapeDtypeStruct((n, att_w), BF16),
        jax.ShapeDtypeStruct((n, LANES), F32),
    ]
    return pl.pallas_call(
        functools.partial(_inproj_kernel, n_rw=n_rw, att_w=att_w),
        grid=grid,
        in_specs=[pl.BlockSpec((tm, d), row),
                  pl.BlockSpec((6, 1, rmod, d), lambda i: (0, i // tiles_per_seq, 0, 0)),
                  pl.BlockSpec((1, d), lambda i: (0, 0)),
                  pl.BlockSpec(w_cat.shape, lambda i: (0, 0)),
                  pl.BlockSpec((4, tm, LANES), lambda i: (0, i % n_cs_tiles, 0))],
        out_specs=[pl.BlockSpec((tm, o.shape[1]), row) for o in outs],
        out_shape=outs,
        compiler_params=_params("arbitrary"),
        name="inproj",
    )(x2d, mod6, g1, w_cat, cs)


def _softplus(x):
    return jnp.maximum(x, 0.0) + jnp.log(1.0 + jnp.exp(-jnp.abs(x)))


def _rwkv_kernel(p_ref, sh0_ref, z0_ref, mu_ref, vec_ref, w2_ref, a2_ref, g2_ref, seg_ref, y_ref, zout_ref,
                 carry_ref, z_ref, ops_ref, yh_ref, *, L, nb, t_valid, rw, n_heads):
    c = pl.program_id(1)
    R = nb * L

    @pl.when(c == 0)
    def _():
        carry_ref[...] = sh0_ref[...]
        z_ref[...] = z0_ref[...].reshape(z_ref.shape)

    p = p_ref[...].reshape(R, p_ref.shape[2])
    row = lax.broadcasted_iota(I32, (nb, L, 1), 1).reshape(R, 1)
    seq = lax.broadcasted_iota(I32, (nb, L, 1), 0).reshape(R, 1)

    def per_seq_rows(vals):
        out = vals[0]
        for s in range(1, nb):
            out = jnp.where(seq >= s, vals[s], out)
        return out

    prev = jnp.where(row == 0, per_seq_rows([carry_ref[s] for s in range(nb)]), pltpu.roll(p, 1, axis=0))
    for s in range(nb):
        carry_ref[s] = p[(s + 1) * L - 1:(s + 1) * L, :]
    z = p + (prev - p) * mu_ref[...]
    r, k, v = z[:, 0:rw], z[:, rw:2 * rw], z[:, 2 * rw:3 * rw]
    o = 3 * rw
    zw, za, zg = z[:, o:o + 128], z[:, o + 128:o + 256], z[:, o + 256:o + 512]
    w0, a0, k_k, k_a = vec_ref[0:1], vec_ref[1:2], vec_ref[2:3], vec_ref[3:4]
    lnx_g, lnx_b, r_k = vec_ref[4:5], vec_ref[5:6], vec_ref[6:7]
    seg = [seg_ref[...]]

    def seg_sum(t):
        return _mm(_split(t, 3), seg, order=2)

    def lora(t, w_ref):
        return _mm(_split(t), _split(w_ref[...]))

    w_log = -_softplus(-(w0 + lora(jnp.tanh(zw), w2_ref))) - 0.5
    lw = -jnp.exp(w_log)
    a = _sigmoid(a0 + lora(za, a2_ref))
    g = lora(_sigmoid(zg), g2_ref)
    kk = k * k_k
    kkn = kk / jnp.maximum(jnp.sqrt(seg_sum(kk * kk)), 1e-12)
    k2 = k * (1 + (a - 1) * k_a)
    an, bn = -kkn, kkn * a
    if t_valid < L:
        valid = row < t_valid
        lw = jnp.where(valid, lw, 0.0)
        an, bn = jnp.where(valid, an, 0.0), jnp.where(valid, bn, 0.0)
        k2s, vs = jnp.where(valid, k2, 0.0), jnp.where(valid, v, 0.0)
    else:
        k2s, vs = k2, v

    ri = lax.broadcasted_iota(I32, (L, L), 0)
    ci = lax.broadcasted_iota(I32, (L, L), 1)
    incl, strict = ri >= ci, ri > ci
    lw_terms = _split(lw, 3)
    rr = lax.broadcasted_iota(I32, (R, R), 0)
    rc = lax.broadcasted_iota(I32, (R, R), 1)
    same_seq_incl = (rr >= rc) & (rc >= 0)
    for s in range(1, nb):
        same_seq_incl = same_seq_incl & ((rr < s * L) | (rc >= s * L))
    cum = _mm([same_seq_incl.astype(BF16)], lw_terms, order=2)
    c_last = per_seq_rows([cum[(s + 1) * L - 1:(s + 1) * L, :] for s in range(nb)])
    e_c, e_cp, e_n, e_d = jnp.exp(cum), jnp.exp(cum - lw), jnp.exp(-cum), jnp.exp(c_last - cum)
    ops_ref[0] = an * e_cp
    ops_ref[1] = r * e_c
    ops_ref[2] = bn * e_n
    ops_ref[3] = k2s * e_n
    ops_ref[4] = bn * e_d
    ops_ref[5] = k2s * e_d
    ops_ref[6] = vs
    wl_cols = [jnp.exp(_mm(lw_terms, [jnp.broadcast_to(seq == s, (R, LANES)).astype(BF16)], _TN, order=2))
               for s in range(nb)]
    eye = (ri == ci).astype(F32)
    n_sq = max(int(np.ceil(np.log2(L))) - 1, 0)

    hs = range(nb * n_heads)
    sls = [slice((h % n_heads) * HEAD_DIM, (h % n_heads + 1) * HEAD_DIM) for h in hs]
    rws = [slice((h // n_heads) * L, (h // n_heads + 1) * L) for h in hs]
    at, rt, bt, kt, bh, kh, vh = [[_split(ops_ref[i, rws[h], sls[h]]) for h in hs] for i in range(7)]
    m_ab = [jnp.where(strict, _mm(at[h][:1], bt[h][:1], _NT), 0.0) for h in hs]
    t_inv = [eye + m_ab[h] for h in hs]
    pw = m_ab
    for _ in range(n_sq):
        pw_b = [pw[h].astype(BF16) for h in hs]
        pw = [_mm([pw_b[h]], [pw_b[h]]) for h in hs]
        t_inv = [t_inv[h] + _mm([t_inv[h].astype(BF16)], [pw[h].astype(BF16)]) for h in hs]
    m_ak = [jnp.where(strict, _mm(at[h][:1], kt[h][:1], _NT), 0.0) for h in hs]
    m_rb = [jnp.where(incl, _mm(rt[h][:1], bt[h][:1], _NT), 0.0).astype(BF16) for h in hs]
    m_rk = [jnp.where(incl, _mm(rt[h][:1], kt[h][:1], _NT), 0.0).astype(BF16) for h in hs]
    zs = [z_ref[h] for h in hs]
    zs_t = [_split(zs[h]) for h in hs]
    rhs = [_mm(at[h], zs_t[h]) + _mm(_split(m_ak[h]), vh[h]) for h in hs]
    u_t = [_split(_mm(_split(t_inv[h]), _split(rhs[h]))) for h in hs]
    for h in hs:
        yh_ref[rws[h], sls[h]] = (_mm(rt[h][:1], zs_t[h][:1]) + _mm([m_rb[h]], u_t[h][:1])
                                  + _mm([m_rk[h]], vh[h][:1]))
    for h in hs:
        z_ref[h] = (wl_cols[h // n_heads][sls[h], 0:HEAD_DIM] * zs[h] + _mm(bh[h], u_t[h], _TN)
                    + _mm(kh[h], vh[h], _TN))

    y = yh_ref[...]
    inv_n = 1.0 / HEAD_DIM
    mean = seg_sum(y) * inv_n
    yc = y - mean
    var = seg_sum(yc * yc) * inv_n
    yn = yc * lax.rsqrt(var + LNX_EPS) * lnx_g + lnx_b
    bonus = seg_sum(r * k2 * r_k) * v
    y_ref[...] = ((yn + bonus) * g).astype(y_ref.dtype).reshape(y_ref.shape)

    @pl.when(c == pl.num_programs(1) - 1)
    def _():
        zout_ref[...] = z_ref[...].reshape(zout_ref.shape)


def _rwkv(p3, shift0, z0, mu, vec, w2, a2, g2, seg, *, L, nb, t_valid, rw):
    b, t, n_rw = p3.shape
    n_heads = rw // HEAD_DIM
    assert b % nb == 0
    const2 = lambda i, c: (0, 0)
    return pl.pallas_call(
        functools.partial(_rwkv_kernel, L=L, nb=nb, t_valid=t_valid, rw=rw, n_heads=n_heads),
        grid=(b // nb, t // L),
        in_specs=[pl.BlockSpec((nb, L, n_rw), lambda i, c: (i, c, 0)),
                  pl.BlockSpec((nb, 1, n_rw), lambda i, c: (i, 0, 0)),
                  pl.BlockSpec((nb, n_heads, HEAD_DIM, HEAD_DIM), lambda i, c: (i, 0, 0, 0)),
                  pl.BlockSpec((1, n_rw), const2),
                  pl.BlockSpec(vec.shape, const2),
                  pl.BlockSpec(w2.shape, const2),
                  pl.BlockSpec(a2.shape, const2),
                  pl.BlockSpec(g2.shape, const2),
                  pl.BlockSpec(seg.shape, const2)],
        out_specs=[pl.BlockSpec((nb, L, rw), lambda i, c: (i, c, 0)),
                   pl.BlockSpec((nb, n_heads, HEAD_DIM, HEAD_DIM), lambda i, c: (i, 0, 0, 0))],
        out_shape=[jax.ShapeDtypeStruct((b, t, rw), BF16),
                   jax.ShapeDtypeStruct((b, n_heads, HEAD_DIM, HEAD_DIM), F32)],
        scratch_shapes=[pltpu.VMEM((nb, 1, n_rw), F32),
                        pltpu.VMEM((nb * n_heads, HEAD_DIM, HEAD_DIM), F32),
                        pltpu.VMEM((7, nb * L, rw), F32),
                        pltpu.VMEM((nb * L, rw), F32)],
        compiler_params=_params("arbitrary", "arbitrary"),
        name="rwkv_scan",
    )(p3, shift0, z0, mu, vec, w2, a2, g2, seg)


def _key_to_float(key):
    bits = key ^ ((key >> 31) & 0x7FFFFFFF)
    return lax.bitcast_convert_type(bits, F32)


def _fold_lanes(m):
    acc = m[:, 0:LANES]
    for j in range(1, m.shape[1] // LANES):
        acc = acc + m[:, j * LANES:(j + 1) * LANES]
    return acc


def _select_bias(sc_ref, pb_ref, nc, n_q, tk, k_top, n_keys_total, key_axis=1):
    if key_axis == 1:
        q_shape, part_shape = (n_q, 1), (n_q, LANES)
        fold = lambda m: _fold_lanes(m.astype(F32))
        total = lambda acc: jnp.sum(acc, axis=1, keepdims=True)
    else:
        part_rows = min(tk, 8 * SUBLANES)
        q_shape, part_shape = (1, n_q), (part_rows, n_q)
        fold = lambda m: jnp.sum(m.astype(F32).reshape(tk // part_rows, part_rows, n_q), axis=0)
        total = lambda acc: jnp.sum(acc, axis=0, keepdims=True)

    def count(pred):
        def body(c, acc):
            return acc + fold(pred(c, sc_ref[c]))
        return total(lax.fori_loop(0, nc, body, jnp.zeros(part_shape, F32)))

    kf = float(k_top)
    cnt0 = count(lambda c, s: s >= 0.0)
    tau0 = jnp.where(cnt0 >= kf, 0, INT_MIN).astype(I32)

    def bit_body(i, tau):
        cand = tau + jnp.left_shift(jnp.int32(1), 30 - i)
        cand_f = _key_to_float(cand)
        cnt = count(lambda c, s: s >= cand_f)
        return jnp.where((cnt >= kf) | (cand <= KEY_NEG_INF), cand, tau)

    tau = lax.fori_loop(0, 31, bit_body, tau0)
    tau_f = _key_to_float(tau)
    need = kf - count(lambda c, s: s > tau_f)
    excess = count(lambda c, s: s == tau_f) - need

    def kidx(c):
        return c * tk + lax.broadcasted_iota(I32, (1, tk) if key_axis == 1 else (tk, 1), key_axis)

    n_bits = max(int(np.ceil(np.log2(n_keys_total))), 1)
    pb_ref[...] = jnp.full(q_shape, n_keys_total, I32)

    @pl.when(jnp.max(excess) > 0.0)
    def _():
        def tie_body(i, pb):
            cand = pb + jnp.left_shift(jnp.int32(1), n_bits - 1 - i)
            cnt = count(lambda c, s: (s == tau_f) & (kidx(c) < cand))
            return jnp.where(cnt < need, cand, pb)

        pb_ref[...] = lax.fori_loop(0, n_bits, tie_body, jnp.zeros(q_shape, I32))

    pb = pb_ref[...]

    def write(c, _):
        s = sc_ref[c]
        sel = ((s > tau_f) | ((s == tau_f) & (kidx(c) <= pb))) & (s > -jnp.inf)
        sc_ref[c] = jnp.where(sel, 0.0, NEG_BIG)
        return 0

    lax.fori_loop(0, nc, write, 0)


def _dsa_prompt_kernel(qt_ref, qit_ref, wit_ref, k_ref, vt_ref, ki_ref, o_ref, sc_ref, pb_ref, m_ref, l_ref, acc_ref,
                       *, tq, tk, k_top, n_heads, n_idx, t_total):
    qb = pl.program_id(1)
    q0 = qb * tq
    nc = (q0 + tq + tk - 1) // tk
    qpos = q0 + lax.broadcasted_iota(I32, (1, tq), 1)
    hd = lambda h: slice(h * HEAD_DIM, (h + 1) * HEAD_DIM)

    def scores(c, _):
        kic = ki_ref[0, pl.ds(pl.multiple_of(c * tk, tk), tk), :]
        acc = jnp.zeros((tk, tq), F32)
        for h in range(n_idx):
            acc = acc + jnp.maximum(_dot(kic, qit_ref[0, hd(h), :]), 0.0) * wit_ref[0, h:h + 1, :]
        spos = c * tk + lax.broadcasted_iota(I32, (tk, 1), 0)
        sc_ref[c] = jnp.where(spos <= qpos, acc, -jnp.inf)
        return 0

    lax.fori_loop(0, nc, scores, 0)
    _select_bias(sc_ref, pb_ref, nc, tq, tk, k_top, t_total, key_axis=0)

    m_ref[...] = jnp.full(m_ref.shape, NEG_BIG, F32)
    l_ref[...] = jnp.zeros(l_ref.shape, F32)
    acc_ref[...] = jnp.zeros(acc_ref.shape, F32)

    def attend(c, _):
        rows = pl.ds(pl.multiple_of(c * tk, tk), tk)
        bias = sc_ref[c]
        m_all, l_all, acc_all = m_ref[...], l_ref[...], acc_ref[...]
        hs = range(n_heads)
        s = [_dot(k_ref[0, h, rows, :], qt_ref[0, hd(h), :]) + bias for h in hs]
        m_new = [jnp.maximum(m_all[h:h + 1], jnp.max(s[h], axis=0, keepdims=True)) for h in hs]
        p = [jnp.exp(s[h] - m_new[h]) for h in hs]
        pv = [_dot(vt_ref[0, c, hd(h), :], p[h].astype(BF16)) for h in hs]
        alpha = [jnp.exp(m_all[h:h + 1] - m_new[h]) for h in hs]
        l_new = [alpha[h] * l_all[h:h + 1] + jnp.sum(p[h], axis=0, keepdims=True) for h in hs]
        acc_ref[...] = jnp.concatenate([alpha[h] * acc_all[hd(h)] + pv[h] for h in hs], axis=0)
        m_ref[...] = jnp.concatenate(m_new, axis=0)
        l_ref[...] = jnp.concatenate(l_new, axis=0)
        return 0

    lax.fori_loop(0, nc, attend, 0)
    l_all = l_ref[...]
    for h in range(n_heads):
        o_ref[0, hd(h), :] = (acc_ref[hd(h), :] / l_all[h:h + 1]).astype(o_ref.dtype)


def _dsa_prompt(qt, qit, wit, k_hm, vt_ck, ki, *, tq, tk, k_top):
    b, att_w, t = qt.shape
    n_heads = att_w // HEAD_DIM
    n_idx = qit.shape[1] // HEAD_DIM
    assert n_heads == SUBLANES
    qblk = lambda i, j: (i, 0, j)
    return pl.pallas_call(
        functools.partial(_dsa_prompt_kernel, tq=tq, tk=tk, k_top=k_top, n_heads=n_heads, n_idx=n_idx,
                          t_total=t),
        grid=(b, t // tq),
        in_specs=[pl.BlockSpec((1, att_w, tq), qblk),
                  pl.BlockSpec((1, qit.shape[1], tq), qblk),
                  pl.BlockSpec((1, wit.shape[1], tq), qblk),
                  pl.BlockSpec((1,) + k_hm.shape[1:], lambda i, j: (i, 0, 0, 0)),
                  pl.BlockSpec((1,) + vt_ck.shape[1:], lambda i, j: (i, 0, 0, 0)),
                  pl.BlockSpec((1, t, HEAD_DIM), lambda i, j: (i, 0, 0))],
        out_specs=pl.BlockSpec((1, att_w, tq), qblk),
        out_shape=jax.ShapeDtypeStruct((b, att_w, t), BF16),
        scratch_shapes=[pltpu.VMEM((t // tk, tk, tq), F32),
                        pltpu.VMEM((1, tq), I32),
                        pltpu.VMEM((n_heads, tq), F32),
                        pltpu.VMEM((n_heads, tq), F32),
                        pltpu.VMEM((att_w, tq), F32)],
        compiler_params=_params("arbitrary", "arbitrary"),
        name="dsa_prompt",
    )(qt, qit, wit, k_hm, vt_ck, ki)


def _idx_page_scores(qim, wcol, ki_t, n_idx):
    d = jnp.maximum(_dot(qim, ki_t.astype(BF16)), 0.0) * wcol
    sc = d[0:SUBLANES]
    for h in range(1, n_idx):
        sc = sc + d[h * SUBLANES:(h + 1) * SUBLANES]
    return sc


def _idx_sample_kernel(pt_ref, qim_ref, wcol_ref, *refs, t_new, n_idx, g_pages):
    page_refs, o_ref = refs[:g_pages], refs[g_pages]
    qrow = lax.broadcasted_iota(I32, (SUBLANES, PAGE_SIZE), 0)
    for j in range(g_pages):
        sc = _idx_page_scores(qim_ref[0], wcol_ref[0], page_refs[j][0, 0], n_idx)
        o_ref[j] = jnp.where(qrow < t_new, sc, -jnp.inf)


def _idx_sample(page_table, qim, wcol, cache_idx_t, *, t_new, g_pages, layer):
    b, n_pages = page_table.shape
    n_idx = qim.shape[1] // SUBLANES
    per_seq = lambda i, g, pt: (i, 0, 0)
    page_specs = [pl.BlockSpec((1, 1, HEAD_DIM, PAGE_SIZE),
                               lambda i, g, pt, j=j: (layer, pt[i, g * g_pages + j], 0, 0))
                  for j in range(g_pages)]
    grid_spec = pltpu.PrefetchScalarGridSpec(
        num_scalar_prefetch=1,
        grid=(b, n_pages // g_pages),
        in_specs=[pl.BlockSpec((1,) + qim.shape[1:], per_seq),
                  pl.BlockSpec((1,) + wcol.shape[1:], per_seq)] + page_specs,
        out_specs=pl.BlockSpec((g_pages, SUBLANES, PAGE_SIZE), lambda i, g, pt: (g, i, 0)),
    )
    return pl.pallas_call(
        functools.partial(_idx_sample_kernel, t_new=t_new, n_idx=n_idx, g_pages=g_pages),
        grid_spec=grid_spec,
        out_shape=jax.ShapeDtypeStruct((n_pages, b * SUBLANES, PAGE_SIZE), F32),
        compiler_params=_params("arbitrary", "arbitrary"),
        name="idx_sample",
    )(page_table, qim, wcol, *([cache_idx_t] * g_pages))


def _idx_new_kernel(qim_ref, wcol_ref, new_ref, o_ref, *, t_new, n_idx):
    sc = _idx_page_scores(qim_ref[0], wcol_ref[0], new_ref[0], n_idx)
    qrow = lax.broadcasted_iota(I32, (SUBLANES, PAGE_SIZE), 0)
    lane = lax.broadcasted_iota(I32, (SUBLANES, PAGE_SIZE), 1)
    o_ref[...] = jnp.where((qrow < t_new) & (lane <= qrow), sc, -jnp.inf)


def _idx_new(qim, wcol, ki_new, *, t_new):
    b = qim.shape[0]
    n_idx = qim.shape[1] // SUBLANES
    per_seq = lambda i: (i, 0, 0)
    return pl.pallas_call(
        functools.partial(_idx_new_kernel, t_new=t_new, n_idx=n_idx),
        grid=(b,),
        in_specs=[pl.BlockSpec((1,) + qim.shape[1:], per_seq),
                  pl.BlockSpec((1,) + wcol.shape[1:], per_seq),
                  pl.BlockSpec((1, HEAD_DIM, PAGE_SIZE), per_seq)],
        out_specs=pl.BlockSpec((SUBLANES, PAGE_SIZE), lambda i: (i, 0)),
        out_shape=jax.ShapeDtypeStruct((b * SUBLANES, PAGE_SIZE), F32),
        compiler_params=_params("arbitrary"),
        name="idx_new",
    )(qim, wcol, ki_new)


def _select_sample_kernel(past_ref, new_ref, o_ref, pb_ref, *, k_top):
    n_pages, rows = past_ref.shape[0], past_ref.shape[1]
    o_ref[0:n_pages] = past_ref[...]
    o_ref[n_pages] = new_ref[...]
    _select_bias(o_ref, pb_ref, n_pages + 1, rows, PAGE_SIZE, k_top, (n_pages + 1) * PAGE_SIZE)


def _select_sample(past_scores, new_scores, *, k_top):
    n_pages, rows, _ = past_scores.shape
    out_shape = (n_pages + 1, rows, PAGE_SIZE)
    return pl.pallas_call(
        functools.partial(_select_sample_kernel, k_top=k_top),
        grid=(1,),
        in_specs=[pl.BlockSpec(past_scores.shape, lambda i: (0, 0, 0)),
                  pl.BlockSpec(new_scores.shape, lambda i: (0, 0))],
        out_specs=pl.BlockSpec(out_shape, lambda i: (0, 0, 0)),
        out_shape=jax.ShapeDtypeStruct(out_shape, F32),
        scratch_shapes=[pltpu.VMEM((rows, 1), I32)],
        compiler_params=_params("arbitrary"),
        name="select_sample",
    )(past_scores, new_scores)


def _attn_sample_kernel(pt_ref, qbd_ref, bias_ref, biasn_ref, knew_ref, vnew_ref, *refs, n_heads, g_pages):
    k_refs, v_refs = refs[:g_pages], refs[g_pages:2 * g_pages]
    o_ref, m_ref, l_ref, acc_ref = refs[2 * g_pages:]
    g = pl.program_id(1)

    @pl.when(g == 0)
    def _():
        m_ref[...] = jnp.full(m_ref.shape, NEG_BIG, F32)
        l_ref[...] = jnp.zeros(l_ref.shape, F32)
        acc_ref[...] = jnp.zeros(acc_ref.shape, F32)

    def step(kts, vts, biases):
        bias = jnp.concatenate([jnp.concatenate([b_] * n_heads, axis=0) for b_ in biases], axis=1)
        s = jnp.concatenate([_dot(qbd_ref[0], kt.astype(BF16)) for kt in kts], axis=1) + bias
        m = m_ref[...]
        m_new = jnp.maximum(m, jnp.max(s, axis=-1, keepdims=True))
        alpha = jnp.exp(m - m_new)
        pr = jnp.exp(s - m_new)
        prb = pr.astype(BF16)
        pv = None
        for j, vt in enumerate(vts):
            d = _dot_nt(prb[:, j * PAGE_SIZE:(j + 1) * PAGE_SIZE], vt.astype(BF16))
            pv = d if pv is None else pv + d
        l_ref[...] = alpha * l_ref[...] + jnp.sum(pr, axis=-1, keepdims=True)
        acc_ref[...] = alpha * acc_ref[...] + pv
        m_ref[...] = m_new

    step([r[0, 0] for r in k_refs], [r[0, 0] for r in v_refs], [bias_ref[j] for j in range(g_pages)])

    @pl.when(g == pl.num_programs(1) - 1)
    def _():
        step([knew_ref[0]], [vnew_ref[0]], [biasn_ref[0]])
        full = acc_ref[...] / l_ref[...]
        col_head = lax.broadcasted_iota(I32, (SUBLANES, full.shape[1]), 1) // HEAD_DIM
        out = jnp.zeros((SUBLANES, full.shape[1]), F32)
        for h in range(n_heads):
            out = out + jnp.where(col_head == h, full[h * SUBLANES:(h + 1) * SUBLANES], 0.0)
        o_ref[0] = out.astype(o_ref.dtype)


def _attn_sample(page_table, qbd, bias, cache_k_t, cache_v_t, k_new_t, v_new_t, *, g_pages, layer):
    b, n_pages = page_table.shape
    att_w = qbd.shape[2]
    n_heads = att_w // HEAD_DIM
    per_seq = lambda i, g, pt: (i, 0, 0)
    page_specs = [pl.BlockSpec((1, 1, att_w, PAGE_SIZE),
                               lambda i, g, pt, j=j: (layer, pt[i, g * g_pages + j], 0, 0))
                  for j in range(g_pages)]
    new_spec = pl.BlockSpec((1, att_w, PAGE_SIZE), per_seq)
    grid_spec = pltpu.PrefetchScalarGridSpec(
        num_scalar_prefetch=1,
        grid=(b, n_pages // g_pages),
        in_specs=[pl.BlockSpec((1,) + qbd.shape[1:], per_seq),
                  pl.BlockSpec((g_pages, SUBLANES, PAGE_SIZE), lambda i, g, pt: (g, i, 0)),
                  pl.BlockSpec((1, SUBLANES, PAGE_SIZE), lambda i, g, pt: (n_pages, i, 0)),
                  new_spec, new_spec] + page_specs + page_specs,
        out_specs=pl.BlockSpec((1, SUBLANES, att_w), per_seq),
        scratch_shapes=[pltpu.VMEM((n_heads * SUBLANES, 1), F32),
                        pltpu.VMEM((n_heads * SUBLANES, 1), F32),
                        pltpu.VMEM((n_heads * SUBLANES, att_w), F32)],
    )
    return pl.pallas_call(
        functools.partial(_attn_sample_kernel, n_heads=n_heads, g_pages=g_pages),
        grid_spec=grid_spec,
        out_shape=jax.ShapeDtypeStruct((b, SUBLANES, att_w), BF16),
        compiler_params=_params("arbitrary", "arbitrary"),
        name="attn_sample",
    )(page_table, qbd, bias, bias, k_new_t, v_new_t, *([cache_k_t] * g_pages), *([cache_v_t] * g_pages))


def _outproj_kernel(x_ref, yr_ref, ya_ref, mod_ref, wo_ref, g_ref, rw_ref, rb_ref, x1_ref, h2_ref, gates_ref, *, rw):
    mixed = _dot(yr_ref[...], wo_ref[0:rw, :]) + _dot(ya_ref[...], wo_ref[rw:, :])
    x1 = x_ref[...] + mod_ref[2, 0] * mixed
    x1_ref[...] = x1
    h2 = _rms(x1, g_ref[...]) * (1 + mod_ref[4, 0]) + mod_ref[3, 0]
    h2_ref[...] = h2.astype(BF16)
    logits = _dot(h2, rw_ref[...], HI) + rb_ref[...]
    lane = lax.broadcasted_iota(I32, logits.shape, 1).astype(F32)
    work = logits
    picked = jnp.zeros(logits.shape, jnp.bool_)
    top = None
    for i in range(TOP_K):
        mx = jnp.max(work, axis=-1, keepdims=True)
        first = jnp.min(jnp.where(work == mx, lane, float(LANES)), axis=-1, keepdims=True)
        hit = lane == first
        picked = picked | hit
        work = jnp.where(hit, -jnp.inf, work)
        if i == 0:
            top = mx
    e = jnp.where(picked, jnp.exp(logits - top), 0.0)
    gates_ref[...] = e / jnp.sum(e, axis=-1, keepdims=True)


def _outproj(x2d, yr, ya, mod6, w_out_b, g2n, router_w_pad, router_b_pad, *, tm, tiles_per_seq, rw):
    n, d = x2d.shape
    rmod = mod6.shape[2]
    row = lambda i: (i, 0)
    const = lambda i: (0, 0)
    outs = [jax.ShapeDtypeStruct((n, d), F32), jax.ShapeDtypeStruct((n, d), BF16),
            jax.ShapeDtypeStruct((n, LANES), F32)]
    return pl.pallas_call(
        functools.partial(_outproj_kernel, rw=rw),
        grid=(n // tm,),
        in_specs=[pl.BlockSpec((tm, d), row),
                  pl.BlockSpec((tm, yr.shape[1]), row),
                  pl.BlockSpec((tm, ya.shape[1]), row),
                  pl.BlockSpec((6, 1, rmod, d), lambda i: (0, i // tiles_per_seq, 0, 0)),
                  pl.BlockSpec(w_out_b.shape, const),
                  pl.BlockSpec((1, d), const),
                  pl.BlockSpec(router_w_pad.shape, const),
                  pl.BlockSpec((1, LANES), const)],
        out_specs=[pl.BlockSpec((tm, o.shape[1]), row) for o in outs],
        out_shape=outs,
        compiler_params=_params("arbitrary"),
        name="outproj_router",
    )(x2d, yr, ya, mod6, w_out_b, g2n, router_w_pad, router_b_pad)


ROUTE_TILE = 512
ROUTE_WIN = 128
ROUTE_ALIGN = 16
FFN_ROWS = 512


def _route_meta_kernel(g_ref, rank_ref, gate_ref, cnt_ref, *, n_exp):
    gt = jnp.transpose(g_ref[...])
    sel = gt > 0.0
    tt = gt.shape[1]
    earlier = (lax.broadcasted_iota(I32, (tt, tt), 0) < lax.broadcasted_iota(I32, (tt, tt), 1)).astype(BF16)
    rank = _dot(sel.astype(BF16), earlier)
    rank_ref[...] = jnp.where(sel, rank, -1.0)[:n_exp]
    gate_ref[...] = gt[:n_exp]
    cnt_ref[0] = jnp.broadcast_to(jnp.sum(sel.astype(F32), axis=-1, keepdims=True), (LANES, LANES))


def _route_meta(gates, *, n_exp):
    n = gates.shape[0]
    n_tiles = n // ROUTE_TILE
    return pl.pallas_call(
        functools.partial(_route_meta_kernel, n_exp=n_exp),
        grid=(n_tiles,),
        in_specs=[pl.BlockSpec((ROUTE_TILE, LANES), lambda i: (i, 0))],
        out_specs=[pl.BlockSpec((n_exp, ROUTE_TILE), lambda i: (0, i)),
                   pl.BlockSpec((n_exp, ROUTE_TILE), lambda i: (0, i)),
                   pl.BlockSpec((1, LANES, LANES), lambda i: (i, 0, 0))],
        out_shape=[jax.ShapeDtypeStruct((n_exp, n), F32), jax.ShapeDtypeStruct((n_exp, n), F32),
                   jax.ShapeDtypeStruct((n_tiles, LANES, LANES), F32)],
        compiler_params=_params("arbitrary"),
        name="route_meta",
    )(gates)


def _window_onehot(rank_row, w):
    rows = lax.broadcasted_iota(I32, (ROUTE_WIN, 1), 0).astype(F32) + float(ROUTE_WIN) * w
    return rank_row == rows


def _dispatch_kernel(base_ref, cnt_ref, h_ref, rank_ref, gate_ref, xs_in, gs_in, xs_out, gs_out, p_ref, sx_ref,
                     sg_ref, sem_x, sem_g, *, n_exp, group):
    del xs_in, gs_in
    i = pl.program_id(0)
    h = h_ref[...]
    win = ROUTE_WIN

    def copies(slot, row0):
        rows = pl.ds(pl.multiple_of(row0, ROUTE_ALIGN), win)
        return (pltpu.make_async_copy(sx_ref.at[slot], xs_out.at[rows], sem_x.at[slot]),
                pltpu.make_async_copy(sg_ref.at[slot], gs_out.at[rows], sem_g.at[slot]))

    def fill(slot, e, w):
        p = _window_onehot(rank_ref[pl.ds(e, 1), :], w)
        p_ref[slot * win:(slot + 1) * win, :] = p.astype(BF16)
        g = jnp.sum(jnp.where(p, gate_ref[pl.ds(e, 1), :], 0.0), axis=-1, keepdims=True)
        sg_ref[slot] = jnp.broadcast_to(g, (win, LANES))

    for e0 in range(0, n_exp, group):
        for j in range(group):
            fill(e0 + j, e0 + j, 0)
        x = _dot(p_ref[e0 * win:(e0 + group) * win, :], h)
        for j in range(group):
            e = e0 + j
            sx_ref[e] = x[j * win:(j + 1) * win].astype(BF16)
            for c in copies(e, base_ref[i, e]):
                c.start()
    for e in range(n_exp):
        for c in copies(e, base_ref[i, e]):
            c.wait()

    def per_expert(e, _):
        def extra(w, _):
            fill(0, e, w)
            sx_ref[0] = _dot(p_ref[0:win, :], h_ref[...]).astype(BF16)
            cs = copies(0, base_ref[i, e] + w * win)
            for c in cs:
                c.start()
            for c in cs:
                c.wait()
            return 0

        return lax.fori_loop(1, (cnt_ref[i, e] + win - 1) // win, extra, 0)

    lax.fori_loop(0, n_exp, per_expert, 0)


def _dispatch(base, cnt, h_all, rank_t, gate_t, n_rows):
    n, d = h_all.shape
    n_exp = rank_t.shape[0]
    n_tiles = n // ROUTE_TILE
    tile = lambda i, b_, c_: (0, i)
    grid_spec = pltpu.PrefetchScalarGridSpec(
        num_scalar_prefetch=2,
        grid=(n_tiles,),
        in_specs=[pl.BlockSpec((ROUTE_TILE, d), lambda i, b_, c_: (i, 0)),
                  pl.BlockSpec((n_exp, ROUTE_TILE), tile),
                  pl.BlockSpec((n_exp, ROUTE_TILE), tile),
                  pl.BlockSpec(memory_space=pl.ANY),
                  pl.BlockSpec(memory_space=pl.ANY)],
        out_specs=[pl.BlockSpec(memory_space=pl.ANY), pl.BlockSpec(memory_space=pl.ANY)],
        scratch_shapes=[pltpu.VMEM((n_exp * ROUTE_WIN, ROUTE_TILE), BF16),
                        pltpu.VMEM((n_exp, ROUTE_WIN, d), BF16),
                        pltpu.VMEM((n_exp, ROUTE_WIN, LANES), F32),
                        pltpu.SemaphoreType.DMA((n_exp,)),
                        pltpu.SemaphoreType.DMA((n_exp,))],
    )
    xs0 = jnp.zeros((n_rows, d), BF16)
    gs0 = jnp.zeros((n_rows, LANES), F32)
    return pl.pallas_call(
        functools.partial(_dispatch_kernel, n_exp=n_exp, group=8),
        grid_spec=grid_spec,
        out_shape=[jax.ShapeDtypeStruct(xs0.shape, BF16), jax.ShapeDtypeStruct(gs0.shape, F32)],
        input_output_aliases={5: 0, 6: 1},
        compiler_params=_params("arbitrary"),
        name="moe_dispatch",
    )(base, cnt, h_all, rank_t, gate_t, xs0, gs0)


def _ffn_sorted_kernel(te_ref, nu_ref, xs_ref, gs_ref, wgu_ref, bgu_ref, wd_ref, bd_ref, ys_ref, wgu_b, wd_b, *,
                       d_ff):
    r = pl.program_id(0)

    @pl.when(r < nu_ref[0])
    def _():
        @pl.when((r == 0) | (te_ref[r] != te_ref[jnp.maximum(r - 1, 0)]))
        def _():
            wgu_b[...] = wgu_ref[0].astype(BF16)
            wd_b[...] = wd_ref[0].astype(BF16)

        gu = _dot(xs_ref[...], wgu_b[...]) + bgu_ref[0]
        gate = jnp.minimum(gu[:, :d_ff], SWIGLU_LIMIT)
        up = jnp.clip(gu[:, d_ff:], -SWIGLU_LIMIT, SWIGLU_LIMIT)
        act = (up + 1) * gate * _sigmoid(SWIGLU_ALPHA * gate)
        y = _dot(act.astype(BF16), wd_b[...]) + bd_ref[0]
        ys_ref[...] = (gs_ref[:, 0:1] * y).astype(ys_ref.dtype)

    @pl.when(r >= nu_ref[0])
    def _():
        ys_ref[...] = jnp.zeros(ys_ref.shape, ys_ref.dtype)


def _ffn_sorted(tile_expert, n_used, xs, gs, wgu, bgu, wd, bd):
    n_rows, d = xs.shape
    n_exp, _, two_ff = wgu.shape
    row = lambda r, te, nu: (r, 0)
    exp3 = lambda r, te, nu: (te[r], 0, 0)
    grid_spec = pltpu.PrefetchScalarGridSpec(
        num_scalar_prefetch=2,
        grid=(n_rows // FFN_ROWS,),
        in_specs=[pl.BlockSpec((FFN_ROWS, d), row),
                  pl.BlockSpec((FFN_ROWS, LANES), row),
                  pl.BlockSpec((1, d, two_ff), exp3),
                  pl.BlockSpec((1, 1, two_ff), exp3),
                  pl.BlockSpec((1, two_ff // 2, d), exp3),
                  pl.BlockSpec((1, 1, d), exp3)],
        out_specs=pl.BlockSpec((FFN_ROWS, d), row),
        scratch_shapes=[pltpu.VMEM((d, two_ff), BF16), pltpu.VMEM((two_ff // 2, d), BF16)],
    )
    return pl.pallas_call(
        functools.partial(_ffn_sorted_kernel, d_ff=two_ff // 2),
        grid_spec=grid_spec,
        out_shape=jax.ShapeDtypeStruct((n_rows, d), BF16),
        compiler_params=_params("arbitrary"),
        name="moe_ffn_sorted",
    )(tile_expert, n_used, xs, gs, wgu, bgu, wd, bd)


def _combine_kernel(base_ref, cnt_ref, rank_ref, x1_ref, mod_ref, fg_ref, ys_hbm, o_ref, p_ref, yb_ref, acc_ref,
                    sem, *, n_exp, tile0):
    i = pl.program_id(0) + tile0
    win = ROUTE_WIN

    def copy(slot, row0):
        rows = pl.ds(pl.multiple_of(row0, ROUTE_ALIGN), win)
        return pltpu.make_async_copy(ys_hbm.at[rows], yb_ref.at[pl.ds(slot * win, win)], sem.at[slot])

    for e in range(n_exp):
        copy(e, base_ref[i, e]).start()
    for e in range(n_exp):
        p_ref[e * win:(e + 1) * win, :] = _window_onehot(rank_ref[e:e + 1, :], 0).astype(BF16)
    for e in range(n_exp):
        copy(e, base_ref[i, e]).wait()
    acc_ref[...] = _dot_tn(p_ref[...], yb_ref[...])

    def per_expert(e, _):
        def extra(w, _):
            c = copy(0, base_ref[i, e] + w * win)
            c.start()
            p = _window_onehot(rank_ref[pl.ds(e, 1), :], w).astype(BF16)
            c.wait()
            acc_ref[...] += _dot_tn(p, yb_ref[0:win, :])
            return 0

        return lax.fori_loop(1, (cnt_ref[i, e] + win - 1) // win, extra, 0)

    lax.fori_loop(0, n_exp, per_expert, 0)
    x2 = x1_ref[...] + mod_ref[5, 0] * acc_ref[...]
    o_ref[...] = _rms(x2, fg_ref[...])


def _combine(base, cnt, rank_t, x1, mod6, fg, ys, *, tile0, tiles_per_seq):
    n, d = x1.shape
    n_exp = rank_t.shape[0]
    rmod = mod6.shape[2]
    grid_spec = pltpu.PrefetchScalarGridSpec(
        num_scalar_prefetch=2,
        grid=(n // ROUTE_TILE,),
        in_specs=[pl.BlockSpec((n_exp, ROUTE_TILE), lambda i, b_, c_: (0, i + tile0)),
                  pl.BlockSpec((ROUTE_TILE, d), lambda i, b_, c_: (i, 0)),
                  pl.BlockSpec((6, 1, rmod, d), lambda i, b_, c_: (0, i // tiles_per_seq, 0, 0)),
                  pl.BlockSpec((1, d), lambda i, b_, c_: (0, 0)),
                  pl.BlockSpec(memory_space=pl.ANY)],
        out_specs=pl.BlockSpec((ROUTE_TILE, d), lambda i, b_, c_: (i, 0)),
        scratch_shapes=[pltpu.VMEM((n_exp * ROUTE_WIN, ROUTE_TILE), BF16),
                        pltpu.VMEM((n_exp * ROUTE_WIN, d), BF16),
                        pltpu.VMEM((ROUTE_TILE, d), F32),
                        pltpu.SemaphoreType.DMA((n_exp,))],
    )
    return pl.pallas_call(
        functools.partial(_combine_kernel, n_exp=n_exp, tile0=tile0),
        grid_spec=grid_spec,
        out_shape=jax.ShapeDtypeStruct((n, d), F32),
        compiler_params=_params("arbitrary"),
        name="moe_combine",
    )(base, cnt, rank_t, x1, mod6, fg, ys)


def _round_up(a, m):
    return (a + m - 1) // m * m


def _moe_routed(pre_p, pre_s, wts):
    x1_p, h2_p, gates_p, mod6_p, tps_p = pre_p
    x1_s, h2_s, gates_s, mod6_s, _ = pre_s
    n_p, d = x1_p.shape
    n_s = x1_s.shape[0]
    n_exp = wts["bgu"].shape[0]
    assert n_p % ROUTE_TILE == 0 and n_s <= ROUTE_TILE and mod6_s.shape[2] == n_s
    pad_rows = lambda a: jnp.pad(a, ((0, ROUTE_TILE - n_s), (0, 0)))
    h_all = jnp.concatenate([h2_p, pad_rows(h2_s)], axis=0)
    g_all = jnp.concatenate([gates_p, pad_rows(gates_s)], axis=0)
    n_all = h_all.shape[0]
    n_tiles = n_all // ROUTE_TILE
    rank_t, gate_t, cnt_f = _route_meta(g_all, n_exp=n_exp)

    cnt = cnt_f[:, :n_exp, 0].astype(I32)
    seg = _round_up(cnt, ROUTE_ALIGN)
    before = lambda m: jnp.arange(m)[:, None] > jnp.arange(m)[None, :]
    seg_off = jnp.sum(jnp.where(before(n_tiles)[:, :, None], seg[None, :, :], 0), axis=1)
    region = _round_up(jnp.sum(seg, axis=0) + ROUTE_WIN, FFN_ROWS)
    region_start = jnp.sum(jnp.where(before(n_exp), region[None, :], 0), axis=1)
    region_end = region_start + region
    base = (region_start[None, :] + seg_off).astype(I32)
    n_rows = _round_up(TOP_K * n_all + (ROUTE_ALIGN - 1) * n_tiles * n_exp + (ROUTE_WIN + FFN_ROWS) * n_exp, FFN_ROWS)
    tile_start = jnp.arange(n_rows // FFN_ROWS, dtype=I32) * FFN_ROWS
    tile_expert = jnp.minimum(jnp.sum((tile_start[:, None] >= region_end[None, :]).astype(I32), axis=1), n_exp - 1)
    n_used = (region_end[-1:] // FFN_ROWS).astype(I32)

    xs, gs = _dispatch(base, cnt, h_all, rank_t, gate_t, n_rows)
    ys = _ffn_sorted(tile_expert, n_used, xs, gs, wts["wgu32"], wts["bgu"], wts["wd32"], wts["bd"])
    y_p = _combine(base, cnt, rank_t, x1_p, mod6_p, wts["fg"], ys, tile0=0, tiles_per_seq=tps_p)
    mod6_sp = jnp.pad(mod6_s, ((0, 0), (0, 0), (0, ROUTE_TILE - n_s), (0, 0)))
    y_s = _combine(base, cnt, rank_t, pad_rows(x1_s), mod6_sp, wts["fg"], ys, tile0=n_p // ROUTE_TILE,
                   tiles_per_seq=1)
    return y_p, y_s[:n_s]


def _pad_cols(a, width):
    return jnp.pad(a, ((0, 0), (0, width - a.shape[1])))


def _rope_tables(pos):
    half = HEAD_DIM // 2
    inv = ROPE_THETA ** (-jnp.arange(half, dtype=F32) / half)
    ang = pos.astype(F32)[:, None] * inv[None, :]
    cos, sin = jnp.cos(ang), jnp.sin(ang)
    cos_h = jnp.concatenate([cos, cos], axis=1)
    sin_h = jnp.concatenate([-sin, sin], axis=1)
    ones = jnp.ones_like(cos_h)
    zeros = jnp.zeros_like(cos_h)
    tabs = jnp.stack([jnp.concatenate([cos_h, cos_h], 1), jnp.concatenate([sin_h, sin_h], 1),
                      jnp.concatenate([cos_h, ones], 1), jnp.concatenate([sin_h, zeros], 1)])
    return tabs


def _layer(x, mod_rows, pos, shift_prev, wkv_prev, wts, attend, *, tm, chunk, seqs_per_step):
    b, t, d = x.shape
    n = b * t
    rw, att_w, n_rw, rwkv_in = wts["rw"], wts["att_w"], wts["n_rw"], wts["rwkv_in"]
    per_token_mod = mod_rows.shape[0] != b or tm > t
    if per_token_mod:
        mod6 = jnp.repeat(mod_rows, t, axis=0).reshape(n // tm, tm, 6, d).transpose(2, 0, 1, 3)
        tiles_per_seq = 1
    else:
        mod6 = mod_rows.reshape(b, 1, 6, d).transpose(2, 0, 1, 3)
        tiles_per_seq = t // tm
    cs = _rope_tables(pos)
    n_idx_scale = wts["idx_scale"]
    cs = cs.at[2, :, HEAD_DIM:].set(n_idx_scale)
    if cs.shape[1] < tm:
        cs = jnp.tile(cs, (1, tm // cs.shape[1], 1))
    x2d = x.reshape(n, d)
    p, q, k32, kb, v32, vb, qi, kw = _inproj(x2d, mod6, wts["g1"], wts["w_cat"], cs, tm=tm,
                                             tiles_per_seq=tiles_per_seq, n_rw=n_rw, att_w=att_w)
    t_pad = -(-t // chunk) * chunk
    p3 = p.reshape(b, t, n_rw)
    if t_pad != t:
        p3 = jnp.pad(p3, ((0, 0), (0, t_pad - t), (0, 0)))
    z0 = jnp.swapaxes(wkv_prev, -1, -2)
    yr, z_fin = _rwkv(p3, shift_prev.reshape(b, 1, n_rw), z0, wts["mu"], wts["vec"], wts["w2"], wts["a2"], wts["g2"],
                      wts["seg"], L=chunk, nb=seqs_per_step, t_valid=t if t_pad != t else chunk, rw=rw)
    yr = yr[:, :t].reshape(n, rw)
    wkv_new = jnp.swapaxes(z_fin, -1, -2)
    ya = attend(q, qi, kw, k32, kb, v32, vb).reshape(n, att_w)
    x1, h2, gates = _outproj(x2d, yr, ya, mod6, wts["w_out"], wts["g2n"], wts["router_w"], wts["router_b"], tm=tm,
                             tiles_per_seq=tiles_per_seq, rw=rw)
    pre_moe = (x1, h2, gates, mod6, 1 if per_token_mod else t // ROUTE_TILE)
    p_last = p3[:, t - 1, :]
    new_shift = jnp.concatenate([p_last[:, :3 * rw], p_last[:, 3 * rw:3 * rw + wts["lora"][0]],
                                 p_last[:, 3 * rw + 128:3 * rw + 128 + wts["lora"][1]],
                                 p_last[:, 3 * rw + 256:3 * rw + 256 + wts["lora"][2]]], axis=1)
    n_heads = att_w // HEAD_DIM
    state = (k32.reshape(b, t, n_heads, HEAD_DIM), v32.reshape(b, t, n_heads, HEAD_DIM),
             kw[:, :HEAD_DIM].reshape(b, t, HEAD_DIM), wkv_new, new_shift)
    return pre_moe, state


def kernel(x_prompt, x_sample, c_prompt, c_sample, cache_k, cache_v, cache_idx_k, state_wkv, state_shift, page_table, norm1_g, norm2_g, w_mod, b_mod, w_in, mu_shift, w0, w2, a0, a2, g2, k_k, k_a, r_k, lnx_g, lnx_b, w_out, router_w, router_b, w_gate_up, b_gate_up, w_down, b_down, final_g):
    depth = w_in.shape[0]
    assert depth == 1, "single-layer trunk"
    l = 0
    bp, tp, d = x_prompt.shape
    bs, ts, _ = x_sample.shape
    rw = w0.shape[1]
    att_w = w_out.shape[1] - rw
    lora = (w2.shape[1], a2.shape[1], g2.shape[1])
    rwkv_in = 3 * rw + sum(lora)
    n_idx = (w_in.shape[2] - rwkv_in - 3 * att_w - HEAD_DIM) // (HEAD_DIM + 1)
    n_exp = router_w.shape[2]
    n_pages = page_table.shape[1]
    past = n_pages * PAGE_SIZE
    n_rw = 3 * rw + 512
    assert lora[0] <= 128 and lora[1] <= 128 and lora[2] <= 256 and n_idx * HEAD_DIM == att_w

    wi = w_in[l]
    o = 3 * rw
    rw_cols = [wi[:, :o], _pad_cols(wi[:, o:o + lora[0]], 128), _pad_cols(wi[:, o + lora[0]:o + lora[0] + lora[1]], 128),
               _pad_cols(wi[:, o + lora[0] + lora[1]:rwkv_in], 256)]
    a0_ = rwkv_in
    att_cols = [wi[:, a0_:a0_ + 4 * att_w], _pad_cols(wi[:, a0_ + 4 * att_w:], LANES)]
    w_cat = jnp.concatenate(rw_cols + att_cols, axis=1).astype(BF16)
    mu = mu_shift[l]
    mu_pad = jnp.concatenate([mu[:o], jnp.pad(mu[o:o + lora[0]], (0, 128 - lora[0])),
                              jnp.pad(mu[o + lora[0]:o + lora[0] + lora[1]], (0, 128 - lora[1])),
                              jnp.pad(mu[o + lora[0] + lora[1]:], (0, 256 - lora[2]))]).reshape(1, n_rw)
    vec = jnp.stack([w0[l], a0[l], k_k[l], k_a[l], lnx_g[l], lnx_b[l], r_k[l].reshape(-1),
                     jnp.zeros((rw,), F32)])
    head_of = np.arange(rw) // HEAD_DIM
    seg = jnp.asarray((head_of[:, None] == head_of[None, :]).astype(np.float32)).astype(BF16)
    wts = dict(
        rw=rw, att_w=att_w, n_rw=n_rw, rwkv_in=rwkv_in, lora=lora,
        idx_scale=float((n_idx * HEAD_DIM) ** -0.5),
        g1=norm1_g[l].reshape(1, d), w_cat=w_cat, mu=mu_pad, vec=vec,
        w2=jnp.pad(w2[l], ((0, 128 - lora[0]), (0, 0))), a2=jnp.pad(a2[l], ((0, 128 - lora[1]), (0, 0))),
        g2=jnp.pad(g2[l], ((0, 256 - lora[2]), (0, 0))), seg=seg,
        w_out=w_out[l].astype(BF16), g2n=norm2_g[l].reshape(1, d),
        router_w=_pad_cols(router_w[l], LANES),
        router_b=jnp.concatenate([router_b[l], jnp.full((LANES - n_exp,), NEG_BIG, F32)]).reshape(1, LANES),
        wgu32=w_gate_up[l], bgu=b_gate_up[l].reshape(n_exp, 1, -1),
        wd32=w_down[l], bd=b_down[l].reshape(n_exp, 1, -1), fg=final_g.reshape(1, d),
    )

    c_all = jnp.concatenate([c_prompt, c_sample], axis=0)
    rows_pad = -(-c_all.shape[0] // SUBLANES) * SUBLANES
    mod_all = _modulation(jnp.pad(c_all, ((0, rows_pad - c_all.shape[0]), (0, 0))), w_mod[l], b_mod[l])
    mod_p, mod_s = mod_all[:bp], mod_all[bp:bp + bs]

    k_top_p = min(TOPK_MAX, tp // 4)

    def attend_prompt(q, qi, kw, k32, kb, v32, vb):
        tk = min(512, tp)
        r3 = lambda a: a.reshape(bp, tp, a.shape[1])
        tr = lambda a: jnp.swapaxes(r3(a), 1, 2)
        nh = att_w // HEAD_DIM
        k_hm = kb.reshape(bp, tp, nh, HEAD_DIM).transpose(0, 2, 1, 3)
        vt_ck = vb.reshape(bp, tp // tk, tk, att_w).transpose(0, 1, 3, 2)
        wit = tr(kw[:, HEAD_DIM:HEAD_DIM + SUBLANES])
        kib = r3(kw[:, :HEAD_DIM].astype(BF16))
        out_t = _dsa_prompt(tr(q), tr(qi), wit, k_hm, vt_ck, kib, tq=min(256, tp), tk=tk, k_top=k_top_p)
        return jnp.swapaxes(out_t, 1, 2)

    pre_p, st_p = _layer(x_prompt, mod_p, jnp.arange(tp), jnp.zeros((bp, n_rw), F32),
                       jnp.zeros((bp, rw // HEAD_DIM, HEAD_DIM, HEAD_DIM), F32), wts, attend_prompt,
                       tm=min(512, tp), chunk=64, seqs_per_step=2 if bp % 2 == 0 else 1)

    k_top_s = min(TOPK_MAX, (past + ts) // 4)
    n_heads = att_w // HEAD_DIM
    assert ts <= SUBLANES
    pool_t = lambda c: jnp.transpose(c, (0, 1, 3, 4, 2)).reshape(c.shape[0], c.shape[1], att_w, PAGE_SIZE)
    ck_t, cv_t = pool_t(cache_k), pool_t(cache_v)
    cik_t = jnp.transpose(cache_idx_k, (0, 1, 3, 2))
    g_idx = 16 if n_pages % 16 == 0 else 1
    g_att = 16 if n_pages % 16 == 0 else (8 if n_pages % 8 == 0 else 1)

    def head_query_rows(a, nh):
        a = a.reshape(bs, ts, nh, HEAD_DIM).transpose(0, 2, 1, 3)
        a = jnp.pad(a, ((0, 0), (0, 0), (0, SUBLANES - ts), (0, 0)))
        return a.reshape(bs, nh * SUBLANES, HEAD_DIM)

    def new_rows_t(a):
        a = a.reshape(bs, ts, a.shape[1]).transpose(0, 2, 1)
        return jnp.pad(a, ((0, 0), (0, 0), (0, PAGE_SIZE - ts)))

    def attend_sample(q, qi, kw, k32, kb, v32, vb):
        qim = head_query_rows(qi, n_idx)
        wcol = kw[:, HEAD_DIM:HEAD_DIM + n_idx].reshape(bs, ts, n_idx).transpose(0, 2, 1)
        wcol = jnp.pad(wcol, ((0, 0), (0, 0), (0, SUBLANES - ts))).reshape(bs, n_idx * SUBLANES, 1)
        past_scores = _idx_sample(page_table, qim, wcol, cik_t, t_new=ts, g_pages=g_idx, layer=l)
        new_scores = _idx_new(qim, wcol, new_rows_t(kw[:, :HEAD_DIM]), t_new=ts)
        bias = _select_sample(past_scores, new_scores, k_top=k_top_s)
        qh = head_query_rows(q, n_heads)
        eye = jnp.eye(n_heads, dtype=BF16)
        qbd = (qh.reshape(bs, n_heads, SUBLANES, 1, HEAD_DIM) * eye[None, :, None, :, None]).reshape(
            bs, n_heads * SUBLANES, att_w)
        out = _attn_sample(page_table, qbd, bias, ck_t, cv_t, new_rows_t(k32), new_rows_t(v32), g_pages=g_att,
                           layer=l)
        return out[:, :ts]

    shift_s = state_shift[l]
    o1, o2 = o + lora[0], o + lora[0] + lora[1]
    shift_pad = jnp.concatenate([shift_s[:, :o], _pad_cols(shift_s[:, o:o1], 128), _pad_cols(shift_s[:, o1:o2], 128),
                                 _pad_cols(shift_s[:, o2:], 256)], axis=1)
    pre_s, st_s = _layer(x_sample, mod_s, past + jnp.arange(ts), shift_pad, state_wkv[l], wts, attend_sample,
                       tm=bs * ts, chunk=SUBLANES, seqs_per_step=4 if bs % 4 == 0 else 1)

    y_p, y_s = _moe_routed(pre_p, pre_s, wts)
    y_p, y_s = y_p.reshape(bp, tp, d), y_s.reshape(bs, ts, d)
    ex = lambda a: a[None]
    return (y_p, y_s,ex(st_p[0]), ex(st_p[1]), ex(st_p[2]), ex(st_p[3]), ex(st_p[4]),
            ex(st_s[0]), ex(st_s[1]), ex(st_s[2]), ex(st_s[3]), ex(st_s[4]))
```

```python
import functools

import numpy as np
import jax
import jax.numpy as jnp
from jax import lax
from jax.experimental import pallas as pl
from jax.experimental.pallas import tpu as pltpu

F32 = jnp.float32
BF16 = jnp.bfloat16
I32 = jnp.int32
HI = lax.Precision.HIGHEST

HEAD_DIM = 64
PAGE_SIZE = 128
TOPK_MAX = 256
ROPE_THETA = 10000.0
TOP_K = 4
SWIGLU_LIMIT = 7.0
SWIGLU_ALPHA = 1.702
NORM_EPS = 1e-5
LNX_EPS = 64e-5

LANES = 128
SUBLANES = 8
VMEM_LIMIT = 56 * 1024 * 1024

NEG_BIG = -1e30
INT_MIN = -(2 ** 31)
KEY_NEG_INF = INT_MIN + 0x7FFFFF


def _dot(a, b, prec=None):
    return lax.dot_general(a, b, (((1,), (0,)), ((), ())), precision=prec, preferred_element_type=F32)


def _dot_nt(a, b, prec=None):
    return lax.dot_general(a, b, (((1,), (1,)), ((), ())), precision=prec, preferred_element_type=F32)


def _dot_tn(a, b, prec=None):
    return lax.dot_general(a, b, (((0,), (0,)), ((), ())), precision=prec, preferred_element_type=F32)


_NN = ((1,), (0,))
_NT = ((1,), (1,))
_TN = ((0,), (0,))


def _split(a, terms=2):
    out = []
    for i in range(terms):
        t = a.astype(BF16)
        out.append(t)
        if i + 1 < terms:
            a = a - t.astype(F32)
    return out


def _mm(a_terms, b_terms, dims=_NN, order=1):
    acc = None
    for i, a in enumerate(a_terms):
        for j, b in enumerate(b_terms):
            if i + j <= order:
                d = lax.dot_general(a, b, (dims, ((), ())), preferred_element_type=F32)
                acc = d if acc is None else acc + d
    return acc


def _params(*sem):
    return pltpu.CompilerParams(dimension_semantics=sem, vmem_limit_bytes=VMEM_LIMIT)


def _sigmoid(x):
    return jax.nn.sigmoid(x)


def _rms(x, g):
    return x * lax.rsqrt(jnp.mean(x * x, axis=-1, keepdims=True) + NORM_EPS) * g


def _mod_kernel(c_ref, w_ref, b_ref, o_ref):
    c = c_ref[...]
    o_ref[...] = _dot(c * _sigmoid(c), w_ref[...], HI) + b_ref[...]


def _modulation(c_pad, w_mod, b_mod):
    rows, d = c_pad.shape
    n = w_mod.shape[1]
    return pl.pallas_call(
        _mod_kernel,
        grid=(n // d,),
        in_specs=[pl.BlockSpec((rows, d), lambda j: (0, 0)),
                  pl.BlockSpec((d, d), lambda j: (0, j)),
                  pl.BlockSpec((1, d), lambda j: (0, j))],
        out_specs=pl.BlockSpec((rows, d), lambda j: (0, j)),
        out_shape=jax.ShapeDtypeStruct((rows, n), F32),
        compiler_params=_params("arbitrary"),
        name="modulation",
    )(c_pad, w_mod, b_mod.reshape(1, n))


def _rope_block(x, cos, sin_signed):
    lane = lax.broadcasted_iota(I32, x.shape, 1)
    partner = jnp.where((lane & 32) == 0, pltpu.roll(x, LANES - 32, axis=1), pltpu.roll(x, 32, axis=1))
    return x * cos + partner * sin_signed


def _inproj_kernel(x_ref, mod_ref, g_ref, w_ref, cs_ref, p_ref, qt_ref, kt_ref, khm_ref, vt_ref, vtb_ref, qit_ref,
                   kw_ref, kwt_ref, *, n_rw, att_w):
    x = x_ref[...]
    h = _rms(x, g_ref[...]) * (1 + mod_ref[1, 0]) + mod_ref[0, 0]
    hb = h.astype(BF16)
    cb = 512
    for j in range(n_rw // cb):
        p_ref[:, j * cb:(j + 1) * cb] = _dot(hb, w_ref[:, j * cb:(j + 1) * cb])
    cos, sin = cs_ref[0], cs_ref[1]
    base = n_rw
    ng = att_w // LANES
    grp = lambda g: slice(g * LANES, (g + 1) * LANES)

    def roped(off):
        t = _dot(hb, w_ref[:, off:off + att_w])
        return [_rope_block(t[:, grp(g)], cos, sin) for g in range(ng)]

    for g, blk in enumerate(roped(base)):
        qt_ref[0, grp(g), :] = jnp.transpose(blk * (HEAD_DIM ** -0.5)).astype(BF16)
    for g, blk in enumerate(roped(base + att_w)):
        kt_ref[0, grp(g), :] = jnp.transpose(blk)
        khm_ref[0, 2 * g] = blk[:, :HEAD_DIM].astype(BF16)
        khm_ref[0, 2 * g + 1] = blk[:, HEAD_DIM:].astype(BF16)
    v = _dot(hb, w_ref[:, base + 2 * att_w:base + 3 * att_w])
    for g in range(ng):
        vg_t = jnp.transpose(v[:, grp(g)])
        vt_ref[0, grp(g), :] = vg_t
        vtb_ref[0, 0, grp(g), :] = vg_t.astype(BF16)
    for g, blk in enumerate(roped(base + 3 * att_w)):
        qit_ref[0, grp(g), :] = jnp.transpose(blk).astype(BF16)
    kw = _rope_block(_dot(hb, w_ref[:, base + 4 * att_w:base + 4 * att_w + LANES]), cs_ref[2], cs_ref[3])
    kw_ref[...] = kw
    kwt_ref[0] = jnp.transpose(kw)


def _inproj(x2d, mod6, g1, w_cat, cs, *, tm, tiles_per_seq, n_rw, att_w):
    n, d = x2d.shape
    rmod = mod6.shape[2]
    n_cs_tiles = cs.shape[1] // tm
    t_seq = tiles_per_seq * tm
    n_seq = n // t_seq
    nh = att_w // HEAD_DIM
    grid = (n // tm,)
    row = lambda i: (i, 0)
    feat = lambda i: (i // tiles_per_seq, 0, i % tiles_per_seq)
    outs = [
        (jax.ShapeDtypeStruct((n, n_rw), F32), pl.BlockSpec((tm, n_rw), row)),
        (jax.ShapeDtypeStruct((n_seq, att_w, t_seq), BF16), pl.BlockSpec((1, att_w, tm), feat)),
        (jax.ShapeDtypeStruct((n_seq, att_w, t_seq), F32), pl.BlockSpec((1, att_w, tm), feat)),
        (jax.ShapeDtypeStruct((n_seq, nh, t_seq, HEAD_DIM), BF16),
         pl.BlockSpec((1, nh, tm, HEAD_DIM), lambda i: (i // tiles_per_seq, 0, i % tiles_per_seq, 0))),
        (jax.ShapeDtypeStruct((n_seq, att_w, t_seq), F32), pl.BlockSpec((1, att_w, tm), feat)),
        (jax.ShapeDtypeStruct((n_seq, tiles_per_seq, att_w, tm), BF16),
         pl.BlockSpec((1, 1, att_w, tm), lambda i: (i // tiles_per_seq, i % tiles_per_seq, 0, 0))),
        (jax.ShapeDtypeStruct((n_seq, att_w, t_seq), BF16), pl.BlockSpec((1, att_w, tm), feat)),
        (jax.ShapeDtypeStruct((n, LANES), F32), pl.BlockSpec((tm, LANES), row)),
        (jax.ShapeDtypeStruct((n_seq, LANES, t_seq), F32), pl.BlockSpec((1, LANES, tm), feat)),
    ]
    return pl.pallas_call(
        functools.partial(_inproj_kernel, n_rw=n_rw, att_w=att_w),
        grid=grid,
        in_specs=[pl.BlockSpec((tm, d), row),
                  pl.BlockSpec((6, 1, rmod, d), lambda i: (0, i // tiles_per_seq, 0, 0)),
                  pl.BlockSpec((1, d), lambda i: (0, 0)),
                  pl.BlockSpec(w_cat.shape, lambda i: (0, 0)),
                  pl.BlockSpec((4, tm, LANES), lambda i: (0, i % n_cs_tiles, 0))],
        out_specs=[o[1] for o in outs],
        out_shape=[o[0] for o in outs],
        compiler_params=_params("arbitrary"),
        name="inproj",
    )(x2d, mod6, g1, w_cat, cs)


def _softplus(x):
    return jnp.maximum(x, 0.0) + jnp.log(1.0 + jnp.exp(-jnp.abs(x)))


def _rwkv_kernel(p_ref, sh0_ref, z0_ref, mu_ref, vec_ref, w2_ref, a2_ref, g2_ref, seg_ref, y_ref, zout_ref,
                 carry_ref, z_ref, ops_ref, yh_ref, *, L, nb, t_valid, rw, n_heads):
    c = pl.program_id(1)
    R = nb * L

    @pl.when(c == 0)
    def _():
        carry_ref[...] = sh0_ref[...]
        z_ref[...] = z0_ref[...].reshape(z_ref.shape)

    p = p_ref[...].reshape(R, p_ref.shape[2])
    row = lax.broadcasted_iota(I32, (nb, L, 1), 1).reshape(R, 1)
    seq = lax.broadcasted_iota(I32, (nb, L, 1), 0).reshape(R, 1)

    def per_seq_rows(vals):
        out = vals[0]
        for s in range(1, nb):
            out = jnp.where(seq >= s, vals[s], out)
        return out

    prev = jnp.where(row == 0, per_seq_rows([carry_ref[s] for s in range(nb)]), pltpu.roll(p, 1, axis=0))
    for s in range(nb):
        carry_ref[s] = p[(s + 1) * L - 1:(s + 1) * L, :]
    z = p + (prev - p) * mu_ref[...]
    r, k, v = z[:, 0:rw], z[:, rw:2 * rw], z[:, 2 * rw:3 * rw]
    o = 3 * rw
    zw, za, zg = z[:, o:o + 128], z[:, o + 128:o + 256], z[:, o + 256:o + 512]
    w0, a0, k_k, k_a = vec_ref[0:1], vec_ref[1:2], vec_ref[2:3], vec_ref[3:4]
    lnx_g, lnx_b, r_k = vec_ref[4:5], vec_ref[5:6], vec_ref[6:7]
    seg = [seg_ref[...]]

    def seg_sum(t):
        return _mm(_split(t, 3), seg, order=2)

    def lora(t, w_ref):
        return _mm(_split(t), _split(w_ref[...]))

    w_log = -_softplus(-(w0 + lora(jnp.tanh(zw), w2_ref))) - 0.5
    lw = -jnp.exp(w_log)
    a = _sigmoid(a0 + lora(za, a2_ref))
    g = lora(_sigmoid(zg), g2_ref)
    kk = k * k_k
    kkn = kk / jnp.maximum(jnp.sqrt(seg_sum(kk * kk)), 1e-12)
    k2 = k * (1 + (a - 1) * k_a)
    an, bn = -kkn, kkn * a
    if t_valid < L:
        valid = row < t_valid
        lw = jnp.where(valid, lw, 0.0)
        an, bn = jnp.where(valid, an, 0.0), jnp.where(valid, bn, 0.0)
        k2s, vs = jnp.where(valid, k2, 0.0), jnp.where(valid, v, 0.0)
    else:
        k2s, vs = k2, v

    ri = lax.broadcasted_iota(I32, (L, L), 0)
    ci = lax.broadcasted_iota(I32, (L, L), 1)
    incl, strict = ri >= ci, ri > ci
    lw_terms = _split(lw, 3)
    rr = lax.broadcasted_iota(I32, (R, R), 0)
    rc = lax.broadcasted_iota(I32, (R, R), 1)
    same_seq_incl = (rr >= rc) & (rc >= 0)
    for s in range(1, nb):
        same_seq_incl = same_seq_incl & ((rr < s * L) | (rc >= s * L))
    cum = _mm([same_seq_incl.astype(BF16)], lw_terms, order=2)
    c_last = per_seq_rows([cum[(s + 1) * L - 1:(s + 1) * L, :] for s in range(nb)])
    e_c, e_cp, e_n, e_d = jnp.exp(cum), jnp.exp(cum - lw), jnp.exp(-cum), jnp.exp(c_last - cum)
    ops_ref[0] = an * e_cp
    ops_ref[1] = r * e_c
    ops_ref[2] = bn * e_n
    ops_ref[3] = k2s * e_n
    ops_ref[4] = bn * e_d
    ops_ref[5] = k2s * e_d
    ops_ref[6] = vs
    wl_cols = [jnp.exp(_mm(lw_terms, [jnp.broadcast_to(seq == s, (R, LANES)).astype(BF16)], _TN, order=2))
               for s in range(nb)]
    eye = (ri == ci).astype(F32)
    n_sq = max(int(np.ceil(np.log2(L))) - 1, 0)

    hs = range(nb * n_heads)
    sls = [slice((h % n_heads) * HEAD_DIM, (h % n_heads + 1) * HEAD_DIM) for h in hs]
    rws = [slice((h // n_heads) * L, (h // n_heads + 1) * L) for h in hs]
    at, rt, bt, kt, bh, kh, vh = [[_split(ops_ref[i, rws[h], sls[h]]) for h in hs] for i in range(7)]
    m_ab = [jnp.where(strict, _mm(at[h][:1], bt[h][:1], _NT), 0.0) for h in hs]
    t_inv = [eye + m_ab[h] for h in hs]
    pw = m_ab
    for _ in range(n_sq):
        pw_b = [pw[h].astype(BF16) for h in hs]
        pw = [_mm([pw_b[h]], [pw_b[h]]) for h in hs]
        t_inv = [t_inv[h] + _mm([t_inv[h].astype(BF16)], [pw[h].astype(BF16)]) for h in hs]
    m_ak = [jnp.where(strict, _mm(at[h][:1], kt[h][:1], _NT), 0.0) for h in hs]
    m_rb = [jnp.where(incl, _mm(rt[h][:1], bt[h][:1], _NT), 0.0).astype(BF16) for h in hs]
    m_rk = [jnp.where(incl, _mm(rt[h][:1], kt[h][:1], _NT), 0.0).astype(BF16) for h in hs]
    zs = [z_ref[h] for h in hs]
    zs_t = [_split(zs[h]) for h in hs]
    rhs = [_mm(at[h], zs_t[h]) + _mm(_split(m_ak[h]), vh[h]) for h in hs]
    u_t = [_split(_mm(_split(t_inv[h]), _split(rhs[h]))) for h in hs]
    for h in hs:
        yh_ref[rws[h], sls[h]] = (_mm(rt[h][:1], zs_t[h][:1]) + _mm([m_rb[h]], u_t[h][:1])
                                  + _mm([m_rk[h]], vh[h][:1]))
    for h in hs:
        z_ref[h] = (wl_cols[h // n_heads][sls[h], 0:HEAD_DIM] * zs[h] + _mm(bh[h], u_t[h], _TN)
                    + _mm(kh[h], vh[h], _TN))

    y = yh_ref[...]
    inv_n = 1.0 / HEAD_DIM
    mean = seg_sum(y) * inv_n
    yc = y - mean
    var = seg_sum(yc * yc) * inv_n
    yn = yc * lax.rsqrt(var + LNX_EPS) * lnx_g + lnx_b
    bonus = seg_sum(r * k2 * r_k) * v
    y_ref[...] = ((yn + bonus) * g).astype(y_ref.dtype).reshape(y_ref.shape)

    @pl.when(c == pl.num_programs(1) - 1)
    def _():
        zout_ref[...] = z_ref[...].reshape(zout_ref.shape)


def _rwkv(p3, shift0, z0, mu, vec, w2, a2, g2, seg, *, L, nb, t_valid, rw):
    b, t, n_rw = p3.shape
    n_heads = rw // HEAD_DIM
    assert b % nb == 0
    const2 = lambda i, c: (0, 0)
    return pl.pallas_call(
        functools.partial(_rwkv_kernel, L=L, nb=nb, t_valid=t_valid, rw=rw, n_heads=n_heads),
        grid=(b // nb, t // L),
        in_specs=[pl.BlockSpec((nb, L, n_rw), lambda i, c: (i, c, 0)),
                  pl.BlockSpec((nb, 1, n_rw), lambda i, c: (i, 0, 0)),
                  pl.BlockSpec((nb, n_heads, HEAD_DIM, HEAD_DIM), lambda i, c: (i, 0, 0, 0)),
                  pl.BlockSpec((1, n_rw), const2),
                  pl.BlockSpec(vec.shape, const2),
                  pl.BlockSpec(w2.shape, const2),
                  pl.BlockSpec(a2.shape, const2),
                  pl.BlockSpec(g2.shape, const2),
                  pl.BlockSpec(seg.shape, const2)],
        out_specs=[pl.BlockSpec((nb, L, rw), lambda i, c: (i, c, 0)),
                   pl.BlockSpec((nb, n_heads, HEAD_DIM, HEAD_DIM), lambda i, c: (i, 0, 0, 0))],
        out_shape=[jax.ShapeDtypeStruct((b, t, rw), BF16),
                   jax.ShapeDtypeStruct((b, n_heads, HEAD_DIM, HEAD_DIM), F32)],
        scratch_shapes=[pltpu.VMEM((nb, 1, n_rw), F32),
                        pltpu.VMEM((nb * n_heads, HEAD_DIM, HEAD_DIM), F32),
                        pltpu.VMEM((7, nb * L, rw), F32),
                        pltpu.VMEM((nb * L, rw), F32)],
        compiler_params=_params("arbitrary", "arbitrary"),
        name="rwkv_scan",
    )(p3, shift0, z0, mu, vec, w2, a2, g2, seg)


def _key_to_float(key):
    bits = key ^ ((key >> 31) & 0x7FFFFFFF)
    return lax.bitcast_convert_type(bits, F32)


def _fold_lanes(m):
    acc = m[:, 0:LANES]
    for j in range(1, m.shape[1] // LANES):
        acc = acc + m[:, j * LANES:(j + 1) * LANES]
    return acc


def _select_bias(sc_ref, pb_ref, nc, n_q, tk, k_top, n_keys_total, key_axis=1):
    if key_axis == 1:
        q_shape, part_shape = (n_q, 1), (n_q, LANES)
        fold = lambda m: _fold_lanes(m.astype(F32))
        total = lambda acc: jnp.sum(acc, axis=1, keepdims=True)
    else:
        part_rows = min(tk, 8 * SUBLANES)
        q_shape, part_shape = (1, n_q), (part_rows, n_q)
        fold = lambda m: jnp.sum(m.astype(F32).reshape(tk // part_rows, part_rows, n_q), axis=0)
        total = lambda acc: jnp.sum(acc, axis=0, keepdims=True)

    def count(pred):
        def body(c, acc):
            return acc + fold(pred(c, sc_ref[c]))
        return total(lax.fori_loop(0, nc, body, jnp.zeros(part_shape, F32)))

    kf = float(k_top)
    cnt0 = count(lambda c, s: s >= 0.0)
    tau0 = jnp.where(cnt0 >= kf, 0, INT_MIN).astype(I32)

    def bit_body(i, tau):
        cand = tau + jnp.left_shift(jnp.int32(1), 30 - i)
        cand_f = _key_to_float(cand)
        cnt = count(lambda c, s: s >= cand_f)
        return jnp.where((cnt >= kf) | (cand <= KEY_NEG_INF), cand, tau)

    tau = lax.fori_loop(0, 31, bit_body, tau0)
    tau_f = _key_to_float(tau)
    need = kf - count(lambda c, s: s > tau_f)
    excess = count(lambda c, s: s == tau_f) - need

    def kidx(c):
        return c * tk + lax.broadcasted_iota(I32, (1, tk) if key_axis == 1 else (tk, 1), key_axis)

    n_bits = max(int(np.ceil(np.log2(n_keys_total))), 1)
    pb_ref[...] = jnp.full(q_shape, n_keys_total, I32)

    @pl.when(jnp.max(excess) > 0.0)
    def _():
        def tie_body(i, pb):
            cand = pb + jnp.left_shift(jnp.int32(1), n_bits - 1 - i)
            cnt = count(lambda c, s: (s == tau_f) & (kidx(c) < cand))
            return jnp.where(cnt < need, cand, pb)

        pb_ref[...] = lax.fori_loop(0, n_bits, tie_body, jnp.zeros(q_shape, I32))

    pb = pb_ref[...]

    def write(c, _):
        s = sc_ref[c]
        sel = ((s > tau_f) | ((s == tau_f) & (kidx(c) <= pb))) & (s > -jnp.inf)
        sc_ref[c] = jnp.where(sel, 0.0, NEG_BIG)
        return 0

    lax.fori_loop(0, nc, write, 0)


def _dsa_prompt_kernel(qt_ref, qit_ref, wit_ref, k_ref, vt_ref, ki_ref, o_ref, sc_ref, pb_ref, m_ref, l_ref, acc_ref,
                       *, tq, tk, k_top, n_heads, n_idx, t_total):
    qb = pl.program_id(1)
    q0 = qb * tq
    nc = (q0 + tq + tk - 1) // tk
    qpos = q0 + lax.broadcasted_iota(I32, (1, tq), 1)
    hd = lambda h: slice(h * HEAD_DIM, (h + 1) * HEAD_DIM)

    def scores(c, _):
        kic = ki_ref[0, pl.ds(pl.multiple_of(c * tk, tk), tk), :]
        acc = jnp.zeros((tk, tq), F32)
        for h in range(n_idx):
            acc = acc + jnp.maximum(_dot(kic, qit_ref[0, hd(h), :]), 0.0) * wit_ref[0, h:h + 1, :]
        spos = c * tk + lax.broadcasted_iota(I32, (tk, 1), 0)
        sc_ref[c] = jnp.where(spos <= qpos, acc, -jnp.inf)
        return 0

    lax.fori_loop(0, nc, scores, 0)
    _select_bias(sc_ref, pb_ref, nc, tq, tk, k_top, t_total, key_axis=0)

    m_ref[...] = jnp.full(m_ref.shape, NEG_BIG, F32)
    l_ref[...] = jnp.zeros(l_ref.shape, F32)
    acc_ref[...] = jnp.zeros(acc_ref.shape, F32)

    def attend(c, _):
        rows = pl.ds(pl.multiple_of(c * tk, tk), tk)
        bias = sc_ref[c]
        m_all, l_all, acc_all = m_ref[...], l_ref[...], acc_ref[...]
        hs = range(n_heads)
        s = [_dot(k_ref[0, h, rows, :], qt_ref[0, hd(h), :]) + bias for h in hs]
        m_new = [jnp.maximum(m_all[h:h + 1], jnp.max(s[h], axis=0, keepdims=True)) for h in hs]
        p = [jnp.exp(s[h] - m_new[h]) for h in hs]
        pv = [_dot(vt_ref[0, c, hd(h), :], p[h].astype(BF16)) for h in hs]
        alpha = [jnp.exp(m_all[h:h + 1] - m_new[h]) for h in hs]
        l_new = [alpha[h] * l_all[h:h + 1] + jnp.sum(p[h], axis=0, keepdims=True) for h in hs]
        acc_ref[...] = jnp.concatenate([alpha[h] * acc_all[hd(h)] + pv[h] for h in hs], axis=0)
        m_ref[...] = jnp.concatenate(m_new, axis=0)
        l_ref[...] = jnp.concatenate(l_new, axis=0)
        return 0

    lax.fori_loop(0, nc, attend, 0)
    l_all = l_ref[...]
    for h in range(n_heads):
        o_ref[0, hd(h), :] = (acc_ref[hd(h), :] / l_all[h:h + 1]).astype(o_ref.dtype)


def _dsa_prompt(qt, qit, wit, k_hm, vt_ck, ki, *, tq, tk, k_top):
    b, att_w, t = qt.shape
    n_heads = att_w // HEAD_DIM
    n_idx = qit.shape[1] // HEAD_DIM
    assert n_heads == SUBLANES
    qblk = lambda i, j: (i, 0, j)
    return pl.pallas_call(
        functools.partial(_dsa_prompt_kernel, tq=tq, tk=tk, k_top=k_top, n_heads=n_heads, n_idx=n_idx,
                          t_total=t),
        grid=(b, t // tq),
        in_specs=[pl.BlockSpec((1, att_w, tq), qblk),
                  pl.BlockSpec((1, qit.shape[1], tq), qblk),
                  pl.BlockSpec((1, wit.shape[1], tq), qblk),
                  pl.BlockSpec((1,) + k_hm.shape[1:], lambda i, j: (i, 0, 0, 0)),
                  pl.BlockSpec((1,) + vt_ck.shape[1:], lambda i, j: (i, 0, 0, 0)),
                  pl.BlockSpec((1, t, HEAD_DIM), lambda i, j: (i, 0, 0))],
        out_specs=pl.BlockSpec((1, att_w, tq), qblk),
        out_shape=jax.ShapeDtypeStruct((b, att_w, t), BF16),
        scratch_shapes=[pltpu.VMEM((t // tk, tk, tq), F32),
                        pltpu.VMEM((1, tq), I32),
                        pltpu.VMEM((n_heads, tq), F32),
                        pltpu.VMEM((n_heads, tq), F32),
                        pltpu.VMEM((att_w, tq), F32)],
        compiler_params=_params("arbitrary", "arbitrary"),
        name="dsa_prompt",
    )(qt, qit, wit, k_hm, vt_ck, ki)


def _idx_page_scores(qim, wcol, ki_t, n_idx):
    d = jnp.maximum(_dot(qim, ki_t.astype(BF16)), 0.0) * wcol
    sc = d[0:SUBLANES]
    for h in range(1, n_idx):
        sc = sc + d[h * SUBLANES:(h + 1) * SUBLANES]
    return sc


def _idx_sample_kernel(pt_ref, qim_ref, wcol_ref, *refs, t_new, n_idx, g_pages):
    page_refs, o_ref = refs[:g_pages], refs[g_pages]
    qrow = lax.broadcasted_iota(I32, (SUBLANES, PAGE_SIZE), 0)
    for j in range(g_pages):
        sc = _idx_page_scores(qim_ref[0], wcol_ref[0], page_refs[j][0, 0], n_idx)
        o_ref[j] = jnp.where(qrow < t_new, sc, -jnp.inf)


def _idx_sample(page_table, qim, wcol, cache_idx_t, *, t_new, g_pages, layer):
    b, n_pages = page_table.shape
    n_idx = qim.shape[1] // SUBLANES
    per_seq = lambda i, g, pt: (i, 0, 0)
    page_specs = [pl.BlockSpec((1, 1, HEAD_DIM, PAGE_SIZE),
                               lambda i, g, pt, j=j: (layer, pt[i, g * g_pages + j], 0, 0))
                  for j in range(g_pages)]
    grid_spec = pltpu.PrefetchScalarGridSpec(
        num_scalar_prefetch=1,
        grid=(b, n_pages // g_pages),
        in_specs=[pl.BlockSpec((1,) + qim.shape[1:], per_seq),
                  pl.BlockSpec((1,) + wcol.shape[1:], per_seq)] + page_specs,
        out_specs=pl.BlockSpec((g_pages, SUBLANES, PAGE_SIZE), lambda i, g, pt: (g, i, 0)),
    )
    return pl.pallas_call(
        functools.partial(_idx_sample_kernel, t_new=t_new, n_idx=n_idx, g_pages=g_pages),
        grid_spec=grid_spec,
        out_shape=jax.ShapeDtypeStruct((n_pages, b * SUBLANES, PAGE_SIZE), F32),
        compiler_params=_params("arbitrary", "arbitrary"),
        name="idx_sample",
    )(page_table, qim, wcol, *([cache_idx_t] * g_pages))


def _idx_new_kernel(qim_ref, wcol_ref, new_ref, o_ref, *, t_new, n_idx):
    sc = _idx_page_scores(qim_ref[0], wcol_ref[0], new_ref[0], n_idx)
    qrow = lax.broadcasted_iota(I32, (SUBLANES, PAGE_SIZE), 0)
    lane = lax.broadcasted_iota(I32, (SUBLANES, PAGE_SIZE), 1)
    o_ref[...] = jnp.where((qrow < t_new) & (lane <= qrow), sc, -jnp.inf)


def _idx_new(qim, wcol, ki_new, *, t_new):
    b = qim.shape[0]
    n_idx = qim.shape[1] // SUBLANES
    per_seq = lambda i: (i, 0, 0)
    return pl.pallas_call(
        functools.partial(_idx_new_kernel, t_new=t_new, n_idx=n_idx),
        grid=(b,),
        in_specs=[pl.BlockSpec((1,) + qim.shape[1:], per_seq),
                  pl.BlockSpec((1,) + wcol.shape[1:], per_seq),
                  pl.BlockSpec((1, HEAD_DIM, PAGE_SIZE), per_seq)],
        out_specs=pl.BlockSpec((SUBLANES, PAGE_SIZE), lambda i: (i, 0)),
        out_shape=jax.ShapeDtypeStruct((b * SUBLANES, PAGE_SIZE), F32),
        compiler_params=_params("arbitrary"),
        name="idx_new",
    )(qim, wcol, ki_new)


def _select_sample_kernel(past_ref, new_ref, o_ref, pb_ref, *, k_top):
    n_pages, rows = past_ref.shape[0], past_ref.shape[1]
    o_ref[0:n_pages] = past_ref[...]
    o_ref[n_pages] = new_ref[...]
    _select_bias(o_ref, pb_ref, n_pages + 1, rows, PAGE_SIZE, k_top, (n_pages + 1) * PAGE_SIZE)


def _select_sample(past_scores, new_scores, *, k_top):
    n_pages, rows, _ = past_scores.shape
    out_shape = (n_pages + 1, rows, PAGE_SIZE)
    return pl.pallas_call(
        functools.partial(_select_sample_kernel, k_top=k_top),
        grid=(1,),
        in_specs=[pl.BlockSpec(past_scores.shape, lambda i: (0, 0, 0)),
                  pl.BlockSpec(new_scores.shape, lambda i: (0, 0))],
        out_specs=pl.BlockSpec(out_shape, lambda i: (0, 0, 0)),
        out_shape=jax.ShapeDtypeStruct(out_shape, F32),
        scratch_shapes=[pltpu.VMEM((rows, 1), I32)],
        compiler_params=_params("arbitrary"),
        name="select_sample",
    )(past_scores, new_scores)


def _attn_sample_kernel(pt_ref, qbd_ref, bias_ref, biasn_ref, knew_ref, vnew_ref, *refs, n_heads, g_pages):
    k_refs, v_refs = refs[:g_pages], refs[g_pages:2 * g_pages]
    o_ref, m_ref, l_ref, acc_ref = refs[2 * g_pages:]
    g = pl.program_id(1)

    @pl.when(g == 0)
    def _():
        m_ref[...] = jnp.full(m_ref.shape, NEG_BIG, F32)
        l_ref[...] = jnp.zeros(l_ref.shape, F32)
        acc_ref[...] = jnp.zeros(acc_ref.shape, F32)

    def step(kts, vts, biases):
        bias = jnp.concatenate([jnp.concatenate([b_] * n_heads, axis=0) for b_ in biases], axis=1)
        s = jnp.concatenate([_dot(qbd_ref[0], kt.astype(BF16)) for kt in kts], axis=1) + bias
        m = m_ref[...]
        m_new = jnp.maximum(m, jnp.max(s, axis=-1, keepdims=True))
        alpha = jnp.exp(m - m_new)
        pr = jnp.exp(s - m_new)
        prb = pr.astype(BF16)
        pv = None
        for j, vt in enumerate(vts):
            d = _dot_nt(prb[:, j * PAGE_SIZE:(j + 1) * PAGE_SIZE], vt.astype(BF16))
            pv = d if pv is None else pv + d
        l_ref[...] = alpha * l_ref[...] + jnp.sum(pr, axis=-1, keepdims=True)
        acc_ref[...] = alpha * acc_ref[...] + pv
        m_ref[...] = m_new

    step([r[0, 0] for r in k_refs], [r[0, 0] for r in v_refs], [bias_ref[j] for j in range(g_pages)])

    @pl.when(g == pl.num_programs(1) - 1)
    def _():
        step([knew_ref[0]], [vnew_ref[0]], [biasn_ref[0]])
        full = acc_ref[...] / l_ref[...]
        col_head = lax.broadcasted_iota(I32, (SUBLANES, full.shape[1]), 1) // HEAD_DIM
        out = jnp.zeros((SUBLANES, full.shape[1]), F32)
        for h in range(n_heads):
            out = out + jnp.where(col_head == h, full[h * SUBLANES:(h + 1) * SUBLANES], 0.0)
        o_ref[0] = out.astype(o_ref.dtype)


def _attn_sample(page_table, qbd, bias, cache_k_t, cache_v_t, k_new_t, v_new_t, *, g_pages, layer):
    b, n_pages = page_table.shape
    att_w = qbd.shape[2]
    n_heads = att_w // HEAD_DIM
    per_seq = lambda i, g, pt: (i, 0, 0)
    page_specs = [pl.BlockSpec((1, 1, att_w, PAGE_SIZE),
                               lambda i, g, pt, j=j: (layer, pt[i, g * g_pages + j], 0, 0))
                  for j in range(g_pages)]
    new_spec = pl.BlockSpec((1, att_w, PAGE_SIZE), per_seq)
    grid_spec = pltpu.PrefetchScalarGridSpec(
        num_scalar_prefetch=1,
        grid=(b, n_pages // g_pages),
        in_specs=[pl.BlockSpec((1,) + qbd.shape[1:], per_seq),
                  pl.BlockSpec((g_pages, SUBLANES, PAGE_SIZE), lambda i, g, pt: (g, i, 0)),
                  pl.BlockSpec((1, SUBLANES, PAGE_SIZE), lambda i, g, pt: (n_pages, i, 0)),
                  new_spec, new_spec] + page_specs + page_specs,
        out_specs=pl.BlockSpec((1, SUBLANES, att_w), per_seq),
        scratch_shapes=[pltpu.VMEM((n_heads * SUBLANES, 1), F32),
                        pltpu.VMEM((n_heads * SUBLANES, 1), F32),
                        pltpu.VMEM((n_heads * SUBLANES, att_w), F32)],
    )
    return pl.pallas_call(
        functools.partial(_attn_sample_kernel, n_heads=n_heads, g_pages=g_pages),
        grid_spec=grid_spec,
        out_shape=jax.ShapeDtypeStruct((b, SUBLANES, att_w), BF16),
        compiler_params=_params("arbitrary", "arbitrary"),
        name="attn_sample",
    )(page_table, qbd, bias, bias, k_new_t, v_new_t, *([cache_k_t] * g_pages), *([cache_v_t] * g_pages))


def _outproj_kernel(x_ref, yr_ref, ya_ref, mod_ref, wo_ref, g_ref, rw_ref, rb_ref, x1_ref, h2_ref, gates_ref, *, rw):
    mixed = _dot(yr_ref[...], wo_ref[0:rw, :]) + _dot(ya_ref[...], wo_ref[rw:, :])
    x1 = x_ref[...] + mod_ref[2, 0] * mixed
    x1_ref[...] = x1
    h2 = _rms(x1, g_ref[...]) * (1 + mod_ref[4, 0]) + mod_ref[3, 0]
    h2_ref[...] = h2.astype(BF16)
    logits = _dot(h2, rw_ref[...], HI) + rb_ref[...]
    lane = lax.broadcasted_iota(I32, logits.shape, 1).astype(F32)
    work = logits
    picked = jnp.zeros(logits.shape, jnp.bool_)
    top = None
    for i in range(TOP_K):
        mx = jnp.max(work, axis=-1, keepdims=True)
        first = jnp.min(jnp.where(work == mx, lane, float(LANES)), axis=-1, keepdims=True)
        hit = lane == first
        picked = picked | hit
        work = jnp.where(hit, -jnp.inf, work)
        if i == 0:
            top = mx
    e = jnp.where(picked, jnp.exp(logits - top), 0.0)
    gates_ref[...] = e / jnp.sum(e, axis=-1, keepdims=True)


def _outproj(x2d, yr, ya, mod6, w_out_b, g2n, router_w_pad, router_b_pad, *, tm, tiles_per_seq, rw):
    n, d = x2d.shape
    rmod = mod6.shape[2]
    row = lambda i: (i, 0)
    const = lambda i: (0, 0)
    outs = [jax.ShapeDtypeStruct((n, d), F32), jax.ShapeDtypeStruct((n, d), BF16),
            jax.ShapeDtypeStruct((n, LANES), F32)]
    return pl.pallas_call(
        functools.partial(_outproj_kernel, rw=rw),
        grid=(n // tm,),
        in_specs=[pl.BlockSpec((tm, d), row),
                  pl.BlockSpec((tm, yr.shape[1]), row),
                  pl.BlockSpec((tm, ya.shape[1]), row),
                  pl.BlockSpec((6, 1, rmod, d), lambda i: (0, i // tiles_per_seq, 0, 0)),
                  pl.BlockSpec(w_out_b.shape, const),
                  pl.BlockSpec((1, d), const),
                  pl.BlockSpec(router_w_pad.shape, const),
                  pl.BlockSpec((1, LANES), const)],
        out_specs=[pl.BlockSpec((tm, o.shape[1]), row) for o in outs],
        out_shape=outs,
        compiler_params=_params("arbitrary"),
        name="outproj_router",
    )(x2d, yr, ya, mod6, w_out_b, g2n, router_w_pad, router_b_pad)


ROUTE_TILE = 512
ROUTE_WIN = 96
ROUTE_ALIGN = 16
FFN_ROWS = 512


def _route_meta_kernel(g_ref, rank_ref, gate_ref, cnt_ref, *, n_exp):
    gt = jnp.transpose(g_ref[...])
    sel = gt > 0.0
    tt = gt.shape[1]
    earlier = (lax.broadcasted_iota(I32, (tt, tt), 0) < lax.broadcasted_iota(I32, (tt, tt), 1)).astype(BF16)
    rank = _dot(sel.astype(BF16), earlier)
    rank_ref[...] = jnp.where(sel, rank, -1.0)[:n_exp]
    gate_ref[...] = gt[:n_exp]
    cnt_ref[0] = jnp.broadcast_to(jnp.sum(sel.astype(F32), axis=-1, keepdims=True), (LANES, LANES))


def _route_meta(gates, *, n_exp):
    n = gates.shape[0]
    n_tiles = n // ROUTE_TILE
    return pl.pallas_call(
        functools.partial(_route_meta_kernel, n_exp=n_exp),
        grid=(n_tiles,),
        in_specs=[pl.BlockSpec((ROUTE_TILE, LANES), lambda i: (i, 0))],
        out_specs=[pl.BlockSpec((n_exp, ROUTE_TILE), lambda i: (0, i)),
                   pl.BlockSpec((n_exp, ROUTE_TILE), lambda i: (0, i)),
                   pl.BlockSpec((1, LANES, LANES), lambda i: (i, 0, 0))],
        out_shape=[jax.ShapeDtypeStruct((n_exp, n), F32), jax.ShapeDtypeStruct((n_exp, n), F32),
                   jax.ShapeDtypeStruct((n_tiles, LANES, LANES), F32)],
        compiler_params=_params("arbitrary"),
        name="route_meta",
    )(gates)


def _window_onehot(rank_row, w):
    rows = lax.broadcasted_iota(I32, (ROUTE_WIN, 1), 0).astype(F32) + float(ROUTE_WIN) * w
    return rank_row == rows


def _dispatch_kernel(base_ref, cnt_ref, h_ref, rank_ref, gate_ref, xs_in, gs_in, xs_out, gs_out, p_ref, sx_ref,
                     sg_ref, sem_x, sem_g, *, n_exp, group):
    del xs_in, gs_in
    i = pl.program_id(0)
    h = h_ref[...]
    win = ROUTE_WIN

    def copies(slot, row0):
        rows = pl.ds(pl.multiple_of(row0, ROUTE_ALIGN), win)
        return (pltpu.make_async_copy(sx_ref.at[slot], xs_out.at[rows], sem_x.at[slot]),
                pltpu.make_async_copy(sg_ref.at[slot], gs_out.at[rows], sem_g.at[slot]))

    def fill(slot, e, w):
        p = _window_onehot(rank_ref[pl.ds(e, 1), :], w)
        p_ref[slot * win:(slot + 1) * win, :] = p.astype(BF16)
        g = jnp.sum(jnp.where(p, gate_ref[pl.ds(e, 1), :], 0.0), axis=-1, keepdims=True)
        sg_ref[slot] = jnp.broadcast_to(g, (win, LANES))

    for e0 in range(0, n_exp, group):
        for j in range(group):
            fill(e0 + j, e0 + j, 0)
        x = _dot(p_ref[e0 * win:(e0 + group) * win, :], h)
        for j in range(group):
            e = e0 + j
            sx_ref[e] = x[j * win:(j + 1) * win].astype(BF16)
            for c in copies(e, base_ref[i, e]):
                c.start()
    for e in range(n_exp):
        for c in copies(e, base_ref[i, e]):
            c.wait()

    def per_expert(e, _):
        def extra(w, _):
            fill(0, e, w)
            sx_ref[0] = _dot(p_ref[0:win, :], h_ref[...]).astype(BF16)
            cs = copies(0, base_ref[i, e] + w * win)
            for c in cs:
                c.start()
            for c in cs:
                c.wait()
            return 0

        return lax.fori_loop(1, (cnt_ref[i, e] + win - 1) // win, extra, 0)

    lax.fori_loop(0, n_exp, per_expert, 0)


def _dispatch(base, cnt, h_all, rank_t, gate_t, n_rows):
    n, d = h_all.shape
    n_exp = rank_t.shape[0]
    n_tiles = n // ROUTE_TILE
    tile = lambda i, b_, c_: (0, i)
    grid_spec = pltpu.PrefetchScalarGridSpec(
        num_scalar_prefetch=2,
        grid=(n_tiles,),
        in_specs=[pl.BlockSpec((ROUTE_TILE, d), lambda i, b_, c_: (i, 0)),
                  pl.BlockSpec((n_exp, ROUTE_TILE), tile),
                  pl.BlockSpec((n_exp, ROUTE_TILE), tile),
                  pl.BlockSpec(memory_space=pl.ANY),
                  pl.BlockSpec(memory_space=pl.ANY)],
        out_specs=[pl.BlockSpec(memory_space=pl.ANY), pl.BlockSpec(memory_space=pl.ANY)],
        scratch_shapes=[pltpu.VMEM((n_exp * ROUTE_WIN, ROUTE_TILE), BF16),
                        pltpu.VMEM((n_exp, ROUTE_WIN, d), BF16),
                        pltpu.VMEM((n_exp, ROUTE_WIN, LANES), F32),
                        pltpu.SemaphoreType.DMA((n_exp,)),
                        pltpu.SemaphoreType.DMA((n_exp,))],
    )
    xs0 = jnp.zeros((n_rows, d), BF16)
    gs0 = jnp.zeros((n_rows, LANES), F32)
    return pl.pallas_call(
        functools.partial(_dispatch_kernel, n_exp=n_exp, group=8),
        grid_spec=grid_spec,
        out_shape=[jax.ShapeDtypeStruct(xs0.shape, BF16), jax.ShapeDtypeStruct(gs0.shape, F32)],
        input_output_aliases={5: 0, 6: 1},
        compiler_params=_params("arbitrary"),
        name="moe_dispatch",
    )(base, cnt, h_all, rank_t, gate_t, xs0, gs0)


def _ffn_sorted_kernel(te_ref, nu_ref, xs_ref, gs_ref, wgu_ref, bgu_ref, wd_ref, bd_ref, ys_ref, wgu_b, wd_b, *,
                       d_ff):
    r = pl.program_id(0)

    @pl.when(r < nu_ref[0])
    def _():
        @pl.when((r == 0) | (te_ref[r] != te_ref[jnp.maximum(r - 1, 0)]))
        def _():
            wgu_b[...] = wgu_ref[0].astype(BF16)
            wd_b[...] = wd_ref[0].astype(BF16)

        gu = _dot(xs_ref[...], wgu_b[...]) + bgu_ref[0]
        gate = jnp.minimum(gu[:, :d_ff], SWIGLU_LIMIT)
        up = jnp.clip(gu[:, d_ff:], -SWIGLU_LIMIT, SWIGLU_LIMIT)
        act = (up + 1) * gate * _sigmoid(SWIGLU_ALPHA * gate)
        y = _dot(act.astype(BF16), wd_b[...]) + bd_ref[0]
        ys_ref[...] = (gs_ref[:, 0:1] * y).astype(ys_ref.dtype)

    @pl.when(r >= nu_ref[0])
    def _():
        ys_ref[...] = jnp.zeros(ys_ref.shape, ys_ref.dtype)


def _ffn_sorted(tile_expert, n_used, xs, gs, wgu, bgu, wd, bd):
    n_rows, d = xs.shape
    n_exp, _, two_ff = wgu.shape
    row = lambda r, te, nu: (r, 0)
    exp3 = lambda r, te, nu: (te[r], 0, 0)
    grid_spec = pltpu.PrefetchScalarGridSpec(
        num_scalar_prefetch=2,
        grid=(n_rows // FFN_ROWS,),
        in_specs=[pl.BlockSpec((FFN_ROWS, d), row),
                  pl.BlockSpec((FFN_ROWS, LANES), row),
                  pl.BlockSpec((1, d, two_ff), exp3),
                  pl.BlockSpec((1, 1, two_ff), exp3),
                  pl.BlockSpec((1, two_ff // 2, d), exp3),
                  pl.BlockSpec((1, 1, d), exp3)],
        out_specs=pl.BlockSpec((FFN_ROWS, d), row),
        scratch_shapes=[pltpu.VMEM((d, two_ff), BF16), pltpu.VMEM((two_ff // 2, d), BF16)],
    )
    return pl.pallas_call(
        functools.partial(_ffn_sorted_kernel, d_ff=two_ff // 2),
        grid_spec=grid_spec,
        out_shape=jax.ShapeDtypeStruct((n_rows, d), BF16),
        compiler_params=_params("arbitrary"),
        name="moe_ffn_sorted",
    )(tile_expert, n_used, xs, gs, wgu, bgu, wd, bd)


def _combine_kernel(base_ref, cnt_ref, rank_ref, x1_ref, mod_ref, fg_ref, ys_hbm, o_ref, p_ref, yb_ref, acc_ref,
                    sem, *, n_exp, tile0):
    i = pl.program_id(0) + tile0
    win = ROUTE_WIN

    def copy(slot, row0):
        rows = pl.ds(pl.multiple_of(row0, ROUTE_ALIGN), win)
        return pltpu.make_async_copy(ys_hbm.at[rows], yb_ref.at[pl.ds(slot * win, win)], sem.at[slot])

    for e in range(n_exp):
        copy(e, base_ref[i, e]).start()
    for e in range(n_exp):
        p_ref[e * win:(e + 1) * win, :] = _window_onehot(rank_ref[e:e + 1, :], 0).astype(BF16)
    for e in range(n_exp):
        copy(e, base_ref[i, e]).wait()
    acc_ref[...] = _dot_tn(p_ref[...], yb_ref[...])

    def per_expert(e, _):
        def extra(w, _):
            c = copy(0, base_ref[i, e] + w * win)
            c.start()
            p = _window_onehot(rank_ref[pl.ds(e, 1), :], w).astype(BF16)
            c.wait()
            acc_ref[...] += _dot_tn(p, yb_ref[0:win, :])
            return 0

        return lax.fori_loop(1, (cnt_ref[i, e] + win - 1) // win, extra, 0)

    lax.fori_loop(0, n_exp, per_expert, 0)
    x2 = x1_ref[...] + mod_ref[5, 0] * acc_ref[...]
    o_ref[...] = _rms(x2, fg_ref[...])


def _combine(base, cnt, rank_t, x1, mod6, fg, ys, *, tile0, tiles_per_seq):
    n, d = x1.shape
    n_exp = rank_t.shape[0]
    rmod = mod6.shape[2]
    grid_spec = pltpu.PrefetchScalarGridSpec(
        num_scalar_prefetch=2,
        grid=(n // ROUTE_TILE,),
        in_specs=[pl.BlockSpec((n_exp, ROUTE_TILE), lambda i, b_, c_: (0, i + tile0)),
                  pl.BlockSpec((ROUTE_TILE, d), lambda i, b_, c_: (i, 0)),
                  pl.BlockSpec((6, 1, rmod, d), lambda i, b_, c_: (0, i // tiles_per_seq, 0, 0)),
                  pl.BlockSpec((1, d), lambda i, b_, c_: (0, 0)),
                  pl.BlockSpec(memory_space=pl.ANY)],
        out_specs=pl.BlockSpec((ROUTE_TILE, d), lambda i, b_, c_: (i, 0)),
        scratch_shapes=[pltpu.VMEM((n_exp * ROUTE_WIN, ROUTE_TILE), BF16),
                        pltpu.VMEM((n_exp * ROUTE_WIN, d), BF16),
                        pltpu.VMEM((ROUTE_TILE, d), F32),
                        pltpu.SemaphoreType.DMA((n_exp,))],
    )
    return pl.pallas_call(
        functools.partial(_combine_kernel, n_exp=n_exp, tile0=tile0),
        grid_spec=grid_spec,
        out_shape=jax.ShapeDtypeStruct((n, d), F32),
        compiler_params=_params("arbitrary"),
        name="moe_combine",
    )(base, cnt, rank_t, x1, mod6, fg, ys)


def _round_up(a, m):
    return (a + m - 1) // m * m


def _moe_routed(pre_p, pre_s, wts):
    x1_p, h2_p, gates_p, mod6_p, tps_p = pre_p
    x1_s, h2_s, gates_s, mod6_s, _ = pre_s
    n_p, d = x1_p.shape
    n_s = x1_s.shape[0]
    n_exp = wts["bgu"].shape[0]
    assert n_p % ROUTE_TILE == 0 and n_s <= ROUTE_TILE and mod6_s.shape[2] == n_s
    pad_rows = lambda a: jnp.pad(a, ((0, ROUTE_TILE - n_s), (0, 0)))
    h_all = jnp.concatenate([h2_p, pad_rows(h2_s)], axis=0)
    g_all = jnp.concatenate([gates_p, pad_rows(gates_s)], axis=0)
    n_all = h_all.shape[0]
    n_tiles = n_all // ROUTE_TILE
    rank_t, gate_t, cnt_f = _route_meta(g_all, n_exp=n_exp)

    cnt = cnt_f[:, :n_exp, 0].astype(I32)
    seg = _round_up(cnt, ROUTE_ALIGN)
    before = lambda m: jnp.arange(m)[:, None] > jnp.arange(m)[None, :]
    seg_off = jnp.sum(jnp.where(before(n_tiles)[:, :, None], seg[None, :, :], 0), axis=1)
    region = _round_up(jnp.sum(seg, axis=0) + ROUTE_WIN, FFN_ROWS)
    region_start = jnp.sum(jnp.where(before(n_exp), region[None, :], 0), axis=1)
    region_end = region_start + region
    base = (region_start[None, :] + seg_off).astype(I32)
    n_rows = _round_up(TOP_K * n_all + (ROUTE_ALIGN - 1) * n_tiles * n_exp + (ROUTE_WIN + FFN_ROWS) * n_exp, FFN_ROWS)
    tile_start = jnp.arange(n_rows // FFN_ROWS, dtype=I32) * FFN_ROWS
    tile_expert = jnp.minimum(jnp.sum((tile_start[:, None] >= region_end[None, :]).astype(I32), axis=1), n_exp - 1)
    n_used = (region_end[-1:] // FFN_ROWS).astype(I32)

    xs, gs = _dispatch(base, cnt, h_all, rank_t, gate_t, n_rows)
    ys = _ffn_sorted(tile_expert, n_used, xs, gs, wts["wgu32"], wts["bgu"], wts["wd32"], wts["bd"])
    y_p = _combine(base, cnt, rank_t, x1_p, mod6_p, wts["fg"], ys, tile0=0, tiles_per_seq=tps_p)
    mod6_sp = jnp.pad(mod6_s, ((0, 0), (0, 0), (0, ROUTE_TILE - n_s), (0, 0)))
    y_s = _combine(base, cnt, rank_t, pad_rows(x1_s), mod6_sp, wts["fg"], ys, tile0=n_p // ROUTE_TILE,
                   tiles_per_seq=1)
    return y_p, y_s[:n_s]


def _pad_cols(a, width):
    return jnp.pad(a, ((0, 0), (0, width - a.shape[1])))


def _rope_tables(pos):
    half = HEAD_DIM // 2
    inv = ROPE_THETA ** (-jnp.arange(half, dtype=F32) / half)
    ang = pos.astype(F32)[:, None] * inv[None, :]
    cos, sin = jnp.cos(ang), jnp.sin(ang)
    cos_h = jnp.concatenate([cos, cos], axis=1)
    sin_h = jnp.concatenate([-sin, sin], axis=1)
    ones = jnp.ones_like(cos_h)
    zeros = jnp.zeros_like(cos_h)
    tabs = jnp.stack([jnp.concatenate([cos_h, cos_h], 1), jnp.concatenate([sin_h, sin_h], 1),
                      jnp.concatenate([cos_h, ones], 1), jnp.concatenate([sin_h, zeros], 1)])
    return tabs


def _layer(x, mod_rows, pos, shift_prev, wkv_prev, wts, attend, *, tm, chunk, seqs_per_step):
    b, t, d = x.shape
    n = b * t
    rw, att_w, n_rw, rwkv_in = wts["rw"], wts["att_w"], wts["n_rw"], wts["rwkv_in"]
    per_token_mod = mod_rows.shape[0] != b or tm > t
    if per_token_mod:
        mod6 = jnp.repeat(mod_rows, t, axis=0).reshape(n // tm, tm, 6, d).transpose(2, 0, 1, 3)
        tiles_per_seq = 1
    else:
        mod6 = mod_rows.reshape(b, 1, 6, d).transpose(2, 0, 1, 3)
        tiles_per_seq = t // tm
    cs = _rope_tables(pos)
    n_idx_scale = wts["idx_scale"]
    cs = cs.at[2, :, HEAD_DIM:].set(n_idx_scale)
    if cs.shape[1] < tm:
        cs = jnp.tile(cs, (1, tm // cs.shape[1], 1))
    x2d = x.reshape(n, d)
    p, qt, kt32, k_hm, vt32, vt_ck, qit, kw, kwt = _inproj(x2d, mod6, wts["g1"], wts["w_cat"], cs, tm=tm,
                                                           tiles_per_seq=tiles_per_seq, n_rw=n_rw, att_w=att_w)
    t_pad = -(-t // chunk) * chunk
    p3 = p.reshape(b, t, n_rw)
    if t_pad != t:
        p3 = jnp.pad(p3, ((0, 0), (0, t_pad - t), (0, 0)))
    z0 = jnp.swapaxes(wkv_prev, -1, -2)
    yr, z_fin = _rwkv(p3, shift_prev.reshape(b, 1, n_rw), z0, wts["mu"], wts["vec"], wts["w2"], wts["a2"], wts["g2"],
                      wts["seg"], L=chunk, nb=seqs_per_step, t_valid=t if t_pad != t else chunk, rw=rw)
    yr = yr[:, :t].reshape(n, rw)
    wkv_new = jnp.swapaxes(z_fin, -1, -2)
    n_heads = att_w // HEAD_DIM
    tok = lambda a_t: jnp.swapaxes(a_t, 1, 2).reshape(b, t, a_t.shape[1])
    ya = attend(qt, qit, kw, kwt, kt32, k_hm, vt32, vt_ck).reshape(n, att_w)
    x1, h2, gates = _outproj(x2d, yr, ya, mod6, wts["w_out"], wts["g2n"], wts["router_w"], wts["router_b"], tm=tm,
                             tiles_per_seq=tiles_per_seq, rw=rw)
    pre_moe = (x1, h2, gates, mod6, 1 if per_token_mod else t // ROUTE_TILE)
    p_last = p3[:, t - 1, :]
    new_shift = jnp.concatenate([p_last[:, :3 * rw], p_last[:, 3 * rw:3 * rw + wts["lora"][0]],
                                 p_last[:, 3 * rw + 128:3 * rw + 128 + wts["lora"][1]],
                                 p_last[:, 3 * rw + 256:3 * rw + 256 + wts["lora"][2]]], axis=1)
    state = (tok(kt32).reshape(b, t, n_heads, HEAD_DIM), tok(vt32).reshape(b, t, n_heads, HEAD_DIM),
             tok(kwt[:, :HEAD_DIM]), wkv_new, new_shift)
    return pre_moe, state


def kernel(x_prompt, x_sample, c_prompt, c_sample, cache_k, cache_v, cache_idx_k, state_wkv, state_shift, page_table, norm1_g, norm2_g, w_mod, b_mod, w_in, mu_shift, w0, w2, a0, a2, g2, k_k, k_a, r_k, lnx_g, lnx_b, w_out, router_w, router_b, w_gate_up, b_gate_up, w_down, b_down, final_g):
    depth = w_in.shape[0]
    assert depth == 1, "single-layer trunk"
    l = 0
    bp, tp, d = x_prompt.shape
    bs, ts, _ = x_sample.shape
    rw = w0.shape[1]
    att_w = w_out.shape[1] - rw
    lora = (w2.shape[1], a2.shape[1], g2.shape[1])
    rwkv_in = 3 * rw + sum(lora)
    n_idx = (w_in.shape[2] - rwkv_in - 3 * att_w - HEAD_DIM) // (HEAD_DIM + 1)
    n_exp = router_w.shape[2]
    n_pages = page_table.shape[1]
    past = n_pages * PAGE_SIZE
    n_rw = 3 * rw + 512
    assert lora[0] <= 128 and lora[1] <= 128 and lora[2] <= 256 and n_idx * HEAD_DIM == att_w

    wi = w_in[l]
    o = 3 * rw
    rw_cols = [wi[:, :o], _pad_cols(wi[:, o:o + lora[0]], 128), _pad_cols(wi[:, o + lora[0]:o + lora[0] + lora[1]], 128),
               _pad_cols(wi[:, o + lora[0] + lora[1]:rwkv_in], 256)]
    a0_ = rwkv_in
    att_cols = [wi[:, a0_:a0_ + 4 * att_w], _pad_cols(wi[:, a0_ + 4 * att_w:], LANES)]
    w_cat = jnp.concatenate(rw_cols + att_cols, axis=1).astype(BF16)
    mu = mu_shift[l]
    mu_pad = jnp.concatenate([mu[:o], jnp.pad(mu[o:o + lora[0]], (0, 128 - lora[0])),
                              jnp.pad(mu[o + lora[0]:o + lora[0] + lora[1]], (0, 128 - lora[1])),
                              jnp.pad(mu[o + lora[0] + lora[1]:], (0, 256 - lora[2]))]).reshape(1, n_rw)
    vec = jnp.stack([w0[l], a0[l], k_k[l], k_a[l], lnx_g[l], lnx_b[l], r_k[l].reshape(-1),
                     jnp.zeros((rw,), F32)])
    head_of = np.arange(rw) // HEAD_DIM
    seg = jnp.asarray((head_of[:, None] == head_of[None, :]).astype(np.float32)).astype(BF16)
    wts = dict(
        rw=rw, att_w=att_w, n_rw=n_rw, rwkv_in=rwkv_in, lora=lora,
        idx_scale=float((n_idx * HEAD_DIM) ** -0.5),
        g1=norm1_g[l].reshape(1, d), w_cat=w_cat, mu=mu_pad, vec=vec,
        w2=jnp.pad(w2[l], ((0, 128 - lora[0]), (0, 0))), a2=jnp.pad(a2[l], ((0, 128 - lora[1]), (0, 0))),
        g2=jnp.pad(g2[l], ((0, 256 - lora[2]), (0, 0))), seg=seg,
        w_out=w_out[l].astype(BF16), g2n=norm2_g[l].reshape(1, d),
        router_w=_pad_cols(router_w[l], LANES),
        router_b=jnp.concatenate([router_b[l], jnp.full((LANES - n_exp,), NEG_BIG, F32)]).reshape(1, LANES),
        wgu32=w_gate_up[l], bgu=b_gate_up[l].reshape(n_exp, 1, -1),
        wd32=w_down[l], bd=b_down[l].reshape(n_exp, 1, -1), fg=final_g.reshape(1, d),
    )

    c_all = jnp.concatenate([c_prompt, c_sample], axis=0)
    rows_pad = -(-c_all.shape[0] // SUBLANES) * SUBLANES
    mod_all = _modulation(jnp.pad(c_all, ((0, rows_pad - c_all.shape[0]), (0, 0))), w_mod[l], b_mod[l])
    mod_p, mod_s = mod_all[:bp], mod_all[bp:bp + bs]

    k_top_p = min(TOPK_MAX, tp // 4)

    tm_p = min(512, tp)

    def attend_prompt(qt, qit, kw, kwt, kt32, k_hm, vt32, vt_ck):
        wit = kwt[:, HEAD_DIM:HEAD_DIM + SUBLANES, :]
        kib = kw[:, :HEAD_DIM].astype(BF16).reshape(bp, tp, HEAD_DIM)
        out_t = _dsa_prompt(qt, qit, wit, k_hm, vt_ck, kib, tq=min(256, tp), tk=tm_p, k_top=k_top_p)
        return jnp.swapaxes(out_t, 1, 2)

    pre_p, st_p = _layer(x_prompt, mod_p, jnp.arange(tp), jnp.zeros((bp, n_rw), F32),
                       jnp.zeros((bp, rw // HEAD_DIM, HEAD_DIM, HEAD_DIM), F32), wts, attend_prompt,
                       tm=tm_p, chunk=64, seqs_per_step=2 if bp % 2 == 0 else 1)

    k_top_s = min(TOPK_MAX, (past + ts) // 4)
    n_heads = att_w // HEAD_DIM
    assert ts <= SUBLANES
    pool_t = lambda c: jnp.transpose(c, (0, 1, 3, 4, 2)).reshape(c.shape[0], c.shape[1], att_w, PAGE_SIZE)
    ck_t, cv_t = pool_t(cache_k), pool_t(cache_v)
    cik_t = jnp.transpose(cache_idx_k, (0, 1, 3, 2))
    g_idx = 16 if n_pages % 16 == 0 else 1
    g_att = 16 if n_pages % 16 == 0 else (8 if n_pages % 8 == 0 else 1)

    def head_query_rows(a, nh):
        a = a.reshape(bs, ts, nh, HEAD_DIM).transpose(0, 2, 1, 3)
        a = jnp.pad(a, ((0, 0), (0, 0), (0, SUBLANES - ts), (0, 0)))
        return a.reshape(bs, nh * SUBLANES, HEAD_DIM)

    def new_rows_t(a):
        a = a.reshape(bs, ts, a.shape[1]).transpose(0, 2, 1)
        return jnp.pad(a, ((0, 0), (0, 0), (0, PAGE_SIZE - ts)))

    def attend_sample(qt, qit, kw, kwt, kt32, k_hm, vt32, vt_ck):
        tok2 = lambda a_t: jnp.swapaxes(a_t, 1, 2).reshape(bs * ts, a_t.shape[1])
        q, qi, k32, v32 = tok2(qt), tok2(qit), tok2(kt32), tok2(vt32)
        qim = head_query_rows(qi, n_idx)
        wcol = kw[:, HEAD_DIM:HEAD_DIM + n_idx].reshape(bs, ts, n_idx).transpose(0, 2, 1)
        wcol = jnp.pad(wcol, ((0, 0), (0, 0), (0, SUBLANES - ts))).reshape(bs, n_idx * SUBLANES, 1)
        past_scores = _idx_sample(page_table, qim, wcol, cik_t, t_new=ts, g_pages=g_idx, layer=l)
        new_scores = _idx_new(qim, wcol, new_rows_t(kw[:, :HEAD_DIM]), t_new=ts)
        bias = _select_sample(past_scores, new_scores, k_top=k_top_s)
        qh = head_query_rows(q, n_heads)
        eye = jnp.eye(n_heads, dtype=BF16)
        qbd = (qh.reshape(bs, n_heads, SUBLANES, 1, HEAD_DIM) * eye[None, :, None, :, None]).reshape(
            bs, n_heads * SUBLANES, att_w)
        out = _attn_sample(page_table, qbd, bias, ck_t, cv_t, new_rows_t(k32), new_rows_t(v32), g_pages=g_att,
                           layer=l)
        return out[:, :ts]

    shift_s = state_shift[l]
    o1, o2 = o + lora[0], o + lora[0] + lora[1]
    shift_pad = jnp.concatenate([shift_s[:, :o], _pad_cols(shift_s[:, o:o1], 128), _pad_cols(shift_s[:, o1:o2], 128),
                                 _pad_cols(shift_s[:, o2:], 256)], axis=1)
    pre_s, st_s = _layer(x_sample, mod_s, past + jnp.arange(ts), shift_pad, state_wkv[l], wts, attend_sample,
                       tm=bs * ts, chunk=SUBLANES, seqs_per_step=4 if bs % 4 == 0 else 1)

    y_p, y_s = _moe_routed(pre_p, pre_s, wts)
    y_p, y_s = y_p.reshape(bp, tp, d), y_s.reshape(bs, ts, d)
    ex = lambda a: a[None]
    return (y_p, y_s,ex(st_p[0]), ex(st_p[1]), ex(st_p[2]), ex(st_p[3]), ex(st_p[4]),
            ex(st_s[0]), ex(st_s[1]), ex(st_s[2]), ex(st_s[3]), ex(st_s[4]))
```

```python
import functools

import numpy as np
import jax
import jax.numpy as jnp
from jax import lax
from jax.experimental import pallas as pl
from jax.experimental.pallas import tpu as pltpu

F32 = jnp.float32
BF16 = jnp.bfloat16
I32 = jnp.int32
HI = lax.Precision.HIGHEST

HEAD_DIM = 64
PAGE_SIZE = 128
TOPK_MAX = 256
ROPE_THETA = 10000.0
TOP_K = 4
SWIGLU_LIMIT = 7.0
SWIGLU_ALPHA = 1.702
NORM_EPS = 1e-5
LNX_EPS = 64e-5

LANES = 128
SUBLANES = 8
VMEM_LIMIT = 56 * 1024 * 1024

NEG_BIG = -1e30
INT_MIN = -(2 ** 31)
KEY_NEG_INF = INT_MIN + 0x7FFFFF


def _dot(a, b, prec=None):
    return lax.dot_general(a, b, (((1,), (0,)), ((), ())), precision=prec, preferred_element_type=F32)


def _dot_nt(a, b, prec=None):
    return lax.dot_general(a, b, (((1,), (1,)), ((), ())), precision=prec, preferred_element_type=F32)


def _dot_tn(a, b, prec=None):
    return lax.dot_general(a, b, (((0,), (0,)), ((), ())), precision=prec, preferred_element_type=F32)


_NN = ((1,), (0,))
_NT = ((1,), (1,))
_TN = ((0,), (0,))


def _split(a, terms=2):
    out = []
    for i in range(terms):
        t = a.astype(BF16)
        out.append(t)
        if i + 1 < terms:
            a = a - t.astype(F32)
    return out


def _mm(a_terms, b_terms, dims=_NN, order=1):
    acc = None
    for i, a in enumerate(a_terms):
        for j, b in enumerate(b_terms):
            if i + j <= order:
                d = lax.dot_general(a, b, (dims, ((), ())), preferred_element_type=F32)
                acc = d if acc is None else acc + d
    return acc


def _params(*sem):
    return pltpu.CompilerParams(dimension_semantics=sem, vmem_limit_bytes=VMEM_LIMIT)


def _sigmoid(x):
    return jax.nn.sigmoid(x)


def _rms(x, g):
    return x * lax.rsqrt(jnp.mean(x * x, axis=-1, keepdims=True) + NORM_EPS) * g


def _mod_kernel(c_ref, w_ref, b_ref, o_ref):
    c = c_ref[...]
    o_ref[...] = _dot(c * _sigmoid(c), w_ref[...], HI) + b_ref[...]


def _modulation(c_pad, w_mod, b_mod):
    rows, d = c_pad.shape
    n = w_mod.shape[1]
    return pl.pallas_call(
        _mod_kernel,
        grid=(n // d,),
        in_specs=[pl.BlockSpec((rows, d), lambda j: (0, 0)),
                  pl.BlockSpec((d, d), lambda j: (0, j)),
                  pl.BlockSpec((1, d), lambda j: (0, j))],
        out_specs=pl.BlockSpec((rows, d), lambda j: (0, j)),
        out_shape=jax.ShapeDtypeStruct((rows, n), F32),
        compiler_params=_params("arbitrary"),
        name="modulation",
    )(c_pad, w_mod, b_mod.reshape(1, n))


def _rope_block(x, cos, sin_signed):
    lane = lax.broadcasted_iota(I32, x.shape, 1)
    partner = jnp.where((lane & 32) == 0, pltpu.roll(x, LANES - 32, axis=1), pltpu.roll(x, 32, axis=1))
    return x * cos + partner * sin_signed


def _inproj_kernel(x_ref, mod_ref, g_ref, w_ref, cs_ref, p_ref, qt_ref, kt_ref, khm_ref, vt_ref, vtb_ref, qit_ref,
                   kw_ref, kwt_ref, *, n_rw, att_w):
    x = x_ref[...]
    h = _rms(x, g_ref[...]) * (1 + mod_ref[1, 0]) + mod_ref[0, 0]
    hb = h.astype(BF16)
    cb = 512
    for j in range(n_rw // cb):
        p_ref[:, j * cb:(j + 1) * cb] = _dot(hb, w_ref[:, j * cb:(j + 1) * cb])
    cos, sin = cs_ref[0], cs_ref[1]
    base = n_rw
    ng = att_w // LANES
    grp = lambda g: slice(g * LANES, (g + 1) * LANES)

    def roped(off):
        t = _dot(hb, w_ref[:, off:off + att_w])
        return [_rope_block(t[:, grp(g)], cos, sin) for g in range(ng)]

    for g, blk in enumerate(roped(base)):
        qt_ref[0, grp(g), :] = jnp.transpose(blk * (HEAD_DIM ** -0.5)).astype(BF16)
    for g, blk in enumerate(roped(base + att_w)):
        kt_ref[0, grp(g), :] = jnp.transpose(blk)
        khm_ref[0, 2 * g] = blk[:, :HEAD_DIM].astype(BF16)
        khm_ref[0, 2 * g + 1] = blk[:, HEAD_DIM:].astype(BF16)
    v = _dot(hb, w_ref[:, base + 2 * att_w:base + 3 * att_w])
    for g in range(ng):
        vg_t = jnp.transpose(v[:, grp(g)])
        vt_ref[0, grp(g), :] = vg_t
        vtb_ref[0, 0, grp(g), :] = vg_t.astype(BF16)
    for g, blk in enumerate(roped(base + 3 * att_w)):
        qit_ref[0, grp(g), :] = jnp.transpose(blk).astype(BF16)
    kw = _rope_block(_dot(hb, w_ref[:, base + 4 * att_w:base + 4 * att_w + LANES]), cs_ref[2], cs_ref[3])
    kw_ref[...] = kw
    kwt_ref[0] = jnp.transpose(kw)


def _inproj(x2d, mod6, g1, w_cat, cs, *, tm, tiles_per_seq, n_rw, att_w):
    n, d = x2d.shape
    rmod = mod6.shape[2]
    n_cs_tiles = cs.shape[1] // tm
    t_seq = tiles_per_seq * tm
    n_seq = n // t_seq
    nh = att_w // HEAD_DIM
    grid = (n // tm,)
    row = lambda i: (i, 0)
    feat = lambda i: (i // tiles_per_seq, 0, i % tiles_per_seq)
    outs = [
        (jax.ShapeDtypeStruct((n, n_rw), F32), pl.BlockSpec((tm, n_rw), row)),
        (jax.ShapeDtypeStruct((n_seq, att_w, t_seq), BF16), pl.BlockSpec((1, att_w, tm), feat)),
        (jax.ShapeDtypeStruct((n_seq, att_w, t_seq), F32), pl.BlockSpec((1, att_w, tm), feat)),
        (jax.ShapeDtypeStruct((n_seq, nh, t_seq, HEAD_DIM), BF16),
         pl.BlockSpec((1, nh, tm, HEAD_DIM), lambda i: (i // tiles_per_seq, 0, i % tiles_per_seq, 0))),
        (jax.ShapeDtypeStruct((n_seq, att_w, t_seq), F32), pl.BlockSpec((1, att_w, tm), feat)),
        (jax.ShapeDtypeStruct((n_seq, tiles_per_seq, att_w, tm), BF16),
         pl.BlockSpec((1, 1, att_w, tm), lambda i: (i // tiles_per_seq, i % tiles_per_seq, 0, 0))),
        (jax.ShapeDtypeStruct((n_seq, att_w, t_seq), BF16), pl.BlockSpec((1, att_w, tm), feat)),
        (jax.ShapeDtypeStruct((n, LANES), F32), pl.BlockSpec((tm, LANES), row)),
        (jax.ShapeDtypeStruct((n_seq, LANES, t_seq), F32), pl.BlockSpec((1, LANES, tm), feat)),
    ]
    return pl.pallas_call(
        functools.partial(_inproj_kernel, n_rw=n_rw, att_w=att_w),
        grid=grid,
        in_specs=[pl.BlockSpec((tm, d), row),
                  pl.BlockSpec((6, 1, rmod, d), lambda i: (0, i // tiles_per_seq, 0, 0)),
                  pl.BlockSpec((1, d), lambda i: (0, 0)),
                  pl.BlockSpec(w_cat.shape, lambda i: (0, 0)),
                  pl.BlockSpec((4, tm, LANES), lambda i: (0, i % n_cs_tiles, 0))],
        out_specs=[o[1] for o in outs],
        out_shape=[o[0] for o in outs],
        compiler_params=_params("arbitrary"),
        name="inproj",
    )(x2d, mod6, g1, w_cat, cs)


def _softplus(x):
    return jnp.maximum(x, 0.0) + jnp.log(1.0 + jnp.exp(-jnp.abs(x)))


def _rwkv_kernel(p_ref, sh0_ref, z0_ref, mu_ref, vec_ref, w2_ref, a2_ref, g2_ref, seg_ref, y_ref, zout_ref,
                 carry_ref, z_ref, ops_ref, yh_ref, *, L, nb, t_valid, rw, n_heads):
    c = pl.program_id(1)
    R = nb * L

    @pl.when(c == 0)
    def _():
        carry_ref[...] = sh0_ref[...]
        z_ref[...] = z0_ref[...].reshape(z_ref.shape)

    p = p_ref[...].reshape(R, p_ref.shape[2])
    row = lax.broadcasted_iota(I32, (nb, L, 1), 1).reshape(R, 1)
    seq = lax.broadcasted_iota(I32, (nb, L, 1), 0).reshape(R, 1)

    def per_seq_rows(vals):
        out = vals[0]
        for s in range(1, nb):
            out = jnp.where(seq >= s, vals[s], out)
        return out

    prev = jnp.where(row == 0, per_seq_rows([carry_ref[s] for s in range(nb)]), pltpu.roll(p, 1, axis=0))
    for s in range(nb):
        carry_ref[s] = p[(s + 1) * L - 1:(s + 1) * L, :]
    z = p + (prev - p) * mu_ref[...]
    r, k, v = z[:, 0:rw], z[:, rw:2 * rw], z[:, 2 * rw:3 * rw]
    o = 3 * rw
    zw, za, zg = z[:, o:o + 128], z[:, o + 128:o + 256], z[:, o + 256:o + 512]
    w0, a0, k_k, k_a = vec_ref[0:1], vec_ref[1:2], vec_ref[2:3], vec_ref[3:4]
    lnx_g, lnx_b, r_k = vec_ref[4:5], vec_ref[5:6], vec_ref[6:7]
    seg = [seg_ref[...]]

    def seg_sum(t):
        return _mm(_split(t, 3), seg, order=2)

    def lora(t, w_ref):
        return _mm(_split(t), _split(w_ref[...]))

    w_log = -_softplus(-(w0 + lora(jnp.tanh(zw), w2_ref))) - 0.5
    lw = -jnp.exp(w_log)
    a = _sigmoid(a0 + lora(za, a2_ref))
    g = lora(_sigmoid(zg), g2_ref)
    kk = k * k_k
    kkn = kk / jnp.maximum(jnp.sqrt(seg_sum(kk * kk)), 1e-12)
    k2 = k * (1 + (a - 1) * k_a)
    an, bn = -kkn, kkn * a
    if t_valid < L:
        valid = row < t_valid
        lw = jnp.where(valid, lw, 0.0)
        an, bn = jnp.where(valid, an, 0.0), jnp.where(valid, bn, 0.0)
        k2s, vs = jnp.where(valid, k2, 0.0), jnp.where(valid, v, 0.0)
    else:
        k2s, vs = k2, v

    ri = lax.broadcasted_iota(I32, (L, L), 0)
    ci = lax.broadcasted_iota(I32, (L, L), 1)
    incl, strict = ri >= ci, ri > ci
    lw_terms = _split(lw, 3)
    rr = lax.broadcasted_iota(I32, (R, R), 0)
    rc = lax.broadcasted_iota(I32, (R, R), 1)
    same_seq_incl = (rr >= rc) & (rc >= 0)
    for s in range(1, nb):
        same_seq_incl = same_seq_incl & ((rr < s * L) | (rc >= s * L))
    cum = _mm([same_seq_incl.astype(BF16)], lw_terms, order=2)
    c_last = per_seq_rows([cum[(s + 1) * L - 1:(s + 1) * L, :] for s in range(nb)])
    e_c, e_cp, e_n, e_d = jnp.exp(cum), jnp.exp(cum - lw), jnp.exp(-cum), jnp.exp(c_last - cum)
    ops_ref[0] = an * e_cp
    ops_ref[1] = r * e_c
    ops_ref[2] = bn * e_n
    ops_ref[3] = k2s * e_n
    ops_ref[4] = bn * e_d
    ops_ref[5] = k2s * e_d
    ops_ref[6] = vs
    wl_cols = [jnp.exp(_mm(lw_terms, [jnp.broadcast_to(seq == s, (R, LANES)).astype(BF16)], _TN, order=2))
               for s in range(nb)]
    eye = (ri == ci).astype(F32)
    n_sq = max(int(np.ceil(np.log2(L))) - 1, 0)

    hs = range(nb * n_heads)
    sls = [slice((h % n_heads) * HEAD_DIM, (h % n_heads + 1) * HEAD_DIM) for h in hs]
    rws = [slice((h // n_heads) * L, (h // n_heads + 1) * L) for h in hs]
    at, rt, bt, kt, bh, kh, vh = [[_split(ops_ref[i, rws[h], sls[h]]) for h in hs] for i in range(7)]
    m_ab = [jnp.where(strict, _mm(at[h][:1], bt[h][:1], _NT), 0.0) for h in hs]
    t_inv = [eye + m_ab[h] for h in hs]
    pw = m_ab
    for _ in range(n_sq):
        pw_b = [pw[h].astype(BF16) for h in hs]
        pw = [_mm([pw_b[h]], [pw_b[h]]) for h in hs]
        t_inv = [t_inv[h] + _mm([t_inv[h].astype(BF16)], [pw[h].astype(BF16)]) for h in hs]
    m_ak = [jnp.where(strict, _mm(at[h][:1], kt[h][:1], _NT), 0.0) for h in hs]
    m_rb = [jnp.where(incl, _mm(rt[h][:1], bt[h][:1], _NT), 0.0).astype(BF16) for h in hs]
    m_rk = [jnp.where(incl, _mm(rt[h][:1], kt[h][:1], _NT), 0.0).astype(BF16) for h in hs]
    zs = [z_ref[h] for h in hs]
    zs_t = [_split(zs[h]) for h in hs]
    rhs = [_mm(at[h], zs_t[h]) + _mm(_split(m_ak[h]), vh[h]) for h in hs]
    u_t = [_split(_mm(_split(t_inv[h]), _split(rhs[h]))) for h in hs]
    for h in hs:
        yh_ref[rws[h], sls[h]] = (_mm(rt[h][:1], zs_t[h][:1]) + _mm([m_rb[h]], u_t[h][:1])
                                  + _mm([m_rk[h]], vh[h][:1]))
    for h in hs:
        z_ref[h] = (wl_cols[h // n_heads][sls[h], 0:HEAD_DIM] * zs[h] + _mm(bh[h], u_t[h], _TN)
                    + _mm(kh[h], vh[h], _TN))

    y = yh_ref[...]
    inv_n = 1.0 / HEAD_DIM
    mean = seg_sum(y) * inv_n
    yc = y - mean
    var = seg_sum(yc * yc) * inv_n
    yn = yc * lax.rsqrt(var + LNX_EPS) * lnx_g + lnx_b
    bonus = seg_sum(r * k2 * r_k) * v
    y_ref[...] = ((yn + bonus) * g).astype(y_ref.dtype).reshape(y_ref.shape)

    @pl.when(c == pl.num_programs(1) - 1)
    def _():
        zout_ref[...] = z_ref[...].reshape(zout_ref.shape)


def _rwkv(p3, shift0, z0, mu, vec, w2, a2, g2, seg, *, L, nb, t_valid, rw):
    b, t, n_rw = p3.shape
    n_heads = rw // HEAD_DIM
    assert b % nb == 0
    const2 = lambda i, c: (0, 0)
    return pl.pallas_call(
        functools.partial(_rwkv_kernel, L=L, nb=nb, t_valid=t_valid, rw=rw, n_heads=n_heads),
        grid=(b // nb, t // L),
        in_specs=[pl.BlockSpec((nb, L, n_rw), lambda i, c: (i, c, 0)),
                  pl.BlockSpec((nb, 1, n_rw), lambda i, c: (i, 0, 0)),
                  pl.BlockSpec((nb, n_heads, HEAD_DIM, HEAD_DIM), lambda i, c: (i, 0, 0, 0)),
                  pl.BlockSpec((1, n_rw), const2),
                  pl.BlockSpec(vec.shape, const2),
                  pl.BlockSpec(w2.shape, const2),
                  pl.BlockSpec(a2.shape, const2),
                  pl.BlockSpec(g2.shape, const2),
                  pl.BlockSpec(seg.shape, const2)],
        out_specs=[pl.BlockSpec((nb, L, rw), lambda i, c: (i, c, 0)),
                   pl.BlockSpec((nb, n_heads, HEAD_DIM, HEAD_DIM), lambda i, c: (i, 0, 0, 0))],
        out_shape=[jax.ShapeDtypeStruct((b, t, rw), BF16),
                   jax.ShapeDtypeStruct((b, n_heads, HEAD_DIM, HEAD_DIM), F32)],
        scratch_shapes=[pltpu.VMEM((nb, 1, n_rw), F32),
                        pltpu.VMEM((nb * n_heads, HEAD_DIM, HEAD_DIM), F32),
                        pltpu.VMEM((7, nb * L, rw), F32),
                        pltpu.VMEM((nb * L, rw), F32)],
        compiler_params=_params("arbitrary", "arbitrary"),
        name="rwkv_scan",
    )(p3, shift0, z0, mu, vec, w2, a2, g2, seg)


def _key_to_float(key):
    bits = key ^ ((key >> 31) & 0x7FFFFFFF)
    return lax.bitcast_convert_type(bits, F32)


def _fold_lanes(m):
    acc = m[:, 0:LANES]
    for j in range(1, m.shape[1] // LANES):
        acc = acc + m[:, j * LANES:(j + 1) * LANES]
    return acc


def _select_bias(sc_ref, pb_ref, nc, n_q, tk, k_top, n_keys_total, key_axis=1):
    if key_axis == 1:
        q_shape, part_shape = (n_q, 1), (n_q, LANES)
        fold = lambda m: _fold_lanes(m.astype(F32))
        total = lambda acc: jnp.sum(acc, axis=1, keepdims=True)
    else:
        part_rows = min(tk, 8 * SUBLANES)
        q_shape, part_shape = (1, n_q), (part_rows, n_q)
        fold = lambda m: jnp.sum(m.astype(F32).reshape(tk // part_rows, part_rows, n_q), axis=0)
        total = lambda acc: jnp.sum(acc, axis=0, keepdims=True)

    def count(pred):
        def body(c, acc):
            return acc + fold(pred(c, sc_ref[c]))
        return total(lax.fori_loop(0, nc, body, jnp.zeros(part_shape, F32)))

    kf = float(k_top)
    cnt0 = count(lambda c, s: s >= 0.0)
    tau0 = jnp.where(cnt0 >= kf, 0, INT_MIN).astype(I32)

    def bit_body(i, tau):
        cand = tau + jnp.left_shift(jnp.int32(1), 30 - i)
        cand_f = _key_to_float(cand)
        cnt = count(lambda c, s: s >= cand_f)
        return jnp.where((cnt >= kf) | (cand <= KEY_NEG_INF), cand, tau)

    tau = lax.fori_loop(0, 31, bit_body, tau0)
    tau_f = _key_to_float(tau)
    need = kf - count(lambda c, s: s > tau_f)
    excess = count(lambda c, s: s == tau_f) - need

    def kidx(c):
        return c * tk + lax.broadcasted_iota(I32, (1, tk) if key_axis == 1 else (tk, 1), key_axis)

    n_bits = max(int(np.ceil(np.log2(n_keys_total))), 1)
    pb_ref[...] = jnp.full(q_shape, n_keys_total, I32)

    @pl.when(jnp.max(excess) > 0.0)
    def _():
        def tie_body(i, pb):
            cand = pb + jnp.left_shift(jnp.int32(1), n_bits - 1 - i)
            cnt = count(lambda c, s: (s == tau_f) & (kidx(c) < cand))
            return jnp.where(cnt < need, cand, pb)

        pb_ref[...] = lax.fori_loop(0, n_bits, tie_body, jnp.zeros(q_shape, I32))

    pb = pb_ref[...]

    def write(c, _):
        s = sc_ref[c]
        sel = ((s > tau_f) | ((s == tau_f) & (kidx(c) <= pb))) & (s > -jnp.inf)
        sc_ref[c] = jnp.where(sel, 0.0, NEG_BIG)
        return 0

    lax.fori_loop(0, nc, write, 0)


def _dsa_prompt_kernel(qt_ref, qit_ref, wit_ref, k_ref, vt_ref, ki_ref, o_ref, sc_ref, pb_ref, m_ref, l_ref, acc_ref,
                       *, tq, tk, k_top, n_heads, n_idx, t_total):
    qb = pl.program_id(1)
    q0 = qb * tq
    nc = (q0 + tq + tk - 1) // tk
    qpos = q0 + lax.broadcasted_iota(I32, (1, tq), 1)
    hd = lambda h: slice(h * HEAD_DIM, (h + 1) * HEAD_DIM)

    def scores(c, _):
        kic = ki_ref[0, pl.ds(pl.multiple_of(c * tk, tk), tk), :]
        acc = jnp.zeros((tk, tq), F32)
        for h in range(n_idx):
            acc = acc + jnp.maximum(_dot(kic, qit_ref[0, hd(h), :]), 0.0) * wit_ref[0, h:h + 1, :]
        spos = c * tk + lax.broadcasted_iota(I32, (tk, 1), 0)
        sc_ref[c] = jnp.where(spos <= qpos, acc, -jnp.inf)
        return 0

    lax.fori_loop(0, nc, scores, 0)
    _select_bias(sc_ref, pb_ref, nc, tq, tk, k_top, t_total, key_axis=0)

    m_ref[...] = jnp.full(m_ref.shape, NEG_BIG, F32)
    l_ref[...] = jnp.zeros(l_ref.shape, F32)
    acc_ref[...] = jnp.zeros(acc_ref.shape, F32)

    def attend(c, _):
        rows = pl.ds(pl.multiple_of(c * tk, tk), tk)
        bias = sc_ref[c]
        m_all, l_all, acc_all = m_ref[...], l_ref[...], acc_ref[...]
        hs = range(n_heads)
        s = [_dot(k_ref[0, h, rows, :], qt_ref[0, hd(h), :]) + bias for h in hs]
        m_new = [jnp.maximum(m_all[h:h + 1], jnp.max(s[h], axis=0, keepdims=True)) for h in hs]
        p = [jnp.exp(s[h] - m_new[h]) for h in hs]
        pv = [_dot(vt_ref[0, c, hd(h), :], p[h].astype(BF16)) for h in hs]
        alpha = [jnp.exp(m_all[h:h + 1] - m_new[h]) for h in hs]
        l_new = [alpha[h] * l_all[h:h + 1] + jnp.sum(p[h], axis=0, keepdims=True) for h in hs]
        acc_ref[...] = jnp.concatenate([alpha[h] * acc_all[hd(h)] + pv[h] for h in hs], axis=0)
        m_ref[...] = jnp.concatenate(m_new, axis=0)
        l_ref[...] = jnp.concatenate(l_new, axis=0)
        return 0

    lax.fori_loop(0, nc, attend, 0)
    l_all = l_ref[...]
    for h in range(n_heads):
        o_ref[0, hd(h), :] = (acc_ref[hd(h), :] / l_all[h:h + 1]).astype(o_ref.dtype)


def _dsa_prompt(qt, qit, wit, k_hm, vt_ck, ki, *, tq, tk, k_top):
    b, att_w, t = qt.shape
    n_heads = att_w // HEAD_DIM
    n_idx = qit.shape[1] // HEAD_DIM
    assert n_heads == SUBLANES
    qblk = lambda i, j: (i, 0, j)
    return pl.pallas_call(
        functools.partial(_dsa_prompt_kernel, tq=tq, tk=tk, k_top=k_top, n_heads=n_heads, n_idx=n_idx,
                          t_total=t),
        grid=(b, t // tq),
        in_specs=[pl.BlockSpec((1, att_w, tq), qblk),
                  pl.BlockSpec((1, qit.shape[1], tq), qblk),
                  pl.BlockSpec((1, wit.shape[1], tq), qblk),
                  pl.BlockSpec((1,) + k_hm.shape[1:], lambda i, j: (i, 0, 0, 0)),
                  pl.BlockSpec((1,) + vt_ck.shape[1:], lambda i, j: (i, 0, 0, 0)),
                  pl.BlockSpec((1, t, HEAD_DIM), lambda i, j: (i, 0, 0))],
        out_specs=pl.BlockSpec((1, att_w, tq), qblk),
        out_shape=jax.ShapeDtypeStruct((b, att_w, t), BF16),
        scratch_shapes=[pltpu.VMEM((t // tk, tk, tq), F32),
                        pltpu.VMEM((1, tq), I32),
                        pltpu.VMEM((n_heads, tq), F32),
                        pltpu.VMEM((n_heads, tq), F32),
                        pltpu.VMEM((att_w, tq), F32)],
        compiler_params=_params("arbitrary", "arbitrary"),
        name="dsa_prompt",
    )(qt, qit, wit, k_hm, vt_ck, ki)


def _idx_page_scores(qim, wcol, ki_t, n_idx):
    d = jnp.maximum(_dot(qim, ki_t.astype(BF16)), 0.0) * wcol
    sc = d[0:SUBLANES]
    for h in range(1, n_idx):
        sc = sc + d[h * SUBLANES:(h + 1) * SUBLANES]
    return sc


def _idx_sample_kernel(pt_ref, qim_ref, wcol_ref, *refs, t_new, n_idx, g_pages):
    page_refs, o_ref = refs[:g_pages], refs[g_pages]
    qrow = lax.broadcasted_iota(I32, (SUBLANES, PAGE_SIZE), 0)
    for j in range(g_pages):
        sc = _idx_page_scores(qim_ref[0], wcol_ref[0], page_refs[j][0, 0], n_idx)
        o_ref[j] = jnp.where(qrow < t_new, sc, -jnp.inf)


def _idx_sample(page_table, qim, wcol, cache_idx_t, *, t_new, g_pages, layer):
    b, n_pages = page_table.shape
    n_idx = qim.shape[1] // SUBLANES
    per_seq = lambda i, g, pt: (i, 0, 0)
    page_specs = [pl.BlockSpec((1, 1, HEAD_DIM, PAGE_SIZE),
                               lambda i, g, pt, j=j: (layer, pt[i, g * g_pages + j], 0, 0))
                  for j in range(g_pages)]
    grid_spec = pltpu.PrefetchScalarGridSpec(
        num_scalar_prefetch=1,
        grid=(b, n_pages // g_pages),
        in_specs=[pl.BlockSpec((1,) + qim.shape[1:], per_seq),
                  pl.BlockSpec((1,) + wcol.shape[1:], per_seq)] + page_specs,
        out_specs=pl.BlockSpec((g_pages, SUBLANES, PAGE_SIZE), lambda i, g, pt: (g, i, 0)),
    )
    return pl.pallas_call(
        functools.partial(_idx_sample_kernel, t_new=t_new, n_idx=n_idx, g_pages=g_pages),
        grid_spec=grid_spec,
        out_shape=jax.ShapeDtypeStruct((n_pages, b * SUBLANES, PAGE_SIZE), F32),
        compiler_params=_params("arbitrary", "arbitrary"),
        name="idx_sample",
    )(page_table, qim, wcol, *([cache_idx_t] * g_pages))


def _idx_new_kernel(qim_ref, wcol_ref, new_ref, o_ref, *, t_new, n_idx):
    sc = _idx_page_scores(qim_ref[0], wcol_ref[0], new_ref[0], n_idx)
    qrow = lax.broadcasted_iota(I32, (SUBLANES, PAGE_SIZE), 0)
    lane = lax.broadcasted_iota(I32, (SUBLANES, PAGE_SIZE), 1)
    o_ref[...] = jnp.where((qrow < t_new) & (lane <= qrow), sc, -jnp.inf)


def _idx_new(qim, wcol, ki_new, *, t_new):
    b = qim.shape[0]
    n_idx = qim.shape[1] // SUBLANES
    per_seq = lambda i: (i, 0, 0)
    return pl.pallas_call(
        functools.partial(_idx_new_kernel, t_new=t_new, n_idx=n_idx),
        grid=(b,),
        in_specs=[pl.BlockSpec((1,) + qim.shape[1:], per_seq),
                  pl.BlockSpec((1,) + wcol.shape[1:], per_seq),
                  pl.BlockSpec((1, HEAD_DIM, PAGE_SIZE), per_seq)],
        out_specs=pl.BlockSpec((SUBLANES, PAGE_SIZE), lambda i: (i, 0)),
        out_shape=jax.ShapeDtypeStruct((b * SUBLANES, PAGE_SIZE), F32),
        compiler_params=_params("arbitrary"),
        name="idx_new",
    )(qim, wcol, ki_new)


def _select_sample_kernel(past_ref, new_ref, o_ref, pb_ref, *, k_top):
    n_pages, rows = past_ref.shape[0], past_ref.shape[1]
    o_ref[0:n_pages] = past_ref[...]
    o_ref[n_pages] = new_ref[...]
    _select_bias(o_ref, pb_ref, n_pages + 1, rows, PAGE_SIZE, k_top, (n_pages + 1) * PAGE_SIZE)


def _select_sample(past_scores, new_scores, *, k_top):
    n_pages, rows, _ = past_scores.shape
    out_shape = (n_pages + 1, rows, PAGE_SIZE)
    return pl.pallas_call(
        functools.partial(_select_sample_kernel, k_top=k_top),
        grid=(1,),
        in_specs=[pl.BlockSpec(past_scores.shape, lambda i: (0, 0, 0)),
                  pl.BlockSpec(new_scores.shape, lambda i: (0, 0))],
        out_specs=pl.BlockSpec(out_shape, lambda i: (0, 0, 0)),
        out_shape=jax.ShapeDtypeStruct(out_shape, F32),
        scratch_shapes=[pltpu.VMEM((rows, 1), I32)],
        compiler_params=_params("arbitrary"),
        name="select_sample",
    )(past_scores, new_scores)


def _attn_sample_kernel(pt_ref, qbd_ref, bias_ref, biasn_ref, knew_ref, vnew_ref, *refs, n_heads, g_pages):
    k_refs, v_refs = refs[:g_pages], refs[g_pages:2 * g_pages]
    o_ref, m_ref, l_ref, acc_ref = refs[2 * g_pages:]
    g = pl.program_id(1)

    @pl.when(g == 0)
    def _():
        m_ref[...] = jnp.full(m_ref.shape, NEG_BIG, F32)
        l_ref[...] = jnp.zeros(l_ref.shape, F32)
        acc_ref[...] = jnp.zeros(acc_ref.shape, F32)

    def step(kts, vts, biases):
        bias = jnp.concatenate([jnp.concatenate([b_] * n_heads, axis=0) for b_ in biases], axis=1)
        s = jnp.concatenate([_dot(qbd_ref[0], kt.astype(BF16)) for kt in kts], axis=1) + bias
        m = m_ref[...]
        m_new = jnp.maximum(m, jnp.max(s, axis=-1, keepdims=True))
        alpha = jnp.exp(m - m_new)
        pr = jnp.exp(s - m_new)
        prb = pr.astype(BF16)
        pv = None
        for j, vt in enumerate(vts):
            d = _dot_nt(prb[:, j * PAGE_SIZE:(j + 1) * PAGE_SIZE], vt.astype(BF16))
            pv = d if pv is None else pv + d
        l_ref[...] = alpha * l_ref[...] + jnp.sum(pr, axis=-1, keepdims=True)
        acc_ref[...] = alpha * acc_ref[...] + pv
        m_ref[...] = m_new

    step([r[0, 0] for r in k_refs], [r[0, 0] for r in v_refs], [bias_ref[j] for j in range(g_pages)])

    @pl.when(g == pl.num_programs(1) - 1)
    def _():
        step([knew_ref[0]], [vnew_ref[0]], [biasn_ref[0]])
        full = acc_ref[...] / l_ref[...]
        col_head = lax.broadcasted_iota(I32, (SUBLANES, full.shape[1]), 1) // HEAD_DIM
        out = jnp.zeros((SUBLANES, full.shape[1]), F32)
        for h in range(n_heads):
            out = out + jnp.where(col_head == h, full[h * SUBLANES:(h + 1) * SUBLANES], 0.0)
        o_ref[0] = out.astype(o_ref.dtype)


def _attn_sample(page_table, qbd, bias, cache_k_t, cache_v_t, k_new_t, v_new_t, *, g_pages, layer):
    b, n_pages = page_table.shape
    att_w = qbd.shape[2]
    n_heads = att_w // HEAD_DIM
    per_seq = lambda i, g, pt: (i, 0, 0)
    page_specs = [pl.BlockSpec((1, 1, att_w, PAGE_SIZE),
                               lambda i, g, pt, j=j: (layer, pt[i, g * g_pages + j], 0, 0))
                  for j in range(g_pages)]
    new_spec = pl.BlockSpec((1, att_w, PAGE_SIZE), per_seq)
    grid_spec = pltpu.PrefetchScalarGridSpec(
        num_scalar_prefetch=1,
        grid=(b, n_pages // g_pages),
        in_specs=[pl.BlockSpec((1,) + qbd.shape[1:], per_seq),
                  pl.BlockSpec((g_pages, SUBLANES, PAGE_SIZE), lambda i, g, pt: (g, i, 0)),
                  pl.BlockSpec((1, SUBLANES, PAGE_SIZE), lambda i, g, pt: (n_pages, i, 0)),
                  new_spec, new_spec] + page_specs + page_specs,
        out_specs=pl.BlockSpec((1, SUBLANES, att_w), per_seq),
        scratch_shapes=[pltpu.VMEM((n_heads * SUBLANES, 1), F32),
                        pltpu.VMEM((n_heads * SUBLANES, 1), F32),
                        pltpu.VMEM((n_heads * SUBLANES, att_w), F32)],
    )
    return pl.pallas_call(
        functools.partial(_attn_sample_kernel, n_heads=n_heads, g_pages=g_pages),
        grid_spec=grid_spec,
        out_shape=jax.ShapeDtypeStruct((b, SUBLANES, att_w), BF16),
        compiler_params=_params("arbitrary", "arbitrary"),
        name="attn_sample",
    )(page_table, qbd, bias, bias, k_new_t, v_new_t, *([cache_k_t] * g_pages), *([cache_v_t] * g_pages))


def _outproj_kernel(x_ref, yr_ref, ya_ref, mod_ref, wo_ref, g_ref, rw_ref, rb_ref, x1_ref, h2_ref, gates_ref, *, rw):
    mixed = _dot(yr_ref[...], wo_ref[0:rw, :]) + _dot(ya_ref[...], wo_ref[rw:, :])
    x1 = x_ref[...] + mod_ref[2, 0] * mixed
    x1_ref[...] = x1
    h2 = _rms(x1, g_ref[...]) * (1 + mod_ref[4, 0]) + mod_ref[3, 0]
    h2_ref[...] = h2.astype(BF16)
    logits = _mm(_split(h2), _split(rw_ref[...])) + rb_ref[...]
    lane = lax.broadcasted_iota(I32, logits.shape, 1).astype(F32)
    work = logits
    picked = jnp.zeros(logits.shape, jnp.bool_)
    top = None
    for i in range(TOP_K):
        mx = jnp.max(work, axis=-1, keepdims=True)
        first = jnp.min(jnp.where(work == mx, lane, float(LANES)), axis=-1, keepdims=True)
        hit = lane == first
        picked = picked | hit
        work = jnp.where(hit, -jnp.inf, work)
        if i == 0:
            top = mx
    e = jnp.where(picked, jnp.exp(logits - top), 0.0)
    gates_ref[...] = e / jnp.sum(e, axis=-1, keepdims=True)


def _outproj(x2d, yr, ya, mod6, w_out_b, g2n, router_w_pad, router_b_pad, *, tm, tiles_per_seq, rw):
    n, d = x2d.shape
    rmod = mod6.shape[2]
    row = lambda i: (i, 0)
    const = lambda i: (0, 0)
    outs = [jax.ShapeDtypeStruct((n, d), F32), jax.ShapeDtypeStruct((n, d), BF16),
            jax.ShapeDtypeStruct((n, LANES), F32)]
    return pl.pallas_call(
        functools.partial(_outproj_kernel, rw=rw),
        grid=(n // tm,),
        in_specs=[pl.BlockSpec((tm, d), row),
                  pl.BlockSpec((tm, yr.shape[1]), row),
                  pl.BlockSpec((tm, ya.shape[1]), row),
                  pl.BlockSpec((6, 1, rmod, d), lambda i: (0, i // tiles_per_seq, 0, 0)),
                  pl.BlockSpec(w_out_b.shape, const),
                  pl.BlockSpec((1, d), const),
                  pl.BlockSpec(router_w_pad.shape, const),
                  pl.BlockSpec((1, LANES), const)],
        out_specs=[pl.BlockSpec((tm, o.shape[1]), row) for o in outs],
        out_shape=outs,
        compiler_params=_params("arbitrary"),
        name="outproj_router",
    )(x2d, yr, ya, mod6, w_out_b, g2n, router_w_pad, router_b_pad)


ROUTE_TILE = 512
ROUTE_WIN = 128
ROUTE_ALIGN = 16
FFN_ROWS = 512


def _route_meta_kernel(g_ref, rank_ref, gate_ref, cnt_ref, *, n_exp):
    gt = jnp.transpose(g_ref[...])
    sel = gt > 0.0
    tt = gt.shape[1]
    earlier = (lax.broadcasted_iota(I32, (tt, tt), 0) < lax.broadcasted_iota(I32, (tt, tt), 1)).astype(BF16)
    rank = _dot(sel.astype(BF16), earlier)
    rank_ref[...] = jnp.where(sel, rank, -1.0)[:n_exp]
    gate_ref[...] = gt[:n_exp]
    cnt_ref[0] = jnp.broadcast_to(jnp.sum(sel.astype(F32), axis=-1, keepdims=True), (LANES, LANES))


def _route_meta(gates, *, n_exp):
    n = gates.shape[0]
    n_tiles = n // ROUTE_TILE
    return pl.pallas_call(
        functools.partial(_route_meta_kernel, n_exp=n_exp),
        grid=(n_tiles,),
        in_specs=[pl.BlockSpec((ROUTE_TILE, LANES), lambda i: (i, 0))],
        out_specs=[pl.BlockSpec((n_exp, ROUTE_TILE), lambda i: (0, i)),
                   pl.BlockSpec((n_exp, ROUTE_TILE), lambda i: (0, i)),
                   pl.BlockSpec((1, LANES, LANES), lambda i: (i, 0, 0))],
        out_shape=[jax.ShapeDtypeStruct((n_exp, n), F32), jax.ShapeDtypeStruct((n_exp, n), F32),
                   jax.ShapeDtypeStruct((n_tiles, LANES, LANES), F32)],
        compiler_params=_params("arbitrary"),
        name="route_meta",
    )(gates)


def _window_onehot(rank_row, w):
    rows = lax.broadcasted_iota(I32, (ROUTE_WIN, 1), 0).astype(F32) + float(ROUTE_WIN) * w
    return rank_row == rows


def _dispatch_kernel(base_ref, cnt_ref, h_ref, rank_ref, gate_ref, xs_in, gs_in, xs_out, gs_out, p_ref, sx_ref,
                     sg_ref, sem_x, sem_g, *, n_exp, group):
    del xs_in, gs_in
    i = pl.program_id(0)
    h = h_ref[...]
    win = ROUTE_WIN

    def copies(slot, row0):
        rows = pl.ds(pl.multiple_of(row0, ROUTE_ALIGN), win)
        return (pltpu.make_async_copy(sx_ref.at[slot], xs_out.at[rows], sem_x.at[slot]),
                pltpu.make_async_copy(sg_ref.at[slot], gs_out.at[rows], sem_g.at[slot]))

    def fill(slot, e, w):
        p = _window_onehot(rank_ref[pl.ds(e, 1), :], w)
        p_ref[slot * win:(slot + 1) * win, :] = p.astype(BF16)
        g = jnp.sum(jnp.where(p, gate_ref[pl.ds(e, 1), :], 0.0), axis=-1, keepdims=True)
        sg_ref[slot] = jnp.broadcast_to(g, (win, LANES))

    for e0 in range(0, n_exp, group):
        for j in range(group):
            fill(e0 + j, e0 + j, 0)
        x = _dot(p_ref[e0 * win:(e0 + group) * win, :], h)
        for j in range(group):
            e = e0 + j
            sx_ref[e] = x[j * win:(j + 1) * win].astype(BF16)
            for c in copies(e, base_ref[i, e]):
                c.start()
    for e in range(n_exp):
        for c in copies(e, base_ref[i, e]):
            c.wait()

    def per_expert(e, _):
        def extra(w, _):
            fill(0, e, w)
            sx_ref[0] = _dot(p_ref[0:win, :], h_ref[...]).astype(BF16)
            cs = copies(0, base_ref[i, e] + w * win)
            for c in cs:
                c.start()
            for c in cs:
                c.wait()
            return 0

        return lax.fori_loop(1, (cnt_ref[i, e] + win - 1) // win, extra, 0)

    lax.fori_loop(0, n_exp, per_expert, 0)


def _dispatch(base, cnt, h_all, rank_t, gate_t, n_rows):
    n, d = h_all.shape
    n_exp = rank_t.shape[0]
    n_tiles = n // ROUTE_TILE
    tile = lambda i, b_, c_: (0, i)
    grid_spec = pltpu.PrefetchScalarGridSpec(
        num_scalar_prefetch=2,
        grid=(n_tiles,),
        in_specs=[pl.BlockSpec((ROUTE_TILE, d), lambda i, b_, c_: (i, 0)),
                  pl.BlockSpec((n_exp, ROUTE_TILE), tile),
                  pl.BlockSpec((n_exp, ROUTE_TILE), tile),
                  pl.BlockSpec(memory_space=pl.ANY),
                  pl.BlockSpec(memory_space=pl.ANY)],
        out_specs=[pl.BlockSpec(memory_space=pl.ANY), pl.BlockSpec(memory_space=pl.ANY)],
        scratch_shapes=[pltpu.VMEM((n_exp * ROUTE_WIN, ROUTE_TILE), BF16),
                        pltpu.VMEM((n_exp, ROUTE_WIN, d), BF16),
                        pltpu.VMEM((n_exp, ROUTE_WIN, LANES), F32),
                        pltpu.SemaphoreType.DMA((n_exp,)),
                        pltpu.SemaphoreType.DMA((n_exp,))],
    )
    xs0 = jnp.zeros((n_rows, d), BF16)
    gs0 = jnp.zeros((n_rows, LANES), F32)
    return pl.pallas_call(
        functools.partial(_dispatch_kernel, n_exp=n_exp, group=8),
        grid_spec=grid_spec,
        out_shape=[jax.ShapeDtypeStruct(xs0.shape, BF16), jax.ShapeDtypeStruct(gs0.shape, F32)],
        input_output_aliases={5: 0, 6: 1},
        compiler_params=_params("arbitrary"),
        name="moe_dispatch",
    )(base, cnt, h_all, rank_t, gate_t, xs0, gs0)


def _ffn_sorted_kernel(te_ref, nu_ref, xs_ref, gs_ref, wgu_ref, bgu_ref, wd_ref, bd_ref, ys_ref, wgu_b, wd_b, *,
                       d_ff):
    r = pl.program_id(0)

    @pl.when(r < nu_ref[0])
    def _():
        @pl.when((r == 0) | (te_ref[r] != te_ref[jnp.maximum(r - 1, 0)]))
        def _():
            wgu_b[...] = wgu_ref[0].astype(BF16)
            wd_b[...] = wd_ref[0].astype(BF16)

        gu = _dot(xs_ref[...], wgu_b[...]) + bgu_ref[0]
        gate = jnp.minimum(gu[:, :d_ff], SWIGLU_LIMIT)
        up = jnp.clip(gu[:, d_ff:], -SWIGLU_LIMIT, SWIGLU_LIMIT)
        act = (up + 1) * gate * _sigmoid(SWIGLU_ALPHA * gate)
        y = _dot(act.astype(BF16), wd_b[...]) + bd_ref[0]
        ys_ref[...] = (gs_ref[:, 0:1] * y).astype(ys_ref.dtype)

    @pl.when(r >= nu_ref[0])
    def _():
        ys_ref[...] = jnp.zeros(ys_ref.shape, ys_ref.dtype)


def _ffn_sorted(tile_expert, n_used, xs, gs, wgu, bgu, wd, bd):
    n_rows, d = xs.shape
    n_exp, _, two_ff = wgu.shape
    row = lambda r, te, nu: (r, 0)
    exp3 = lambda r, te, nu: (te[r], 0, 0)
    grid_spec = pltpu.PrefetchScalarGridSpec(
        num_scalar_prefetch=2,
        grid=(n_rows // FFN_ROWS,),
        in_specs=[pl.BlockSpec((FFN_ROWS, d), row),
                  pl.BlockSpec((FFN_ROWS, LANES), row),
                  pl.BlockSpec((1, d, two_ff), exp3),
                  pl.BlockSpec((1, 1, two_ff), exp3),
                  pl.BlockSpec((1, two_ff // 2, d), exp3),
                  pl.BlockSpec((1, 1, d), exp3)],
        out_specs=pl.BlockSpec((FFN_ROWS, d), row),
        scratch_shapes=[pltpu.VMEM((d, two_ff), BF16), pltpu.VMEM((two_ff // 2, d), BF16)],
    )
    return pl.pallas_call(
        functools.partial(_ffn_sorted_kernel, d_ff=two_ff // 2),
        grid_spec=grid_spec,
        out_shape=jax.ShapeDtypeStruct((n_rows, d), BF16),
        compiler_params=_params("arbitrary"),
        name="moe_ffn_sorted",
    )(tile_expert, n_used, xs, gs, wgu, bgu, wd, bd)


def _combine_kernel(base_ref, cnt_ref, rank_ref, x1_ref, mod_ref, fg_ref, ys_hbm, o_ref, p_ref, yb_ref, acc_ref,
                    sem, *, n_exp, tile0):
    i = pl.program_id(0) + tile0
    win = ROUTE_WIN

    def copy(slot, row0):
        rows = pl.ds(pl.multiple_of(row0, ROUTE_ALIGN), win)
        return pltpu.make_async_copy(ys_hbm.at[rows], yb_ref.at[pl.ds(slot * win, win)], sem.at[slot])

    for e in range(n_exp):
        copy(e, base_ref[i, e]).start()
    for e in range(n_exp):
        p_ref[e * win:(e + 1) * win, :] = _window_onehot(rank_ref[e:e + 1, :], 0).astype(BF16)
    for e in range(n_exp):
        copy(e, base_ref[i, e]).wait()
    acc_ref[...] = _dot_tn(p_ref[...], yb_ref[...])

    def per_expert(e, _):
        def extra(w, _):
            c = copy(0, base_ref[i, e] + w * win)
            c.start()
            p = _window_onehot(rank_ref[pl.ds(e, 1), :], w).astype(BF16)
            c.wait()
            acc_ref[...] += _dot_tn(p, yb_ref[0:win, :])
            return 0

        return lax.fori_loop(1, (cnt_ref[i, e] + win - 1) // win, extra, 0)

    lax.fori_loop(0, n_exp, per_expert, 0)
    x2 = x1_ref[...] + mod_ref[5, 0] * acc_ref[...]
    o_ref[...] = _rms(x2, fg_ref[...])


def _combine(base, cnt, rank_t, x1, mod6, fg, ys, *, tile0, tiles_per_seq):
    n, d = x1.shape
    n_exp = rank_t.shape[0]
    rmod = mod6.shape[2]
    grid_spec = pltpu.PrefetchScalarGridSpec(
        num_scalar_prefetch=2,
        grid=(n // ROUTE_TILE,),
        in_specs=[pl.BlockSpec((n_exp, ROUTE_TILE), lambda i, b_, c_: (0, i + tile0)),
                  pl.BlockSpec((ROUTE_TILE, d), lambda i, b_, c_: (i, 0)),
                  pl.BlockSpec((6, 1, rmod, d), lambda i, b_, c_: (0, i // tiles_per_seq, 0, 0)),
                  pl.BlockSpec((1, d), lambda i, b_, c_: (0, 0)),
                  pl.BlockSpec(memory_space=pl.ANY)],
        out_specs=pl.BlockSpec((ROUTE_TILE, d), lambda i, b_, c_: (i, 0)),
        scratch_shapes=[pltpu.VMEM((n_exp * ROUTE_WIN, ROUTE_TILE), BF16),
                        pltpu.VMEM((n_exp * ROUTE_WIN, d), BF16),
                        pltpu.VMEM((ROUTE_TILE, d), F32),
                        pltpu.SemaphoreType.DMA((n_exp,))],
    )
    return pl.pallas_call(
        functools.partial(_combine_kernel, n_exp=n_exp, tile0=tile0),
        grid_spec=grid_spec,
        out_shape=jax.ShapeDtypeStruct((n, d), F32),
        compiler_params=_params("arbitrary"),
        name="moe_combine",
    )(base, cnt, rank_t, x1, mod6, fg, ys)


def _round_up(a, m):
    return (a + m - 1) // m * m


def _moe_routed(pre_p, pre_s, wts):
    x1_p, h2_p, gates_p, mod6_p, tps_p = pre_p
    x1_s, h2_s, gates_s, mod6_s, _ = pre_s
    n_p, d = x1_p.shape
    n_s = x1_s.shape[0]
    n_exp = wts["bgu"].shape[0]
    assert n_p % ROUTE_TILE == 0 and n_s <= ROUTE_TILE and mod6_s.shape[2] == n_s
    pad_rows = lambda a: jnp.pad(a, ((0, ROUTE_TILE - n_s), (0, 0)))
    h_all = jnp.concatenate([h2_p, pad_rows(h2_s)], axis=0)
    g_all = jnp.concatenate([gates_p, pad_rows(gates_s)], axis=0)
    n_all = h_all.shape[0]
    n_tiles = n_all // ROUTE_TILE
    rank_t, gate_t, cnt_f = _route_meta(g_all, n_exp=n_exp)

    cnt = cnt_f[:, :n_exp, 0].astype(I32)
    seg = _round_up(cnt, ROUTE_ALIGN)
    before = lambda m: jnp.arange(m)[:, None] > jnp.arange(m)[None, :]
    seg_off = jnp.sum(jnp.where(before(n_tiles)[:, :, None], seg[None, :, :], 0), axis=1)
    region = _round_up(jnp.sum(seg, axis=0) + ROUTE_WIN, FFN_ROWS)
    region_start = jnp.sum(jnp.where(before(n_exp), region[None, :], 0), axis=1)
    region_end = region_start + region
    base = (region_start[None, :] + seg_off).astype(I32)
    n_rows = _round_up(TOP_K * n_all + (ROUTE_ALIGN - 1) * n_tiles * n_exp + (ROUTE_WIN + FFN_ROWS) * n_exp, FFN_ROWS)
    tile_start = jnp.arange(n_rows // FFN_ROWS, dtype=I32) * FFN_ROWS
    tile_expert = jnp.minimum(jnp.sum((tile_start[:, None] >= region_end[None, :]).astype(I32), axis=1), n_exp - 1)
    n_used = (region_end[-1:] // FFN_ROWS).astype(I32)

    xs, gs = _dispatch(base, cnt, h_all, rank_t, gate_t, n_rows)
    ys = _ffn_sorted(tile_expert, n_used, xs, gs, wts["wgu32"], wts["bgu"], wts["wd32"], wts["bd"])
    y_p = _combine(base, cnt, rank_t, x1_p, mod6_p, wts["fg"], ys, tile0=0, tiles_per_seq=tps_p)
    mod6_sp = jnp.pad(mod6_s, ((0, 0), (0, 0), (0, ROUTE_TILE - n_s), (0, 0)))
    y_s = _combine(base, cnt, rank_t, pad_rows(x1_s), mod6_sp, wts["fg"], ys, tile0=n_p // ROUTE_TILE,
                   tiles_per_seq=1)
    return y_p, y_s[:n_s]


def _pad_cols(a, width):
    return jnp.pad(a, ((0, 0), (0, width - a.shape[1])))


def _rope_tables(pos):
    half = HEAD_DIM // 2
    inv = ROPE_THETA ** (-jnp.arange(half, dtype=F32) / half)
    ang = pos.astype(F32)[:, None] * inv[None, :]
    cos, sin = jnp.cos(ang), jnp.sin(ang)
    cos_h = jnp.concatenate([cos, cos], axis=1)
    sin_h = jnp.concatenate([-sin, sin], axis=1)
    ones = jnp.ones_like(cos_h)
    zeros = jnp.zeros_like(cos_h)
    tabs = jnp.stack([jnp.concatenate([cos_h, cos_h], 1), jnp.concatenate([sin_h, sin_h], 1),
                      jnp.concatenate([cos_h, ones], 1), jnp.concatenate([sin_h, zeros], 1)])
    return tabs


def _layer(x, mod_rows, pos, shift_prev, wkv_prev, wts, attend, *, tm, chunk, seqs_per_step):
    b, t, d = x.shape
    n = b * t
    rw, att_w, n_rw, rwkv_in = wts["rw"], wts["att_w"], wts["n_rw"], wts["rwkv_in"]
    per_token_mod = mod_rows.shape[0] != b or tm > t
    if per_token_mod:
        mod6 = jnp.repeat(mod_rows, t, axis=0).reshape(n // tm, tm, 6, d).transpose(2, 0, 1, 3)
        tiles_per_seq = 1
    else:
        mod6 = mod_rows.reshape(b, 1, 6, d).transpose(2, 0, 1, 3)
        tiles_per_seq = t // tm
    cs = _rope_tables(pos)
    n_idx_scale = wts["idx_scale"]
    cs = cs.at[2, :, HEAD_DIM:].set(n_idx_scale)
    if cs.shape[1] < tm:
        cs = jnp.tile(cs, (1, tm // cs.shape[1], 1))
    x2d = x.reshape(n, d)
    p, qt, kt32, k_hm, vt32, vt_ck, qit, kw, kwt = _inproj(x2d, mod6, wts["g1"], wts["w_cat"], cs, tm=tm,
                                                           tiles_per_seq=tiles_per_seq, n_rw=n_rw, att_w=att_w)
    t_pad = -(-t // chunk) * chunk
    p3 = p.reshape(b, t, n_rw)
    if t_pad != t:
        p3 = jnp.pad(p3, ((0, 0), (0, t_pad - t), (0, 0)))
    z0 = jnp.swapaxes(wkv_prev, -1, -2)
    yr, z_fin = _rwkv(p3, shift_prev.reshape(b, 1, n_rw), z0, wts["mu"], wts["vec"], wts["w2"], wts["a2"], wts["g2"],
                      wts["seg"], L=chunk, nb=seqs_per_step, t_valid=t if t_pad != t else chunk, rw=rw)
    yr = yr[:, :t].reshape(n, rw)
    wkv_new = jnp.swapaxes(z_fin, -1, -2)
    n_heads = att_w // HEAD_DIM
    tok = lambda a_t: jnp.swapaxes(a_t, 1, 2).reshape(b, t, a_t.shape[1])
    ya = attend(qt, qit, kw, kwt, kt32, k_hm, vt32, vt_ck).reshape(n, att_w)
    x1, h2, gates = _outproj(x2d, yr, ya, mod6, wts["w_out"], wts["g2n"], wts["router_w"], wts["router_b"], tm=tm,
                             tiles_per_seq=tiles_per_seq, rw=rw)
    pre_moe = (x1, h2, gates, mod6, 1 if per_token_mod else t // ROUTE_TILE)
    p_last = p3[:, t - 1, :]
    new_shift = jnp.concatenate([p_last[:, :3 * rw], p_last[:, 3 * rw:3 * rw + wts["lora"][0]],
                                 p_last[:, 3 * rw + 128:3 * rw + 128 + wts["lora"][1]],
                                 p_last[:, 3 * rw + 256:3 * rw + 256 + wts["lora"][2]]], axis=1)
    state = (tok(kt32).reshape(b, t, n_heads, HEAD_DIM), tok(vt32).reshape(b, t, n_heads, HEAD_DIM),
             tok(kwt[:, :HEAD_DIM]), wkv_new, new_shift)
    return pre_moe, state


def kernel(x_prompt, x_sample, c_prompt, c_sample, cache_k, cache_v, cache_idx_k, state_wkv, state_shift, page_table, norm1_g, norm2_g, w_mod, b_mod, w_in, mu_shift, w0, w2, a0, a2, g2, k_k, k_a, r_k, lnx_g, lnx_b, w_out, router_w, router_b, w_gate_up, b_gate_up, w_down, b_down, final_g):
    depth = w_in.shape[0]
    assert depth == 1, "single-layer trunk"
    l = 0
    bp, tp, d = x_prompt.shape
    bs, ts, _ = x_sample.shape
    rw = w0.shape[1]
    att_w = w_out.shape[1] - rw
    lora = (w2.shape[1], a2.shape[1], g2.shape[1])
    rwkv_in = 3 * rw + sum(lora)
    n_idx = (w_in.shape[2] - rwkv_in - 3 * att_w - HEAD_DIM) // (HEAD_DIM + 1)
    n_exp = router_w.shape[2]
    n_pages = page_table.shape[1]
    past = n_pages * PAGE_SIZE
    n_rw = 3 * rw + 512
    assert lora[0] <= 128 and lora[1] <= 128 and lora[2] <= 256 and n_idx * HEAD_DIM == att_w

    wi = w_in[l]
    o = 3 * rw
    rw_cols = [wi[:, :o], _pad_cols(wi[:, o:o + lora[0]], 128), _pad_cols(wi[:, o + lora[0]:o + lora[0] + lora[1]], 128),
               _pad_cols(wi[:, o + lora[0] + lora[1]:rwkv_in], 256)]
    a0_ = rwkv_in
    att_cols = [wi[:, a0_:a0_ + 4 * att_w], _pad_cols(wi[:, a0_ + 4 * att_w:], LANES)]
    w_cat = jnp.concatenate(rw_cols + att_cols, axis=1).astype(BF16)
    mu = mu_shift[l]
    mu_pad = jnp.concatenate([mu[:o], jnp.pad(mu[o:o + lora[0]], (0, 128 - lora[0])),
                              jnp.pad(mu[o + lora[0]:o + lora[0] + lora[1]], (0, 128 - lora[1])),
                              jnp.pad(mu[o + lora[0] + lora[1]:], (0, 256 - lora[2]))]).reshape(1, n_rw)
    vec = jnp.stack([w0[l], a0[l], k_k[l], k_a[l], lnx_g[l], lnx_b[l], r_k[l].reshape(-1),
                     jnp.zeros((rw,), F32)])
    head_of = np.arange(rw) // HEAD_DIM
    seg = jnp.asarray((head_of[:, None] == head_of[None, :]).astype(np.float32)).astype(BF16)
    wts = dict(
        rw=rw, att_w=att_w, n_rw=n_rw, rwkv_in=rwkv_in, lora=lora,
        idx_scale=float((n_idx * HEAD_DIM) ** -0.5),
        g1=norm1_g[l].reshape(1, d), w_cat=w_cat, mu=mu_pad, vec=vec,
        w2=jnp.pad(w2[l], ((0, 128 - lora[0]), (0, 0))), a2=jnp.pad(a2[l], ((0, 128 - lora[1]), (0, 0))),
        g2=jnp.pad(g2[l], ((0, 256 - lora[2]), (0, 0))), seg=seg,
        w_out=w_out[l].astype(BF16), g2n=norm2_g[l].reshape(1, d),
        router_w=_pad_cols(router_w[l], LANES),
        router_b=jnp.concatenate([router_b[l], jnp.full((LANES - n_exp,), NEG_BIG, F32)]).reshape(1, LANES),
        wgu32=w_gate_up[l], bgu=b_gate_up[l].reshape(n_exp, 1, -1),
        wd32=w_down[l], bd=b_down[l].reshape(n_exp, 1, -1), fg=final_g.reshape(1, d),
    )

    c_all = jnp.concatenate([c_prompt, c_sample], axis=0)
    rows_pad = -(-c_all.shape[0] // SUBLANES) * SUBLANES
    mod_all = _modulation(jnp.pad(c_all, ((0, rows_pad - c_all.shape[0]), (0, 0))), w_mod[l], b_mod[l])
    mod_p, mod_s = mod_all[:bp], mod_all[bp:bp + bs]

    k_top_p = min(TOPK_MAX, tp // 4)

    tm_p = min(512, tp)

    def attend_prompt(qt, qit, kw, kwt, kt32, k_hm, vt32, vt_ck):
        wit = kwt[:, HEAD_DIM:HEAD_DIM + SUBLANES, :]
        kib = kw[:, :HEAD_DIM].astype(BF16).reshape(bp, tp, HEAD_DIM)
        out_t = _dsa_prompt(qt, qit, wit, k_hm, vt_ck, kib, tq=min(256, tp), tk=tm_p, k_top=k_top_p)
        return jnp.swapaxes(out_t, 1, 2)

    pre_p, st_p = _layer(x_prompt, mod_p, jnp.arange(tp), jnp.zeros((bp, n_rw), F32),
                       jnp.zeros((bp, rw // HEAD_DIM, HEAD_DIM, HEAD_DIM), F32), wts, attend_prompt,
                       tm=tm_p, chunk=64, seqs_per_step=4 if bp % 4 == 0 else (2 if bp % 2 == 0 else 1))

    k_top_s = min(TOPK_MAX, (past + ts) // 4)
    n_heads = att_w // HEAD_DIM
    assert ts <= SUBLANES
    pool_t = lambda c: jnp.transpose(c, (0, 1, 3, 4, 2)).reshape(c.shape[0], c.shape[1], att_w, PAGE_SIZE)
    ck_t, cv_t = pool_t(cache_k), pool_t(cache_v)
    cik_t = jnp.transpose(cache_idx_k, (0, 1, 3, 2))
    g_idx = 32 if n_pages % 32 == 0 else (16 if n_pages % 16 == 0 else 1)
    g_att = 16 if n_pages % 16 == 0 else (8 if n_pages % 8 == 0 else 1)

    def head_query_rows(a, nh):
        a = a.reshape(bs, ts, nh, HEAD_DIM).transpose(0, 2, 1, 3)
        a = jnp.pad(a, ((0, 0), (0, 0), (0, SUBLANES - ts), (0, 0)))
        return a.reshape(bs, nh * SUBLANES, HEAD_DIM)

    def new_rows_t(a):
        a = a.reshape(bs, ts, a.shape[1]).transpose(0, 2, 1)
        return jnp.pad(a, ((0, 0), (0, 0), (0, PAGE_SIZE - ts)))

    def attend_sample(qt, qit, kw, kwt, kt32, k_hm, vt32, vt_ck):
        tok2 = lambda a_t: jnp.swapaxes(a_t, 1, 2).reshape(bs * ts, a_t.shape[1])
        q, qi, k32, v32 = tok2(qt), tok2(qit), tok2(kt32), tok2(vt32)
        qim = head_query_rows(qi, n_idx)
        wcol = kw[:, HEAD_DIM:HEAD_DIM + n_idx].reshape(bs, ts, n_idx).transpose(0, 2, 1)
        wcol = jnp.pad(wcol, ((0, 0), (0, 0), (0, SUBLANES - ts))).reshape(bs, n_idx * SUBLANES, 1)
        past_scores = _idx_sample(page_table, qim, wcol, cik_t, t_new=ts, g_pages=g_idx, layer=l)
        new_scores = _idx_new(qim, wcol, new_rows_t(kw[:, :HEAD_DIM]), t_new=ts)
        bias = _select_sample(past_scores, new_scores, k_top=k_top_s)
        qh = head_query_rows(q, n_heads)
        eye = jnp.eye(n_heads, dtype=BF16)
        qbd = (qh.reshape(bs, n_heads, SUBLANES, 1, HEAD_DIM) * eye[None, :, None, :, None]).reshape(
            bs, n_heads * SUBLANES, att_w)
        out = _attn_sample(page_table, qbd, bias, ck_t, cv_t, new_rows_t(k32), new_rows_t(v32), g_pages=g_att,
                           layer=l)
        return out[:, :ts]

    shift_s = state_shift[l]
    o1, o2 = o + lora[0], o + lora[0] + lora[1]
    shift_pad = jnp.concatenate([shift_s[:, :o], _pad_cols(shift_s[:, o:o1], 128), _pad_cols(shift_s[:, o1:o2], 128),
                                 _pad_cols(shift_s[:, o2:], 256)], axis=1)
    pre_s, st_s = _layer(x_sample, mod_s, past + jnp.arange(ts), shift_pad, state_wkv[l], wts, attend_sample,
                       tm=bs * ts, chunk=SUBLANES, seqs_per_step=4 if bs % 4 == 0 else 1)

    y_p, y_s = _moe_routed(pre_p, pre_s, wts)
    y_p, y_s = y_p.reshape(bp, tp, d), y_s.reshape(bs, ts, d)
    ex = lambda a: a[None]
    return (y_p, y_s,ex(st_p[0]), ex(st_p[1]), ex(st_p[2]), ex(st_p[3]), ex(st_p[4]),
            ex(st_s[0]), ex(st_s[1]), ex(st_s[2]), ex(st_s[3]), ex(st_s[4]))
```

```python
import functools

import numpy as np
import jax
import jax.numpy as jnp
from jax import lax
from jax.experimental import pallas as pl
from jax.experimental.pallas import tpu as pltpu

F32 = jnp.float32
BF16 = jnp.bfloat16
I32 = jnp.int32
HI = lax.Precision.HIGHEST

HEAD_DIM = 64
PAGE_SIZE = 128
TOPK_MAX = 256
ROPE_THETA = 10000.0
TOP_K = 4
SWIGLU_LIMIT = 7.0
SWIGLU_ALPHA = 1.702
NORM_EPS = 1e-5
LNX_EPS = 64e-5

LANES = 128
SUBLANES = 8
VMEM_LIMIT = 56 * 1024 * 1024

NEG_BIG = -1e30
INT_MIN = -(2 ** 31)
KEY_NEG_INF = INT_MIN + 0x7FFFFF


def _dot(a, b, prec=None):
    return lax.dot_general(a, b, (((1,), (0,)), ((), ())), precision=prec, preferred_element_type=F32)


def _dot_nt(a, b, prec=None):
    return lax.dot_general(a, b, (((1,), (1,)), ((), ())), precision=prec, preferred_element_type=F32)


def _dot_tn(a, b, prec=None):
    return lax.dot_general(a, b, (((0,), (0,)), ((), ())), precision=prec, preferred_element_type=F32)


_NN = ((1,), (0,))
_NT = ((1,), (1,))
_TN = ((0,), (0,))


def _split(a, terms=2):
    out = []
    for i in range(terms):
        t = a.astype(BF16)
        out.append(t)
        if i + 1 < terms:
            a = a - t.astype(F32)
    return out


def _mm(a_terms, b_terms, dims=_NN, order=1):
    acc = None
    for i, a in enumerate(a_terms):
        for j, b in enumerate(b_terms):
            if i + j <= order:
                d = lax.dot_general(a, b, (dims, ((), ())), preferred_element_type=F32)
                acc = d if acc is None else acc + d
    return acc


def _params(*sem):
    return pltpu.CompilerParams(dimension_semantics=sem, vmem_limit_bytes=VMEM_LIMIT)


def _sigmoid(x):
    return jax.nn.sigmoid(x)


def _rms(x, g):
    return x * lax.rsqrt(jnp.mean(x * x, axis=-1, keepdims=True) + NORM_EPS) * g


def _mod_kernel(c_ref, w_ref, b_ref, o_ref):
    c = c_ref[...]
    o_ref[...] = _dot(c * _sigmoid(c), w_ref[...], HI) + b_ref[...]


def _modulation(c_pad, w_mod, b_mod):
    rows, d = c_pad.shape
    n = w_mod.shape[1]
    return pl.pallas_call(
        _mod_kernel,
        grid=(n // d,),
        in_specs=[pl.BlockSpec((rows, d), lambda j: (0, 0)),
                  pl.BlockSpec((d, d), lambda j: (0, j)),
                  pl.BlockSpec((1, d), lambda j: (0, j))],
        out_specs=pl.BlockSpec((rows, d), lambda j: (0, j)),
        out_shape=jax.ShapeDtypeStruct((rows, n), F32),
        compiler_params=_params("arbitrary"),
        name="modulation",
    )(c_pad, w_mod, b_mod.reshape(1, n))


def _rope_block(x, cos, sin_signed):
    lane = lax.broadcasted_iota(I32, x.shape, 1)
    partner = jnp.where((lane & 32) == 0, pltpu.roll(x, LANES - 32, axis=1), pltpu.roll(x, 32, axis=1))
    return x * cos + partner * sin_signed


def _inproj_kernel(x_ref, mod_ref, g_ref, w_ref, cs_ref, p_ref, qt_ref, kt_ref, khm_ref, vt_ref, vtb_ref, qit_ref,
                   kw_ref, kwt_ref, *, n_rw, att_w):
    x = x_ref[...]
    h = _rms(x, g_ref[...]) * (1 + mod_ref[1, 0]) + mod_ref[0, 0]
    hb = h.astype(BF16)
    cb = 512
    for j in range(n_rw // cb):
        p_ref[:, j * cb:(j + 1) * cb] = _dot(hb, w_ref[:, j * cb:(j + 1) * cb])
    cos, sin = cs_ref[0], cs_ref[1]
    base = n_rw
    ng = att_w // LANES
    grp = lambda g: slice(g * LANES, (g + 1) * LANES)

    def roped(off):
        t = _dot(hb, w_ref[:, off:off + att_w])
        return [_rope_block(t[:, grp(g)], cos, sin) for g in range(ng)]

    for g, blk in enumerate(roped(base)):
        qt_ref[0, grp(g), :] = jnp.transpose(blk * (HEAD_DIM ** -0.5)).astype(BF16)
    for g, blk in enumerate(roped(base + att_w)):
        kt_ref[0, grp(g), :] = jnp.transpose(blk)
        khm_ref[0, 2 * g] = blk[:, :HEAD_DIM].astype(BF16)
        khm_ref[0, 2 * g + 1] = blk[:, HEAD_DIM:].astype(BF16)
    v = _dot(hb, w_ref[:, base + 2 * att_w:base + 3 * att_w])
    for g in range(ng):
        vg_t = jnp.transpose(v[:, grp(g)])
        vt_ref[0, grp(g), :] = vg_t
        vtb_ref[0, 0, grp(g), :] = vg_t.astype(BF16)
    for g, blk in enumerate(roped(base + 3 * att_w)):
        qit_ref[0, grp(g), :] = jnp.transpose(blk).astype(BF16)
    kw = _rope_block(_dot(hb, w_ref[:, base + 4 * att_w:base + 4 * att_w + LANES]), cs_ref[2], cs_ref[3])
    kw_ref[...] = kw
    kwt_ref[0] = jnp.transpose(kw)


def _inproj(x2d, mod6, g1, w_cat, cs, *, tm, tiles_per_seq, n_rw, att_w):
    n, d = x2d.shape
    rmod = mod6.shape[2]
    n_cs_tiles = cs.shape[1] // tm
    t_seq = tiles_per_seq * tm
    n_seq = n // t_seq
    nh = att_w // HEAD_DIM
    grid = (n // tm,)
    row = lambda i: (i, 0)
    feat = lambda i: (i // tiles_per_seq, 0, i % tiles_per_seq)
    outs = [
        (jax.ShapeDtypeStruct((n, n_rw), F32), pl.BlockSpec((tm, n_rw), row)),
        (jax.ShapeDtypeStruct((n_seq, att_w, t_seq), BF16), pl.BlockSpec((1, att_w, tm), feat)),
        (jax.ShapeDtypeStruct((n_seq, att_w, t_seq), F32), pl.BlockSpec((1, att_w, tm), feat)),
        (jax.ShapeDtypeStruct((n_seq, nh, t_seq, HEAD_DIM), BF16),
         pl.BlockSpec((1, nh, tm, HEAD_DIM), lambda i: (i // tiles_per_seq, 0, i % tiles_per_seq, 0))),
        (jax.ShapeDtypeStruct((n_seq, att_w, t_seq), F32), pl.BlockSpec((1, att_w, tm), feat)),
        (jax.ShapeDtypeStruct((n_seq, tiles_per_seq, att_w, tm), BF16),
         pl.BlockSpec((1, 1, att_w, tm), lambda i: (i // tiles_per_seq, i % tiles_per_seq, 0, 0))),
        (jax.ShapeDtypeStruct((n_seq, att_w, t_seq), BF16), pl.BlockSpec((1, att_w, tm), feat)),
        (jax.ShapeDtypeStruct((n, LANES), F32), pl.BlockSpec((tm, LANES), row)),
        (jax.ShapeDtypeStruct((n_seq, LANES, t_seq), F32), pl.BlockSpec((1, LANES, tm), feat)),
    ]
    return pl.pallas_call(
        functools.partial(_inproj_kernel, n_rw=n_rw, att_w=att_w),
        grid=grid,
        in_specs=[pl.BlockSpec((tm, d), row),
                  pl.BlockSpec((6, 1, rmod, d), lambda i: (0, i // tiles_per_seq, 0, 0)),
                  pl.BlockSpec((1, d), lambda i: (0, 0)),
                  pl.BlockSpec(w_cat.shape, lambda i: (0, 0)),
                  pl.BlockSpec((4, tm, LANES), lambda i: (0, i % n_cs_tiles, 0))],
        out_specs=[o[1] for o in outs],
        out_shape=[o[0] for o in outs],
        compiler_params=_params("arbitrary"),
        name="inproj",
    )(x2d, mod6, g1, w_cat, cs)


def _softplus(x):
    return jnp.maximum(x, 0.0) + jnp.log(1.0 + jnp.exp(-jnp.abs(x)))


def _rwkv_kernel(p_ref, sh0_ref, z0_ref, mu_ref, vec_ref, w2_ref, a2_ref, g2_ref, seg_ref, y_ref, zout_ref,
                 carry_ref, z_ref, ops_ref, yh_ref, *, L, nb, t_valid, rw, n_heads):
    c = pl.program_id(1)
    R = nb * L

    @pl.when(c == 0)
    def _():
        carry_ref[...] = sh0_ref[...]
        z_ref[...] = z0_ref[...].reshape(z_ref.shape)

    p = p_ref[...].reshape(R, p_ref.shape[2])
    row = lax.broadcasted_iota(I32, (nb, L, 1), 1).reshape(R, 1)
    seq = lax.broadcasted_iota(I32, (nb, L, 1), 0).reshape(R, 1)

    def per_seq_rows(vals):
        out = vals[0]
        for s in range(1, nb):
            out = jnp.where(seq >= s, vals[s], out)
        return out

    prev = jnp.where(row == 0, per_seq_rows([carry_ref[s] for s in range(nb)]), pltpu.roll(p, 1, axis=0))
    for s in range(nb):
        carry_ref[s] = p[(s + 1) * L - 1:(s + 1) * L, :]
    z = p + (prev - p) * mu_ref[...]
    r, k, v = z[:, 0:rw], z[:, rw:2 * rw], z[:, 2 * rw:3 * rw]
    o = 3 * rw
    zw, za, zg = z[:, o:o + 128], z[:, o + 128:o + 256], z[:, o + 256:o + 512]
    w0, a0, k_k, k_a = vec_ref[0:1], vec_ref[1:2], vec_ref[2:3], vec_ref[3:4]
    lnx_g, lnx_b, r_k = vec_ref[4:5], vec_ref[5:6], vec_ref[6:7]
    seg = [seg_ref[...]]

    def seg_sum(t):
        return _mm(_split(t, 3), seg, order=2)

    def lora(t, w_ref):
        return _mm(_split(t), _split(w_ref[...]))

    w_log = -_softplus(-(w0 + lora(jnp.tanh(zw), w2_ref))) - 0.5
    lw = -jnp.exp(w_log)
    a = _sigmoid(a0 + lora(za, a2_ref))
    g = lora(_sigmoid(zg), g2_ref)
    kk = k * k_k
    kkn = kk / jnp.maximum(jnp.sqrt(seg_sum(kk * kk)), 1e-12)
    k2 = k * (1 + (a - 1) * k_a)
    an, bn = -kkn, kkn * a
    if t_valid < L:
        valid = row < t_valid
        lw = jnp.where(valid, lw, 0.0)
        an, bn = jnp.where(valid, an, 0.0), jnp.where(valid, bn, 0.0)
        k2s, vs = jnp.where(valid, k2, 0.0), jnp.where(valid, v, 0.0)
    else:
        k2s, vs = k2, v

    ri = lax.broadcasted_iota(I32, (L, L), 0)
    ci = lax.broadcasted_iota(I32, (L, L), 1)
    incl, strict = ri >= ci, ri > ci
    lw_terms = _split(lw, 3)
    rr = lax.broadcasted_iota(I32, (R, R), 0)
    rc = lax.broadcasted_iota(I32, (R, R), 1)
    same_seq_incl = (rr >= rc) & (rc >= 0)
    for s in range(1, nb):
        same_seq_incl = same_seq_incl & ((rr < s * L) | (rc >= s * L))
    cum = _mm([same_seq_incl.astype(BF16)], lw_terms, order=2)
    c_last = per_seq_rows([cum[(s + 1) * L - 1:(s + 1) * L, :] for s in range(nb)])
    e_c, e_cp, e_n, e_d = jnp.exp(cum), jnp.exp(cum - lw), jnp.exp(-cum), jnp.exp(c_last - cum)
    ops_ref[0] = an * e_cp
    ops_ref[1] = r * e_c
    ops_ref[2] = bn * e_n
    ops_ref[3] = k2s * e_n
    ops_ref[4] = bn * e_d
    ops_ref[5] = k2s * e_d
    ops_ref[6] = vs
    wl_cols = [jnp.exp(_mm(lw_terms, [jnp.broadcast_to(seq == s, (R, LANES)).astype(BF16)], _TN, order=2))
               for s in range(nb)]
    eye = (ri == ci).astype(F32)
    n_sq = max(int(np.ceil(np.log2(L))) - 1, 0)

    hs = range(nb * n_heads)
    sls = [slice((h % n_heads) * HEAD_DIM, (h % n_heads + 1) * HEAD_DIM) for h in hs]
    rws = [slice((h // n_heads) * L, (h // n_heads + 1) * L) for h in hs]
    at, rt, bt, kt, bh, kh, vh = [[_split(ops_ref[i, rws[h], sls[h]]) for h in hs] for i in range(7)]
    m_ab = [jnp.where(strict, _mm(at[h][:1], bt[h][:1], _NT), 0.0) for h in hs]
    t_inv = [eye + m_ab[h] for h in hs]
    pw = m_ab
    for _ in range(n_sq):
        pw_b = [pw[h].astype(BF16) for h in hs]
        pw = [_mm([pw_b[h]], [pw_b[h]]) for h in hs]
        t_inv = [t_inv[h] + _mm([t_inv[h].astype(BF16)], [pw[h].astype(BF16)]) for h in hs]
    m_ak = [jnp.where(strict, _mm(at[h][:1], kt[h][:1], _NT), 0.0) for h in hs]
    m_rb = [jnp.where(incl, _mm(rt[h][:1], bt[h][:1], _NT), 0.0).astype(BF16) for h in hs]
    m_rk = [jnp.where(incl, _mm(rt[h][:1], kt[h][:1], _NT), 0.0).astype(BF16) for h in hs]
    zs = [z_ref[h] for h in hs]
    zs_t = [_split(zs[h]) for h in hs]
    rhs = [_mm(at[h], zs_t[h]) + _mm(_split(m_ak[h]), vh[h]) for h in hs]
    u_t = [_split(_mm(_split(t_inv[h]), _split(rhs[h]))) for h in hs]
    for h in hs:
        yh_ref[rws[h], sls[h]] = (_mm(rt[h][:1], zs_t[h][:1]) + _mm([m_rb[h]], u_t[h][:1])
                                  + _mm([m_rk[h]], vh[h][:1]))
    for h in hs:
        z_ref[h] = (wl_cols[h // n_heads][sls[h], 0:HEAD_DIM] * zs[h] + _mm(bh[h], u_t[h], _TN)
                    + _mm(kh[h], vh[h], _TN))

    y = yh_ref[...]
    inv_n = 1.0 / HEAD_DIM
    mean = seg_sum(y) * inv_n
    yc = y - mean
    var = seg_sum(yc * yc) * inv_n
    yn = yc * lax.rsqrt(var + LNX_EPS) * lnx_g + lnx_b
    bonus = seg_sum(r * k2 * r_k) * v
    y_ref[...] = ((yn + bonus) * g).astype(y_ref.dtype).reshape(y_ref.shape)

    @pl.when(c == pl.num_programs(1) - 1)
    def _():
        zout_ref[...] = z_ref[...].reshape(zout_ref.shape)


def _rwkv(p3, shift0, z0, mu, vec, w2, a2, g2, seg, *, L, nb, t_valid, rw):
    b, t, n_rw = p3.shape
    n_heads = rw // HEAD_DIM
    assert b % nb == 0
    const2 = lambda i, c: (0, 0)
    return pl.pallas_call(
        functools.partial(_rwkv_kernel, L=L, nb=nb, t_valid=t_valid, rw=rw, n_heads=n_heads),
        grid=(b // nb, t // L),
        in_specs=[pl.BlockSpec((nb, L, n_rw), lambda i, c: (i, c, 0)),
                  pl.BlockSpec((nb, 1, n_rw), lambda i, c: (i, 0, 0)),
                  pl.BlockSpec((nb, n_heads, HEAD_DIM, HEAD_DIM), lambda i, c: (i, 0, 0, 0)),
                  pl.BlockSpec((1, n_rw), const2),
                  pl.BlockSpec(vec.shape, const2),
                  pl.BlockSpec(w2.shape, const2),
                  pl.BlockSpec(a2.shape, const2),
                  pl.BlockSpec(g2.shape, const2),
                  pl.BlockSpec(seg.shape, const2)],
        out_specs=[pl.BlockSpec((nb, L, rw), lambda i, c: (i, c, 0)),
                   pl.BlockSpec((nb, n_heads, HEAD_DIM, HEAD_DIM), lambda i, c: (i, 0, 0, 0))],
        out_shape=[jax.ShapeDtypeStruct((b, t, rw), BF16),
                   jax.ShapeDtypeStruct((b, n_heads, HEAD_DIM, HEAD_DIM), F32)],
        scratch_shapes=[pltpu.VMEM((nb, 1, n_rw), F32),
                        pltpu.VMEM((nb * n_heads, HEAD_DIM, HEAD_DIM), F32),
                        pltpu.VMEM((7, nb * L, rw), F32),
                        pltpu.VMEM((nb * L, rw), F32)],
        compiler_params=_params("arbitrary", "arbitrary"),
        name="rwkv_scan",
    )(p3, shift0, z0, mu, vec, w2, a2, g2, seg)


def _key_to_float(key):
    bits = key ^ ((key >> 31) & 0x7FFFFFFF)
    return lax.bitcast_convert_type(bits, F32)


def _fold_lanes(m):
    acc = m[:, 0:LANES]
    for j in range(1, m.shape[1] // LANES):
        acc = acc + m[:, j * LANES:(j + 1) * LANES]
    return acc


def _select_bias(sc_ref, pb_ref, nc, n_q, tk, k_top, n_keys_total, key_axis=1):
    if key_axis == 1:
        q_shape, part_shape = (n_q, 1), (n_q, LANES)
        fold = lambda m: _fold_lanes(m.astype(F32))
        total = lambda acc: jnp.sum(acc, axis=1, keepdims=True)
    else:
        part_rows = min(tk, 8 * SUBLANES)
        q_shape, part_shape = (1, n_q), (part_rows, n_q)
        fold = lambda m: jnp.sum(m.astype(F32).reshape(tk // part_rows, part_rows, n_q), axis=0)
        total = lambda acc: jnp.sum(acc, axis=0, keepdims=True)

    def count(pred):
        if isinstance(nc, int):
            accs = [jnp.zeros(part_shape, F32) for _ in range(min(4, nc))]
            for c in range(nc):
                accs[c % len(accs)] = accs[c % len(accs)] + fold(pred(c, sc_ref[c]))
            return total(functools.reduce(lambda a, b_: a + b_, accs))

        def body(c, acc):
            return acc + fold(pred(c, sc_ref[c]))
        return total(lax.fori_loop(0, nc, body, jnp.zeros(part_shape, F32)))

    kf = float(k_top)
    cnt0 = count(lambda c, s: s >= 0.0)
    tau0 = jnp.where(cnt0 >= kf, 0, INT_MIN).astype(I32)

    def bit_body(i, tau):
        cand = tau + jnp.left_shift(jnp.int32(1), 30 - i)
        cand_f = _key_to_float(cand)
        cnt = count(lambda c, s: s >= cand_f)
        return jnp.where((cnt >= kf) | (cand <= KEY_NEG_INF), cand, tau)

    tau = lax.fori_loop(0, 31, bit_body, tau0)
    tau_f = _key_to_float(tau)
    need = kf - count(lambda c, s: s > tau_f)
    excess = count(lambda c, s: s == tau_f) - need

    def kidx(c):
        return c * tk + lax.broadcasted_iota(I32, (1, tk) if key_axis == 1 else (tk, 1), key_axis)

    n_bits = max(int(np.ceil(np.log2(n_keys_total))), 1)
    pb_ref[...] = jnp.full(q_shape, n_keys_total, I32)

    @pl.when(jnp.max(excess) > 0.0)
    def _():
        def tie_body(i, pb):
            cand = pb + jnp.left_shift(jnp.int32(1), n_bits - 1 - i)
            cnt = count(lambda c, s: (s == tau_f) & (kidx(c) < cand))
            return jnp.where(cnt < need, cand, pb)

        pb_ref[...] = lax.fori_loop(0, n_bits, tie_body, jnp.zeros(q_shape, I32))

    pb = pb_ref[...]

    def write(c, _):
        s = sc_ref[c]
        sel = ((s > tau_f) | ((s == tau_f) & (kidx(c) <= pb))) & (s > -jnp.inf)
        sc_ref[c] = jnp.where(sel, 0.0, NEG_BIG)
        return 0

    lax.fori_loop(0, nc, write, 0)


def _dsa_prompt_kernel(qt_ref, qit_ref, wit_ref, k_ref, vt_ref, ki_ref, o_ref, sc_ref, pb_ref, m_ref, l_ref, acc_ref,
                       *, tq, tk, k_top, n_heads, n_idx, t_total):
    qb = pl.program_id(1)
    q0 = qb * tq
    nc = (q0 + tq + tk - 1) // tk
    qpos = q0 + lax.broadcasted_iota(I32, (1, tq), 1)
    hd = lambda h: slice(h * HEAD_DIM, (h + 1) * HEAD_DIM)

    def scores(c, _):
        kic = ki_ref[0, pl.ds(pl.multiple_of(c * tk, tk), tk), :]
        acc = jnp.zeros((tk, tq), F32)
        for h in range(n_idx):
            acc = acc + jnp.maximum(_dot(kic, qit_ref[0, hd(h), :]), 0.0) * wit_ref[0, h:h + 1, :]
        spos = c * tk + lax.broadcasted_iota(I32, (tk, 1), 0)
        sc_ref[c] = jnp.where(spos <= qpos, acc, -jnp.inf)
        return 0

    lax.fori_loop(0, nc, scores, 0)
    _select_bias(sc_ref, pb_ref, nc, tq, tk, k_top, t_total, key_axis=0)

    m_ref[...] = jnp.full(m_ref.shape, NEG_BIG, F32)
    l_ref[...] = jnp.zeros(l_ref.shape, F32)
    acc_ref[...] = jnp.zeros(acc_ref.shape, F32)

    def attend(c, _):
        rows = pl.ds(pl.multiple_of(c * tk, tk), tk)
        bias = sc_ref[c]
        m_all, l_all, acc_all = m_ref[...], l_ref[...], acc_ref[...]
        hs = range(n_heads)
        s = [_dot(k_ref[0, h, rows, :], qt_ref[0, hd(h), :]) + bias for h in hs]
        m_new = [jnp.maximum(m_all[h:h + 1], jnp.max(s[h], axis=0, keepdims=True)) for h in hs]
        p = [jnp.exp((s[h] - m_new[h]).astype(BF16)) for h in hs]
        pv = [_dot(vt_ref[0, c, hd(h), :], p[h]) for h in hs]
        alpha = [jnp.exp(m_all[h:h + 1] - m_new[h]) for h in hs]
        l_new = [alpha[h] * l_all[h:h + 1] + jnp.sum(p[h].astype(F32), axis=0, keepdims=True) for h in hs]
        acc_ref[...] = jnp.concatenate([alpha[h] * acc_all[hd(h)] + pv[h] for h in hs], axis=0)
        m_ref[...] = jnp.concatenate(m_new, axis=0)
        l_ref[...] = jnp.concatenate(l_new, axis=0)
        return 0

    lax.fori_loop(0, nc, attend, 0)
    l_all = l_ref[...]
    for h in range(n_heads):
        o_ref[0, hd(h), :] = (acc_ref[hd(h), :] / l_all[h:h + 1]).astype(o_ref.dtype)


def _dsa_prompt(qt, qit, wit, k_hm, vt_ck, ki, *, tq, tk, k_top):
    b, att_w, t = qt.shape
    n_heads = att_w // HEAD_DIM
    n_idx = qit.shape[1] // HEAD_DIM
    assert n_heads == SUBLANES
    qblk = lambda i, j: (i, 0, j)
    return pl.pallas_call(
        functools.partial(_dsa_prompt_kernel, tq=tq, tk=tk, k_top=k_top, n_heads=n_heads, n_idx=n_idx,
                          t_total=t),
        grid=(b, t // tq),
        in_specs=[pl.BlockSpec((1, att_w, tq), qblk),
                  pl.BlockSpec((1, qit.shape[1], tq), qblk),
                  pl.BlockSpec((1, wit.shape[1], tq), qblk),
                  pl.BlockSpec((1,) + k_hm.shape[1:], lambda i, j: (i, 0, 0, 0)),
                  pl.BlockSpec((1,) + vt_ck.shape[1:], lambda i, j: (i, 0, 0, 0)),
                  pl.BlockSpec((1, t, HEAD_DIM), lambda i, j: (i, 0, 0))],
        out_specs=pl.BlockSpec((1, att_w, tq), qblk),
        out_shape=jax.ShapeDtypeStruct((b, att_w, t), BF16),
        scratch_shapes=[pltpu.VMEM((t // tk, tk, tq), F32),
                        pltpu.VMEM((1, tq), I32),
                        pltpu.VMEM((n_heads, tq), F32),
                        pltpu.VMEM((n_heads, tq), F32),
                        pltpu.VMEM((att_w, tq), F32)],
        compiler_params=_params("arbitrary", "arbitrary"),
        name="dsa_prompt",
    )(qt, qit, wit, k_hm, vt_ck, ki)


def _idx_page_scores(qim, wcol, ki_t, n_idx):
    d = jnp.maximum(_dot(qim, ki_t.astype(BF16)), 0.0) * wcol
    sc = d[0:SUBLANES]
    for h in range(1, n_idx):
        sc = sc + d[h * SUBLANES:(h + 1) * SUBLANES]
    return sc


def _idx_sample_kernel(pt_ref, qim_ref, wcol_ref, *refs, t_new, n_idx, g_pages):
    page_refs, o_ref = refs[:g_pages], refs[g_pages]
    qrow = lax.broadcasted_iota(I32, (SUBLANES, PAGE_SIZE), 0)
    for j in range(g_pages):
        sc = _idx_page_scores(qim_ref[0], wcol_ref[0], page_refs[j][0, 0], n_idx)
        o_ref[j] = jnp.where(qrow < t_new, sc, -jnp.inf)


def _idx_sample(page_table, qim, wcol, cache_idx_t, *, t_new, g_pages, layer):
    b, n_pages = page_table.shape
    n_idx = qim.shape[1] // SUBLANES
    per_seq = lambda i, g, pt: (i, 0, 0)
    page_specs = [pl.BlockSpec((1, 1, HEAD_DIM, PAGE_SIZE),
                               lambda i, g, pt, j=j: (layer, pt[i, g * g_pages + j], 0, 0))
                  for j in range(g_pages)]
    grid_spec = pltpu.PrefetchScalarGridSpec(
        num_scalar_prefetch=1,
        grid=(b, n_pages // g_pages),
        in_specs=[pl.BlockSpec((1,) + qim.shape[1:], per_seq),
                  pl.BlockSpec((1,) + wcol.shape[1:], per_seq)] + page_specs,
        out_specs=pl.BlockSpec((g_pages, SUBLANES, PAGE_SIZE), lambda i, g, pt: (g, i, 0)),
    )
    return pl.pallas_call(
        functools.partial(_idx_sample_kernel, t_new=t_new, n_idx=n_idx, g_pages=g_pages),
        grid_spec=grid_spec,
        out_shape=jax.ShapeDtypeStruct((n_pages, b * SUBLANES, PAGE_SIZE), F32),
        compiler_params=_params("arbitrary", "arbitrary"),
        name="idx_sample",
    )(page_table, qim, wcol, *([cache_idx_t] * g_pages))


def _idx_new_kernel(qim_ref, wcol_ref, new_ref, o_ref, *, t_new, n_idx):
    sc = _idx_page_scores(qim_ref[0], wcol_ref[0], new_ref[0], n_idx)
    qrow = lax.broadcasted_iota(I32, (SUBLANES, PAGE_SIZE), 0)
    lane = lax.broadcasted_iota(I32, (SUBLANES, PAGE_SIZE), 1)
    o_ref[...] = jnp.where((qrow < t_new) & (lane <= qrow), sc, -jnp.inf)


def _idx_new(qim, wcol, ki_new, *, t_new):
    b = qim.shape[0]
    n_idx = qim.shape[1] // SUBLANES
    per_seq = lambda i: (i, 0, 0)
    return pl.pallas_call(
        functools.partial(_idx_new_kernel, t_new=t_new, n_idx=n_idx),
        grid=(b,),
        in_specs=[pl.BlockSpec((1,) + qim.shape[1:], per_seq),
                  pl.BlockSpec((1,) + wcol.shape[1:], per_seq),
                  pl.BlockSpec((1, HEAD_DIM, PAGE_SIZE), per_seq)],
        out_specs=pl.BlockSpec((SUBLANES, PAGE_SIZE), lambda i: (i, 0)),
        out_shape=jax.ShapeDtypeStruct((b * SUBLANES, PAGE_SIZE), F32),
        compiler_params=_params("arbitrary"),
        name="idx_new",
    )(qim, wcol, ki_new)


def _select_sample_kernel(past_ref, new_ref, o_ref, pb_ref, *, k_top):
    n_pages, rows = past_ref.shape[0], past_ref.shape[1]
    o_ref[0:n_pages] = past_ref[...]
    o_ref[n_pages] = new_ref[...]
    _select_bias(o_ref, pb_ref, n_pages + 1, rows, PAGE_SIZE, k_top, (n_pages + 1) * PAGE_SIZE)


def _select_sample(past_scores, new_scores, *, k_top):
    n_pages, rows, _ = past_scores.shape
    out_shape = (n_pages + 1, rows, PAGE_SIZE)
    return pl.pallas_call(
        functools.partial(_select_sample_kernel, k_top=k_top),
        grid=(1,),
        in_specs=[pl.BlockSpec(past_scores.shape, lambda i: (0, 0, 0)),
                  pl.BlockSpec(new_scores.shape, lambda i: (0, 0))],
        out_specs=pl.BlockSpec(out_shape, lambda i: (0, 0, 0)),
        out_shape=jax.ShapeDtypeStruct(out_shape, F32),
        scratch_shapes=[pltpu.VMEM((rows, 1), I32)],
        compiler_params=_params("arbitrary"),
        name="select_sample",
    )(past_scores, new_scores)


def _attn_sample_kernel(pt_ref, qbd_ref, bias_ref, biasn_ref, knew_ref, vnew_ref, *refs, n_heads, g_pages):
    k_refs, v_refs = refs[:g_pages], refs[g_pages:2 * g_pages]
    o_ref, m_ref, l_ref, acc_ref = refs[2 * g_pages:]
    g = pl.program_id(1)

    @pl.when(g == 0)
    def _():
        m_ref[...] = jnp.full(m_ref.shape, NEG_BIG, F32)
        l_ref[...] = jnp.zeros(l_ref.shape, F32)
        acc_ref[...] = jnp.zeros(acc_ref.shape, F32)

    def step(kts, vts, biases):
        bias = jnp.concatenate([jnp.concatenate([b_] * n_heads, axis=0) for b_ in biases], axis=1)
        s = jnp.concatenate([_dot(qbd_ref[0], kt.astype(BF16)) for kt in kts], axis=1) + bias
        m = m_ref[...]
        m_new = jnp.maximum(m, jnp.max(s, axis=-1, keepdims=True))
        alpha = jnp.exp(m - m_new)
        pr = jnp.exp(s - m_new)
        prb = pr.astype(BF16)
        pv = None
        for j, vt in enumerate(vts):
            d = _dot_nt(prb[:, j * PAGE_SIZE:(j + 1) * PAGE_SIZE], vt.astype(BF16))
            pv = d if pv is None else pv + d
        l_ref[...] = alpha * l_ref[...] + jnp.sum(pr, axis=-1, keepdims=True)
        acc_ref[...] = alpha * acc_ref[...] + pv
        m_ref[...] = m_new

    step([r[0, 0] for r in k_refs], [r[0, 0] for r in v_refs], [bias_ref[j] for j in range(g_pages)])

    @pl.when(g == pl.num_programs(1) - 1)
    def _():
        step([knew_ref[0]], [vnew_ref[0]], [biasn_ref[0]])
        full = acc_ref[...] / l_ref[...]
        col_head = lax.broadcasted_iota(I32, (SUBLANES, full.shape[1]), 1) // HEAD_DIM
        out = jnp.zeros((SUBLANES, full.shape[1]), F32)
        for h in range(n_heads):
            out = out + jnp.where(col_head == h, full[h * SUBLANES:(h + 1) * SUBLANES], 0.0)
        o_ref[0] = out.astype(o_ref.dtype)


def _attn_sample(page_table, qbd, bias, cache_k_t, cache_v_t, k_new_t, v_new_t, *, g_pages, layer):
    b, n_pages = page_table.shape
    att_w = qbd.shape[2]
    n_heads = att_w // HEAD_DIM
    per_seq = lambda i, g, pt: (i, 0, 0)
    page_specs = [pl.BlockSpec((1, 1, att_w, PAGE_SIZE),
                               lambda i, g, pt, j=j: (layer, pt[i, g * g_pages + j], 0, 0))
                  for j in range(g_pages)]
    new_spec = pl.BlockSpec((1, att_w, PAGE_SIZE), per_seq)
    grid_spec = pltpu.PrefetchScalarGridSpec(
        num_scalar_prefetch=1,
        grid=(b, n_pages // g_pages),
        in_specs=[pl.BlockSpec((1,) + qbd.shape[1:], per_seq),
                  pl.BlockSpec((g_pages, SUBLANES, PAGE_SIZE), lambda i, g, pt: (g, i, 0)),
                  pl.BlockSpec((1, SUBLANES, PAGE_SIZE), lambda i, g, pt: (n_pages, i, 0)),
                  new_spec, new_spec] + page_specs + page_specs,
        out_specs=pl.BlockSpec((1, SUBLANES, att_w), per_seq),
        scratch_shapes=[pltpu.VMEM((n_heads * SUBLANES, 1), F32),
                        pltpu.VMEM((n_heads * SUBLANES, 1), F32),
                        pltpu.VMEM((n_heads * SUBLANES, att_w), F32)],
    )
    return pl.pallas_call(
        functools.partial(_attn_sample_kernel, n_heads=n_heads, g_pages=g_pages),
        grid_spec=grid_spec,
        out_shape=jax.ShapeDtypeStruct((b, SUBLANES, att_w), BF16),
        compiler_params=_params("arbitrary", "arbitrary"),
        name="attn_sample",
    )(page_table, qbd, bias, bias, k_new_t, v_new_t, *([cache_k_t] * g_pages), *([cache_v_t] * g_pages))


def _outproj_kernel(x_ref, yr_ref, ya_ref, mod_ref, wo_ref, g_ref, rw_ref, rb_ref, x1_ref, h2_ref, gates_ref, *, rw):
    mixed = _dot(yr_ref[...], wo_ref[0:rw, :]) + _dot(ya_ref[...], wo_ref[rw:, :])
    x1 = x_ref[...] + mod_ref[2, 0] * mixed
    x1_ref[...] = x1
    h2 = _rms(x1, g_ref[...]) * (1 + mod_ref[4, 0]) + mod_ref[3, 0]
    h2_ref[...] = h2.astype(BF16)
    logits = _mm(_split(h2), _split(rw_ref[...])) + rb_ref[...]
    lane = lax.broadcasted_iota(I32, logits.shape, 1).astype(F32)
    work = logits
    picked = jnp.zeros(logits.shape, jnp.bool_)
    top = None
    for i in range(TOP_K):
        mx = jnp.max(work, axis=-1, keepdims=True)
        first = jnp.min(jnp.where(work == mx, lane, float(LANES)), axis=-1, keepdims=True)
        hit = lane == first
        picked = picked | hit
        work = jnp.where(hit, -jnp.inf, work)
        if i == 0:
            top = mx
    e = jnp.where(picked, jnp.exp(logits - top), 0.0)
    gates_ref[...] = e / jnp.sum(e, axis=-1, keepdims=True)


def _outproj(x2d, yr, ya, mod6, w_out_b, g2n, router_w_pad, router_b_pad, *, tm, tiles_per_seq, rw):
    n, d = x2d.shape
    rmod = mod6.shape[2]
    row = lambda i: (i, 0)
    const = lambda i: (0, 0)
    outs = [jax.ShapeDtypeStruct((n, d), F32), jax.ShapeDtypeStruct((n, d), BF16),
            jax.ShapeDtypeStruct((n, LANES), F32)]
    return pl.pallas_call(
        functools.partial(_outproj_kernel, rw=rw),
        grid=(n // tm,),
        in_specs=[pl.BlockSpec((tm, d), row),
                  pl.BlockSpec((tm, yr.shape[1]), row),
                  pl.BlockSpec((tm, ya.shape[1]), row),
                  pl.BlockSpec((6, 1, rmod, d), lambda i: (0, i // tiles_per_seq, 0, 0)),
                  pl.BlockSpec(w_out_b.shape, const),
                  pl.BlockSpec((1, d), const),
                  pl.BlockSpec(router_w_pad.shape, const),
                  pl.BlockSpec((1, LANES), const)],
        out_specs=[pl.BlockSpec((tm, o.shape[1]), row) for o in outs],
        out_shape=outs,
        compiler_params=_params("arbitrary"),
        name="outproj_router",
    )(x2d, yr, ya, mod6, w_out_b, g2n, router_w_pad, router_b_pad)


ROUTE_TILE = 512
ROUTE_WIN = 128
ROUTE_ALIGN = 16
FFN_ROWS = 512


def _route_meta_kernel(g_ref, rank_ref, gate_ref, cnt_ref, *, n_exp):
    gt = jnp.transpose(g_ref[...])
    sel = gt > 0.0
    tt = gt.shape[1]
    earlier = (lax.broadcasted_iota(I32, (tt, tt), 0) < lax.broadcasted_iota(I32, (tt, tt), 1)).astype(BF16)
    rank = _dot(sel.astype(BF16), earlier)
    rank_ref[...] = jnp.where(sel, rank, -1.0)[:n_exp]
    gate_ref[...] = gt[:n_exp]
    cnt_ref[0] = jnp.broadcast_to(jnp.sum(sel.astype(F32), axis=-1, keepdims=True), (LANES, LANES))


def _route_meta(gates, *, n_exp):
    n = gates.shape[0]
    n_tiles = n // ROUTE_TILE
    return pl.pallas_call(
        functools.partial(_route_meta_kernel, n_exp=n_exp),
        grid=(n_tiles,),
        in_specs=[pl.BlockSpec((ROUTE_TILE, LANES), lambda i: (i, 0))],
        out_specs=[pl.BlockSpec((n_exp, ROUTE_TILE), lambda i: (0, i)),
                   pl.BlockSpec((n_exp, ROUTE_TILE), lambda i: (0, i)),
                   pl.BlockSpec((1, LANES, LANES), lambda i: (i, 0, 0))],
        out_shape=[jax.ShapeDtypeStruct((n_exp, n), F32), jax.ShapeDtypeStruct((n_exp, n), F32),
                   jax.ShapeDtypeStruct((n_tiles, LANES, LANES), F32)],
        compiler_params=_params("arbitrary"),
        name="route_meta",
    )(gates)


def _window_onehot(rank_row, w):
    rows = lax.broadcasted_iota(I32, (ROUTE_WIN, 1), 0).astype(F32) + float(ROUTE_WIN) * w
    return rank_row == rows


def _dispatch_kernel(base_ref, cnt_ref, h_ref, rank_ref, gate_ref, xs_in, gs_in, xs_out, gs_out, p_ref, sx_ref,
                     sg_ref, sem_x, sem_g, *, n_exp, group):
    del xs_in, gs_in
    i = pl.program_id(0)
    h = h_ref[...]
    win = ROUTE_WIN

    def copies(slot, row0):
        rows = pl.ds(pl.multiple_of(row0, ROUTE_ALIGN), win)
        return (pltpu.make_async_copy(sx_ref.at[slot], xs_out.at[rows], sem_x.at[slot]),
                pltpu.make_async_copy(sg_ref.at[slot], gs_out.at[rows], sem_g.at[slot]))

    def fill(slot, e, w):
        p = _window_onehot(rank_ref[pl.ds(e, 1), :], w)
        p_ref[slot * win:(slot + 1) * win, :] = p.astype(BF16)
        g = jnp.sum(jnp.where(p, gate_ref[pl.ds(e, 1), :], 0.0), axis=-1, keepdims=True)
        sg_ref[slot] = jnp.broadcast_to(g, (win, LANES))

    for e0 in range(0, n_exp, group):
        for j in range(group):
            fill(e0 + j, e0 + j, 0)
        x = _dot(p_ref[e0 * win:(e0 + group) * win, :], h)
        for j in range(group):
            e = e0 + j
            sx_ref[e] = x[j * win:(j + 1) * win].astype(BF16)
            for c in copies(e, base_ref[i, e]):
                c.start()
    for e in range(n_exp):
        for c in copies(e, base_ref[i, e]):
            c.wait()

    def per_expert(e, _):
        def extra(w, _):
            fill(0, e, w)
            sx_ref[0] = _dot(p_ref[0:win, :], h_ref[...]).astype(BF16)
            cs = copies(0, base_ref[i, e] + w * win)
            for c in cs:
                c.start()
            for c in cs:
                c.wait()
            return 0

        return lax.fori_loop(1, (cnt_ref[i, e] + win - 1) // win, extra, 0)

    lax.fori_loop(0, n_exp, per_expert, 0)


def _dispatch(base, cnt, h_all, rank_t, gate_t, n_rows):
    n, d = h_all.shape
    n_exp = rank_t.shape[0]
    n_tiles = n // ROUTE_TILE
    tile = lambda i, b_, c_: (0, i)
    grid_spec = pltpu.PrefetchScalarGridSpec(
        num_scalar_prefetch=2,
        grid=(n_tiles,),
        in_specs=[pl.BlockSpec((ROUTE_TILE, d), lambda i, b_, c_: (i, 0)),
                  pl.BlockSpec((n_exp, ROUTE_TILE), tile),
                  pl.BlockSpec((n_exp, ROUTE_TILE), tile),
                  pl.BlockSpec(memory_space=pl.ANY),
                  pl.BlockSpec(memory_space=pl.ANY)],
        out_specs=[pl.BlockSpec(memory_space=pl.ANY), pl.BlockSpec(memory_space=pl.ANY)],
        scratch_shapes=[pltpu.VMEM((n_exp * ROUTE_WIN, ROUTE_TILE), BF16),
                        pltpu.VMEM((n_exp, ROUTE_WIN, d), BF16),
                        pltpu.VMEM((n_exp, ROUTE_WIN, LANES), F32),
                        pltpu.SemaphoreType.DMA((n_exp,)),
                        pltpu.SemaphoreType.DMA((n_exp,))],
    )
    xs0 = jnp.zeros((n_rows, d), BF16)
    gs0 = jnp.zeros((n_rows, LANES), F32)
    return pl.pallas_call(
        functools.partial(_dispatch_kernel, n_exp=n_exp, group=8),
        grid_spec=grid_spec,
        out_shape=[jax.ShapeDtypeStruct(xs0.shape, BF16), jax.ShapeDtypeStruct(gs0.shape, F32)],
        input_output_aliases={5: 0, 6: 1},
        compiler_params=_params("arbitrary"),
        name="moe_dispatch",
    )(base, cnt, h_all, rank_t, gate_t, xs0, gs0)


def _ffn_sorted_kernel(te_ref, nu_ref, xs_ref, gs_ref, wgu_ref, bgu_ref, wd_ref, bd_ref, ys_ref, wgu_b, wd_b, *,
                       d_ff):
    r = pl.program_id(0)

    @pl.when(r < nu_ref[0])
    def _():
        @pl.when((r == 0) | (te_ref[r] != te_ref[jnp.maximum(r - 1, 0)]))
        def _():
            wgu_b[...] = wgu_ref[0].astype(BF16)
            wd_b[...] = wd_ref[0].astype(BF16)

        gu = _dot(xs_ref[...], wgu_b[...]) + bgu_ref[0]
        gate = jnp.minimum(gu[:, :d_ff], SWIGLU_LIMIT)
        up = jnp.clip(gu[:, d_ff:], -SWIGLU_LIMIT, SWIGLU_LIMIT)
        act = (up + 1) * gate * _sigmoid(SWIGLU_ALPHA * gate)
        y = _dot(act.astype(BF16), wd_b[...]) + bd_ref[0]
        ys_ref[...] = (gs_ref[:, 0:1] * y).astype(ys_ref.dtype)

    @pl.when(r >= nu_ref[0])
    def _():
        ys_ref[...] = jnp.zeros(ys_ref.shape, ys_ref.dtype)


def _ffn_sorted(tile_expert, n_used, xs, gs, wgu, bgu, wd, bd):
    n_rows, d = xs.shape
    n_exp, _, two_ff = wgu.shape
    row = lambda r, te, nu: (r, 0)
    exp3 = lambda r, te, nu: (te[r], 0, 0)
    grid_spec = pltpu.PrefetchScalarGridSpec(
        num_scalar_prefetch=2,
        grid=(n_rows // FFN_ROWS,),
        in_specs=[pl.BlockSpec((FFN_ROWS, d), row),
                  pl.BlockSpec((FFN_ROWS, LANES), row),
                  pl.BlockSpec((1, d, two_ff), exp3),
                  pl.BlockSpec((1, 1, two_ff), exp3),
                  pl.BlockSpec((1, two_ff // 2, d), exp3),
                  pl.BlockSpec((1, 1, d), exp3)],
        out_specs=pl.BlockSpec((FFN_ROWS, d), row),
        scratch_shapes=[pltpu.VMEM((d, two_ff), BF16), pltpu.VMEM((two_ff // 2, d), BF16)],
    )
    return pl.pallas_call(
        functools.partial(_ffn_sorted_kernel, d_ff=two_ff // 2),
        grid_spec=grid_spec,
        out_shape=jax.ShapeDtypeStruct((n_rows, d), BF16),
        compiler_params=_params("arbitrary"),
        name="moe_ffn_sorted",
    )(tile_expert, n_used, xs, gs, wgu, bgu, wd, bd)


def _combine_kernel(base_ref, cnt_ref, rank_ref, x1_ref, mod_ref, fg_ref, ys_hbm, o_ref, p_ref, yb_ref, acc_ref,
                    sem, *, n_exp, tile0):
    i = pl.program_id(0) + tile0
    win = ROUTE_WIN

    def copy(slot, row0):
        rows = pl.ds(pl.multiple_of(row0, ROUTE_ALIGN), win)
        return pltpu.make_async_copy(ys_hbm.at[rows], yb_ref.at[pl.ds(slot * win, win)], sem.at[slot])

    for e in range(n_exp):
        copy(e, base_ref[i, e]).start()
    for e in range(n_exp):
        p_ref[e * win:(e + 1) * win, :] = _window_onehot(rank_ref[e:e + 1, :], 0).astype(BF16)
    for e in range(n_exp):
        copy(e, base_ref[i, e]).wait()
    acc_ref[...] = _dot_tn(p_ref[...], yb_ref[...])

    def per_expert(e, _):
        def extra(w, _):
            c = copy(0, base_ref[i, e] + w * win)
            c.start()
            p = _window_onehot(rank_ref[pl.ds(e, 1), :], w).astype(BF16)
            c.wait()
            acc_ref[...] += _dot_tn(p, yb_ref[0:win, :])
            return 0

        return lax.fori_loop(1, (cnt_ref[i, e] + win - 1) // win, extra, 0)

    lax.fori_loop(0, n_exp, per_expert, 0)
    x2 = x1_ref[...] + mod_ref[5, 0] * acc_ref[...]
    o_ref[...] = _rms(x2, fg_ref[...])


def _combine(base, cnt, rank_t, x1, mod6, fg, ys, *, tile0, tiles_per_seq):
    n, d = x1.shape
    n_exp = rank_t.shape[0]
    rmod = mod6.shape[2]
    grid_spec = pltpu.PrefetchScalarGridSpec(
        num_scalar_prefetch=2,
        grid=(n // ROUTE_TILE,),
        in_specs=[pl.BlockSpec((n_exp, ROUTE_TILE), lambda i, b_, c_: (0, i + tile0)),
                  pl.BlockSpec((ROUTE_TILE, d), lambda i, b_, c_: (i, 0)),
                  pl.BlockSpec((6, 1, rmod, d), lambda i, b_, c_: (0, i // tiles_per_seq, 0, 0)),
                  pl.BlockSpec((1, d), lambda i, b_, c_: (0, 0)),
                  pl.BlockSpec(memory_space=pl.ANY)],
        out_specs=pl.BlockSpec((ROUTE_TILE, d), lambda i, b_, c_: (i, 0)),
        scratch_shapes=[pltpu.VMEM((n_exp * ROUTE_WIN, ROUTE_TILE), BF16),
                        pltpu.VMEM((n_exp * ROUTE_WIN, d), BF16),
                        pltpu.VMEM((ROUTE_TILE, d), F32),
                        pltpu.SemaphoreType.DMA((n_exp,))],
    )
    return pl.pallas_call(
        functools.partial(_combine_kernel, n_exp=n_exp, tile0=tile0),
        grid_spec=grid_spec,
        out_shape=jax.ShapeDtypeStruct((n, d), F32),
        compiler_params=_params("arbitrary"),
        name="moe_combine",
    )(base, cnt, rank_t, x1, mod6, fg, ys)


def _round_up(a, m):
    return (a + m - 1) // m * m


def _moe_routed(pre_p, pre_s, wts):
    x1_p, h2_p, gates_p, mod6_p, tps_p = pre_p
    x1_s, h2_s, gates_s, mod6_s, _ = pre_s
    n_p, d = x1_p.shape
    n_s = x1_s.shape[0]
    n_exp = wts["bgu"].shape[0]
    assert n_p % ROUTE_TILE == 0 and n_s <= ROUTE_TILE and mod6_s.shape[2] == n_s
    pad_rows = lambda a: jnp.pad(a, ((0, ROUTE_TILE - n_s), (0, 0)))
    h_all = jnp.concatenate([h2_p, pad_rows(h2_s)], axis=0)
    g_all = jnp.concatenate([gates_p, pad_rows(gates_s)], axis=0)
    n_all = h_all.shape[0]
    n_tiles = n_all // ROUTE_TILE
    rank_t, gate_t, cnt_f = _route_meta(g_all, n_exp=n_exp)

    cnt = cnt_f[:, :n_exp, 0].astype(I32)
    seg = _round_up(cnt, ROUTE_ALIGN)
    before = lambda m: jnp.arange(m)[:, None] > jnp.arange(m)[None, :]
    seg_off = jnp.sum(jnp.where(before(n_tiles)[:, :, None], seg[None, :, :], 0), axis=1)
    region = _round_up(jnp.sum(seg, axis=0) + ROUTE_WIN, FFN_ROWS)
    region_start = jnp.sum(jnp.where(before(n_exp), region[None, :], 0), axis=1)
    region_end = region_start + region
    base = (region_start[None, :] + seg_off).astype(I32)
    n_rows = _round_up(TOP_K * n_all + (ROUTE_ALIGN - 1) * n_tiles * n_exp + (ROUTE_WIN + FFN_ROWS) * n_exp, FFN_ROWS)
    tile_start = jnp.arange(n_rows // FFN_ROWS, dtype=I32) * FFN_ROWS
    tile_expert = jnp.minimum(jnp.sum((tile_start[:, None] >= region_end[None, :]).astype(I32), axis=1), n_exp - 1)
    n_used = (region_end[-1:] // FFN_ROWS).astype(I32)

    xs, gs = _dispatch(base, cnt, h_all, rank_t, gate_t, n_rows)
    ys = _ffn_sorted(tile_expert, n_used, xs, gs, wts["wgu32"], wts["bgu"], wts["wd32"], wts["bd"])
    y_p = _combine(base, cnt, rank_t, x1_p, mod6_p, wts["fg"], ys, tile0=0, tiles_per_seq=tps_p)
    mod6_sp = jnp.pad(mod6_s, ((0, 0), (0, 0), (0, ROUTE_TILE - n_s), (0, 0)))
    y_s = _combine(base, cnt, rank_t, pad_rows(x1_s), mod6_sp, wts["fg"], ys, tile0=n_p // ROUTE_TILE,
                   tiles_per_seq=1)
    return y_p, y_s[:n_s]


def _pad_cols(a, width):
    return jnp.pad(a, ((0, 0), (0, width - a.shape[1])))


def _rope_tables(pos):
    half = HEAD_DIM // 2
    inv = ROPE_THETA ** (-jnp.arange(half, dtype=F32) / half)
    ang = pos.astype(F32)[:, None] * inv[None, :]
    cos, sin = jnp.cos(ang), jnp.sin(ang)
    cos_h = jnp.concatenate([cos, cos], axis=1)
    sin_h = jnp.concatenate([-sin, sin], axis=1)
    ones = jnp.ones_like(cos_h)
    zeros = jnp.zeros_like(cos_h)
    tabs = jnp.stack([jnp.concatenate([cos_h, cos_h], 1), jnp.concatenate([sin_h, sin_h], 1),
                      jnp.concatenate([cos_h, ones], 1), jnp.concatenate([sin_h, zeros], 1)])
    return tabs


def _layer(x, mod_rows, pos, shift_prev, wkv_prev, wts, attend, *, tm, chunk, seqs_per_step):
    b, t, d = x.shape
    n = b * t
    rw, att_w, n_rw, rwkv_in = wts["rw"], wts["att_w"], wts["n_rw"], wts["rwkv_in"]
    per_token_mod = mod_rows.shape[0] != b or tm > t
    if per_token_mod:
        mod6 = jnp.repeat(mod_rows, t, axis=0).reshape(n // tm, tm, 6, d).transpose(2, 0, 1, 3)
        tiles_per_seq = 1
    else:
        mod6 = mod_rows.reshape(b, 1, 6, d).transpose(2, 0, 1, 3)
        tiles_per_seq = t // tm
    cs = _rope_tables(pos)
    n_idx_scale = wts["idx_scale"]
    cs = cs.at[2, :, HEAD_DIM:].set(n_idx_scale)
    if cs.shape[1] < tm:
        cs = jnp.tile(cs, (1, tm // cs.shape[1], 1))
    x2d = x.reshape(n, d)
    p, qt, kt32, k_hm, vt32, vt_ck, qit, kw, kwt = _inproj(x2d, mod6, wts["g1"], wts["w_cat"], cs, tm=tm,
                                                           tiles_per_seq=tiles_per_seq, n_rw=n_rw, att_w=att_w)
    t_pad = -(-t // chunk) * chunk
    p3 = p.reshape(b, t, n_rw)
    if t_pad != t:
        p3 = jnp.pad(p3, ((0, 0), (0, t_pad - t), (0, 0)))
    z0 = jnp.swapaxes(wkv_prev, -1, -2)
    yr, z_fin = _rwkv(p3, shift_prev.reshape(b, 1, n_rw), z0, wts["mu"], wts["vec"], wts["w2"], wts["a2"], wts["g2"],
                      wts["seg"], L=chunk, nb=seqs_per_step, t_valid=t if t_pad != t else chunk, rw=rw)
    yr = yr[:, :t].reshape(n, rw)
    wkv_new = jnp.swapaxes(z_fin, -1, -2)
    n_heads = att_w // HEAD_DIM
    tok = lambda a_t: jnp.swapaxes(a_t, 1, 2).reshape(b, t, a_t.shape[1])
    ya = attend(qt, qit, kw, kwt, kt32, k_hm, vt32, vt_ck).reshape(n, att_w)
    x1, h2, gates = _outproj(x2d, yr, ya, mod6, wts["w_out"], wts["g2n"], wts["router_w"], wts["router_b"], tm=tm,
                             tiles_per_seq=tiles_per_seq, rw=rw)
    pre_moe = (x1, h2, gates, mod6, 1 if per_token_mod else t // ROUTE_TILE)
    p_last = p3[:, t - 1, :]
    new_shift = jnp.concatenate([p_last[:, :3 * rw], p_last[:, 3 * rw:3 * rw + wts["lora"][0]],
                                 p_last[:, 3 * rw + 128:3 * rw + 128 + wts["lora"][1]],
                                 p_last[:, 3 * rw + 256:3 * rw + 256 + wts["lora"][2]]], axis=1)
    state = (tok(kt32).reshape(b, t, n_heads, HEAD_DIM), tok(vt32).reshape(b, t, n_heads, HEAD_DIM),
             tok(kwt[:, :HEAD_DIM]), wkv_new, new_shift)
    return pre_moe, state


def kernel(x_prompt, x_sample, c_prompt, c_sample, cache_k, cache_v, cache_idx_k, state_wkv, state_shift, page_table, norm1_g, norm2_g, w_mod, b_mod, w_in, mu_shift, w0, w2, a0, a2, g2, k_k, k_a, r_k, lnx_g, lnx_b, w_out, router_w, router_b, w_gate_up, b_gate_up, w_down, b_down, final_g):
    depth = w_in.shape[0]
    assert depth == 1, "single-layer trunk"
    l = 0
    bp, tp, d = x_prompt.shape
    bs, ts, _ = x_sample.shape
    rw = w0.shape[1]
    att_w = w_out.shape[1] - rw
    lora = (w2.shape[1], a2.shape[1], g2.shape[1])
    rwkv_in = 3 * rw + sum(lora)
    n_idx = (w_in.shape[2] - rwkv_in - 3 * att_w - HEAD_DIM) // (HEAD_DIM + 1)
    n_exp = router_w.shape[2]
    n_pages = page_table.shape[1]
    past = n_pages * PAGE_SIZE
    n_rw = 3 * rw + 512
    assert lora[0] <= 128 and lora[1] <= 128 and lora[2] <= 256 and n_idx * HEAD_DIM == att_w

    wi = w_in[l]
    o = 3 * rw
    rw_cols = [wi[:, :o], _pad_cols(wi[:, o:o + lora[0]], 128), _pad_cols(wi[:, o + lora[0]:o + lora[0] + lora[1]], 128),
               _pad_cols(wi[:, o + lora[0] + lora[1]:rwkv_in], 256)]
    a0_ = rwkv_in
    att_cols = [wi[:, a0_:a0_ + 4 * att_w], _pad_cols(wi[:, a0_ + 4 * att_w:], LANES)]
    w_cat = jnp.concatenate(rw_cols + att_cols, axis=1).astype(BF16)
    mu = mu_shift[l]
    mu_pad = jnp.concatenate([mu[:o], jnp.pad(mu[o:o + lora[0]], (0, 128 - lora[0])),
                              jnp.pad(mu[o + lora[0]:o + lora[0] + lora[1]], (0, 128 - lora[1])),
                              jnp.pad(mu[o + lora[0] + lora[1]:], (0, 256 - lora[2]))]).reshape(1, n_rw)
    vec = jnp.stack([w0[l], a0[l], k_k[l], k_a[l], lnx_g[l], lnx_b[l], r_k[l].reshape(-1),
                     jnp.zeros((rw,), F32)])
    head_of = np.arange(rw) // HEAD_DIM
    seg = jnp.asarray((head_of[:, None] == head_of[None, :]).astype(np.float32)).astype(BF16)
    wts = dict(
        rw=rw, att_w=att_w, n_rw=n_rw, rwkv_in=rwkv_in, lora=lora,
        idx_scale=float((n_idx * HEAD_DIM) ** -0.5),
        g1=norm1_g[l].reshape(1, d), w_cat=w_cat, mu=mu_pad, vec=vec,
        w2=jnp.pad(w2[l], ((0, 128 - lora[0]), (0, 0))), a2=jnp.pad(a2[l], ((0, 128 - lora[1]), (0, 0))),
        g2=jnp.pad(g2[l], ((0, 256 - lora[2]), (0, 0))), seg=seg,
        w_out=w_out[l].astype(BF16), g2n=norm2_g[l].reshape(1, d),
        router_w=_pad_cols(router_w[l], LANES),
        router_b=jnp.concatenate([router_b[l], jnp.full((LANES - n_exp,), NEG_BIG, F32)]).reshape(1, LANES),
        wgu32=w_gate_up[l], bgu=b_gate_up[l].reshape(n_exp, 1, -1),
        wd32=w_down[l], bd=b_down[l].reshape(n_exp, 1, -1), fg=final_g.reshape(1, d),
    )

    c_all = jnp.concatenate([c_prompt, c_sample], axis=0)
    rows_pad = -(-c_all.shape[0] // SUBLANES) * SUBLANES
    mod_all = _modulation(jnp.pad(c_all, ((0, rows_pad - c_all.shape[0]), (0, 0))), w_mod[l], b_mod[l])
    mod_p, mod_s = mod_all[:bp], mod_all[bp:bp + bs]

    k_top_p = min(TOPK_MAX, tp // 4)

    tm_p = min(512, tp)

    def attend_prompt(qt, qit, kw, kwt, kt32, k_hm, vt32, vt_ck):
        wit = kwt[:, HEAD_DIM:HEAD_DIM + SUBLANES, :]
        kib = kw[:, :HEAD_DIM].astype(BF16).reshape(bp, tp, HEAD_DIM)
        out_t = _dsa_prompt(qt, qit, wit, k_hm, vt_ck, kib, tq=min(256, tp), tk=tm_p, k_top=k_top_p)
        return jnp.swapaxes(out_t, 1, 2)

    pre_p, st_p = _layer(x_prompt, mod_p, jnp.arange(tp), jnp.zeros((bp, n_rw), F32),
                       jnp.zeros((bp, rw // HEAD_DIM, HEAD_DIM, HEAD_DIM), F32), wts, attend_prompt,
                       tm=tm_p, chunk=64, seqs_per_step=4 if bp % 4 == 0 else (2 if bp % 2 == 0 else 1))

    k_top_s = min(TOPK_MAX, (past + ts) // 4)
    n_heads = att_w // HEAD_DIM
    assert ts <= SUBLANES
    pool_t = lambda c: jnp.transpose(c, (0, 1, 3, 4, 2)).reshape(c.shape[0], c.shape[1], att_w, PAGE_SIZE)
    ck_t, cv_t = pool_t(cache_k), pool_t(cache_v)
    cik_t = jnp.transpose(cache_idx_k, (0, 1, 3, 2))
    g_idx = 32 if n_pages % 32 == 0 else (16 if n_pages % 16 == 0 else 1)
    g_att = 16 if n_pages % 16 == 0 else (8 if n_pages % 8 == 0 else 1)

    def head_query_rows(a, nh):
        a = a.reshape(bs, ts, nh, HEAD_DIM).transpose(0, 2, 1, 3)
        a = jnp.pad(a, ((0, 0), (0, 0), (0, SUBLANES - ts), (0, 0)))
        return a.reshape(bs, nh * SUBLANES, HEAD_DIM)

    def new_rows_t(a):
        a = a.reshape(bs, ts, a.shape[1]).transpose(0, 2, 1)
        return jnp.pad(a, ((0, 0), (0, 0), (0, PAGE_SIZE - ts)))

    def attend_sample(qt, qit, kw, kwt, kt32, k_hm, vt32, vt_ck):
        tok2 = lambda a_t: jnp.swapaxes(a_t, 1, 2).reshape(bs * ts, a_t.shape[1])
        q, qi, k32, v32 = tok2(qt), tok2(qit), tok2(kt32), tok2(vt32)
        qim = head_query_rows(qi, n_idx)
        wcol = kw[:, HEAD_DIM:HEAD_DIM + n_idx].reshape(bs, ts, n_idx).transpose(0, 2, 1)
        wcol = jnp.pad(wcol, ((0, 0), (0, 0), (0, SUBLANES - ts))).reshape(bs, n_idx * SUBLANES, 1)
        past_scores = _idx_sample(page_table, qim, wcol, cik_t, t_new=ts, g_pages=g_idx, layer=l)
        new_scores = _idx_new(qim, wcol, new_rows_t(kw[:, :HEAD_DIM]), t_new=ts)
        bias = _select_sample(past_scores, new_scores, k_top=k_top_s)
        qh = head_query_rows(q, n_heads)
        eye = jnp.eye(n_heads, dtype=BF16)
        qbd = (qh.reshape(bs, n_heads, SUBLANES, 1, HEAD_DIM) * eye[None, :, None, :, None]).reshape(
            bs, n_heads * SUBLANES, att_w)
        out = _attn_sample(page_table, qbd, bias, ck_t, cv_t, new_rows_t(k32), new_rows_t(v32), g_pages=g_att,
                           layer=l)
        return out[:, :ts]

    shift_s = state_shift[l]
    o1, o2 = o + lora[0], o + lora[0] + lora[1]
    shift_pad = jnp.concatenate([shift_s[:, :o], _pad_cols(shift_s[:, o:o1], 128), _pad_cols(shift_s[:, o1:o2], 128),
                                 _pad_cols(shift_s[:, o2:], 256)], axis=1)
    pre_s, st_s = _layer(x_sample, mod_s, past + jnp.arange(ts), shift_pad, state_wkv[l], wts, attend_sample,
                       tm=bs * ts, chunk=SUBLANES, seqs_per_step=4 if bs % 4 == 0 else 1)

    y_p, y_s = _moe_routed(pre_p, pre_s, wts)
    y_p, y_s = y_p.reshape(bp, tp, d), y_s.reshape(bs, ts, d)
    ex = lambda a: a[None]
    return (y_p, y_s,ex(st_p[0]), ex(st_p[1]), ex(st_p[2]), ex(st_p[3]), ex(st_p[4]),
            ex(st_s[0]), ex(st_s[1]), ex(st_s[2]), ex(st_s[3]), ex(st_s[4]))
```

```python
import functools

import numpy as np
import jax
import jax.numpy as jnp
from jax import lax
from jax.experimental import pallas as pl
from jax.experimental.pallas import tpu as pltpu

F32 = jnp.float32
BF16 = jnp.bfloat16
I32 = jnp.int32
HI = lax.Precision.HIGHEST

HEAD_DIM = 64
PAGE_SIZE = 128
TOPK_MAX = 256
ROPE_THETA = 10000.0
TOP_K = 4
SWIGLU_LIMIT = 7.0
SWIGLU_ALPHA = 1.702
NORM_EPS = 1e-5
LNX_EPS = 64e-5

LANES = 128
SUBLANES = 8
VMEM_LIMIT = 56 * 1024 * 1024

NEG_BIG = -1e30
INT_MIN = -(2 ** 31)
KEY_NEG_INF = INT_MIN + 0x7FFFFF


def _dot(a, b, prec=None):
    return lax.dot_general(a, b, (((1,), (0,)), ((), ())), precision=prec, preferred_element_type=F32)


def _dot_nt(a, b, prec=None):
    return lax.dot_general(a, b, (((1,), (1,)), ((), ())), precision=prec, preferred_element_type=F32)


def _dot_tn(a, b, prec=None):
    return lax.dot_general(a, b, (((0,), (0,)), ((), ())), precision=prec, preferred_element_type=F32)


_NN = ((1,), (0,))
_NT = ((1,), (1,))
_TN = ((0,), (0,))


def _split(a, terms=2):
    out = []
    for i in range(terms):
        t = a.astype(BF16)
        out.append(t)
        if i + 1 < terms:
            a = a - t.astype(F32)
    return out


def _mm(a_terms, b_terms, dims=_NN, order=1):
    acc = None
    for i, a in enumerate(a_terms):
        for j, b in enumerate(b_terms):
            if i + j <= order:
                d = lax.dot_general(a, b, (dims, ((), ())), preferred_element_type=F32)
                acc = d if acc is None else acc + d
    return acc


def _params(*sem):
    return pltpu.CompilerParams(dimension_semantics=sem, vmem_limit_bytes=VMEM_LIMIT)


def _sigmoid(x):
    return jax.nn.sigmoid(x)


def _rms(x, g):
    return x * lax.rsqrt(jnp.mean(x * x, axis=-1, keepdims=True) + NORM_EPS) * g


def _mod_kernel(c_ref, w_ref, b_ref, o_ref):
    c = c_ref[...]
    o_ref[...] = _dot(c * _sigmoid(c), w_ref[...], HI) + b_ref[...]


def _modulation(c_pad, w_mod, b_mod):
    rows, d = c_pad.shape
    n = w_mod.shape[1]
    return pl.pallas_call(
        _mod_kernel,
        grid=(n // d,),
        in_specs=[pl.BlockSpec((rows, d), lambda j: (0, 0)),
                  pl.BlockSpec((d, d), lambda j: (0, j)),
                  pl.BlockSpec((1, d), lambda j: (0, j))],
        out_specs=pl.BlockSpec((rows, d), lambda j: (0, j)),
        out_shape=jax.ShapeDtypeStruct((rows, n), F32),
        compiler_params=_params("arbitrary"),
        name="modulation",
    )(c_pad, w_mod, b_mod.reshape(1, n))


def _rope_block(x, cos, sin_signed):
    lane = lax.broadcasted_iota(I32, x.shape, 1)
    partner = jnp.where((lane & 32) == 0, pltpu.roll(x, LANES - 32, axis=1), pltpu.roll(x, 32, axis=1))
    return x * cos + partner * sin_signed


def _inproj_kernel(x_ref, mod_ref, g_ref, w_ref, cs_ref, p_ref, qt_ref, kt_ref, khm_ref, vt_ref, vtb_ref, qit_ref,
                   kw_ref, kwt_ref, *, n_rw, att_w):
    x = x_ref[...]
    h = _rms(x, g_ref[...]) * (1 + mod_ref[1, 0]) + mod_ref[0, 0]
    hb = h.astype(BF16)
    cb = 512
    for j in range(n_rw // cb):
        p_ref[:, j * cb:(j + 1) * cb] = _dot(hb, w_ref[:, j * cb:(j + 1) * cb])
    cos, sin = cs_ref[0], cs_ref[1]
    base = n_rw
    ng = att_w // LANES
    grp = lambda g: slice(g * LANES, (g + 1) * LANES)

    def roped(off):
        t = _dot(hb, w_ref[:, off:off + att_w])
        return [_rope_block(t[:, grp(g)], cos, sin) for g in range(ng)]

    for g, blk in enumerate(roped(base)):
        qt_ref[0, grp(g), :] = jnp.transpose(blk * (HEAD_DIM ** -0.5)).astype(BF16)
    for g, blk in enumerate(roped(base + att_w)):
        kt_ref[0, grp(g), :] = jnp.transpose(blk)
        khm_ref[0, 2 * g] = blk[:, :HEAD_DIM].astype(BF16)
        khm_ref[0, 2 * g + 1] = blk[:, HEAD_DIM:].astype(BF16)
    v = _dot(hb, w_ref[:, base + 2 * att_w:base + 3 * att_w])
    for g in range(ng):
        vg_t = jnp.transpose(v[:, grp(g)])
        vt_ref[0, grp(g), :] = vg_t
        vtb_ref[0, 0, grp(g), :] = vg_t.astype(BF16)
    for g, blk in enumerate(roped(base + 3 * att_w)):
        qit_ref[0, grp(g), :] = jnp.transpose(blk).astype(BF16)
    kw = _rope_block(_dot(hb, w_ref[:, base + 4 * att_w:base + 4 * att_w + LANES]), cs_ref[2], cs_ref[3])
    kw_ref[...] = kw
    kwt_ref[0] = jnp.transpose(kw)


def _inproj(x2d, mod6, g1, w_cat, cs, *, tm, tiles_per_seq, n_rw, att_w):
    n, d = x2d.shape
    rmod = mod6.shape[2]
    n_cs_tiles = cs.shape[1] // tm
    t_seq = tiles_per_seq * tm
    n_seq = n // t_seq
    nh = att_w // HEAD_DIM
    grid = (n // tm,)
    row = lambda i: (i, 0)
    feat = lambda i: (i // tiles_per_seq, 0, i % tiles_per_seq)
    outs = [
        (jax.ShapeDtypeStruct((n, n_rw), F32), pl.BlockSpec((tm, n_rw), row)),
        (jax.ShapeDtypeStruct((n_seq, att_w, t_seq), BF16), pl.BlockSpec((1, att_w, tm), feat)),
        (jax.ShapeDtypeStruct((n_seq, att_w, t_seq), F32), pl.BlockSpec((1, att_w, tm), feat)),
        (jax.ShapeDtypeStruct((n_seq, nh, t_seq, HEAD_DIM), BF16),
         pl.BlockSpec((1, nh, tm, HEAD_DIM), lambda i: (i // tiles_per_seq, 0, i % tiles_per_seq, 0))),
        (jax.ShapeDtypeStruct((n_seq, att_w, t_seq), F32), pl.BlockSpec((1, att_w, tm), feat)),
        (jax.ShapeDtypeStruct((n_seq, tiles_per_seq, att_w, tm), BF16),
         pl.BlockSpec((1, 1, att_w, tm), lambda i: (i // tiles_per_seq, i % tiles_per_seq, 0, 0))),
        (jax.ShapeDtypeStruct((n_seq, att_w, t_seq), BF16), pl.BlockSpec((1, att_w, tm), feat)),
        (jax.ShapeDtypeStruct((n, LANES), F32), pl.BlockSpec((tm, LANES), row)),
        (jax.ShapeDtypeStruct((n_seq, LANES, t_seq), F32), pl.BlockSpec((1, LANES, tm), feat)),
    ]
    return pl.pallas_call(
        functools.partial(_inproj_kernel, n_rw=n_rw, att_w=att_w),
        grid=grid,
        in_specs=[pl.BlockSpec((tm, d), row),
                  pl.BlockSpec((6, 1, rmod, d), lambda i: (0, i // tiles_per_seq, 0, 0)),
                  pl.BlockSpec((1, d), lambda i: (0, 0)),
                  pl.BlockSpec(w_cat.shape, lambda i: (0, 0)),
                  pl.BlockSpec((4, tm, LANES), lambda i: (0, i % n_cs_tiles, 0))],
        out_specs=[o[1] for o in outs],
        out_shape=[o[0] for o in outs],
        compiler_params=_params("arbitrary"),
        name="inproj",
    )(x2d, mod6, g1, w_cat, cs)


def _softplus(x):
    return jnp.maximum(x, 0.0) + jnp.log(1.0 + jnp.exp(-jnp.abs(x)))


def _rwkv_kernel(p_ref, sh0_ref, z0_ref, mu_ref, vec_ref, w2_ref, a2_ref, g2_ref, seg_ref, y_ref, zout_ref,
                 carry_ref, z_ref, ops_ref, yh_ref, *, L, nb, t_valid, rw, n_heads):
    c = pl.program_id(1)
    R = nb * L

    @pl.when(c == 0)
    def _():
        carry_ref[...] = sh0_ref[...]
        z_ref[...] = z0_ref[...].reshape(z_ref.shape)

    p = p_ref[...].reshape(R, p_ref.shape[2])
    row = lax.broadcasted_iota(I32, (nb, L, 1), 1).reshape(R, 1)
    seq = lax.broadcasted_iota(I32, (nb, L, 1), 0).reshape(R, 1)

    def per_seq_rows(vals):
        out = vals[0]
        for s in range(1, nb):
            out = jnp.where(seq >= s, vals[s], out)
        return out

    prev = jnp.where(row == 0, per_seq_rows([carry_ref[s] for s in range(nb)]), pltpu.roll(p, 1, axis=0))
    for s in range(nb):
        carry_ref[s] = p[(s + 1) * L - 1:(s + 1) * L, :]
    z = p + (prev - p) * mu_ref[...]
    r, k, v = z[:, 0:rw], z[:, rw:2 * rw], z[:, 2 * rw:3 * rw]
    o = 3 * rw
    zw, za, zg = z[:, o:o + 128], z[:, o + 128:o + 256], z[:, o + 256:o + 512]
    w0, a0, k_k, k_a = vec_ref[0:1], vec_ref[1:2], vec_ref[2:3], vec_ref[3:4]
    lnx_g, lnx_b, r_k = vec_ref[4:5], vec_ref[5:6], vec_ref[6:7]
    seg = [seg_ref[...]]

    def seg_sum(t):
        return _mm(_split(t), seg)

    def lora(t, w_ref):
        return _mm(_split(t), _split(w_ref[...]))

    w_log = -_softplus(-(w0 + lora(jnp.tanh(zw), w2_ref))) - 0.5
    lw = -jnp.exp(w_log)
    a = _sigmoid(a0 + lora(za, a2_ref))
    g = lora(_sigmoid(zg), g2_ref)
    kk = k * k_k
    kkn = kk / jnp.maximum(jnp.sqrt(seg_sum(kk * kk)), 1e-12)
    k2 = k * (1 + (a - 1) * k_a)
    an, bn = -kkn, kkn * a
    if t_valid < L:
        valid = row < t_valid
        lw = jnp.where(valid, lw, 0.0)
        an, bn = jnp.where(valid, an, 0.0), jnp.where(valid, bn, 0.0)
        k2s, vs = jnp.where(valid, k2, 0.0), jnp.where(valid, v, 0.0)
    else:
        k2s, vs = k2, v

    ri = lax.broadcasted_iota(I32, (L, L), 0)
    ci = lax.broadcasted_iota(I32, (L, L), 1)
    incl, strict = ri >= ci, ri > ci
    lw_terms = _split(lw, 3)
    rr = lax.broadcasted_iota(I32, (R, R), 0)
    rc = lax.broadcasted_iota(I32, (R, R), 1)
    same_seq_incl = (rr >= rc) & (rc >= 0)
    for s in range(1, nb):
        same_seq_incl = same_seq_incl & ((rr < s * L) | (rc >= s * L))
    cum = _mm([same_seq_incl.astype(BF16)], lw_terms, order=2)
    c_last = per_seq_rows([cum[(s + 1) * L - 1:(s + 1) * L, :] for s in range(nb)])
    e_c, e_cp, e_n, e_d = jnp.exp(cum), jnp.exp(cum - lw), jnp.exp(-cum), jnp.exp(c_last - cum)
    ops_ref[0] = an * e_cp
    ops_ref[1] = r * e_c
    ops_ref[2] = bn * e_n
    ops_ref[3] = k2s * e_n
    ops_ref[4] = bn * e_d
    ops_ref[5] = k2s * e_d
    ops_ref[6] = vs
    wl_cols = [jnp.exp(_mm(lw_terms, [jnp.broadcast_to(seq == s, (R, LANES)).astype(BF16)], _TN, order=2))
               for s in range(nb)]
    eye = (ri == ci).astype(F32)
    n_sq = max(int(np.ceil(np.log2(L))) - 1, 0)

    hs = range(nb * n_heads)
    sls = [slice((h % n_heads) * HEAD_DIM, (h % n_heads + 1) * HEAD_DIM) for h in hs]
    rws = [slice((h // n_heads) * L, (h // n_heads + 1) * L) for h in hs]
    at, rt, bt, kt, bh, kh, vh = [[_split(ops_ref[i, rws[h], sls[h]]) for h in hs] for i in range(7)]
    m_ab = [jnp.where(strict, _mm(at[h][:1], bt[h][:1], _NT), 0.0) for h in hs]
    t_inv = [eye + m_ab[h] for h in hs]
    pw = m_ab
    for _ in range(n_sq):
        pw_b = [pw[h].astype(BF16) for h in hs]
        pw = [_mm([pw_b[h]], [pw_b[h]]) for h in hs]
        t_inv = [t_inv[h] + _mm([t_inv[h].astype(BF16)], [pw[h].astype(BF16)]) for h in hs]
    m_ak = [jnp.where(strict, _mm(at[h][:1], kt[h][:1], _NT), 0.0) for h in hs]
    m_rb = [jnp.where(incl, _mm(rt[h][:1], bt[h][:1], _NT), 0.0).astype(BF16) for h in hs]
    m_rk = [jnp.where(incl, _mm(rt[h][:1], kt[h][:1], _NT), 0.0).astype(BF16) for h in hs]
    zs = [z_ref[h] for h in hs]
    zs_t = [_split(zs[h]) for h in hs]
    rhs = [_mm(at[h], zs_t[h]) + _mm(_split(m_ak[h]), vh[h]) for h in hs]
    u_t = [_split(_mm(_split(t_inv[h]), _split(rhs[h]))) for h in hs]
    for h in hs:
        yh_ref[rws[h], sls[h]] = (_mm(rt[h][:1], zs_t[h][:1]) + _mm([m_rb[h]], u_t[h][:1])
                                  + _mm([m_rk[h]], vh[h][:1]))
    for h in hs:
        z_ref[h] = (wl_cols[h // n_heads][sls[h], 0:HEAD_DIM] * zs[h] + _mm(bh[h], u_t[h], _TN)
                    + _mm(kh[h], vh[h], _TN))

    y = yh_ref[...]
    inv_n = 1.0 / HEAD_DIM
    mean = seg_sum(y) * inv_n
    yc = y - mean
    var = seg_sum(yc * yc) * inv_n
    yn = yc * lax.rsqrt(var + LNX_EPS) * lnx_g + lnx_b
    bonus = seg_sum(r * k2 * r_k) * v
    y_ref[...] = ((yn + bonus) * g).astype(y_ref.dtype).reshape(y_ref.shape)

    @pl.when(c == pl.num_programs(1) - 1)
    def _():
        zout_ref[...] = z_ref[...].reshape(zout_ref.shape)


def _rwkv(p3, shift0, z0, mu, vec, w2, a2, g2, seg, *, L, nb, t_valid, rw):
    b, t, n_rw = p3.shape
    n_heads = rw // HEAD_DIM
    assert b % nb == 0
    const2 = lambda i, c: (0, 0)
    return pl.pallas_call(
        functools.partial(_rwkv_kernel, L=L, nb=nb, t_valid=t_valid, rw=rw, n_heads=n_heads),
        grid=(b // nb, t // L),
        in_specs=[pl.BlockSpec((nb, L, n_rw), lambda i, c: (i, c, 0)),
                  pl.BlockSpec((nb, 1, n_rw), lambda i, c: (i, 0, 0)),
                  pl.BlockSpec((nb, n_heads, HEAD_DIM, HEAD_DIM), lambda i, c: (i, 0, 0, 0)),
                  pl.BlockSpec((1, n_rw), const2),
                  pl.BlockSpec(vec.shape, const2),
                  pl.BlockSpec(w2.shape, const2),
                  pl.BlockSpec(a2.shape, const2),
                  pl.BlockSpec(g2.shape, const2),
                  pl.BlockSpec(seg.shape, const2)],
        out_specs=[pl.BlockSpec((nb, L, rw), lambda i, c: (i, c, 0)),
                   pl.BlockSpec((nb, n_heads, HEAD_DIM, HEAD_DIM), lambda i, c: (i, 0, 0, 0))],
        out_shape=[jax.ShapeDtypeStruct((b, t, rw), BF16),
                   jax.ShapeDtypeStruct((b, n_heads, HEAD_DIM, HEAD_DIM), F32)],
        scratch_shapes=[pltpu.VMEM((nb, 1, n_rw), F32),
                        pltpu.VMEM((nb * n_heads, HEAD_DIM, HEAD_DIM), F32),
                        pltpu.VMEM((7, nb * L, rw), F32),
                        pltpu.VMEM((nb * L, rw), F32)],
        compiler_params=_params("arbitrary", "arbitrary"),
        name="rwkv_scan",
    )(p3, shift0, z0, mu, vec, w2, a2, g2, seg)


def _key_to_float(key):
    bits = key ^ ((key >> 31) & 0x7FFFFFFF)
    return lax.bitcast_convert_type(bits, F32)


def _fold_lanes(m):
    acc = m[:, 0:LANES]
    for j in range(1, m.shape[1] // LANES):
        acc = acc + m[:, j * LANES:(j + 1) * LANES]
    return acc


def _select_bias(sc_ref, pb_ref, nc, n_q, tk, k_top, n_keys_total, key_axis=1):
    if key_axis == 1:
        q_shape, part_shape = (n_q, 1), (n_q, LANES)
        fold = lambda m: _fold_lanes(m.astype(F32))
        total = lambda acc: jnp.sum(acc, axis=1, keepdims=True)
    else:
        part_rows = min(tk, 8 * SUBLANES)
        q_shape, part_shape = (1, n_q), (part_rows, n_q)
        fold = lambda m: jnp.sum(m.astype(F32).reshape(tk // part_rows, part_rows, n_q), axis=0)
        total = lambda acc: jnp.sum(acc, axis=0, keepdims=True)

    def count(pred):
        if isinstance(nc, int):
            accs = [jnp.zeros(part_shape, F32) for _ in range(min(4, nc))]
            for c in range(nc):
                accs[c % len(accs)] = accs[c % len(accs)] + fold(pred(c, sc_ref[c]))
            return total(functools.reduce(lambda a, b_: a + b_, accs))

        def body(c, acc):
            return acc + fold(pred(c, sc_ref[c]))
        return total(lax.fori_loop(0, nc, body, jnp.zeros(part_shape, F32)))

    kf = float(k_top)
    cnt0 = count(lambda c, s: s >= 0.0)
    tau0 = jnp.where(cnt0 >= kf, 0, INT_MIN).astype(I32)

    def bit_body(i, tau):
        cand = tau + jnp.left_shift(jnp.int32(1), 30 - i)
        cand_f = _key_to_float(cand)
        cnt = count(lambda c, s: s >= cand_f)
        return jnp.where((cnt >= kf) | (cand <= KEY_NEG_INF), cand, tau)

    tau = lax.fori_loop(0, 31, bit_body, tau0)
    tau_f = _key_to_float(tau)
    need = kf - count(lambda c, s: s > tau_f)
    excess = count(lambda c, s: s == tau_f) - need

    def kidx(c):
        return c * tk + lax.broadcasted_iota(I32, (1, tk) if key_axis == 1 else (tk, 1), key_axis)

    n_bits = max(int(np.ceil(np.log2(n_keys_total))), 1)
    pb_ref[...] = jnp.full(q_shape, n_keys_total, I32)

    @pl.when(jnp.max(excess) > 0.0)
    def _():
        def tie_body(i, pb):
            cand = pb + jnp.left_shift(jnp.int32(1), n_bits - 1 - i)
            cnt = count(lambda c, s: (s == tau_f) & (kidx(c) < cand))
            return jnp.where(cnt < need, cand, pb)

        pb_ref[...] = lax.fori_loop(0, n_bits, tie_body, jnp.zeros(q_shape, I32))

    pb = pb_ref[...]

    def write(c, _):
        s = sc_ref[c]
        sel = ((s > tau_f) | ((s == tau_f) & (kidx(c) <= pb))) & (s > -jnp.inf)
        sc_ref[c] = jnp.where(sel, 0.0, NEG_BIG)
        return 0

    lax.fori_loop(0, nc, write, 0)


def _dsa_prompt_kernel(qt_ref, qit_ref, wit_ref, k_ref, vt_ref, ki_ref, o_ref, sc_ref, pb_ref, m_ref, l_ref, acc_ref,
                       *, tq, tk, k_top, n_heads, n_idx, t_total):
    qb = pl.program_id(1)
    q0 = qb * tq
    nc = (q0 + tq + tk - 1) // tk
    qpos = q0 + lax.broadcasted_iota(I32, (1, tq), 1)
    hd = lambda h: slice(h * HEAD_DIM, (h + 1) * HEAD_DIM)

    def scores(c, _):
        kic = ki_ref[0, pl.ds(pl.multiple_of(c * tk, tk), tk), :]
        acc = jnp.zeros((tk, tq), F32)
        for h in range(n_idx):
            acc = acc + jnp.maximum(_dot(kic, qit_ref[0, hd(h), :]), 0.0) * wit_ref[0, h:h + 1, :]
        spos = c * tk + lax.broadcasted_iota(I32, (tk, 1), 0)
        sc_ref[c] = jnp.where(spos <= qpos, acc, -jnp.inf)
        return 0

    lax.fori_loop(0, nc, scores, 0)
    _select_bias(sc_ref, pb_ref, nc, tq, tk, k_top, t_total, key_axis=0)

    m_ref[...] = jnp.full(m_ref.shape, NEG_BIG, F32)
    l_ref[...] = jnp.zeros(l_ref.shape, F32)
    acc_ref[...] = jnp.zeros(acc_ref.shape, F32)

    def attend(c, _):
        rows = pl.ds(pl.multiple_of(c * tk, tk), tk)
        bias = sc_ref[c]
        m_all, l_all, acc_all = m_ref[...], l_ref[...], acc_ref[...]
        hs = range(n_heads)
        s = [_dot(k_ref[0, h, rows, :], qt_ref[0, hd(h), :]) + bias for h in hs]
        m_new = [jnp.maximum(m_all[h:h + 1], jnp.max(s[h], axis=0, keepdims=True)) for h in hs]
        p = [jnp.exp((s[h] - m_new[h]).astype(BF16)) for h in hs]
        pv = [_dot(vt_ref[0, c, hd(h), :], p[h]) for h in hs]
        alpha = [jnp.exp(m_all[h:h + 1] - m_new[h]) for h in hs]
        l_new = [alpha[h] * l_all[h:h + 1] + jnp.sum(p[h].astype(F32), axis=0, keepdims=True) for h in hs]
        acc_ref[...] = jnp.concatenate([alpha[h] * acc_all[hd(h)] + pv[h] for h in hs], axis=0)
        m_ref[...] = jnp.concatenate(m_new, axis=0)
        l_ref[...] = jnp.concatenate(l_new, axis=0)
        return 0

    lax.fori_loop(0, nc, attend, 0)
    l_all = l_ref[...]
    for h in range(n_heads):
        o_ref[0, hd(h), :] = (acc_ref[hd(h), :] / l_all[h:h + 1]).astype(o_ref.dtype)


def _dsa_prompt(qt, qit, wit, k_hm, vt_ck, ki, *, tq, tk, k_top):
    b, att_w, t = qt.shape
    n_heads = att_w // HEAD_DIM
    n_idx = qit.shape[1] // HEAD_DIM
    assert n_heads == SUBLANES
    qblk = lambda i, j: (i, 0, j)
    return pl.pallas_call(
        functools.partial(_dsa_prompt_kernel, tq=tq, tk=tk, k_top=k_top, n_heads=n_heads, n_idx=n_idx,
                          t_total=t),
        grid=(b, t // tq),
        in_specs=[pl.BlockSpec((1, att_w, tq), qblk),
                  pl.BlockSpec((1, qit.shape[1], tq), qblk),
                  pl.BlockSpec((1, wit.shape[1], tq), qblk),
                  pl.BlockSpec((1,) + k_hm.shape[1:], lambda i, j: (i, 0, 0, 0)),
                  pl.BlockSpec((1,) + vt_ck.shape[1:], lambda i, j: (i, 0, 0, 0)),
                  pl.BlockSpec((1, t, HEAD_DIM), lambda i, j: (i, 0, 0))],
        out_specs=pl.BlockSpec((1, att_w, tq), qblk),
        out_shape=jax.ShapeDtypeStruct((b, att_w, t), BF16),
        scratch_shapes=[pltpu.VMEM((t // tk, tk, tq), F32),
                        pltpu.VMEM((1, tq), I32),
                        pltpu.VMEM((n_heads, tq), F32),
                        pltpu.VMEM((n_heads, tq), F32),
                        pltpu.VMEM((att_w, tq), F32)],
        compiler_params=_params("arbitrary", "arbitrary"),
        name="dsa_prompt",
    )(qt, qit, wit, k_hm, vt_ck, ki)


def _idx_page_scores(qim, wcol, ki_t, n_idx):
    d = jnp.maximum(_dot(qim, ki_t.astype(BF16)), 0.0) * wcol
    sc = d[0:SUBLANES]
    for h in range(1, n_idx):
        sc = sc + d[h * SUBLANES:(h + 1) * SUBLANES]
    return sc


def _idx_sample_kernel(pt_ref, qim_ref, wcol_ref, *refs, t_new, n_idx, g_pages):
    page_refs, o_ref = refs[:g_pages], refs[g_pages]
    qrow = lax.broadcasted_iota(I32, (SUBLANES, PAGE_SIZE), 0)
    for j in range(g_pages):
        sc = _idx_page_scores(qim_ref[0], wcol_ref[0], page_refs[j][0, 0], n_idx)
        o_ref[j] = jnp.where(qrow < t_new, sc, -jnp.inf)


def _idx_sample(page_table, qim, wcol, cache_idx_t, *, t_new, g_pages, layer):
    b, n_pages = page_table.shape
    n_idx = qim.shape[1] // SUBLANES
    per_seq = lambda i, g, pt: (i, 0, 0)
    page_specs = [pl.BlockSpec((1, 1, HEAD_DIM, PAGE_SIZE),
                               lambda i, g, pt, j=j: (layer, pt[i, g * g_pages + j], 0, 0))
                  for j in range(g_pages)]
    grid_spec = pltpu.PrefetchScalarGridSpec(
        num_scalar_prefetch=1,
        grid=(b, n_pages // g_pages),
        in_specs=[pl.BlockSpec((1,) + qim.shape[1:], per_seq),
                  pl.BlockSpec((1,) + wcol.shape[1:], per_seq)] + page_specs,
        out_specs=pl.BlockSpec((g_pages, SUBLANES, PAGE_SIZE), lambda i, g, pt: (g, i, 0)),
    )
    return pl.pallas_call(
        functools.partial(_idx_sample_kernel, t_new=t_new, n_idx=n_idx, g_pages=g_pages),
        grid_spec=grid_spec,
        out_shape=jax.ShapeDtypeStruct((n_pages, b * SUBLANES, PAGE_SIZE), F32),
        compiler_params=_params("arbitrary", "arbitrary"),
        name="idx_sample",
    )(page_table, qim, wcol, *([cache_idx_t] * g_pages))


def _idx_new_kernel(qim_ref, wcol_ref, new_ref, o_ref, *, t_new, n_idx):
    sc = _idx_page_scores(qim_ref[0], wcol_ref[0], new_ref[0], n_idx)
    qrow = lax.broadcasted_iota(I32, (SUBLANES, PAGE_SIZE), 0)
    lane = lax.broadcasted_iota(I32, (SUBLANES, PAGE_SIZE), 1)
    o_ref[...] = jnp.where((qrow < t_new) & (lane <= qrow), sc, -jnp.inf)


def _idx_new(qim, wcol, ki_new, *, t_new):
    b = qim.shape[0]
    n_idx = qim.shape[1] // SUBLANES
    per_seq = lambda i: (i, 0, 0)
    return pl.pallas_call(
        functools.partial(_idx_new_kernel, t_new=t_new, n_idx=n_idx),
        grid=(b,),
        in_specs=[pl.BlockSpec((1,) + qim.shape[1:], per_seq),
                  pl.BlockSpec((1,) + wcol.shape[1:], per_seq),
                  pl.BlockSpec((1, HEAD_DIM, PAGE_SIZE), per_seq)],
        out_specs=pl.BlockSpec((SUBLANES, PAGE_SIZE), lambda i: (i, 0)),
        out_shape=jax.ShapeDtypeStruct((b * SUBLANES, PAGE_SIZE), F32),
        compiler_params=_params("arbitrary"),
        name="idx_new",
    )(qim, wcol, ki_new)


def _select_sample_kernel(past_ref, new_ref, o_ref, pb_ref, *, k_top):
    n_pages, rows = past_ref.shape[0], past_ref.shape[1]
    o_ref[0:n_pages] = past_ref[...]
    o_ref[n_pages] = new_ref[...]
    _select_bias(o_ref, pb_ref, n_pages + 1, rows, PAGE_SIZE, k_top, (n_pages + 1) * PAGE_SIZE)


def _select_sample(past_scores, new_scores, *, k_top):
    n_pages, rows, _ = past_scores.shape
    out_shape = (n_pages + 1, rows, PAGE_SIZE)
    return pl.pallas_call(
        functools.partial(_select_sample_kernel, k_top=k_top),
        grid=(1,),
        in_specs=[pl.BlockSpec(past_scores.shape, lambda i: (0, 0, 0)),
                  pl.BlockSpec(new_scores.shape, lambda i: (0, 0))],
        out_specs=pl.BlockSpec(out_shape, lambda i: (0, 0, 0)),
        out_shape=jax.ShapeDtypeStruct(out_shape, F32),
        scratch_shapes=[pltpu.VMEM((rows, 1), I32)],
        compiler_params=_params("arbitrary"),
        name="select_sample",
    )(past_scores, new_scores)


def _attn_sample_kernel(pt_ref, qbd_ref, bias_ref, biasn_ref, knew_ref, vnew_ref, *refs, n_heads, g_pages):
    k_refs, v_refs = refs[:g_pages], refs[g_pages:2 * g_pages]
    o_ref, m_ref, l_ref, acc_ref = refs[2 * g_pages:]
    g = pl.program_id(1)

    @pl.when(g == 0)
    def _():
        m_ref[...] = jnp.full(m_ref.shape, NEG_BIG, F32)
        l_ref[...] = jnp.zeros(l_ref.shape, F32)
        acc_ref[...] = jnp.zeros(acc_ref.shape, F32)

    def step(kts, vts, biases):
        bias = jnp.concatenate([jnp.concatenate([b_] * n_heads, axis=0) for b_ in biases], axis=1)
        s = jnp.concatenate([_dot(qbd_ref[0], kt.astype(BF16)) for kt in kts], axis=1) + bias
        m = m_ref[...]
        m_new = jnp.maximum(m, jnp.max(s, axis=-1, keepdims=True))
        alpha = jnp.exp(m - m_new)
        pr = jnp.exp(s - m_new)
        prb = pr.astype(BF16)
        pv = None
        for j, vt in enumerate(vts):
            d = _dot_nt(prb[:, j * PAGE_SIZE:(j + 1) * PAGE_SIZE], vt.astype(BF16))
            pv = d if pv is None else pv + d
        l_ref[...] = alpha * l_ref[...] + jnp.sum(pr, axis=-1, keepdims=True)
        acc_ref[...] = alpha * acc_ref[...] + pv
        m_ref[...] = m_new

    step([r[0, 0] for r in k_refs], [r[0, 0] for r in v_refs], [bias_ref[j] for j in range(g_pages)])

    @pl.when(g == pl.num_programs(1) - 1)
    def _():
        step([knew_ref[0]], [vnew_ref[0]], [biasn_ref[0]])
        full = acc_ref[...] / l_ref[...]
        col_head = lax.broadcasted_iota(I32, (SUBLANES, full.shape[1]), 1) // HEAD_DIM
        out = jnp.zeros((SUBLANES, full.shape[1]), F32)
        for h in range(n_heads):
            out = out + jnp.where(col_head == h, full[h * SUBLANES:(h + 1) * SUBLANES], 0.0)
        o_ref[0] = out.astype(o_ref.dtype)


def _attn_sample(page_table, qbd, bias, cache_k_t, cache_v_t, k_new_t, v_new_t, *, g_pages, layer):
    b, n_pages = page_table.shape
    att_w = qbd.shape[2]
    n_heads = att_w // HEAD_DIM
    per_seq = lambda i, g, pt: (i, 0, 0)
    page_specs = [pl.BlockSpec((1, 1, att_w, PAGE_SIZE),
                               lambda i, g, pt, j=j: (layer, pt[i, g * g_pages + j], 0, 0))
                  for j in range(g_pages)]
    new_spec = pl.BlockSpec((1, att_w, PAGE_SIZE), per_seq)
    grid_spec = pltpu.PrefetchScalarGridSpec(
        num_scalar_prefetch=1,
        grid=(b, n_pages // g_pages),
        in_specs=[pl.BlockSpec((1,) + qbd.shape[1:], per_seq),
                  pl.BlockSpec((g_pages, SUBLANES, PAGE_SIZE), lambda i, g, pt: (g, i, 0)),
                  pl.BlockSpec((1, SUBLANES, PAGE_SIZE), lambda i, g, pt: (n_pages, i, 0)),
                  new_spec, new_spec] + page_specs + page_specs,
        out_specs=pl.BlockSpec((1, SUBLANES, att_w), per_seq),
        scratch_shapes=[pltpu.VMEM((n_heads * SUBLANES, 1), F32),
                        pltpu.VMEM((n_heads * SUBLANES, 1), F32),
                        pltpu.VMEM((n_heads * SUBLANES, att_w), F32)],
    )
    return pl.pallas_call(
        functools.partial(_attn_sample_kernel, n_heads=n_heads, g_pages=g_pages),
        grid_spec=grid_spec,
        out_shape=jax.ShapeDtypeStruct((b, SUBLANES, att_w), BF16),
        compiler_params=_params("arbitrary", "arbitrary"),
        name="attn_sample",
    )(page_table, qbd, bias, bias, k_new_t, v_new_t, *([cache_k_t] * g_pages), *([cache_v_t] * g_pages))


def _outproj_kernel(x_ref, yr_ref, ya_ref, mod_ref, wo_ref, g_ref, rw_ref, rb_ref, x1_ref, h2_ref, gates_ref, *, rw):
    mixed = _dot(yr_ref[...], wo_ref[0:rw, :]) + _dot(ya_ref[...], wo_ref[rw:, :])
    x1 = x_ref[...] + mod_ref[2, 0] * mixed
    x1_ref[...] = x1
    h2 = _rms(x1, g_ref[...]) * (1 + mod_ref[4, 0]) + mod_ref[3, 0]
    h2_ref[...] = h2.astype(BF16)
    logits = _mm(_split(h2), _split(rw_ref[...])) + rb_ref[...]
    lane = lax.broadcasted_iota(I32, logits.shape, 1).astype(F32)
    work = logits
    picked = jnp.zeros(logits.shape, jnp.bool_)
    top = None
    for i in range(TOP_K):
        mx = jnp.max(work, axis=-1, keepdims=True)
        first = jnp.min(jnp.where(work == mx, lane, float(LANES)), axis=-1, keepdims=True)
        hit = lane == first
        picked = picked | hit
        work = jnp.where(hit, -jnp.inf, work)
        if i == 0:
            top = mx
    e = jnp.where(picked, jnp.exp(logits - top), 0.0)
    gates_ref[...] = e / jnp.sum(e, axis=-1, keepdims=True)


def _outproj(x2d, yr, ya, mod6, w_out_b, g2n, router_w_pad, router_b_pad, *, tm, tiles_per_seq, rw):
    n, d = x2d.shape
    rmod = mod6.shape[2]
    row = lambda i: (i, 0)
    const = lambda i: (0, 0)
    outs = [jax.ShapeDtypeStruct((n, d), F32), jax.ShapeDtypeStruct((n, d), BF16),
            jax.ShapeDtypeStruct((n, LANES), F32)]
    return pl.pallas_call(
        functools.partial(_outproj_kernel, rw=rw),
        grid=(n // tm,),
        in_specs=[pl.BlockSpec((tm, d), row),
                  pl.BlockSpec((tm, yr.shape[1]), row),
                  pl.BlockSpec((tm, ya.shape[1]), row),
                  pl.BlockSpec((6, 1, rmod, d), lambda i: (0, i // tiles_per_seq, 0, 0)),
                  pl.BlockSpec(w_out_b.shape, const),
                  pl.BlockSpec((1, d), const),
                  pl.BlockSpec(router_w_pad.shape, const),
                  pl.BlockSpec((1, LANES), const)],
        out_specs=[pl.BlockSpec((tm, o.shape[1]), row) for o in outs],
        out_shape=outs,
        compiler_params=_params("arbitrary"),
        name="outproj_router",
    )(x2d, yr, ya, mod6, w_out_b, g2n, router_w_pad, router_b_pad)


ROUTE_TILE = 512
ROUTE_WIN = 128
ROUTE_ALIGN = 16
FFN_ROWS = 512


def _route_meta_kernel(g_ref, rank_ref, gate_ref, cnt_ref, *, n_exp):
    gt = jnp.transpose(g_ref[...])
    sel = gt > 0.0
    tt = gt.shape[1]
    earlier = (lax.broadcasted_iota(I32, (tt, tt), 0) < lax.broadcasted_iota(I32, (tt, tt), 1)).astype(BF16)
    rank = _dot(sel.astype(BF16), earlier)
    rank_ref[...] = jnp.where(sel, rank, -1.0)[:n_exp]
    gate_ref[...] = gt[:n_exp]
    cnt_ref[0] = jnp.broadcast_to(jnp.sum(sel.astype(F32), axis=-1, keepdims=True), (LANES, LANES))


def _route_meta(gates, *, n_exp):
    n = gates.shape[0]
    n_tiles = n // ROUTE_TILE
    return pl.pallas_call(
        functools.partial(_route_meta_kernel, n_exp=n_exp),
        grid=(n_tiles,),
        in_specs=[pl.BlockSpec((ROUTE_TILE, LANES), lambda i: (i, 0))],
        out_specs=[pl.BlockSpec((n_exp, ROUTE_TILE), lambda i: (0, i)),
                   pl.BlockSpec((n_exp, ROUTE_TILE), lambda i: (0, i)),
                   pl.BlockSpec((1, LANES, LANES), lambda i: (i, 0, 0))],
        out_shape=[jax.ShapeDtypeStruct((n_exp, n), F32), jax.ShapeDtypeStruct((n_exp, n), F32),
                   jax.ShapeDtypeStruct((n_tiles, LANES, LANES), F32)],
        compiler_params=_params("arbitrary"),
        name="route_meta",
    )(gates)


def _window_onehot(rank_row, w):
    rows = lax.broadcasted_iota(I32, (ROUTE_WIN, 1), 0).astype(F32) + float(ROUTE_WIN) * w
    return rank_row == rows


def _dispatch_kernel(base_ref, cnt_ref, h_ref, rank_ref, gate_ref, xs_in, gs_in, xs_out, gs_out, p_ref, sx_ref,
                     sg_ref, sem_x, sem_g, *, n_exp, group):
    del xs_in, gs_in
    i = pl.program_id(0)
    h = h_ref[...]
    win = ROUTE_WIN

    def copies(slot, row0):
        rows = pl.ds(pl.multiple_of(row0, ROUTE_ALIGN), win)
        return (pltpu.make_async_copy(sx_ref.at[slot], xs_out.at[rows], sem_x.at[slot]),
                pltpu.make_async_copy(sg_ref.at[slot], gs_out.at[rows], sem_g.at[slot]))

    def fill(slot, e, w):
        p = _window_onehot(rank_ref[pl.ds(e, 1), :], w)
        p_ref[slot * win:(slot + 1) * win, :] = p.astype(BF16)
        g = jnp.sum(jnp.where(p, gate_ref[pl.ds(e, 1), :], 0.0), axis=-1, keepdims=True)
        sg_ref[slot] = jnp.broadcast_to(g, (win, LANES))

    for e0 in range(0, n_exp, group):
        for j in range(group):
            fill(e0 + j, e0 + j, 0)
        x = _dot(p_ref[e0 * win:(e0 + group) * win, :], h)
        for j in range(group):
            e = e0 + j
            sx_ref[e] = x[j * win:(j + 1) * win].astype(BF16)
            for c in copies(e, base_ref[i, e]):
                c.start()
    for e in range(n_exp):
        for c in copies(e, base_ref[i, e]):
            c.wait()

    def per_expert(e, _):
        def extra(w, _):
            fill(0, e, w)
            sx_ref[0] = _dot(p_ref[0:win, :], h_ref[...]).astype(BF16)
            cs = copies(0, base_ref[i, e] + w * win)
            for c in cs:
                c.start()
            for c in cs:
                c.wait()
            return 0

        return lax.fori_loop(1, (cnt_ref[i, e] + win - 1) // win, extra, 0)

    lax.fori_loop(0, n_exp, per_expert, 0)


def _dispatch(base, cnt, h_all, rank_t, gate_t, n_rows):
    n, d = h_all.shape
    n_exp = rank_t.shape[0]
    n_tiles = n // ROUTE_TILE
    tile = lambda i, b_, c_: (0, i)
    grid_spec = pltpu.PrefetchScalarGridSpec(
        num_scalar_prefetch=2,
        grid=(n_tiles,),
        in_specs=[pl.BlockSpec((ROUTE_TILE, d), lambda i, b_, c_: (i, 0)),
                  pl.BlockSpec((n_exp, ROUTE_TILE), tile),
                  pl.BlockSpec((n_exp, ROUTE_TILE), tile),
                  pl.BlockSpec(memory_space=pl.ANY),
                  pl.BlockSpec(memory_space=pl.ANY)],
        out_specs=[pl.BlockSpec(memory_space=pl.ANY), pl.BlockSpec(memory_space=pl.ANY)],
        scratch_shapes=[pltpu.VMEM((n_exp * ROUTE_WIN, ROUTE_TILE), BF16),
                        pltpu.VMEM((n_exp, ROUTE_WIN, d), BF16),
                        pltpu.VMEM((n_exp, ROUTE_WIN, LANES), F32),
                        pltpu.SemaphoreType.DMA((n_exp,)),
                        pltpu.SemaphoreType.DMA((n_exp,))],
    )
    xs0 = jnp.zeros((n_rows, d), BF16)
    gs0 = jnp.zeros((n_rows, LANES), F32)
    return pl.pallas_call(
        functools.partial(_dispatch_kernel, n_exp=n_exp, group=8),
        grid_spec=grid_spec,
        out_shape=[jax.ShapeDtypeStruct(xs0.shape, BF16), jax.ShapeDtypeStruct(gs0.shape, F32)],
        input_output_aliases={5: 0, 6: 1},
        compiler_params=_params("arbitrary"),
        name="moe_dispatch",
    )(base, cnt, h_all, rank_t, gate_t, xs0, gs0)


def _ffn_sorted_kernel(te_ref, nu_ref, xs_ref, gs_ref, wgu_ref, bgu_ref, wd_ref, bd_ref, ys_ref, wgu_b, wd_b, *,
                       d_ff):
    r = pl.program_id(0)

    @pl.when(r < nu_ref[0])
    def _():
        @pl.when((r == 0) | (te_ref[r] != te_ref[jnp.maximum(r - 1, 0)]))
        def _():
            wgu_b[...] = wgu_ref[0].astype(BF16)
            wd_b[...] = wd_ref[0].astype(BF16)

        gu = _dot(xs_ref[...], wgu_b[...]) + bgu_ref[0]
        gate = jnp.minimum(gu[:, :d_ff], SWIGLU_LIMIT)
        up = jnp.clip(gu[:, d_ff:], -SWIGLU_LIMIT, SWIGLU_LIMIT)
        act = (up + 1) * gate * _sigmoid(SWIGLU_ALPHA * gate)
        y = _dot(act.astype(BF16), wd_b[...]) + bd_ref[0]
        ys_ref[...] = (gs_ref[:, 0:1] * y).astype(ys_ref.dtype)

    @pl.when(r >= nu_ref[0])
    def _():
        ys_ref[...] = jnp.zeros(ys_ref.shape, ys_ref.dtype)


def _ffn_sorted(tile_expert, n_used, xs, gs, wgu, bgu, wd, bd):
    n_rows, d = xs.shape
    n_exp, _, two_ff = wgu.shape
    row = lambda r, te, nu: (r, 0)
    exp3 = lambda r, te, nu: (te[r], 0, 0)
    grid_spec = pltpu.PrefetchScalarGridSpec(
        num_scalar_prefetch=2,
        grid=(n_rows // FFN_ROWS,),
        in_specs=[pl.BlockSpec((FFN_ROWS, d), row),
                  pl.BlockSpec((FFN_ROWS, LANES), row),
                  pl.BlockSpec((1, d, two_ff), exp3),
                  pl.BlockSpec((1, 1, two_ff), exp3),
                  pl.BlockSpec((1, two_ff // 2, d), exp3),
                  pl.BlockSpec((1, 1, d), exp3)],
        out_specs=pl.BlockSpec((FFN_ROWS, d), row),
        scratch_shapes=[pltpu.VMEM((d, two_ff), BF16), pltpu.VMEM((two_ff // 2, d), BF16)],
    )
    return pl.pallas_call(
        functools.partial(_ffn_sorted_kernel, d_ff=two_ff // 2),
        grid_spec=grid_spec,
        out_shape=jax.ShapeDtypeStruct((n_rows, d), BF16),
        compiler_params=_params("arbitrary"),
        name="moe_ffn_sorted",
    )(tile_expert, n_used, xs, gs, wgu, bgu, wd, bd)


def _combine_kernel(base_ref, cnt_ref, rank_ref, x1_ref, mod_ref, fg_ref, ys_hbm, o_ref, p_ref, yb_ref, acc_ref,
                    sem, *, n_exp, tile0):
    i = pl.program_id(0) + tile0
    win = ROUTE_WIN

    def copy(slot, row0):
        rows = pl.ds(pl.multiple_of(row0, ROUTE_ALIGN), win)
        return pltpu.make_async_copy(ys_hbm.at[rows], yb_ref.at[pl.ds(slot * win, win)], sem.at[slot])

    for e in range(n_exp):
        copy(e, base_ref[i, e]).start()
    for e in range(n_exp):
        p_ref[e * win:(e + 1) * win, :] = _window_onehot(rank_ref[e:e + 1, :], 0).astype(BF16)
    for e in range(n_exp):
        copy(e, base_ref[i, e]).wait()
    acc_ref[...] = _dot_tn(p_ref[...], yb_ref[...])

    def per_expert(e, _):
        def extra(w, _):
            c = copy(0, base_ref[i, e] + w * win)
            c.start()
            p = _window_onehot(rank_ref[pl.ds(e, 1), :], w).astype(BF16)
            c.wait()
            acc_ref[...] += _dot_tn(p, yb_ref[0:win, :])
            return 0

        return lax.fori_loop(1, (cnt_ref[i, e] + win - 1) // win, extra, 0)

    lax.fori_loop(0, n_exp, per_expert, 0)
    x2 = x1_ref[...] + mod_ref[5, 0] * acc_ref[...]
    o_ref[...] = _rms(x2, fg_ref[...])


def _combine(base, cnt, rank_t, x1, mod6, fg, ys, *, tile0, tiles_per_seq):
    n, d = x1.shape
    n_exp = rank_t.shape[0]
    rmod = mod6.shape[2]
    grid_spec = pltpu.PrefetchScalarGridSpec(
        num_scalar_prefetch=2,
        grid=(n // ROUTE_TILE,),
        in_specs=[pl.BlockSpec((n_exp, ROUTE_TILE), lambda i, b_, c_: (0, i + tile0)),
                  pl.BlockSpec((ROUTE_TILE, d), lambda i, b_, c_: (i, 0)),
                  pl.BlockSpec((6, 1, rmod, d), lambda i, b_, c_: (0, i // tiles_per_seq, 0, 0)),
                  pl.BlockSpec((1, d), lambda i, b_, c_: (0, 0)),
                  pl.BlockSpec(memory_space=pl.ANY)],
        out_specs=pl.BlockSpec((ROUTE_TILE, d), lambda i, b_, c_: (i, 0)),
        scratch_shapes=[pltpu.VMEM((n_exp * ROUTE_WIN, ROUTE_TILE), BF16),
                        pltpu.VMEM((n_exp * ROUTE_WIN, d), BF16),
                        pltpu.VMEM((ROUTE_TILE, d), F32),
                        pltpu.SemaphoreType.DMA((n_exp,))],
    )
    return pl.pallas_call(
        functools.partial(_combine_kernel, n_exp=n_exp, tile0=tile0),
        grid_spec=grid_spec,
        out_shape=jax.ShapeDtypeStruct((n, d), F32),
        compiler_params=_params("arbitrary"),
        name="moe_combine",
    )(base, cnt, rank_t, x1, mod6, fg, ys)


def _round_up(a, m):
    return (a + m - 1) // m * m


def _moe_routed(pre_p, pre_s, wts):
    x1_p, h2_p, gates_p, mod6_p, tps_p = pre_p
    x1_s, h2_s, gates_s, mod6_s, _ = pre_s
    n_p, d = x1_p.shape
    n_s = x1_s.shape[0]
    n_exp = wts["bgu"].shape[0]
    assert n_p % ROUTE_TILE == 0 and n_s <= ROUTE_TILE and mod6_s.shape[2] == n_s
    pad_rows = lambda a: jnp.pad(a, ((0, ROUTE_TILE - n_s), (0, 0)))
    h_all = jnp.concatenate([h2_p, pad_rows(h2_s)], axis=0)
    g_all = jnp.concatenate([gates_p, pad_rows(gates_s)], axis=0)
    n_all = h_all.shape[0]
    n_tiles = n_all // ROUTE_TILE
    rank_t, gate_t, cnt_f = _route_meta(g_all, n_exp=n_exp)

    cnt = cnt_f[:, :n_exp, 0].astype(I32)
    seg = _round_up(cnt, ROUTE_ALIGN)
    before = lambda m: jnp.arange(m)[:, None] > jnp.arange(m)[None, :]
    seg_off = jnp.sum(jnp.where(before(n_tiles)[:, :, None], seg[None, :, :], 0), axis=1)
    region = _round_up(jnp.sum(seg, axis=0) + ROUTE_WIN, FFN_ROWS)
    region_start = jnp.sum(jnp.where(before(n_exp), region[None, :], 0), axis=1)
    region_end = region_start + region
    base = (region_start[None, :] + seg_off).astype(I32)
    n_rows = _round_up(TOP_K * n_all + (ROUTE_ALIGN - 1) * n_tiles * n_exp + (ROUTE_WIN + FFN_ROWS) * n_exp, FFN_ROWS)
    tile_start = jnp.arange(n_rows // FFN_ROWS, dtype=I32) * FFN_ROWS
    tile_expert = jnp.minimum(jnp.sum((tile_start[:, None] >= region_end[None, :]).astype(I32), axis=1), n_exp - 1)
    n_used = (region_end[-1:] // FFN_ROWS).astype(I32)

    xs, gs = _dispatch(base, cnt, h_all, rank_t, gate_t, n_rows)
    ys = _ffn_sorted(tile_expert, n_used, xs, gs, wts["wgu32"], wts["bgu"], wts["wd32"], wts["bd"])
    y_p = _combine(base, cnt, rank_t, x1_p, mod6_p, wts["fg"], ys, tile0=0, tiles_per_seq=tps_p)
    mod6_sp = jnp.pad(mod6_s, ((0, 0), (0, 0), (0, ROUTE_TILE - n_s), (0, 0)))
    y_s = _combine(base, cnt, rank_t, pad_rows(x1_s), mod6_sp, wts["fg"], ys, tile0=n_p // ROUTE_TILE,
                   tiles_per_seq=1)
    return y_p, y_s[:n_s]


def _pad_cols(a, width):
    return jnp.pad(a, ((0, 0), (0, width - a.shape[1])))


def _rope_tables(pos):
    half = HEAD_DIM // 2
    inv = ROPE_THETA ** (-jnp.arange(half, dtype=F32) / half)
    ang = pos.astype(F32)[:, None] * inv[None, :]
    cos, sin = jnp.cos(ang), jnp.sin(ang)
    cos_h = jnp.concatenate([cos, cos], axis=1)
    sin_h = jnp.concatenate([-sin, sin], axis=1)
    ones = jnp.ones_like(cos_h)
    zeros = jnp.zeros_like(cos_h)
    tabs = jnp.stack([jnp.concatenate([cos_h, cos_h], 1), jnp.concatenate([sin_h, sin_h], 1),
                      jnp.concatenate([cos_h, ones], 1), jnp.concatenate([sin_h, zeros], 1)])
    return tabs


def _layer(x, mod_rows, pos, shift_prev, wkv_prev, wts, attend, *, tm, chunk, seqs_per_step):
    b, t, d = x.shape
    n = b * t
    rw, att_w, n_rw, rwkv_in = wts["rw"], wts["att_w"], wts["n_rw"], wts["rwkv_in"]
    per_token_mod = mod_rows.shape[0] != b or tm > t
    if per_token_mod:
        mod6 = jnp.repeat(mod_rows, t, axis=0).reshape(n // tm, tm, 6, d).transpose(2, 0, 1, 3)
        tiles_per_seq = 1
    else:
        mod6 = mod_rows.reshape(b, 1, 6, d).transpose(2, 0, 1, 3)
        tiles_per_seq = t // tm
    cs = _rope_tables(pos)
    n_idx_scale = wts["idx_scale"]
    cs = cs.at[2, :, HEAD_DIM:].set(n_idx_scale)
    if cs.shape[1] < tm:
        cs = jnp.tile(cs, (1, tm // cs.shape[1], 1))
    x2d = x.reshape(n, d)
    p, qt, kt32, k_hm, vt32, vt_ck, qit, kw, kwt = _inproj(x2d, mod6, wts["g1"], wts["w_cat"], cs, tm=tm,
                                                           tiles_per_seq=tiles_per_seq, n_rw=n_rw, att_w=att_w)
    t_pad = -(-t // chunk) * chunk
    p3 = p.reshape(b, t, n_rw)
    if t_pad != t:
        p3 = jnp.pad(p3, ((0, 0), (0, t_pad - t), (0, 0)))
    z0 = jnp.swapaxes(wkv_prev, -1, -2)
    yr, z_fin = _rwkv(p3, shift_prev.reshape(b, 1, n_rw), z0, wts["mu"], wts["vec"], wts["w2"], wts["a2"], wts["g2"],
                      wts["seg"], L=chunk, nb=seqs_per_step, t_valid=t if t_pad != t else chunk, rw=rw)
    yr = yr[:, :t].reshape(n, rw)
    wkv_new = jnp.swapaxes(z_fin, -1, -2)
    n_heads = att_w // HEAD_DIM
    tok = lambda a_t: jnp.swapaxes(a_t, 1, 2).reshape(b, t, a_t.shape[1])
    ya = attend(qt, qit, kw, kwt, kt32, k_hm, vt32, vt_ck).reshape(n, att_w)
    x1, h2, gates = _outproj(x2d, yr, ya, mod6, wts["w_out"], wts["g2n"], wts["router_w"], wts["router_b"], tm=tm,
                             tiles_per_seq=tiles_per_seq, rw=rw)
    pre_moe = (x1, h2, gates, mod6, 1 if per_token_mod else t // ROUTE_TILE)
    p_last = p3[:, t - 1, :]
    new_shift = jnp.concatenate([p_last[:, :3 * rw], p_last[:, 3 * rw:3 * rw + wts["lora"][0]],
                                 p_last[:, 3 * rw + 128:3 * rw + 128 + wts["lora"][1]],
                                 p_last[:, 3 * rw + 256:3 * rw + 256 + wts["lora"][2]]], axis=1)
    state = (tok(kt32).reshape(b, t, n_heads, HEAD_DIM), tok(vt32).reshape(b, t, n_heads, HEAD_DIM),
             tok(kwt[:, :HEAD_DIM]), wkv_new, new_shift)
    return pre_moe, state


def kernel(x_prompt, x_sample, c_prompt, c_sample, cache_k, cache_v, cache_idx_k, state_wkv, state_shift, page_table, norm1_g, norm2_g, w_mod, b_mod, w_in, mu_shift, w0, w2, a0, a2, g2, k_k, k_a, r_k, lnx_g, lnx_b, w_out, router_w, router_b, w_gate_up, b_gate_up, w_down, b_down, final_g):
    depth = w_in.shape[0]
    assert depth == 1, "single-layer trunk"
    l = 0
    bp, tp, d = x_prompt.shape
    bs, ts, _ = x_sample.shape
    rw = w0.shape[1]
    att_w = w_out.shape[1] - rw
    lora = (w2.shape[1], a2.shape[1], g2.shape[1])
    rwkv_in = 3 * rw + sum(lora)
    n_idx = (w_in.shape[2] - rwkv_in - 3 * att_w - HEAD_DIM) // (HEAD_DIM + 1)
    n_exp = router_w.shape[2]
    n_pages = page_table.shape[1]
    past = n_pages * PAGE_SIZE
    n_rw = 3 * rw + 512
    assert lora[0] <= 128 and lora[1] <= 128 and lora[2] <= 256 and n_idx * HEAD_DIM == att_w

    wi = w_in[l]
    o = 3 * rw
    rw_cols = [wi[:, :o], _pad_cols(wi[:, o:o + lora[0]], 128), _pad_cols(wi[:, o + lora[0]:o + lora[0] + lora[1]], 128),
               _pad_cols(wi[:, o + lora[0] + lora[1]:rwkv_in], 256)]
    a0_ = rwkv_in
    att_cols = [wi[:, a0_:a0_ + 4 * att_w], _pad_cols(wi[:, a0_ + 4 * att_w:], LANES)]
    w_cat = jnp.concatenate(rw_cols + att_cols, axis=1).astype(BF16)
    mu = mu_shift[l]
    mu_pad = jnp.concatenate([mu[:o], jnp.pad(mu[o:o + lora[0]], (0, 128 - lora[0])),
                              jnp.pad(mu[o + lora[0]:o + lora[0] + lora[1]], (0, 128 - lora[1])),
                              jnp.pad(mu[o + lora[0] + lora[1]:], (0, 256 - lora[2]))]).reshape(1, n_rw)
    vec = jnp.stack([w0[l], a0[l], k_k[l], k_a[l], lnx_g[l], lnx_b[l], r_k[l].reshape(-1),
                     jnp.zeros((rw,), F32)])
    head_of = np.arange(rw) // HEAD_DIM
    seg = jnp.asarray((head_of[:, None] == head_of[None, :]).astype(np.float32)).astype(BF16)
    wts = dict(
        rw=rw, att_w=att_w, n_rw=n_rw, rwkv_in=rwkv_in, lora=lora,
        idx_scale=float((n_idx * HEAD_DIM) ** -0.5),
        g1=norm1_g[l].reshape(1, d), w_cat=w_cat, mu=mu_pad, vec=vec,
        w2=jnp.pad(w2[l], ((0, 128 - lora[0]), (0, 0))), a2=jnp.pad(a2[l], ((0, 128 - lora[1]), (0, 0))),
        g2=jnp.pad(g2[l], ((0, 256 - lora[2]), (0, 0))), seg=seg,
        w_out=w_out[l].astype(BF16), g2n=norm2_g[l].reshape(1, d),
        router_w=_pad_cols(router_w[l], LANES),
        router_b=jnp.concatenate([router_b[l], jnp.full((LANES - n_exp,), NEG_BIG, F32)]).reshape(1, LANES),
        wgu32=w_gate_up[l], bgu=b_gate_up[l].reshape(n_exp, 1, -1),
        wd32=w_down[l], bd=b_down[l].reshape(n_exp, 1, -1), fg=final_g.reshape(1, d),
    )

    c_all = jnp.concatenate([c_prompt, c_sample], axis=0)
    rows_pad = -(-c_all.shape[0] // SUBLANES) * SUBLANES
    mod_all = _modulation(jnp.pad(c_all, ((0, rows_pad - c_all.shape[0]), (0, 0))), w_mod[l], b_mod[l])
    mod_p, mod_s = mod_all[:bp], mod_all[bp:bp + bs]

    k_top_p = min(TOPK_MAX, tp // 4)

    tm_p = min(512, tp)

    def attend_prompt(qt, qit, kw, kwt, kt32, k_hm, vt32, vt_ck):
        wit = kwt[:, HEAD_DIM:HEAD_DIM + SUBLANES, :]
        kib = kw[:, :HEAD_DIM].astype(BF16).reshape(bp, tp, HEAD_DIM)
        out_t = _dsa_prompt(qt, qit, wit, k_hm, vt_ck, kib, tq=min(256, tp), tk=tm_p, k_top=k_top_p)
        return jnp.swapaxes(out_t, 1, 2)

    pre_p, st_p = _layer(x_prompt, mod_p, jnp.arange(tp), jnp.zeros((bp, n_rw), F32),
                       jnp.zeros((bp, rw // HEAD_DIM, HEAD_DIM, HEAD_DIM), F32), wts, attend_prompt,
                       tm=tm_p, chunk=64, seqs_per_step=4 if bp % 4 == 0 else (2 if bp % 2 == 0 else 1))

    k_top_s = min(TOPK_MAX, (past + ts) // 4)
    n_heads = att_w // HEAD_DIM
    assert ts <= SUBLANES
    pool_t = lambda c: jnp.transpose(c, (0, 1, 3, 4, 2)).reshape(c.shape[0], c.shape[1], att_w, PAGE_SIZE)
    ck_t, cv_t = pool_t(cache_k), pool_t(cache_v)
    cik_t = jnp.transpose(cache_idx_k, (0, 1, 3, 2))
    g_idx = 32 if n_pages % 32 == 0 else (16 if n_pages % 16 == 0 else 1)
    g_att = 16 if n_pages % 16 == 0 else (8 if n_pages % 8 == 0 else 1)

    def head_query_rows(a, nh):
        a = a.reshape(bs, ts, nh, HEAD_DIM).transpose(0, 2, 1, 3)
        a = jnp.pad(a, ((0, 0), (0, 0), (0, SUBLANES - ts), (0, 0)))
        return a.reshape(bs, nh * SUBLANES, HEAD_DIM)

    def new_rows_t(a):
        a = a.reshape(bs, ts, a.shape[1]).transpose(0, 2, 1)
        return jnp.pad(a, ((0, 0), (0, 0), (0, PAGE_SIZE - ts)))

    def attend_sample(qt, qit, kw, kwt, kt32, k_hm, vt32, vt_ck):
        tok2 = lambda a_t: jnp.swapaxes(a_t, 1, 2).reshape(bs * ts, a_t.shape[1])
        q, qi, k32, v32 = tok2(qt), tok2(qit), tok2(kt32), tok2(vt32)
        qim = head_query_rows(qi, n_idx)
        wcol = kw[:, HEAD_DIM:HEAD_DIM + n_idx].reshape(bs, ts, n_idx).transpose(0, 2, 1)
        wcol = jnp.pad(wcol, ((0, 0), (0, 0), (0, SUBLANES - ts))).reshape(bs, n_idx * SUBLANES, 1)
        past_scores = _idx_sample(page_table, qim, wcol, cik_t, t_new=ts, g_pages=g_idx, layer=l)
        new_scores = _idx_new(qim, wcol, new_rows_t(kw[:, :HEAD_DIM]), t_new=ts)
        bias = _select_sample(past_scores, new_scores, k_top=k_top_s)
        qh = head_query_rows(q, n_heads)
        eye = jnp.eye(n_heads, dtype=BF16)
        qbd = (qh.reshape(bs, n_heads, SUBLANES, 1, HEAD_DIM) * eye[None, :, None, :, None]).reshape(
            bs, n_heads * SUBLANES, att_w)
        out = _attn_sample(page_table, qbd, bias, ck_t, cv_t, new_rows_t(k32), new_rows_t(v32), g_pages=g_att,
                           layer=l)
        return out[:, :ts]

    shift_s = state_shift[l]
    o1, o2 = o + lora[0], o + lora[0] + lora[1]
    shift_pad = jnp.concatenate([shift_s[:, :o], _pad_cols(shift_s[:, o:o1], 128), _pad_cols(shift_s[:, o1:o2], 128),
                                 _pad_cols(shift_s[:, o2:], 256)], axis=1)
    pre_s, st_s = _layer(x_sample, mod_s, past + jnp.arange(ts), shift_pad, state_wkv[l], wts, attend_sample,
                       tm=bs * ts, chunk=SUBLANES, seqs_per_step=4 if bs % 4 == 0 else 1)

    y_p, y_s = _moe_routed(pre_p, pre_s, wts)
    y_p, y_s = y_p.reshape(bp, tp, d), y_s.reshape(bs, ts, d)
    ex = lambda a: a[None]
    return (y_p, y_s,ex(st_p[0]), ex(st_p[1]), ex(st_p[2]), ex(st_p[3]), ex(st_p[4]),
            ex(st_s[0]), ex(st_s[1]), ex(st_s[2]), ex(st_s[3]), ex(st_s[4]))
```
